```python
import math
import numpy as np
import jax, jax.numpy as jnp
from jax import lax

D_MODEL = 1024
BATCH = 4
SEQ = 4096
DEPTH = 1

D_MIX = D_MODEL
D_PLE = 256
DA = D_MIX // 2
HA = 4
DA_V = DA // HA
DA_QK = DA_V // 2
DB = D_MIX - DA
HB = 16
DB_H = DB // HB
D_IN = 3 * DA + 3 * DB
GRID_W = 64
WIN_R = 8
WIN_C = 16
Q_COLS = 16
K_COLS = 32
ROPE_THETA = 10000.0
EPS = 1e-6
Q_BLOCK = 128
NEG = -1e30
N_GROUPS = 4
EXPERTS_PER_GROUP = 8
N_EXPERTS = N_GROUPS * EXPERTS_PER_GROUP
TOP_K = 2
D_EXPERT = 512
MOE_BLOCK = 128

kernel_name = 'hybrid_diffattn_natten_hmoe_encoder'


def rms_norm(x, g):
    xf = x.astype(jnp.float32)
    y = xf * lax.rsqrt(jnp.mean(xf * xf, axis=-1, keepdims=True) + EPS)
    return (y * g.astype(jnp.float32)).astype(x.dtype)


def rope_tables(seq, dim):
    inv = ROPE_THETA ** (-jnp.arange(0, dim, 2, dtype=jnp.float32) / dim)
    ang = jnp.arange(seq, dtype=jnp.float32)[:, None] * inv[None, :]
    return jnp.cos(ang), jnp.sin(ang)


def apply_rope(x, cos, sin):
    s = x.shape[1]
    bshape = (s,) + (1,) * (x.ndim - 3) + (-1,)
    c = cos.reshape(bshape)
    sn = sin.reshape(bshape)
    x1, x2 = jnp.split(x.astype(jnp.float32), 2, axis=-1)
    return jnp.concatenate([x1 * c - x2 * sn, x2 * c + x1 * sn], axis=-1).astype(x.dtype)


def diff_attention(q, k, v, g_q, g_k, lam_q1, lam_k1, lam_q2, lam_k2, g_sub, lam_init):
    b, s = q.shape[:2]
    cos, sin = rope_tables(s, DA_QK)
    q = apply_rope(rms_norm(q, g_q), cos, sin)
    k = apply_rope(rms_norm(k, g_k), cos, sin)
    lam = (jnp.exp(jnp.sum(lam_q1.astype(jnp.float32) * lam_k1.astype(jnp.float32)))
           - jnp.exp(jnp.sum(lam_q2.astype(jnp.float32) * lam_k2.astype(jnp.float32)))
           + lam_init)
    scale = DA_QK ** -0.5
    qt = q.transpose(0, 2, 1, 3, 4)
    kt = k.transpose(0, 2, 1, 3, 4)
    vt = v.transpose(0, 2, 1, 3)
    k1, k2 = kt[..., 0, :], kt[..., 1, :]
    nqb = s // Q_BLOCK
    qb = qt.reshape(b, HA, nqb, Q_BLOCK, 2, DA_QK).transpose(2, 0, 1, 3, 4, 5)

    def block(qblk):
        s1 = jnp.einsum('bhqd,bhkd->bhqk', qblk[..., 0, :], k1).astype(jnp.float32) * scale
        s2 = jnp.einsum('bhqd,bhkd->bhqk', qblk[..., 1, :], k2).astype(jnp.float32) * scale
        w = (jax.nn.softmax(s1, axis=-1) - lam * jax.nn.softmax(s2, axis=-1)).astype(vt.dtype)
        return jnp.einsum('bhqk,bhkd->bhqd', w, vt)

    o = lax.map(block, qb)
    o = o.transpose(1, 0, 3, 2, 4).reshape(b, s, HA, DA_V)
    o = rms_norm(o, g_sub) * (1.0 - lam_init)
    return o.reshape(b, s, HA * DA_V)


def neighbourhood_attention(q, k, v, g_q, g_k, rpb):
    b, s, h, d = q.shape
    rows = s // GRID_W
    kr = min(WIN_R, rows)
    ncb = GRID_W // Q_COLS
    q = rms_norm(q, g_q)
    k = rms_norm(k, g_k)
    r = np.arange(rows)
    rs = np.clip(r - kr // 2, 0, rows - kr)
    key_rows = rs[:, None] + np.arange(kr)[None, :]
    row_off = (key_rows - r[:, None] + (WIN_R - 1)).astype(np.int32)
    c = np.arange(GRID_W).reshape(ncb, Q_COLS)
    cs = np.clip(c - WIN_C // 2, 0, GRID_W - WIN_C)
    kb = np.minimum(cs[:, 0], GRID_W - K_COLS)
    key_cols = kb[:, None] + np.arange(K_COLS)[None, :]
    col_valid = ((key_cols[:, None, :] >= cs[:, :, None])
                 & (key_cols[:, None, :] < cs[:, :, None] + WIN_C))
    col_off = np.clip(key_cols[:, None, :] - c[:, :, None] + (WIN_C - 1), 0, 2 * WIN_C - 2)
    key_idx = (key_rows[:, None, :, None] * GRID_W + key_cols[None, :, None, :])
    key_idx = key_idx.reshape(rows, ncb * kr * K_COLS).astype(np.int32)
    bias_cols = rpb.astype(jnp.float32)[:, :, col_off]
    bias_cols = jnp.where(col_valid, bias_cols, NEG)
    scale = d ** -0.5
    q_rows = q.reshape(b, rows, GRID_W, h, d).transpose(1, 0, 2, 3, 4)

    def row_step(args):
        q_row, idx, roff = args
        qb = q_row.reshape(b, ncb, Q_COLS, h, d)
        kg = k[:, idx].reshape(b, ncb, kr * K_COLS, h, d)
        vg = v[:, idx].reshape(b, ncb, kr * K_COLS, h, d)
        bias = bias_cols[:, roff]
        bias = bias.transpose(0, 2, 3, 1, 4).reshape(h, ncb, Q_COLS, kr * K_COLS)
        sc = jnp.einsum('bjqhd,bjkhd->bhjqk', qb, kg).astype(jnp.float32) * scale + bias
        pr = jax.nn.softmax(sc, axis=-1).astype(vg.dtype)
        o = jnp.einsum('bhjqk,bjkhd->bjqhd', pr, vg)
        return o.reshape(b, GRID_W, h, d)

    o = lax.map(row_step, (q_rows, jnp.asarray(key_idx), jnp.asarray(row_off)))
    return o.transpose(1, 0, 2, 3, 4).reshape(b, s, h * d)


def hier_moe(x, w_rg, w_re, w1, w3, w2):
    b, s, dm = x.shape
    t = b * s
    xf = x.reshape(t, dm)
    g_prob = jax.nn.softmax((xf @ w_rg).astype(jnp.float32), axis=-1)
    g_gate, g_idx = lax.top_k(g_prob, 1)
    e_logits = (xf @ w_re).astype(jnp.float32).reshape(t, N_GROUPS, EXPERTS_PER_GROUP)
    e_logits = jnp.take_along_axis(e_logits, g_idx[:, :, None], axis=1)[:, 0]
    e_w, e_i = lax.top_k(jax.nn.softmax(e_logits, axis=-1), TOP_K)
    weights = g_gate * (e_w / jnp.sum(e_w, axis=-1, keepdims=True))
    experts = (g_idx * EXPERTS_PER_GROUP + e_i).astype(jnp.int32)
    n = t * TOP_K
    slot_e = experts.reshape(n)
    slot_tok = jnp.repeat(jnp.arange(t, dtype=jnp.int32), TOP_K)
    slot_w = weights.reshape(n)
    order = jnp.argsort(slot_e)
    se = slot_e[order]
    counts = jnp.zeros((N_EXPERTS,), jnp.int32).at[slot_e].add(1)
    starts = jnp.cumsum(counts) - counts
    padded = (counts + MOE_BLOCK - 1) // MOE_BLOCK * MOE_BLOCK
    pad_end = jnp.cumsum(padded)
    pad_start = pad_end - padded
    dest = pad_start[se] + jnp.arange(n, dtype=jnp.int32) - starts[se]
    n_blocks = n // MOE_BLOCK + N_EXPERTS
    p_len = n_blocks * MOE_BLOCK
    pad_tok = jnp.full((p_len,), t, jnp.int32).at[dest].set(slot_tok[order])
    pad_w = jnp.zeros((p_len,), jnp.float32).at[dest].set(slot_w[order])
    block_e = jnp.minimum(jnp.searchsorted(pad_end, jnp.arange(n_blocks, dtype=jnp.int32) * MOE_BLOCK,
                                           side='right'), N_EXPERTS - 1).astype(jnp.int32)
    xf_pad = jnp.concatenate([xf, jnp.zeros((1, dm), xf.dtype)], axis=0)
    xs = xf_pad[pad_tok].reshape(n_blocks, MOE_BLOCK, dm)

    def expert_block(args):
        xb, e = args
        hdn = jax.nn.silu(xb @ w1[e]) * (xb @ w3[e])
        return hdn @ w2[e]

    ys = lax.map(expert_block, (xs, block_e)).reshape(p_len, dm)
    out = jnp.zeros((t + 1, dm), ys.dtype).at[pad_tok].add(ys * pad_w[:, None].astype(ys.dtype))
    return out[:t].reshape(b, s, dm).astype(x.dtype)


def setup_inputs(seed: int = 0) -> dict:
    key = jax.random.key(seed)
    ks = jax.random.split(key, 25)
    f = jnp.float32

    def nrm(k, shape, scale):
        return jax.random.normal(k, shape, f) * scale

    return {
        'x': nrm(ks[0], (BATCH, SEQ, D_MODEL), 1.0),
        'p': nrm(ks[1], (DEPTH, BATCH, SEQ, D_PLE), 1.0),
        'g_mix': 1.0 + nrm(ks[2], (DEPTH, D_MODEL), 0.02),
        'w_in': nrm(ks[3], (DEPTH, D_MODEL, D_IN), D_MODEL ** -0.5),
        'g_qa': 1.0 + nrm(ks[4], (DEPTH, DA_QK), 0.02),
        'g_ka': 1.0 + nrm(ks[5], (DEPTH, DA_QK), 0.02),
        'lam_q1': nrm(ks[6], (DEPTH, DA_QK), 0.1),
        'lam_k1': nrm(ks[7], (DEPTH, DA_QK), 0.1),
        'lam_q2': nrm(ks[8], (DEPTH, DA_QK), 0.1),
        'lam_k2': nrm(ks[9], (DEPTH, DA_QK), 0.1),
        'g_sub': 1.0 + nrm(ks[10], (DEPTH, DA_V), 0.02),
        'g_qb': 1.0 + nrm(ks[11], (DEPTH, DB_H), 0.02),
        'g_kb': 1.0 + nrm(ks[12], (DEPTH, DB_H), 0.02),
        'rpb': nrm(ks[13], (DEPTH, HB, 2 * WIN_R - 1, 2 * WIN_C - 1), 0.1),
        'w_out': nrm(ks[14], (DEPTH, D_MIX, D_MODEL), D_MIX ** -0.5),
        'g_ffn': 1.0 + nrm(ks[15], (DEPTH, D_MODEL), 0.02),
        'w_rg': nrm(ks[16], (DEPTH, D_MODEL, N_GROUPS), D_MODEL ** -0.5),
        'w_re': nrm(ks[17], (DEPTH, D_MODEL, N_EXPERTS), D_MODEL ** -0.5),
        'w1': nrm(ks[18], (DEPTH, N_EXPERTS, D_MODEL, D_EXPERT), D_MODEL ** -0.5),
        'w3': nrm(ks[19], (DEPTH, N_EXPERTS, D_MODEL, D_EXPERT), D_MODEL ** -0.5),
        'w2': nrm(ks[20], (DEPTH, N_EXPERTS, D_EXPERT, D_MODEL), D_EXPERT ** -0.5),
        'g_plg': 1.0 + nrm(ks[21], (DEPTH, D_MODEL), 0.02),
        'w_plg': nrm(ks[22], (DEPTH, D_MODEL, D_MODEL), D_MODEL ** -0.5),
        'w_ple': nrm(ks[23], (DEPTH, D_PLE, D_MODEL), D_PLE ** -0.5),
        'g_ple': 1.0 + nrm(ks[24], (DEPTH, D_MODEL), 0.02),
    }


def reference(x, p, g_mix, w_in, g_qa, g_ka, lam_q1, lam_k1, lam_q2, lam_k2, g_sub, g_qb, g_kb,
              rpb, w_out, g_ffn, w_rg, w_re, w1, w3, w2, g_plg, w_plg, w_ple, g_ple):
    b, s, _ = x.shape
    h = x
    for i in range(DEPTH):
        lam_init = 0.8 - 0.6 * math.exp(-0.3 * i)
        a = rms_norm(h, g_mix[i])
        proj = a @ w_in[i]
        qa, ka, va, qb, kb, vb = jnp.split(proj, np.cumsum([DA, DA, DA, DB, DB])[:5].tolist(), axis=-1)
        oa = diff_attention(qa.reshape(b, s, HA, 2, DA_QK), ka.reshape(b, s, HA, 2, DA_QK),
                            va.reshape(b, s, HA, DA_V), g_qa[i], g_ka[i],
                            lam_q1[i], lam_k1[i], lam_q2[i], lam_k2[i], g_sub[i], lam_init)
        ob = neighbourhood_attention(qb.reshape(b, s, HB, DB_H), kb.reshape(b, s, HB, DB_H),
                                     vb.reshape(b, s, HB, DB_H), g_qb[i], g_kb[i], rpb[i])
        h = h + jnp.concatenate([oa, ob], axis=-1) @ w_out[i]
        h = h + hier_moe(rms_norm(h, g_ffn[i]), w_rg[i], w_re[i], w1[i], w3[i], w2[i])
        h = h + rms_norm(p[i] @ w_ple[i], g_ple[i]) * jax.nn.sigmoid(rms_norm(h, g_plg[i]) @ w_plg[i])
    return h
```

```python
import functools
import math

import numpy as np
import jax
import jax.numpy as jnp
from jax import lax
from jax.experimental import pallas as pl
from jax.experimental.pallas import tpu as pltpu

F32 = jnp.float32
BF16 = jnp.bfloat16

LANES = 128
HA = 4
DA_QK = 64
DA_V = 128
HB = 16
DB_H = 32
GRID_W = 64
WIN_R = 8
WIN_C = 16
ROPE_THETA = 10000.0
EPS = 1e-6
NEG = -1e30
N_GROUPS = 4
EXPERTS_PER_GROUP = 8
N_EXPERTS = N_GROUPS * EXPERTS_PER_GROUP
TOP_K = 2
MOE_BLOCK = 128
E_LANE0 = N_GROUPS

VMEM_LIMIT = 56 * 1024 * 1024


def _cparams(sem):
    return pltpu.CompilerParams(dimension_semantics=sem, vmem_limit_bytes=VMEM_LIMIT)


def _rms_rows(x):
    return x * lax.rsqrt(jnp.mean(x * x, axis=-1, keepdims=True) + EPS)


def _inproj_kernel(x_ref, gmix_ref, w_ref, bd64_ref, bd32_ref, gsec_ref, cos_ref, sin_ref, out_ref):
    tm = x_ref.shape[0]
    a = (_rms_rows(x_ref[...]) * gmix_ref[...]).astype(BF16)
    lane = lax.broadcasted_iota(jnp.int32, (tm, LANES), 1)
    first_half = (lane % 64) < 32
    cos = cos_ref[...]
    sin = sin_ref[...]
    for sec in range(6):
        y = jnp.dot(a, w_ref[:, sec * 512:(sec + 1) * 512], preferred_element_type=F32)
        if sec in (0, 1, 3, 4):
            bd = bd64_ref if sec < 2 else bd32_ref
            gi = {0: 0, 1: 1, 3: 2, 4: 3}[sec]
            ms = jnp.dot((y * y).astype(BF16), bd[...], preferred_element_type=F32)
            y = y * lax.rsqrt(ms + EPS) * gsec_ref[gi:gi + 1, :]
        for c in range(4):
            yc = y[:, c * LANES:(c + 1) * LANES]
            if sec < 2:
                rot = jnp.where(first_half, pltpu.roll(yc, 96, 1), pltpu.roll(yc, 32, 1))
                yc = yc * cos + rot * sin
            out_ref[:, sec * 512 + c * LANES: sec * 512 + (c + 1) * LANES] = yc.astype(BF16)


def _inproj(x2, g_mix, w_bf, bd64, bd32, gsec, cos_t, sin_t, seq, tm):
    t, d = x2.shape
    n = w_bf.shape[1]
    nsb = seq // tm
    const = lambda i: (0, 0)
    return pl.pallas_call(
        _inproj_kernel,
        grid=(t // tm,),
        in_specs=[
            pl.BlockSpec((tm, d), lambda i: (i, 0)),
            pl.BlockSpec((1, d), const),
            pl.BlockSpec((d, n), const),
            pl.BlockSpec((512, 512), const),
            pl.BlockSpec((512, 512), const),
            pl.BlockSpec((4, 512), const),
            pl.BlockSpec((tm, LANES), lambda i: (i % nsb, 0)),
            pl.BlockSpec((tm, LANES), lambda i: (i % nsb, 0)),
        ],
        out_specs=pl.BlockSpec((tm, n), lambda i: (i, 0)),
        out_shape=jax.ShapeDtypeStruct((t, n), BF16),
        compiler_params=_cparams(("parallel",)),
        name="inproj",
    )(x2, g_mix, w_bf, bd64, bd32, gsec, cos_t, sin_t)


def _diffattn_kernel(lam_ref, gsub_ref, q_ref, k_ref, v_ref, o_ref, qt_scr, vt_scr, *, seq, tq, tk, lam_init):
    s1 = jnp.sum(lam_ref[0:1, :] * lam_ref[1:2, :], axis=1, keepdims=True)
    s2 = jnp.sum(lam_ref[2:3, :] * lam_ref[3:4, :], axis=1, keepdims=True)
    lam = jnp.exp(s1) - jnp.exp(s2) + lam_init

    def tr_body(c, carry):
        off = pl.multiple_of(c * LANES, LANES)
        qt_scr[:, pl.ds(off, LANES)] = q_ref[0, pl.ds(off, LANES), :].astype(F32).T.astype(BF16)
        vt_scr[:, pl.ds(off, LANES)] = v_ref[0, pl.ds(off, LANES), :].astype(F32).T.astype(BF16)
        return carry
    lax.fori_loop(0, seq // LANES, tr_body, 0)

    row = lax.broadcasted_iota(jnp.int32, (LANES, tq), 0)

    def qblock(qi, carry):
        qoff = pl.multiple_of(qi * tq, tq)
        qt = qt_scr[:, pl.ds(qoff, tq)]
        zero = jnp.zeros_like(qt)
        qm = jnp.concatenate([jnp.where(row < DA_QK, qt, zero), jnp.where(row >= DA_QK, qt, zero)], axis=1)

        def kchunk(j, c):
            m, l, acc = c
            koff = pl.multiple_of(j * tk, tk)
            st = jnp.dot(k_ref[0, pl.ds(koff, tk), :], qm, preferred_element_type=F32)
            m_new = jnp.maximum(m, jnp.max(st, axis=0, keepdims=True))
            alpha = jnp.exp(m - m_new)
            p = jnp.exp(st - m_new)
            l = l * alpha + jnp.sum(p, axis=0, keepdims=True)
            pv = jnp.dot(vt_scr[:, pl.ds(koff, tk)], p.astype(BF16), preferred_element_type=F32)
            return m_new, l, acc * alpha + pv

        m0 = jnp.full((1, 2 * tq), -jnp.inf, F32)
        l0 = jnp.zeros((1, 2 * tq), F32)
        acc0 = jnp.zeros((DA_V, 2 * tq), F32)
        m, l, acc = lax.fori_loop(0, seq // tk, kchunk, (m0, l0, acc0))
        o = acc * (1.0 / l)
        ot = o[:, :tq] - lam * o[:, tq:]
        ot = ot * lax.rsqrt(jnp.mean(ot * ot, axis=0, keepdims=True) + EPS) * (1.0 - lam_init)
        o_ref[0, pl.ds(qoff, tq), :] = (ot.T * gsub_ref[...]).astype(BF16)
        return carry

    lax.fori_loop(0, seq // tq, qblock, 0)


def _diffattn(proj3, lam4, g_sub, lam_init, tq=128, tk=256):
    b, seq, _ = proj3.shape
    kern = functools.partial(_diffattn_kernel, seq=seq, tq=tq, tk=tk, lam_init=lam_init)
    return pl.pallas_call(
        kern,
        grid=(b, HA),
        in_specs=[
            pl.BlockSpec((4, DA_QK), lambda i, h: (0, 0)),
            pl.BlockSpec((1, DA_V), lambda i, h: (0, 0)),
            pl.BlockSpec((1, seq, LANES), lambda i, h: (i, 0, h)),
            pl.BlockSpec((1, seq, LANES), lambda i, h: (i, 0, HA + h)),
            pl.BlockSpec((1, seq, LANES), lambda i, h: (i, 0, 2 * HA + h)),
        ],
        out_specs=pl.BlockSpec((1, seq, LANES), lambda i, h: (i, 0, h)),
        out_shape=jax.ShapeDtypeStruct((b, seq, HA * DA_V), BF16),
        scratch_shapes=[pltpu.VMEM((LANES, seq), BF16), pltpu.VMEM((LANES, seq), BF16)],
        compiler_params=_cparams(("parallel", "parallel")),
        name="diffattn",
    )(lam4, g_sub, proj3, proj3, proj3)


def _natten_kernel(bias_ref, q_ref, k_ref, v_ref, o_ref, *, rows):
    kr = min(WIN_R, rows)
    nkeys = kr * GRID_W
    hpg = LANES // DB_H
    lane_head = lax.broadcasted_iota(jnp.int32, (GRID_W, LANES), 1) // DB_H

    def row_step(r, carry):
        rs = jnp.clip(r - kr // 2, 0, rows - kr)
        base = rs - r + (WIN_R - 1)
        qoff = pl.multiple_of(r * GRID_W, GRID_W)
        koff = pl.multiple_of(rs * GRID_W, GRID_W)
        qr = q_ref[0, pl.ds(qoff, GRID_W), :]
        zero = jnp.zeros_like(qr)
        qs = jnp.concatenate([jnp.where(lane_head == h, qr, zero) for h in range(hpg)], axis=0)
        kw = k_ref[0, pl.ds(koff, nkeys), :]
        vw = v_ref[0, pl.ds(koff, nkeys), :]
        s = lax.dot_general(qs, kw, (((1,), (1,)), ((), ())), preferred_element_type=F32)
        bias = jnp.concatenate([bias_ref[0, base + 2 * i] for i in range(kr // 2)], axis=1)
        s = s + bias
        m = jnp.max(s, axis=1, keepdims=True)
        p = jnp.exp(s - m)
        l = jnp.sum(p, axis=1, keepdims=True)
        o = jnp.dot(p.astype(BF16), vw, preferred_element_type=F32) * (1.0 / l)
        out = jnp.zeros((GRID_W, LANES), F32)
        for h in range(hpg):
            out = out + jnp.where(lane_head == h, o[h * GRID_W:(h + 1) * GRID_W, :], 0.0)
        o_ref[0, pl.ds(qoff, GRID_W), :] = out.astype(BF16)
        return carry

    lax.fori_loop(0, rows, row_step, 0)


def _natten(proj3, bias_tab):
    b, seq, _ = proj3.shape
    rows = seq // GRID_W
    ng = HB * DB_H // LANES
    nro = bias_tab.shape[1]
    kern = functools.partial(_natten_kernel, rows=rows)
    cb = 3 * HA
    return pl.pallas_call(
        kern,
        grid=(b, ng),
        in_specs=[
            pl.BlockSpec((1, nro, bias_tab.shape[2], LANES), lambda i, g: (g, 0, 0, 0)),
            pl.BlockSpec((1, seq, LANES), lambda i, g: (i, 0, cb + g)),
            pl.BlockSpec((1, seq, LANES), lambda i, g: (i, 0, cb + ng + g)),
            pl.BlockSpec((1, seq, LANES), lambda i, g: (i, 0, cb + 2 * ng + g)),
        ],
        out_specs=pl.BlockSpec((1, seq, LANES), lambda i, g: (i, 0, g)),
        out_shape=jax.ShapeDtypeStruct((b, seq, HB * DB_H), BF16),
        compiler_params=_cparams(("parallel", "parallel")),
        name="natten",
    )(bias_tab, proj3, proj3, proj3)


def _natten_bias_table(rpb):
    q = np.arange(GRID_W)[:, None]
    kc = np.arange(GRID_W)[None, :]
    cs = np.clip(q - WIN_C // 2, 0, GRID_W - WIN_C)
    valid = (kc >= cs) & (kc < cs + WIN_C)
    col_off = np.clip(kc - q + (WIN_C - 1), 0, 2 * WIN_C - 2)
    a = rpb.astype(F32)[:, :, col_off]
    a = jnp.where(valid, a, NEG)
    a2 = jnp.concatenate([a[:, :-1], a[:, 1:]], axis=-1)
    hpg = LANES // DB_H
    a2 = a2.reshape(HB // hpg, hpg, 2 * WIN_R - 2, GRID_W, LANES).transpose(0, 2, 1, 3, 4)
    return a2.reshape(HB // hpg, 2 * WIN_R - 2, hpg * GRID_W, LANES)


def _outproj_kernel(x_ref, oa_ref, ob_ref, wo_ref, gffn_ref, wr_ref, tri_ref,
                    h_ref, xn_ref, meta_ref, cnt_ref, run_scr):
    tm = x_ref.shape[0]
    da = oa_ref.shape[1]

    @pl.when(pl.program_id(0) == 0)
    def _():
        run_scr[...] = jnp.zeros_like(run_scr)

    attn = jnp.dot(oa_ref[...], wo_ref[:da, :], preferred_element_type=F32)
    attn = attn + jnp.dot(ob_ref[...], wo_ref[da:, :], preferred_element_type=F32)
    h = x_ref[...] + attn
    h_ref[...] = h
    xn = _rms_rows(h) * gffn_ref[...]
    xn_ref[...] = xn

    lg = jnp.dot(xn, wr_ref[...], precision=lax.Precision.HIGHEST, preferred_element_type=F32)
    lane = lax.broadcasted_iota(jnp.int32, (tm, LANES), 1).astype(F32)
    ninf = -jnp.inf
    is_g = lane < N_GROUPS
    gl = jnp.where(is_g, lg, ninf)
    gmax = jnp.max(gl, axis=1, keepdims=True)
    gidx = jnp.min(jnp.where(gl == gmax, lane, float(LANES)), axis=1, keepdims=True)
    gsum = jnp.sum(jnp.where(is_g, jnp.exp(gl - gmax), 0.0), axis=1, keepdims=True)
    g_gate = 1.0 / gsum
    lo = E_LANE0 + gidx * EXPERTS_PER_GROUP
    is_e = (lane >= lo) & (lane < lo + EXPERTS_PER_GROUP)
    el = jnp.where(is_e, lg, ninf)
    l1 = jnp.max(el, axis=1, keepdims=True)
    i1 = jnp.min(jnp.where(el == l1, lane, float(LANES)), axis=1, keepdims=True)
    el2 = jnp.where(lane == i1, ninf, el)
    l2 = jnp.max(el2, axis=1, keepdims=True)
    i2 = jnp.min(jnp.where(el2 == l2, lane, float(LANES)), axis=1, keepdims=True)
    r = jnp.exp(l2 - l1)
    w1 = g_gate / (1.0 + r)
    w2 = w1 * r

    sel1 = lane == i1
    sel2 = lane == i2
    oh = jnp.where(sel1 | sel2, 1.0, 0.0)
    before = jnp.dot(tri_ref[...], oh.astype(BF16), preferred_element_type=F32) + run_scr[...]
    rank1 = jnp.sum(jnp.where(sel1, before, 0.0), axis=1, keepdims=True)
    rank2 = jnp.sum(jnp.where(sel2, before, 0.0), axis=1, keepdims=True)
    run = run_scr[...] + jnp.sum(oh, axis=0, keepdims=True)
    run_scr[...] = run
    cnt_ref[...] = run

    meta = jnp.where(lane == 0, i1 - E_LANE0, 0.0)
    meta = jnp.where(lane == 1, i2 - E_LANE0, meta)
    meta = jnp.where(lane == 2, rank1, meta)
    meta = jnp.where(lane == 3, rank2, meta)
    meta = jnp.where(lane == 4, w1, meta)
    meta = jnp.where(lane == 5, w2, meta)
    meta_ref[...] = meta


def _outproj(x2, oa2, ob2, wo_bf, g_ffn, w_router, tri, tm):
    t, d = x2.shape
    da = oa2.shape[1]
    const = lambda i: (0, 0)
    row = lambda i: (i, 0)
    return pl.pallas_call(
        _outproj_kernel,
        grid=(t // tm,),
        in_specs=[
            pl.BlockSpec((tm, d), row),
            pl.BlockSpec((tm, da), row),
            pl.BlockSpec((tm, d - da), row),
            pl.BlockSpec((d, d), const),
            pl.BlockSpec((1, d), const),
            pl.BlockSpec((d, LANES), const),
            pl.BlockSpec((tm, tm), const),
        ],
        out_specs=[
            pl.BlockSpec((tm, d), row),
            pl.BlockSpec((tm, d), row),
            pl.BlockSpec((tm, LANES), row),
            pl.BlockSpec((1, LANES), const),
        ],
        out_shape=[
            jax.ShapeDtypeStruct((t, d), F32),
            jax.ShapeDtypeStruct((t, d), F32),
            jax.ShapeDtypeStruct((t, LANES), F32),
            jax.ShapeDtypeStruct((1, LANES), F32),
        ],
        scratch_shapes=[pltpu.VMEM((1, LANES), F32)],
        compiler_params=_cparams(("arbitrary",)),
        name="outproj_router",
    )(x2, oa2, ob2, wo_bf, g_ffn, w_router, tri)


def _dispatch_kernel(dest_ref, xn_ref, xs_in_ref, xs_ref, sem):
    del xs_in_ref
    tm = xn_ref.shape[0]

    def row_copy(i, slot):
        d = dest_ref[0, 0, TOP_K * i + slot]
        return pltpu.make_async_copy(xn_ref.at[pl.ds(i, 1), :], xs_ref.at[pl.ds(d, 1), :], sem)

    def issue(i, c):
        for slot in range(TOP_K):
            row_copy(i, slot).start()
        return c
    lax.fori_loop(0, tm, issue, 0)

    def drain(i, c):
        for slot in range(TOP_K):
            row_copy(i, slot).wait()
        return c
    lax.fori_loop(0, tm, drain, 0)


def _dispatch(xn2, dest3, xs_zero, tm):
    t, d = xn2.shape
    return pl.pallas_call(
        _dispatch_kernel,
        grid=(t // tm,),
        in_specs=[
            pl.BlockSpec((1, 1, TOP_K * tm), lambda i: (i, 0, 0), memory_space=pltpu.SMEM),
            pl.BlockSpec((tm, d), lambda i: (i, 0)),
            pl.BlockSpec(memory_space=pl.ANY),
        ],
        out_specs=pl.BlockSpec(memory_space=pl.ANY),
        out_shape=jax.ShapeDtypeStruct(xs_zero.shape, xs_zero.dtype),
        scratch_shapes=[pltpu.SemaphoreType.DMA(())],
        input_output_aliases={2: 0},
        compiler_params=_cparams(("arbitrary",)),
        name="dispatch",
    )(dest3, xn2, xs_zero)


def _experts_kernel(be_ref, nu_ref, xs_ref, w1_ref, w3_ref, w2_ref, ys_ref, w1_scr, w3_scr, w2_scr):
    i = pl.program_id(0)

    @pl.when(i < nu_ref[0])
    def _():
        prev = be_ref[jnp.maximum(i - 1, 0)]
        new_expert = (i == 0) | (be_ref[i] != prev)

        @pl.when(new_expert)
        def _():
            w1_scr[...] = w1_ref[0].astype(BF16)
            w3_scr[...] = w3_ref[0].astype(BF16)
            w2_scr[...] = w2_ref[0].astype(BF16)

        x = xs_ref[...].astype(BF16)
        a = jnp.dot(x, w1_scr[...], preferred_element_type=F32)
        b = jnp.dot(x, w3_scr[...], preferred_element_type=F32)
        hdn = a * (1.0 / (1.0 + jnp.exp(-a))) * b
        ys_ref[...] = jnp.dot(hdn.astype(BF16), w2_scr[...], preferred_element_type=F32)

    @pl.when(i >= nu_ref[0])
    def _():
        ys_ref[...] = jnp.zeros_like(ys_ref)


def _experts(xs, block_e, n_used, w1, w3, w2):
    p_len, d = xs.shape
    n_blocks = p_len // MOE_BLOCK
    de = w1.shape[2]
    blk = lambda i, be, nu: (jnp.minimum(i, nu[0] - 1), 0)
    wsel = lambda i, be, nu: (be[i], 0, 0)
    grid_spec = pltpu.PrefetchScalarGridSpec(
        num_scalar_prefetch=2,
        grid=(n_blocks,),
        in_specs=[
            pl.BlockSpec((MOE_BLOCK, d), blk),
            pl.BlockSpec((1, d, de), wsel),
            pl.BlockSpec((1, d, de), wsel),
            pl.BlockSpec((1, de, d), wsel),
        ],
        out_specs=pl.BlockSpec((MOE_BLOCK, d), lambda i, be, nu: (i, 0)),
        scratch_shapes=[pltpu.VMEM((d, de), BF16), pltpu.VMEM((d, de), BF16), pltpu.VMEM((de, d), BF16)],
    )
    return pl.pallas_call(
        _experts_kernel,
        grid_spec=grid_spec,
        out_shape=jax.ShapeDtypeStruct((p_len, d), F32),
        compiler_params=_cparams(("arbitrary",)),
        name="experts",
    )(block_e, n_used, xs, w1, w3, w2)


def _final_kernel(dest_ref, h_ref, meta_ref, p_ref, wple_ref, gple_ref, gplg_ref, wplg_ref, ys_ref,
                  out_ref, gbuf, sem):
    tm = h_ref.shape[0]

    def row_copy(i, slot):
        d = dest_ref[0, 0, TOP_K * i + slot]
        return pltpu.make_async_copy(ys_ref.at[pl.ds(d, 1), :], gbuf.at[slot, pl.ds(i, 1), :], sem)

    def issue(i, c):
        for slot in range(TOP_K):
            row_copy(i, slot).start()
        return c
    lax.fori_loop(0, tm, issue, 0)

    emb = jnp.dot(p_ref[...].astype(BF16), wple_ref[...], preferred_element_type=F32)
    emb = _rms_rows(emb) * gple_ref[...]

    def drain(i, c):
        for slot in range(TOP_K):
            row_copy(i, slot).wait()
        return c
    lax.fori_loop(0, tm, drain, 0)

    meta = meta_ref[...]
    h = h_ref[...] + meta[:, 4:5] * gbuf[0] + meta[:, 5:6] * gbuf[1]
    hn = (_rms_rows(h) * gplg_ref[...]).astype(BF16)
    z = jnp.dot(hn, wplg_ref[...], preferred_element_type=F32)
    out_ref[...] = h + emb * (1.0 / (1.0 + jnp.exp(-z)))


def _final(dest3, h2, meta, p2, wple_bf, g_ple, g_plg, wplg_bf, ys, tm):
    t, d = h2.shape
    dp = p2.shape[1]
    const = lambda i: (0, 0)
    row = lambda i: (i, 0)
    return pl.pallas_call(
        _final_kernel,
        grid=(t // tm,),
        in_specs=[
            pl.BlockSpec((1, 1, TOP_K * tm), lambda i: (i, 0, 0), memory_space=pltpu.SMEM),
            pl.BlockSpec((tm, d), row),
            pl.BlockSpec((tm, LANES), row),
            pl.BlockSpec((tm, dp), row),
            pl.BlockSpec((dp, d), const),
            pl.BlockSpec((1, d), const),
            pl.BlockSpec((1, d), const),
            pl.BlockSpec((d, d), const),
            pl.BlockSpec(memory_space=pl.ANY),
        ],
        out_specs=pl.BlockSpec((tm, d), row),
        out_shape=jax.ShapeDtypeStruct((t, d), F32),
        scratch_shapes=[pltpu.VMEM((TOP_K, tm, d), F32), pltpu.SemaphoreType.DMA(())],
        compiler_params=_cparams(("arbitrary",)),
        name="combine_ple",
    )(dest3, h2, meta, p2, wple_bf, g_ple, g_plg, wplg_bf, ys)


def _block_diag_mean(seg):
    m = np.kron(np.eye(512 // seg), np.full((seg, seg), 1.0 / seg))
    return jnp.asarray(m, BF16)


def _rope_tables(seq):
    inv = ROPE_THETA ** (-jnp.arange(0, DA_QK, 2, dtype=F32) / DA_QK)
    ang = jnp.arange(seq, dtype=F32)[:, None] * inv[None, :]
    cos, sin = jnp.cos(ang), jnp.sin(ang)
    cos_t = jnp.tile(cos, (1, LANES // cos.shape[1]))
    sin_t = jnp.tile(jnp.concatenate([-sin, sin], axis=1), (1, LANES // (2 * sin.shape[1])))
    return cos_t, sin_t


def _tile_for(n, pref):
    while n % pref:
        pref //= 2
    return pref


def kernel(x, p, g_mix, w_in, g_qa, g_ka, lam_q1, lam_k1, lam_q2, lam_k2, g_sub, g_qb, g_kb, rpb, w_out,
           g_ffn, w_rg, w_re, w1, w3, w2, g_plg, w_plg, w_ple, g_ple):
    b, s, d = x.shape
    t = b * s
    depth = w_in.shape[0]
    tm = _tile_for(s, 512)
    tmd = _tile_for(t, 512)
    tmf = _tile_for(t, 256)
    n_slots = t * TOP_K
    n_blocks = n_slots // MOE_BLOCK + N_EXPERTS
    p_len = n_blocks * MOE_BLOCK

    bd64 = _block_diag_mean(DA_QK)
    bd32 = _block_diag_mean(DB_H)
    cos_t, sin_t = _rope_tables(s)
    tri = jnp.asarray(np.tril(np.ones((tm, tm)), -1), BF16)

    h = x.reshape(t, d)
    for i in range(depth):
        lam_init = 0.8 - 0.6 * math.exp(-0.3 * i)
        gsec = jnp.stack([
            jnp.tile(g_qa[i], 512 // DA_QK) * DA_QK ** -0.5,
            jnp.tile(g_ka[i], 512 // DA_QK),
            jnp.tile(g_qb[i], 512 // DB_H) * DB_H ** -0.5,
            jnp.tile(g_kb[i], 512 // DB_H),
        ]).astype(F32)
        proj = _inproj(h, g_mix[i][None, :], w_in[i].astype(BF16), bd64, bd32, gsec, cos_t, sin_t, s, tm)
        proj3 = proj.reshape(b, s, proj.shape[1])

        lam4 = jnp.stack([lam_q1[i], lam_k1[i], lam_q2[i], lam_k2[i]]).astype(F32)
        oa = _diffattn(proj3, lam4, g_sub[i][None, :].astype(F32), lam_init)
        ob = _natten(proj3, _natten_bias_table(rpb[i]))

        w_router = jnp.zeros((d, LANES), F32)
        w_router = w_router.at[:, :N_GROUPS].set(w_rg[i]).at[:, E_LANE0:E_LANE0 + N_EXPERTS].set(w_re[i])
        h1, xn, meta, cnt = _outproj(h, oa.reshape(t, -1), ob.reshape(t, -1), w_out[i].astype(BF16),
                                     g_ffn[i][None, :], w_router, tri, tm)

        counts = cnt[0, E_LANE0:E_LANE0 + N_EXPERTS].astype(jnp.int32)
        padded = (counts + MOE_BLOCK - 1) // MOE_BLOCK * MOE_BLOCK
        pad_end = jnp.cumsum(padded)
        pad_start = pad_end - padded
        e_idx = meta[:, 0:TOP_K].astype(jnp.int32)
        dest = pad_start[e_idx] + meta[:, TOP_K:2 * TOP_K].astype(jnp.int32)
        block_e = jnp.minimum(jnp.searchsorted(pad_end, jnp.arange(n_blocks, dtype=jnp.int32) * MOE_BLOCK,
                                               side='right'), N_EXPERTS - 1).astype(jnp.int32)
        n_used = (pad_end[-1:] // MOE_BLOCK).astype(jnp.int32)

        xs = _dispatch(xn, dest.reshape(t // tmd, 1, TOP_K * tmd), jnp.zeros((p_len, d), F32), tmd)
        ys = _experts(xs, block_e, n_used, w1[i], w3[i], w2[i])
        h = _final(dest.reshape(t // tmf, 1, TOP_K * tmf), h1, meta, p[i].reshape(t, -1),
                   w_ple[i].astype(BF16), g_ple[i][None, :], g_plg[i][None, :], w_plg[i].astype(BF16), ys, tmf)
    return h.reshape(b, s, d)
```

```python
import functools
import math

import numpy as np
import jax
import jax.numpy as jnp
from jax import lax
from jax.experimental import pallas as pl
from jax.experimental.pallas import tpu as pltpu

F32 = jnp.float32
BF16 = jnp.bfloat16

LANES = 128
HA = 4
DA_QK = 64
DA_V = 128
HB = 16
DB_H = 32
GRID_W = 64
WIN_R = 8
WIN_C = 16
ROPE_THETA = 10000.0
EPS = 1e-6
NEG = -1e30
N_GROUPS = 4
EXPERTS_PER_GROUP = 8
N_EXPERTS = N_GROUPS * EXPERTS_PER_GROUP
TOP_K = 2
MOE_BLOCK = 128
E_LANE0 = N_GROUPS

VMEM_LIMIT = 56 * 1024 * 1024


def _cparams(sem):
    return pltpu.CompilerParams(dimension_semantics=sem, vmem_limit_bytes=VMEM_LIMIT)


def _rms_rows(x):
    return x * lax.rsqrt(jnp.mean(x * x, axis=-1, keepdims=True) + EPS)


def _inproj_kernel(x_ref, gmix_ref, w_ref, bd64_ref, bd32_ref, gsec_ref, cos_ref, sin_ref, out_ref):
    tm = x_ref.shape[0]
    a = (_rms_rows(x_ref[...]) * gmix_ref[...]).astype(BF16)
    lane = lax.broadcasted_iota(jnp.int32, (tm, LANES), 1)
    first_half = (lane % 64) < 32
    cos = cos_ref[...]
    sin = sin_ref[...]
    for sec in range(6):
        y = jnp.dot(a, w_ref[:, sec * 512:(sec + 1) * 512], preferred_element_type=F32)
        if sec in (0, 1, 3, 4):
            bd = bd64_ref if sec < 2 else bd32_ref
            gi = {0: 0, 1: 1, 3: 2, 4: 3}[sec]
            ms = jnp.dot((y * y).astype(BF16), bd[...], preferred_element_type=F32)
            y = y * lax.rsqrt(ms + EPS) * gsec_ref[gi:gi + 1, :]
        for c in range(4):
            yc = y[:, c * LANES:(c + 1) * LANES]
            if sec < 2:
                rot = jnp.where(first_half, pltpu.roll(yc, 96, 1), pltpu.roll(yc, 32, 1))
                yc = yc * cos + rot * sin
            out_ref[:, sec * 512 + c * LANES: sec * 512 + (c + 1) * LANES] = yc.astype(BF16)


def _inproj(x2, g_mix, w_bf, bd64, bd32, gsec, cos_t, sin_t, seq, tm):
    t, d = x2.shape
    n = w_bf.shape[1]
    nsb = seq // tm
    const = lambda i: (0, 0)
    return pl.pallas_call(
        _inproj_kernel,
        grid=(t // tm,),
        in_specs=[
            pl.BlockSpec((tm, d), lambda i: (i, 0)),
            pl.BlockSpec((1, d), const),
            pl.BlockSpec((d, n), const),
            pl.BlockSpec((512, 512), const),
            pl.BlockSpec((512, 512), const),
            pl.BlockSpec((4, 512), const),
            pl.BlockSpec((tm, LANES), lambda i: (i % nsb, 0)),
            pl.BlockSpec((tm, LANES), lambda i: (i % nsb, 0)),
        ],
        out_specs=pl.BlockSpec((tm, n), lambda i: (i, 0)),
        out_shape=jax.ShapeDtypeStruct((t, n), BF16),
        compiler_params=_cparams(("parallel",)),
        name="inproj",
    )(x2, g_mix, w_bf, bd64, bd32, gsec, cos_t, sin_t)


def _diffattn_kernel(lam_ref, gsub_ref, q_ref, k_ref, v_ref, o_ref, qt_scr, vt_scr, sa_scr, sb_scr, *, seq, tq, tk,
                     lam_init):
    nq = seq // tq
    nk = seq // tk
    s1 = jnp.sum(lam_ref[0:1, :] * lam_ref[1:2, :], axis=1, keepdims=True)
    s2 = jnp.sum(lam_ref[2:3, :] * lam_ref[3:4, :], axis=1, keepdims=True)
    lam = jnp.exp(s1) - jnp.exp(s2) + lam_init

    def tr_body(c, carry):
        off = pl.multiple_of(c * LANES, LANES)
        qt_scr[:, pl.ds(off, LANES)] = q_ref[0, pl.ds(off, LANES), :].astype(F32).T.astype(BF16)
        vt_scr[:, pl.ds(off, LANES)] = v_ref[0, pl.ds(off, LANES), :].astype(F32).T.astype(BF16)
        return carry
    lax.fori_loop(0, seq // LANES, tr_body, 0)

    row = lax.broadcasted_iota(jnp.int32, (LANES, tq), 0)

    def scores(qi, s_scr):
        qoff = pl.multiple_of(qi * tq, tq)
        qt = qt_scr[:, pl.ds(qoff, tq)]
        zero = jnp.zeros_like(qt)
        qm = jnp.concatenate([jnp.where(row < DA_QK, qt, zero), jnp.where(row >= DA_QK, qt, zero)], axis=1)
        mx = None
        for j in range(nk):
            st = jnp.dot(k_ref[0, j * tk:(j + 1) * tk, :], qm, preferred_element_type=F32)
            s_scr[j * tk:(j + 1) * tk, :] = st
            cm = jnp.max(st, axis=0, keepdims=True)
            mx = cm if mx is None else jnp.maximum(mx, cm)
        return mx

    def finish(qi, s_scr, m):
        l = jnp.zeros((1, 2 * tq), F32)
        acc = jnp.zeros((DA_V, 2 * tq), F32)
        for j in range(nk):
            p = jnp.exp2(s_scr[j * tk:(j + 1) * tk, :] - m)
            l = l + jnp.sum(p, axis=0, keepdims=True)
            acc = acc + jnp.dot(vt_scr[:, j * tk:(j + 1) * tk], p.astype(BF16), preferred_element_type=F32)
        o = acc * (1.0 / l)
        ot = o[:, :tq] - lam * o[:, tq:]
        ot = ot * lax.rsqrt(jnp.mean(ot * ot, axis=0, keepdims=True) + EPS) * (1.0 - lam_init)
        qoff = pl.multiple_of(qi * tq, tq)
        o_ref[0, pl.ds(qoff, tq), :] = (ot.T * gsub_ref[...]).astype(BF16)

    def body(i, m_a):
        m_b = scores(2 * i + 1, sb_scr)
        finish(2 * i, sa_scr, m_a)
        m_a_next = scores(jnp.minimum(2 * i + 2, nq - 1), sa_scr)
        finish(2 * i + 1, sb_scr, m_b)
        return m_a_next

    lax.fori_loop(0, nq // 2, body, scores(0, sa_scr))


def _diffattn(proj3, lam4, g_sub, lam_init, tq=128, tk=256):
    b, seq, _ = proj3.shape
    kern = functools.partial(_diffattn_kernel, seq=seq, tq=tq, tk=tk, lam_init=lam_init)
    return pl.pallas_call(
        kern,
        grid=(b, HA),
        in_specs=[
            pl.BlockSpec((4, DA_QK), lambda i, h: (0, 0)),
            pl.BlockSpec((1, DA_V), lambda i, h: (0, 0)),
            pl.BlockSpec((1, seq, LANES), lambda i, h: (i, 0, h)),
            pl.BlockSpec((1, seq, LANES), lambda i, h: (i, 0, HA + h)),
            pl.BlockSpec((1, seq, LANES), lambda i, h: (i, 0, 2 * HA + h)),
        ],
        out_specs=pl.BlockSpec((1, seq, LANES), lambda i, h: (i, 0, h)),
        out_shape=jax.ShapeDtypeStruct((b, seq, HA * DA_V), BF16),
        scratch_shapes=[pltpu.VMEM((LANES, seq), BF16), pltpu.VMEM((LANES, seq), BF16),
                        pltpu.VMEM((seq, 2 * tq), F32), pltpu.VMEM((seq, 2 * tq), F32)],
        compiler_params=_cparams(("parallel", "parallel")),
        name="diffattn",
    )(lam4, g_sub, proj3, proj3, proj3)


def _natten_kernel(bias_ref, q_ref, k_ref, v_ref, o_ref, *, rows):
    kr = min(WIN_R, rows)
    nkeys = kr * GRID_W
    hpg = LANES // DB_H
    lane_head = lax.broadcasted_iota(jnp.int32, (GRID_W, LANES), 1) // DB_H

    def row_step(r, carry):
        rs = jnp.clip(r - kr // 2, 0, rows - kr)
        base = rs - r + (WIN_R - 1)
        qoff = pl.multiple_of(r * GRID_W, GRID_W)
        koff = pl.multiple_of(rs * GRID_W, GRID_W)
        qr = q_ref[0, pl.ds(qoff, GRID_W), :]
        zero = jnp.zeros_like(qr)
        qs = jnp.concatenate([jnp.where(lane_head == h, qr, zero) for h in range(hpg)], axis=0)
        kw = k_ref[0, pl.ds(koff, nkeys), :]
        vw = v_ref[0, pl.ds(koff, nkeys), :]
        s = lax.dot_general(qs, kw, (((1,), (1,)), ((), ())), preferred_element_type=F32)
        bias = jnp.concatenate([bias_ref[0, base + 2 * i] for i in range(kr // 2)], axis=1)
        s = s + bias
        m = jnp.max(s, axis=1, keepdims=True)
        p = jnp.exp(s - m)
        l = jnp.sum(p, axis=1, keepdims=True)
        o = jnp.dot(p.astype(BF16), vw, preferred_element_type=F32) * (1.0 / l)
        out = jnp.zeros((GRID_W, LANES), F32)
        for h in range(hpg):
            out = out + jnp.where(lane_head == h, o[h * GRID_W:(h + 1) * GRID_W, :], 0.0)
        o_ref[0, pl.ds(qoff, GRID_W), :] = out.astype(BF16)
        return carry

    lax.fori_loop(0, rows, row_step, 0, unroll=4)


def _natten(proj3, bias_tab):
    b, seq, _ = proj3.shape
    rows = seq // GRID_W
    ng = HB * DB_H // LANES
    nro = bias_tab.shape[1]
    kern = functools.partial(_natten_kernel, rows=rows)
    cb = 3 * HA
    return pl.pallas_call(
        kern,
        grid=(b, ng),
        in_specs=[
            pl.BlockSpec((1, nro, bias_tab.shape[2], LANES), lambda i, g: (g, 0, 0, 0)),
            pl.BlockSpec((1, seq, LANES), lambda i, g: (i, 0, cb + g)),
            pl.BlockSpec((1, seq, LANES), lambda i, g: (i, 0, cb + ng + g)),
            pl.BlockSpec((1, seq, LANES), lambda i, g: (i, 0, cb + 2 * ng + g)),
        ],
        out_specs=pl.BlockSpec((1, seq, LANES), lambda i, g: (i, 0, g)),
        out_shape=jax.ShapeDtypeStruct((b, seq, HB * DB_H), BF16),
        compiler_params=_cparams(("parallel", "parallel")),
        name="natten",
    )(bias_tab, proj3, proj3, proj3)


def _natten_bias_table(rpb):
    q = np.arange(GRID_W)[:, None]
    kc = np.arange(GRID_W)[None, :]
    cs = np.clip(q - WIN_C // 2, 0, GRID_W - WIN_C)
    valid = (kc >= cs) & (kc < cs + WIN_C)
    col_off = np.clip(kc - q + (WIN_C - 1), 0, 2 * WIN_C - 2)
    ncol = 2 * WIN_C - 1
    nro = 2 * WIN_R - 2
    pick = np.zeros((2, ncol, GRID_W, 2 * GRID_W), np.float32)
    for j in range(2):
        pick[j, :, :, j * GRID_W:(j + 1) * GRID_W] = (np.arange(ncol)[:, None, None] == col_off[None]) & valid[None]
    negmask = np.tile(np.where(valid, 0.0, NEG).astype(np.float32), (1, 2))
    hpg = LANES // DB_H
    r4 = rpb.astype(F32).reshape(HB // hpg, hpg, 2 * WIN_R - 1, ncol)
    rp = jnp.stack([r4[:, :, :nro], r4[:, :, 1:]], axis=0)
    tab = jnp.einsum('jghrc,jcqk->grhqk', rp, jnp.asarray(pick), precision=lax.Precision.HIGHEST) + negmask
    return tab.reshape(HB // hpg, nro, hpg * GRID_W, LANES)


def _outproj_kernel(x_ref, oa_ref, ob_ref, wo_ref, gffn_ref, wr_ref, tri_ref,
                    h_ref, xn_ref, meta_ref, cnt_ref, run_scr):
    tm = x_ref.shape[0]
    da = oa_ref.shape[1]

    @pl.when(pl.program_id(0) == 0)
    def _():
        run_scr[...] = jnp.zeros_like(run_scr)

    attn = jnp.dot(oa_ref[...], wo_ref[:da, :], preferred_element_type=F32)
    attn = attn + jnp.dot(ob_ref[...], wo_ref[da:, :], preferred_element_type=F32)
    h = x_ref[...] + attn
    h_ref[...] = h
    xn = _rms_rows(h) * gffn_ref[...]
    xn_ref[...] = xn

    lg = jnp.dot(xn, wr_ref[...], precision=lax.Precision.HIGHEST, preferred_element_type=F32)
    lane = lax.broadcasted_iota(jnp.int32, (tm, LANES), 1).astype(F32)
    ninf = -jnp.inf
    is_g = lane < N_GROUPS
    gl = jnp.where(is_g, lg, ninf)
    gmax = jnp.max(gl, axis=1, keepdims=True)
    gidx = jnp.min(jnp.where(gl == gmax, lane, float(LANES)), axis=1, keepdims=True)
    gsum = jnp.sum(jnp.where(is_g, jnp.exp(gl - gmax), 0.0), axis=1, keepdims=True)
    g_gate = 1.0 / gsum
    lo = E_LANE0 + gidx * EXPERTS_PER_GROUP
    is_e = (lane >= lo) & (lane < lo + EXPERTS_PER_GROUP)
    el = jnp.where(is_e, lg, ninf)
    l1 = jnp.max(el, axis=1, keepdims=True)
    i1 = jnp.min(jnp.where(el == l1, lane, float(LANES)), axis=1, keepdims=True)
    el2 = jnp.where(lane == i1, ninf, el)
    l2 = jnp.max(el2, axis=1, keepdims=True)
    i2 = jnp.min(jnp.where(el2 == l2, lane, float(LANES)), axis=1, keepdims=True)
    r = jnp.exp(l2 - l1)
    w1 = g_gate / (1.0 + r)
    w2 = w1 * r

    sel1 = lane == i1
    sel2 = lane == i2
    oh = jnp.where(sel1 | sel2, 1.0, 0.0)
    before = jnp.dot(tri_ref[...], oh.astype(BF16), preferred_element_type=F32) + run_scr[...]
    rank1 = jnp.sum(jnp.where(sel1, before, 0.0), axis=1, keepdims=True)
    rank2 = jnp.sum(jnp.where(sel2, before, 0.0), axis=1, keepdims=True)
    run = run_scr[...] + jnp.sum(oh, axis=0, keepdims=True)
    run_scr[...] = run
    cnt_ref[...] = run

    meta = jnp.where(lane == 0, i1 - E_LANE0, 0.0)
    meta = jnp.where(lane == 1, i2 - E_LANE0, meta)
    meta = jnp.where(lane == 2, rank1, meta)
    meta = jnp.where(lane == 3, rank2, meta)
    meta = jnp.where(lane == 4, w1, meta)
    meta = jnp.where(lane == 5, w2, meta)
    meta_ref[...] = meta


def _outproj(x2, oa2, ob2, wo_bf, g_ffn, w_router, tri, tm):
    t, d = x2.shape
    da = oa2.shape[1]
    const = lambda i: (0, 0)
    row = lambda i: (i, 0)
    return pl.pallas_call(
        _outproj_kernel,
        grid=(t // tm,),
        in_specs=[
            pl.BlockSpec((tm, d), row),
            pl.BlockSpec((tm, da), row),
            pl.BlockSpec((tm, d - da), row),
            pl.BlockSpec((d, d), const),
            pl.BlockSpec((1, d), const),
            pl.BlockSpec((d, LANES), const),
            pl.BlockSpec((tm, tm), const),
        ],
        out_specs=[
            pl.BlockSpec((tm, d), row),
            pl.BlockSpec((tm, d), row),
            pl.BlockSpec((tm, LANES), row),
            pl.BlockSpec((1, LANES), const),
        ],
        out_shape=[
            jax.ShapeDtypeStruct((t, d), F32),
            jax.ShapeDtypeStruct((t, d), F32),
            jax.ShapeDtypeStruct((t, LANES), F32),
            jax.ShapeDtypeStruct((1, LANES), F32),
        ],
        scratch_shapes=[pltpu.VMEM((1, LANES), F32)],
        compiler_params=_cparams(("arbitrary",)),
        name="outproj_router",
    )(x2, oa2, ob2, wo_bf, g_ffn, w_router, tri)


def _dispatch_kernel(dest_ref, xn_ref, xs_in_ref, xs_ref, sem):
    del xs_in_ref
    tm = xn_ref.shape[0]

    def row_copy(i, slot):
        d = dest_ref[0, 0, slot * tm + i]
        return pltpu.make_async_copy(xn_ref.at[pl.ds(i, 1), :], xs_ref.at[pl.ds(d, 1), :], sem)

    def issue(i, c):
        for slot in range(TOP_K):
            row_copy(i, slot).start()
        return c
    lax.fori_loop(0, tm, issue, 0)

    def drain(i, c):
        for slot in range(TOP_K):
            row_copy(i, slot).wait()
        return c
    lax.fori_loop(0, tm, drain, 0)


def _dispatch(xn2, dest3, xs_zero, tm):
    t, d = xn2.shape
    return pl.pallas_call(
        _dispatch_kernel,
        grid=(t // tm,),
        in_specs=[
            pl.BlockSpec((1, 1, TOP_K * tm), lambda i: (i, 0, 0), memory_space=pltpu.SMEM),
            pl.BlockSpec((tm, d), lambda i: (i, 0)),
            pl.BlockSpec(memory_space=pl.ANY),
        ],
        out_specs=pl.BlockSpec(memory_space=pl.ANY),
        out_shape=jax.ShapeDtypeStruct(xs_zero.shape, xs_zero.dtype),
        scratch_shapes=[pltpu.SemaphoreType.DMA(())],
        input_output_aliases={2: 0},
        compiler_params=_cparams(("arbitrary",)),
        name="dispatch",
    )(dest3, xn2, xs_zero)


def _experts_kernel(be_ref, nu_ref, xs_ref, w1_ref, w3_ref, w2_ref, ys_ref, w1_scr, w3_scr, w2_scr):
    i = pl.program_id(0)

    @pl.when(i < nu_ref[0])
    def _():
        prev = be_ref[jnp.maximum(i - 1, 0)]
        new_expert = (i == 0) | (be_ref[i] != prev)

        @pl.when(new_expert)
        def _():
            w1_scr[...] = w1_ref[0].astype(BF16)
            w3_scr[...] = w3_ref[0].astype(BF16)
            w2_scr[...] = w2_ref[0].astype(BF16)

        x = xs_ref[...].astype(BF16)
        a = jnp.dot(x, w1_scr[...], preferred_element_type=F32)
        b = jnp.dot(x, w3_scr[...], preferred_element_type=F32)
        hdn = a * (1.0 / (1.0 + jnp.exp(-a))) * b
        ys_ref[...] = jnp.dot(hdn.astype(BF16), w2_scr[...], preferred_element_type=F32)

    @pl.when(i >= nu_ref[0])
    def _():
        ys_ref[...] = jnp.zeros_like(ys_ref)


def _experts(xs, block_e, n_used, w1, w3, w2):
    p_len, d = xs.shape
    n_blocks = p_len // MOE_BLOCK
    de = w1.shape[2]
    blk = lambda i, be, nu: (jnp.minimum(i, nu[0] - 1), 0)
    wsel = lambda i, be, nu: (be[i], 0, 0)
    grid_spec = pltpu.PrefetchScalarGridSpec(
        num_scalar_prefetch=2,
        grid=(n_blocks,),
        in_specs=[
            pl.BlockSpec((MOE_BLOCK, d), blk),
            pl.BlockSpec((1, d, de), wsel),
            pl.BlockSpec((1, d, de), wsel),
            pl.BlockSpec((1, de, d), wsel),
        ],
        out_specs=pl.BlockSpec((MOE_BLOCK, d), lambda i, be, nu: (i, 0)),
        scratch_shapes=[pltpu.VMEM((d, de), BF16), pltpu.VMEM((d, de), BF16), pltpu.VMEM((de, d), BF16)],
    )
    return pl.pallas_call(
        _experts_kernel,
        grid_spec=grid_spec,
        out_shape=jax.ShapeDtypeStruct((p_len, d), F32),
        compiler_params=_cparams(("arbitrary",)),
        name="experts",
    )(block_e, n_used, xs, w1, w3, w2)


def _final_kernel(dest_ref, h_ref, meta_ref, p_ref, wple_ref, gple_ref, gplg_ref, wplg_ref, ys_ref,
                  out_ref, gbuf, sem):
    tm = h_ref.shape[0]

    def row_copy(i, slot):
        d = dest_ref[0, 0, slot * tm + i]
        return pltpu.make_async_copy(ys_ref.at[pl.ds(d, 1), :], gbuf.at[slot, pl.ds(i, 1), :], sem)

    def issue(i, c):
        for slot in range(TOP_K):
            row_copy(i, slot).start()
        return c
    lax.fori_loop(0, tm, issue, 0)

    emb = jnp.dot(p_ref[...].astype(BF16), wple_ref[...], preferred_element_type=F32)
    emb = _rms_rows(emb) * gple_ref[...]

    def drain(i, c):
        for slot in range(TOP_K):
            row_copy(i, slot).wait()
        return c
    lax.fori_loop(0, tm, drain, 0)

    meta = meta_ref[...]
    h = h_ref[...] + meta[:, 4:5] * gbuf[0] + meta[:, 5:6] * gbuf[1]
    hn = (_rms_rows(h) * gplg_ref[...]).astype(BF16)
    z = jnp.dot(hn, wplg_ref[...], preferred_element_type=F32)
    out_ref[...] = h + emb * (1.0 / (1.0 + jnp.exp(-z)))


def _final(dest3, h2, meta, p2, wple_bf, g_ple, g_plg, wplg_bf, ys, tm):
    t, d = h2.shape
    dp = p2.shape[1]
    const = lambda i: (0, 0)
    row = lambda i: (i, 0)
    return pl.pallas_call(
        _final_kernel,
        grid=(t // tm,),
        in_specs=[
            pl.BlockSpec((1, 1, TOP_K * tm), lambda i: (i, 0, 0), memory_space=pltpu.SMEM),
            pl.BlockSpec((tm, d), row),
            pl.BlockSpec((tm, LANES), row),
            pl.BlockSpec((tm, dp), row),
            pl.BlockSpec((dp, d), const),
            pl.BlockSpec((1, d), const),
            pl.BlockSpec((1, d), const),
            pl.BlockSpec((d, d), const),
            pl.BlockSpec(memory_space=pl.ANY),
        ],
        out_specs=pl.BlockSpec((tm, d), row),
        out_shape=jax.ShapeDtypeStruct((t, d), F32),
        scratch_shapes=[pltpu.VMEM((TOP_K, tm, d), F32), pltpu.SemaphoreType.DMA(())],
        compiler_params=_cparams(("arbitrary",)),
        name="combine_ple",
    )(dest3, h2, meta, p2, wple_bf, g_ple, g_plg, wplg_bf, ys)


def _block_diag_mean(seg):
    m = np.kron(np.eye(512 // seg), np.full((seg, seg), 1.0 / seg))
    return jnp.asarray(m, BF16)


def _rope_tables(seq):
    inv = ROPE_THETA ** (-jnp.arange(0, DA_QK, 2, dtype=F32) / DA_QK)
    ang = jnp.arange(seq, dtype=F32)[:, None] * inv[None, :]
    cos, sin = jnp.cos(ang), jnp.sin(ang)
    cos_t = jnp.tile(cos, (1, LANES // cos.shape[1]))
    sin_t = jnp.tile(jnp.concatenate([-sin, sin], axis=1), (1, LANES // (2 * sin.shape[1])))
    return cos_t, sin_t


def _tile_for(n, pref):
    while n % pref:
        pref //= 2
    return pref


def kernel(x, p, g_mix, w_in, g_qa, g_ka, lam_q1, lam_k1, lam_q2, lam_k2, g_sub, g_qb, g_kb, rpb, w_out,
           g_ffn, w_rg, w_re, w1, w3, w2, g_plg, w_plg, w_ple, g_ple):
    b, s, d = x.shape
    t = b * s
    depth = w_in.shape[0]
    tm = _tile_for(s, 512)
    tmd = _tile_for(t, 512)
    tmf = _tile_for(t, 256)
    n_slots = t * TOP_K
    n_blocks = n_slots // MOE_BLOCK + N_EXPERTS
    p_len = n_blocks * MOE_BLOCK

    bd64 = _block_diag_mean(DA_QK)
    bd32 = _block_diag_mean(DB_H)
    cos_t, sin_t = _rope_tables(s)
    tri = jnp.asarray(np.tril(np.ones((tm, tm)), -1), BF16)

    h = x.reshape(t, d)
    for i in range(depth):
        lam_init = 0.8 - 0.6 * math.exp(-0.3 * i)
        gsec = jnp.stack([
            jnp.tile(g_qa[i], 512 // DA_QK) * (DA_QK ** -0.5 * math.log2(math.e)),
            jnp.tile(g_ka[i], 512 // DA_QK),
            jnp.tile(g_qb[i], 512 // DB_H) * DB_H ** -0.5,
            jnp.tile(g_kb[i], 512 // DB_H),
        ]).astype(F32)
        proj = _inproj(h, g_mix[i][None, :], w_in[i].astype(BF16), bd64, bd32, gsec, cos_t, sin_t, s, tm)
        proj3 = proj.reshape(b, s, proj.shape[1])

        lam4 = jnp.stack([lam_q1[i], lam_k1[i], lam_q2[i], lam_k2[i]]).astype(F32)
        oa = _diffattn(proj3, lam4, g_sub[i][None, :].astype(F32), lam_init)
        ob = _natten(proj3, _natten_bias_table(rpb[i]))

        w_router = jnp.concatenate(
            [w_rg[i], w_re[i], jnp.zeros((d, LANES - N_GROUPS - N_EXPERTS), F32)], axis=1).astype(F32)
        h1, xn, meta, cnt = _outproj(h, oa.reshape(t, -1), ob.reshape(t, -1), w_out[i].astype(BF16),
                                     g_ffn[i][None, :], w_router, tri, tm)

        counts = cnt[0, E_LANE0:E_LANE0 + N_EXPERTS].astype(jnp.int32)
        padded = (counts + MOE_BLOCK - 1) // MOE_BLOCK * MOE_BLOCK
        pad_end = jnp.cumsum(padded)
        pad_start = pad_end - padded
        mt = meta[:, 0:2 * TOP_K].T.astype(jnp.int32)
        e_ids = jnp.arange(N_EXPERTS, dtype=jnp.int32)[None, :, None]
        start_of = jnp.sum(jnp.where(mt[:TOP_K, None, :] == e_ids, pad_start[None, :, None], 0), axis=1)
        dest = start_of + mt[TOP_K:]
        blk_start = jnp.arange(n_blocks, dtype=jnp.int32) * MOE_BLOCK
        block_e = jnp.minimum(jnp.sum(pad_end[None, :] <= blk_start[:, None], axis=1), N_EXPERTS - 1)
        block_e = block_e.astype(jnp.int32)
        n_used = (pad_end[-1:] // MOE_BLOCK).astype(jnp.int32)

        def per_tile(tile):
            return dest.reshape(TOP_K, t // tile, tile).transpose(1, 0, 2).reshape(t // tile, 1, TOP_K * tile)

        xs = _dispatch(xn, per_tile(tmd), jnp.zeros((p_len, d), F32), tmd)
        ys = _experts(xs, block_e, n_used, w1[i], w3[i], w2[i])
        h = _final(per_tile(tmf), h1, meta, p[i].reshape(t, -1),
                   w_ple[i].astype(BF16), g_ple[i][None, :], g_plg[i][None, :], w_plg[i].astype(BF16), ys, tmf)
    return h.reshape(b, s, d)
```

```python
import functools
import math

import numpy as np
import jax
import jax.numpy as jnp
from jax import lax
from jax.experimental import pallas as pl
from jax.experimental.pallas import tpu as pltpu

F32 = jnp.float32
BF16 = jnp.bfloat16

LANES = 128
HA = 4
DA_QK = 64
DA_V = 128
HB = 16
DB_H = 32
GRID_W = 64
WIN_R = 8
WIN_C = 16
ROPE_THETA = 10000.0
EPS = 1e-6
NEG = -1e30
N_GROUPS = 4
EXPERTS_PER_GROUP = 8
N_EXPERTS = N_GROUPS * EXPERTS_PER_GROUP
TOP_K = 2
MOE_BLOCK = 256
E_LANE0 = N_GROUPS

VMEM_LIMIT = 56 * 1024 * 1024


def _cparams(sem):
    return pltpu.CompilerParams(dimension_semantics=sem, vmem_limit_bytes=VMEM_LIMIT)


def _rms_rows(x):
    return x * lax.rsqrt(jnp.mean(x * x, axis=-1, keepdims=True) + EPS)


SUBLANES = 8


def _store_token_tiles(ref, val):
    n = val.shape[0]
    for c in range(val.shape[1] // LANES):
        ref[pl.ds(c, n, stride=SUBLANES), :] = val[:, c * LANES:(c + 1) * LANES]


def _load_token_tiles(ref, n):
    return jnp.concatenate([ref[pl.ds(c, n, stride=SUBLANES), :] for c in range(SUBLANES)], axis=1)


def _inproj_kernel(x_ref, gmix_ref, w_ref, bd64_ref, bd32_ref, gsec_ref, cos_ref, sin_ref, out_ref):
    tm = x_ref.shape[0]
    a = (_rms_rows(x_ref[...]) * gmix_ref[...]).astype(BF16)
    lane = lax.broadcasted_iota(jnp.int32, (tm, LANES), 1)
    first_half = (lane % 64) < 32
    cos = cos_ref[...]
    sin = sin_ref[...]
    for sec in range(6):
        y = jnp.dot(a, w_ref[:, sec * 512:(sec + 1) * 512], preferred_element_type=F32)
        if sec in (0, 1, 3, 4):
            bd = bd64_ref if sec < 2 else bd32_ref
            gi = {0: 0, 1: 1, 3: 2, 4: 3}[sec]
            ms = jnp.dot((y * y).astype(BF16), bd[...], preferred_element_type=F32)
            y = y * lax.rsqrt(ms + EPS) * gsec_ref[gi:gi + 1, :]
        for c in range(4):
            yc = y[:, c * LANES:(c + 1) * LANES]
            if sec < 2:
                rot = jnp.where(first_half, pltpu.roll(yc, 96, 1), pltpu.roll(yc, 32, 1))
                yc = yc * cos + rot * sin
            out_ref[:, sec * 512 + c * LANES: sec * 512 + (c + 1) * LANES] = yc.astype(BF16)


def _inproj(x2, g_mix, w_bf, bd64, bd32, gsec, cos_t, sin_t, seq, tm):
    t, d = x2.shape
    n = w_bf.shape[1]
    nsb = seq // tm
    const = lambda i: (0, 0)
    return pl.pallas_call(
        _inproj_kernel,
        grid=(t // tm,),
        in_specs=[
            pl.BlockSpec((tm, d), lambda i: (i, 0)),
            pl.BlockSpec((1, d), const),
            pl.BlockSpec((d, n), const),
            pl.BlockSpec((512, 512), const),
            pl.BlockSpec((512, 512), const),
            pl.BlockSpec((4, 512), const),
            pl.BlockSpec((tm, LANES), lambda i: (i % nsb, 0)),
            pl.BlockSpec((tm, LANES), lambda i: (i % nsb, 0)),
        ],
        out_specs=pl.BlockSpec((tm, n), lambda i: (i, 0)),
        out_shape=jax.ShapeDtypeStruct((t, n), BF16),
        compiler_params=_cparams(("parallel",)),
        name="inproj",
    )(x2, g_mix, w_bf, bd64, bd32, gsec, cos_t, sin_t)


def _diffattn_kernel(lam_ref, gsub_ref, q_ref, k_ref, v_ref, o_ref, qt_scr, vt_scr, sa_scr, sb_scr, *, seq, tq, tk,
                     lam_init):
    nq = seq // tq
    nk = seq // tk
    s1 = jnp.sum(lam_ref[0:1, :] * lam_ref[1:2, :], axis=1, keepdims=True)
    s2 = jnp.sum(lam_ref[2:3, :] * lam_ref[3:4, :], axis=1, keepdims=True)
    lam = jnp.exp(s1) - jnp.exp(s2) + lam_init

    def tr_body(c, carry):
        off = pl.multiple_of(c * LANES, LANES)
        qt_scr[:, pl.ds(off, LANES)] = q_ref[0, pl.ds(off, LANES), :].astype(F32).T.astype(BF16)
        vt_scr[:, pl.ds(off, LANES)] = v_ref[0, pl.ds(off, LANES), :].astype(F32).T.astype(BF16)
        return carry
    lax.fori_loop(0, seq // LANES, tr_body, 0)

    row = lax.broadcasted_iota(jnp.int32, (LANES, tq), 0)

    def scores(qi, s_scr):
        qoff = pl.multiple_of(qi * tq, tq)
        qt = qt_scr[:, pl.ds(qoff, tq)]
        zero = jnp.zeros_like(qt)
        qm = jnp.concatenate([jnp.where(row < DA_QK, qt, zero), jnp.where(row >= DA_QK, qt, zero)], axis=1)
        mx = None
        for j in range(nk):
            st = jnp.dot(k_ref[0, j * tk:(j + 1) * tk, :], qm, preferred_element_type=F32)
            s_scr[j * tk:(j + 1) * tk, :] = st
            cm = jnp.max(st, axis=0, keepdims=True)
            mx = cm if mx is None else jnp.maximum(mx, cm)
        return mx

    def finish(qi, s_scr, m):
        l = jnp.zeros((1, 2 * tq), F32)
        acc = jnp.zeros((DA_V, 2 * tq), F32)
        for j in range(nk):
            p = jnp.exp2(s_scr[j * tk:(j + 1) * tk, :] - m)
            l = l + jnp.sum(p, axis=0, keepdims=True)
            acc = acc + jnp.dot(vt_scr[:, j * tk:(j + 1) * tk], p.astype(BF16), preferred_element_type=F32)
        o = acc * (1.0 / l)
        ot = o[:, :tq] - lam * o[:, tq:]
        ot = ot * lax.rsqrt(jnp.mean(ot * ot, axis=0, keepdims=True) + EPS) * (1.0 - lam_init)
        qoff = pl.multiple_of(qi * tq, tq)
        o_ref[0, pl.ds(qoff, tq), :] = (ot.T * gsub_ref[...]).astype(BF16)

    def body(i, m_a):
        m_b = scores(2 * i + 1, sb_scr)
        finish(2 * i, sa_scr, m_a)
        m_a_next = scores(jnp.minimum(2 * i + 2, nq - 1), sa_scr)
        finish(2 * i + 1, sb_scr, m_b)
        return m_a_next

    lax.fori_loop(0, nq // 2, body, scores(0, sa_scr))


def _diffattn(proj3, lam4, g_sub, lam_init, tq=128, tk=256):
    b, seq, _ = proj3.shape
    kern = functools.partial(_diffattn_kernel, seq=seq, tq=tq, tk=tk, lam_init=lam_init)
    return pl.pallas_call(
        kern,
        grid=(b, HA),
        in_specs=[
            pl.BlockSpec((4, DA_QK), lambda i, h: (0, 0)),
            pl.BlockSpec((1, DA_V), lambda i, h: (0, 0)),
            pl.BlockSpec((1, seq, LANES), lambda i, h: (i, 0, h)),
            pl.BlockSpec((1, seq, LANES), lambda i, h: (i, 0, HA + h)),
            pl.BlockSpec((1, seq, LANES), lambda i, h: (i, 0, 2 * HA + h)),
        ],
        out_specs=pl.BlockSpec((1, seq, LANES), lambda i, h: (i, 0, h)),
        out_shape=jax.ShapeDtypeStruct((b, seq, HA * DA_V), BF16),
        scratch_shapes=[pltpu.VMEM((LANES, seq), BF16), pltpu.VMEM((LANES, seq), BF16),
                        pltpu.VMEM((seq, 2 * tq), F32), pltpu.VMEM((seq, 2 * tq), F32)],
        compiler_params=_cparams(("parallel", "parallel")),
        name="diffattn",
    )(lam4, g_sub, proj3, proj3, proj3)


def _natten_kernel(bias_ref, q_ref, k_ref, v_ref, o_ref, *, rows):
    kr = min(WIN_R, rows)
    nkeys = kr * GRID_W
    hpg = LANES // DB_H
    lane_head = lax.broadcasted_iota(jnp.int32, (GRID_W, LANES), 1) // DB_H

    def row_step(r, carry):
        rs = jnp.clip(r - kr // 2, 0, rows - kr)
        base = rs - r + (WIN_R - 1)
        qoff = pl.multiple_of(r * GRID_W, GRID_W)
        koff = pl.multiple_of(rs * GRID_W, GRID_W)
        qr = q_ref[0, pl.ds(qoff, GRID_W), :]
        zero = jnp.zeros_like(qr)
        qs = jnp.concatenate([jnp.where(lane_head == h, qr, zero) for h in range(hpg)], axis=0)
        kw = k_ref[0, pl.ds(koff, nkeys), :]
        vw = v_ref[0, pl.ds(koff, nkeys), :]
        s = lax.dot_general(qs, kw, (((1,), (1,)), ((), ())), preferred_element_type=F32)
        bias = jnp.concatenate([bias_ref[0, base + 2 * i] for i in range(kr // 2)], axis=1)
        s = s + bias
        m = jnp.max(s, axis=1, keepdims=True)
        p = jnp.exp(s - m).astype(BF16)
        pv = jnp.dot(p, jnp.concatenate([vw, jnp.ones_like(vw)], axis=1), preferred_element_type=F32)
        o = pv[:, :LANES] * (1.0 / pv[:, LANES:])
        out = jnp.zeros((GRID_W, LANES), F32)
        for h in range(hpg):
            out = out + jnp.where(lane_head == h, o[h * GRID_W:(h + 1) * GRID_W, :], 0.0)
        o_ref[0, pl.ds(qoff, GRID_W), :] = out.astype(BF16)
        return carry

    lax.fori_loop(0, rows, row_step, 0, unroll=4)


def _natten(proj3, bias_tab):
    b, seq, _ = proj3.shape
    rows = seq // GRID_W
    ng = HB * DB_H // LANES
    nro = bias_tab.shape[1]
    kern = functools.partial(_natten_kernel, rows=rows)
    cb = 3 * HA
    return pl.pallas_call(
        kern,
        grid=(b, ng),
        in_specs=[
            pl.BlockSpec((1, nro, bias_tab.shape[2], LANES), lambda i, g: (g, 0, 0, 0)),
            pl.BlockSpec((1, seq, LANES), lambda i, g: (i, 0, cb + g)),
            pl.BlockSpec((1, seq, LANES), lambda i, g: (i, 0, cb + ng + g)),
            pl.BlockSpec((1, seq, LANES), lambda i, g: (i, 0, cb + 2 * ng + g)),
        ],
        out_specs=pl.BlockSpec((1, seq, LANES), lambda i, g: (i, 0, g)),
        out_shape=jax.ShapeDtypeStruct((b, seq, HB * DB_H), BF16),
        compiler_params=_cparams(("parallel", "parallel")),
        name="natten",
    )(bias_tab, proj3, proj3, proj3)


def _natten_bias_table(rpb):
    q = np.arange(GRID_W)[:, None]
    kc = np.arange(GRID_W)[None, :]
    cs = np.clip(q - WIN_C // 2, 0, GRID_W - WIN_C)
    valid = (kc >= cs) & (kc < cs + WIN_C)
    col_off = np.clip(kc - q + (WIN_C - 1), 0, 2 * WIN_C - 2)
    ncol = 2 * WIN_C - 1
    nro = 2 * WIN_R - 2
    pick = np.zeros((2, ncol, GRID_W, 2 * GRID_W), np.float32)
    for j in range(2):
        pick[j, :, :, j * GRID_W:(j + 1) * GRID_W] = (np.arange(ncol)[:, None, None] == col_off[None]) & valid[None]
    negmask = np.tile(np.where(valid, 0.0, NEG).astype(np.float32), (1, 2))
    hpg = LANES // DB_H
    r4 = rpb.astype(F32).reshape(HB // hpg, hpg, 2 * WIN_R - 1, ncol)
    rp = jnp.stack([r4[:, :, :nro], r4[:, :, 1:]], axis=0)
    tab = jnp.einsum('jghrc,jcqk->grhqk', rp, jnp.asarray(pick), precision=lax.Precision.HIGHEST) + negmask
    return tab.reshape(HB // hpg, nro, hpg * GRID_W, LANES)


def _outproj_kernel(x_ref, oa_ref, ob_ref, wo_ref, gffn_ref, wr_ref, tri_ref,
                    h_ref, xn_ref, meta_ref, cnt_ref, run_scr):
    tm = x_ref.shape[0]
    da = oa_ref.shape[1]

    @pl.when(pl.program_id(0) == 0)
    def _():
        run_scr[...] = jnp.zeros_like(run_scr)

    attn = jnp.dot(oa_ref[...], wo_ref[:da, :], preferred_element_type=F32)
    attn = attn + jnp.dot(ob_ref[...], wo_ref[da:, :], preferred_element_type=F32)
    h = x_ref[...] + attn
    h_ref[...] = h
    xn = _rms_rows(h) * gffn_ref[...]
    _store_token_tiles(xn_ref, xn)

    xh = xn.astype(BF16)
    xl = (xn - xh.astype(F32)).astype(BF16)
    hh_hl = jnp.dot(xh, wr_ref[...], preferred_element_type=F32)
    lg = hh_hl[:, :LANES] + hh_hl[:, LANES:] + jnp.dot(xl, wr_ref[:, :LANES], preferred_element_type=F32)
    lane = lax.broadcasted_iota(jnp.int32, (tm, LANES), 1).astype(F32)
    ninf = -jnp.inf
    is_g = lane < N_GROUPS
    gl = jnp.where(is_g, lg, ninf)
    gmax = jnp.max(gl, axis=1, keepdims=True)
    gidx = jnp.min(jnp.where(gl == gmax, lane, float(LANES)), axis=1, keepdims=True)
    gsum = jnp.sum(jnp.where(is_g, jnp.exp(gl - gmax), 0.0), axis=1, keepdims=True)
    g_gate = 1.0 / gsum
    lo = E_LANE0 + gidx * EXPERTS_PER_GROUP
    is_e = (lane >= lo) & (lane < lo + EXPERTS_PER_GROUP)
    el = jnp.where(is_e, lg, ninf)
    l1 = jnp.max(el, axis=1, keepdims=True)
    i1 = jnp.min(jnp.where(el == l1, lane, float(LANES)), axis=1, keepdims=True)
    el2 = jnp.where(lane == i1, ninf, el)
    l2 = jnp.max(el2, axis=1, keepdims=True)
    i2 = jnp.min(jnp.where(el2 == l2, lane, float(LANES)), axis=1, keepdims=True)
    r = jnp.exp(l2 - l1)
    w1 = g_gate / (1.0 + r)
    w2 = w1 * r

    sel1 = lane == i1
    sel2 = lane == i2
    oh = jnp.where(sel1 | sel2, 1.0, 0.0)
    before = jnp.dot(tri_ref[...], oh.astype(BF16), preferred_element_type=F32) + run_scr[...]
    rank1 = jnp.sum(jnp.where(sel1, before, 0.0), axis=1, keepdims=True)
    rank2 = jnp.sum(jnp.where(sel2, before, 0.0), axis=1, keepdims=True)
    run = run_scr[...] + jnp.sum(oh, axis=0, keepdims=True)
    run_scr[...] = run
    cnt_ref[...] = run

    meta = jnp.where(lane == 0, i1 - E_LANE0, 0.0)
    meta = jnp.where(lane == 1, i2 - E_LANE0, meta)
    meta = jnp.where(lane == 2, rank1, meta)
    meta = jnp.where(lane == 3, rank2, meta)
    meta = jnp.where(lane == 4, w1, meta)
    meta = jnp.where(lane == 5, w2, meta)
    meta_ref[...] = meta


def _outproj(x2, oa2, ob2, wo_bf, g_ffn, w_router, tri, tm):
    t, d = x2.shape
    da = oa2.shape[1]
    const = lambda i: (0, 0)
    row = lambda i: (i, 0)
    return pl.pallas_call(
        _outproj_kernel,
        grid=(t // tm,),
        in_specs=[
            pl.BlockSpec((tm, d), row),
            pl.BlockSpec((tm, da), row),
            pl.BlockSpec((tm, d - da), row),
            pl.BlockSpec((d, d), const),
            pl.BlockSpec((1, d), const),
            pl.BlockSpec((d, 2 * LANES), const),
            pl.BlockSpec((tm, tm), const),
        ],
        out_specs=[
            pl.BlockSpec((tm, d), row),
            pl.BlockSpec((tm * SUBLANES, LANES), row),
            pl.BlockSpec((tm, LANES), row),
            pl.BlockSpec((1, LANES), const),
        ],
        out_shape=[
            jax.ShapeDtypeStruct((t, d), F32),
            jax.ShapeDtypeStruct((t * SUBLANES, LANES), F32),
            jax.ShapeDtypeStruct((t, LANES), F32),
            jax.ShapeDtypeStruct((1, LANES), F32),
        ],
        scratch_shapes=[pltpu.VMEM((1, LANES), F32)],
        compiler_params=_cparams(("arbitrary",)),
        name="outproj_router",
    )(x2, oa2, ob2, wo_bf, g_ffn, w_router, tri)


DMA_ISSUE_UNROLL = 8


def _dispatch_kernel(dest_ref, xn_ref, xs_in_ref, xs_ref, sem):
    del xs_in_ref
    tm = xn_ref.shape[0] // SUBLANES
    t_total = pl.num_programs(0) * tm
    base = pl.program_id(0) * tm

    def issue(r, c):
        src = xn_ref.at[pl.ds(pl.multiple_of(r * SUBLANES, SUBLANES), SUBLANES), :]
        for slot in range(TOP_K):
            d = dest_ref[slot * t_total + base + r]
            dst = xs_ref.at[pl.ds(pl.multiple_of(d * SUBLANES, SUBLANES), SUBLANES), :]
            pltpu.make_async_copy(src, dst, sem).start()
        return c
    lax.fori_loop(0, tm, issue, 0, unroll=DMA_ISSUE_UNROLL)

    for slot in range(TOP_K):
        pltpu.make_async_copy(xn_ref, xs_ref.at[pl.ds(0, tm * SUBLANES), :], sem).wait()


def _dispatch(xn_tiles, dest_flat, xs_zero, tm):
    t = xn_tiles.shape[0] // SUBLANES
    grid_spec = pltpu.PrefetchScalarGridSpec(
        num_scalar_prefetch=1,
        grid=(t // tm,),
        in_specs=[
            pl.BlockSpec((tm * SUBLANES, LANES), lambda i, dest: (i, 0)),
            pl.BlockSpec(memory_space=pl.ANY),
        ],
        out_specs=pl.BlockSpec(memory_space=pl.ANY),
        scratch_shapes=[pltpu.SemaphoreType.DMA(())],
    )
    return pl.pallas_call(
        _dispatch_kernel,
        grid_spec=grid_spec,
        out_shape=jax.ShapeDtypeStruct(xs_zero.shape, xs_zero.dtype),
        input_output_aliases={2: 0},
        compiler_params=_cparams(("arbitrary",)),
        name="dispatch",
    )(dest_flat, xn_tiles, xs_zero)


def _experts_kernel(be_ref, nu_ref, xs_ref, w1_ref, w3_ref, w2_ref, ys_ref, w1_scr, w3_scr, w2_scr):
    i = pl.program_id(0)

    @pl.when(i < nu_ref[0])
    def _():
        prev = be_ref[jnp.maximum(i - 1, 0)]
        new_expert = (i == 0) | (be_ref[i] != prev)

        @pl.when(new_expert)
        def _():
            w1_scr[...] = w1_ref[0].astype(BF16)
            w3_scr[...] = w3_ref[0].astype(BF16)
            w2_scr[...] = w2_ref[0].astype(BF16)

        x = _load_token_tiles(xs_ref, MOE_BLOCK).astype(BF16)
        a = jnp.dot(x, w1_scr[...], preferred_element_type=F32)
        b = jnp.dot(x, w3_scr[...], preferred_element_type=F32)
        hdn = a * (1.0 / (1.0 + jnp.exp(-a))) * b
        _store_token_tiles(ys_ref, jnp.dot(hdn.astype(BF16), w2_scr[...], preferred_element_type=F32))

    @pl.when(i >= nu_ref[0])
    def _():
        ys_ref[...] = jnp.zeros_like(ys_ref)


def _experts(xs, block_e, n_used, w1, w3, w2):
    p_len = xs.shape[0] // SUBLANES
    n_blocks = p_len // MOE_BLOCK
    d, de = w1.shape[1], w1.shape[2]
    blk = lambda i, be, nu: (jnp.maximum(jnp.minimum(i, nu[0] - 1), 0), 0)
    wsel = lambda i, be, nu: (be[i], 0, 0)
    grid_spec = pltpu.PrefetchScalarGridSpec(
        num_scalar_prefetch=2,
        grid=(n_blocks,),
        in_specs=[
            pl.BlockSpec((MOE_BLOCK * SUBLANES, LANES), blk),
            pl.BlockSpec((1, d, de), wsel),
            pl.BlockSpec((1, d, de), wsel),
            pl.BlockSpec((1, de, d), wsel),
        ],
        out_specs=pl.BlockSpec((MOE_BLOCK * SUBLANES, LANES), lambda i, be, nu: (i, 0)),
        scratch_shapes=[pltpu.VMEM((d, de), BF16), pltpu.VMEM((d, de), BF16), pltpu.VMEM((de, d), BF16)],
    )
    return pl.pallas_call(
        _experts_kernel,
        grid_spec=grid_spec,
        out_shape=jax.ShapeDtypeStruct((p_len * SUBLANES, LANES), F32),
        compiler_params=_cparams(("arbitrary",)),
        name="experts",
    )(block_e, n_used, xs, w1, w3, w2)


def _final_kernel(dest_ref, h_ref, meta_ref, p_ref, wple_ref, gple_ref, gplg_ref, wplg_ref, ys_ref,
                  out_ref, gbuf, sems):
    tm = h_ref.shape[0]
    step = pl.program_id(0)
    n_steps = pl.num_programs(0)
    t_total = n_steps * tm

    def issue_tile(tile, buf):
        def issue(r, c):
            roff = pl.multiple_of(r * SUBLANES, SUBLANES)
            for slot in range(TOP_K):
                d = dest_ref[slot * t_total + tile * tm + r]
                src = ys_ref.at[pl.ds(pl.multiple_of(d * SUBLANES, SUBLANES), SUBLANES), :]
                pltpu.make_async_copy(src, gbuf.at[buf, slot, pl.ds(roff, SUBLANES), :], sems.at[buf]).start()
            return c
        lax.fori_loop(0, tm, issue, 0, unroll=DMA_ISSUE_UNROLL)

    @pl.when(step == 0)
    def _():
        issue_tile(0, 0)

    @pl.when(step + 1 < n_steps)
    def _():
        issue_tile(step + 1, (step + 1) % 2)

    emb = jnp.dot(p_ref[...].astype(BF16), wple_ref[...], preferred_element_type=F32)
    emb = _rms_rows(emb) * gple_ref[...]

    buf = step % 2
    for slot in range(TOP_K):
        pltpu.make_async_copy(ys_ref.at[pl.ds(0, tm * SUBLANES), :], gbuf.at[buf, slot], sems.at[buf]).wait()

    meta = meta_ref[...]
    y1 = _load_token_tiles(gbuf.at[buf, 0], tm)
    y2 = _load_token_tiles(gbuf.at[buf, 1], tm)
    h = h_ref[...] + meta[:, 4:5] * y1 + meta[:, 5:6] * y2
    hn = (_rms_rows(h) * gplg_ref[...]).astype(BF16)
    z = jnp.dot(hn, wplg_ref[...], preferred_element_type=F32)
    out_ref[...] = h + emb * (1.0 / (1.0 + jnp.exp(-z)))


def _final(dest_flat, h2, meta, p2, wple_bf, g_ple, g_plg, wplg_bf, ys, tm):
    t, d = h2.shape
    dp = p2.shape[1]
    const = lambda i, dest: (0, 0)
    row = lambda i, dest: (i, 0)
    grid_spec = pltpu.PrefetchScalarGridSpec(
        num_scalar_prefetch=1,
        grid=(t // tm,),
        in_specs=[
            pl.BlockSpec((tm, d), row),
            pl.BlockSpec((tm, LANES), row),
            pl.BlockSpec((tm, dp), row),
            pl.BlockSpec((dp, d), const),
            pl.BlockSpec((1, d), const),
            pl.BlockSpec((1, d), const),
            pl.BlockSpec((d, d), const),
            pl.BlockSpec(memory_space=pl.ANY),
        ],
        out_specs=pl.BlockSpec((tm, d), row),
        scratch_shapes=[pltpu.VMEM((2, TOP_K, tm * SUBLANES, LANES), F32), pltpu.SemaphoreType.DMA((2,))],
    )
    return pl.pallas_call(
        _final_kernel,
        grid_spec=grid_spec,
        out_shape=jax.ShapeDtypeStruct((t, d), F32),
        compiler_params=_cparams(("arbitrary",)),
        name="combine_ple",
    )(dest_flat, h2, meta, p2, wple_bf, g_ple, g_plg, wplg_bf, ys)


def _block_diag_mean(seg):
    m = np.kron(np.eye(512 // seg), np.full((seg, seg), 1.0 / seg))
    return jnp.asarray(m, BF16)


def _rope_tables(seq):
    inv = ROPE_THETA ** (-jnp.arange(0, DA_QK, 2, dtype=F32) / DA_QK)
    ang = jnp.arange(seq, dtype=F32)[:, None] * inv[None, :]
    cos, sin = jnp.cos(ang), jnp.sin(ang)
    cos_t = jnp.tile(cos, (1, LANES // cos.shape[1]))
    sin_t = jnp.tile(jnp.concatenate([-sin, sin], axis=1), (1, LANES // (2 * sin.shape[1])))
    return cos_t, sin_t


def _tile_for(n, pref):
    while n % pref:
        pref //= 2
    return pref


def kernel(x, p, g_mix, w_in, g_qa, g_ka, lam_q1, lam_k1, lam_q2, lam_k2, g_sub, g_qb, g_kb, rpb, w_out,
           g_ffn, w_rg, w_re, w1, w3, w2, g_plg, w_plg, w_ple, g_ple):
    b, s, d = x.shape
    assert d == SUBLANES * LANES, "token-tile layout holds one token per (8, 128) f32 tile"
    t = b * s
    depth = w_in.shape[0]
    tm = _tile_for(s, 512)
    tmd = _tile_for(t, 512)
    tmf = _tile_for(t, 256)
    n_slots = t * TOP_K
    n_blocks = n_slots // MOE_BLOCK + N_EXPERTS
    p_len = n_blocks * MOE_BLOCK

    bd64 = _block_diag_mean(DA_QK)
    bd32 = _block_diag_mean(DB_H)
    cos_t, sin_t = _rope_tables(s)
    tri = jnp.asarray(np.tril(np.ones((tm, tm)), -1), BF16)

    h = x.reshape(t, d)
    for i in range(depth):
        lam_init = 0.8 - 0.6 * math.exp(-0.3 * i)
        gsec = jnp.stack([
            jnp.tile(g_qa[i], 512 // DA_QK) * (DA_QK ** -0.5 * math.log2(math.e)),
            jnp.tile(g_ka[i], 512 // DA_QK),
            jnp.tile(g_qb[i], 512 // DB_H) * DB_H ** -0.5,
            jnp.tile(g_kb[i], 512 // DB_H),
        ]).astype(F32)
        proj = _inproj(h, g_mix[i][None, :], w_in[i].astype(BF16), bd64, bd32, gsec, cos_t, sin_t, s, tm)
        proj3 = proj.reshape(b, s, proj.shape[1])

        lam4 = jnp.stack([lam_q1[i], lam_k1[i], lam_q2[i], lam_k2[i]]).astype(F32)
        oa = _diffattn(proj3, lam4, g_sub[i][None, :].astype(F32), lam_init)
        ob = _natten(proj3, _natten_bias_table(rpb[i]))

        w_router = jnp.concatenate(
            [w_rg[i], w_re[i], jnp.zeros((d, LANES - N_GROUPS - N_EXPERTS), F32)], axis=1).astype(F32)
        wr_hi = w_router.astype(BF16)
        wr_lo = (w_router - wr_hi.astype(F32)).astype(BF16)
        h1, xn, meta, cnt = _outproj(h, oa.reshape(t, -1), ob.reshape(t, -1), w_out[i].astype(BF16),
                                     g_ffn[i][None, :], jnp.concatenate([wr_hi, wr_lo], axis=1), tri, tm)

        counts = cnt[0, E_LANE0:E_LANE0 + N_EXPERTS].astype(jnp.int32)
        padded = (counts + MOE_BLOCK - 1) // MOE_BLOCK * MOE_BLOCK
        pad_end = jnp.cumsum(padded)
        pad_start = pad_end - padded
        mt = meta[:, 0:2 * TOP_K].T.astype(jnp.int32)
        e_ids = jnp.arange(N_EXPERTS, dtype=jnp.int32)[None, :, None]
        start_of = jnp.sum(jnp.where(mt[:TOP_K, None, :] == e_ids, pad_start[None, :, None], 0), axis=1)
        dest = start_of + mt[TOP_K:]
        blk_start = jnp.arange(n_blocks, dtype=jnp.int32) * MOE_BLOCK
        block_e = jnp.minimum(jnp.sum(pad_end[None, :] <= blk_start[:, None], axis=1), N_EXPERTS - 1)
        block_e = block_e.astype(jnp.int32)
        n_used = (pad_end[-1:] // MOE_BLOCK).astype(jnp.int32)

        dest_flat = dest.reshape(TOP_K * t)

        xs = _dispatch(xn, dest_flat, jnp.zeros((p_len * SUBLANES, LANES), F32), tmd)
        ys = _experts(xs, block_e, n_used, w1[i], w3[i], w2[i])
        h = _final(dest_flat, h1, meta, p[i].reshape(t, -1),
                   w_ple[i].astype(BF16), g_ple[i][None, :], g_plg[i][None, :], w_plg[i].astype(BF16), ys, tmf)
    return h.reshape(b, s, d)
```

```python
import functools
import math

import numpy as np
import jax
import jax.numpy as jnp
from jax import lax
from jax.experimental import pallas as pl
from jax.experimental.pallas import tpu as pltpu

F32 = jnp.float32
BF16 = jnp.bfloat16

LANES = 128
HA = 4
DA_QK = 64
DA_V = 128
HB = 16
DB_H = 32
GRID_W = 64
WIN_R = 8
WIN_C = 16
ROPE_THETA = 10000.0
EPS = 1e-6
NEG = -1e30
N_GROUPS = 4
EXPERTS_PER_GROUP = 8
N_EXPERTS = N_GROUPS * EXPERTS_PER_GROUP
TOP_K = 2
MOE_BLOCK = 256
E_LANE0 = N_GROUPS

VMEM_LIMIT = 56 * 1024 * 1024


def _cparams(sem):
    return pltpu.CompilerParams(dimension_semantics=sem, vmem_limit_bytes=VMEM_LIMIT)


def _rms_rows(x):
    return x * lax.rsqrt(jnp.mean(x * x, axis=-1, keepdims=True) + EPS)


SUBLANES = 8

def _store_token_tiles(ref, val, first=0):
    n = val.shape[0]
    for c in range(val.shape[1] // LANES):
        ref[pl.ds(first * SUBLANES + c, n, stride=SUBLANES), :] = val[:, c * LANES:(c + 1) * LANES]


def _load_token_tiles(ref, n, first=0):
    return jnp.concatenate(
        [ref[pl.ds(first * SUBLANES + c, n, stride=SUBLANES), :] for c in range(SUBLANES)], axis=1)


def _inproj_kernel(x_ref, gmix_ref, w_ref, bd64_ref, bd32_ref, gsec_ref, cos_ref, sin_ref, out_ref):
    tm = x_ref.shape[0]
    a = (_rms_rows(x_ref[...]) * gmix_ref[...]).astype(BF16)
    lane = lax.broadcasted_iota(jnp.int32, (tm, LANES), 1)
    first_half = (lane % 64) < 32
    cos = cos_ref[...]
    sin = sin_ref[...]
    for sec in range(6):
        y = jnp.dot(a, w_ref[:, sec * 512:(sec + 1) * 512], preferred_element_type=F32)
        if sec in (0, 1, 3, 4):
            bd = bd64_ref if sec < 2 else bd32_ref
            gi = {0: 0, 1: 1, 3: 2, 4: 3}[sec]
            ms = jnp.dot((y * y).astype(BF16), bd[...], preferred_element_type=F32)
            y = y * lax.rsqrt(ms + EPS) * gsec_ref[gi:gi + 1, :]
        for c in range(4):
            yc = y[:, c * LANES:(c + 1) * LANES]
            if sec < 2:
                rot = jnp.where(first_half, pltpu.roll(yc, 96, 1), pltpu.roll(yc, 32, 1))
                yc = yc * cos + rot * sin
            out_ref[:, sec * 512 + c * LANES: sec * 512 + (c + 1) * LANES] = yc.astype(BF16)


def _inproj(x2, g_mix, w_bf, bd64, bd32, gsec, cos_t, sin_t, seq, tm):
    t, d = x2.shape
    n = w_bf.shape[1]
    nsb = seq // tm
    const = lambda i: (0, 0)
    return pl.pallas_call(
        _inproj_kernel,
        grid=(t // tm,),
        in_specs=[
            pl.BlockSpec((tm, d), lambda i: (i, 0)),
            pl.BlockSpec((1, d), const),
            pl.BlockSpec((d, n), const),
            pl.BlockSpec((512, 512), const),
            pl.BlockSpec((512, 512), const),
            pl.BlockSpec((4, 512), const),
            pl.BlockSpec((tm, LANES), lambda i: (i % nsb, 0)),
            pl.BlockSpec((tm, LANES), lambda i: (i % nsb, 0)),
        ],
        out_specs=pl.BlockSpec((tm, n), lambda i: (i, 0)),
        out_shape=jax.ShapeDtypeStruct((t, n), BF16),
        compiler_params=_cparams(("parallel",)),
        name="inproj",
    )(x2, g_mix, w_bf, bd64, bd32, gsec, cos_t, sin_t)


def _diffattn_kernel(lam_ref, gsub_ref, q_ref, k_ref, v_ref, o_ref, qt_scr, vt_scr, sa_scr, sb_scr, *, seq, tq, tk,
                     lam_init):
    nq = seq // tq
    nk = seq // tk
    s1 = jnp.sum(lam_ref[0:1, :] * lam_ref[1:2, :], axis=1, keepdims=True)
    s2 = jnp.sum(lam_ref[2:3, :] * lam_ref[3:4, :], axis=1, keepdims=True)
    lam = jnp.exp(s1) - jnp.exp(s2) + lam_init

    def tr_body(c, carry):
        off = pl.multiple_of(c * LANES, LANES)
        qt_scr[:, pl.ds(off, LANES)] = q_ref[0, pl.ds(off, LANES), :].astype(F32).T.astype(BF16)
        vt_scr[:, pl.ds(off, LANES)] = v_ref[0, pl.ds(off, LANES), :].astype(F32).T.astype(BF16)
        return carry
    lax.fori_loop(0, seq // LANES, tr_body, 0)

    row = lax.broadcasted_iota(jnp.int32, (LANES, tq), 0)

    def scores(qi, s_scr):
        qoff = pl.multiple_of(qi * tq, tq)
        qt = qt_scr[:, pl.ds(qoff, tq)]
        zero = jnp.zeros_like(qt)
        qm = jnp.concatenate([jnp.where(row < DA_QK, qt, zero), jnp.where(row >= DA_QK, qt, zero)], axis=1)
        mx = None
        for j in range(nk):
            st = jnp.dot(k_ref[0, j * tk:(j + 1) * tk, :], qm, preferred_element_type=F32)
            s_scr[j * tk:(j + 1) * tk, :] = st
            cm = jnp.max(st, axis=0, keepdims=True)
            mx = cm if mx is None else jnp.maximum(mx, cm)
        return mx

    def finish(qi, s_scr, m):
        l = jnp.zeros((1, 2 * tq), F32)
        acc = jnp.zeros((DA_V, 2 * tq), F32)
        for j in range(nk):
            p = jnp.exp2(s_scr[j * tk:(j + 1) * tk, :] - m)
            l = l + jnp.sum(p, axis=0, keepdims=True)
            acc = acc + jnp.dot(vt_scr[:, j * tk:(j + 1) * tk], p.astype(BF16), preferred_element_type=F32)
        o = acc * (1.0 / l)
        ot = o[:, :tq] - lam * o[:, tq:]
        ot = ot * lax.rsqrt(jnp.mean(ot * ot, axis=0, keepdims=True) + EPS) * (1.0 - lam_init)
        qoff = pl.multiple_of(qi * tq, tq)
        o_ref[0, pl.ds(qoff, tq), :] = (ot.T * gsub_ref[...]).astype(BF16)

    def body(i, m_a):
        m_b = scores(2 * i + 1, sb_scr)
        finish(2 * i, sa_scr, m_a)
        m_a_next = scores(jnp.minimum(2 * i + 2, nq - 1), sa_scr)
        finish(2 * i + 1, sb_scr, m_b)
        return m_a_next

    lax.fori_loop(0, nq // 2, body, scores(0, sa_scr))


def _diffattn(proj3, lam4, g_sub, lam_init, tq=128, tk=256):
    b, seq, _ = proj3.shape
    kern = functools.partial(_diffattn_kernel, seq=seq, tq=tq, tk=tk, lam_init=lam_init)
    return pl.pallas_call(
        kern,
        grid=(b, HA),
        in_specs=[
            pl.BlockSpec((4, DA_QK), lambda i, h: (0, 0)),
            pl.BlockSpec((1, DA_V), lambda i, h: (0, 0)),
            pl.BlockSpec((1, seq, LANES), lambda i, h: (i, 0, h)),
            pl.BlockSpec((1, seq, LANES), lambda i, h: (i, 0, HA + h)),
            pl.BlockSpec((1, seq, LANES), lambda i, h: (i, 0, 2 * HA + h)),
        ],
        out_specs=pl.BlockSpec((1, seq, LANES), lambda i, h: (i, 0, h)),
        out_shape=jax.ShapeDtypeStruct((b, seq, HA * DA_V), BF16),
        scratch_shapes=[pltpu.VMEM((LANES, seq), BF16), pltpu.VMEM((LANES, seq), BF16),
                        pltpu.VMEM((seq, 2 * tq), F32), pltpu.VMEM((seq, 2 * tq), F32)],
        compiler_params=_cparams(("parallel", "parallel")),
        name="diffattn",
    )(lam4, g_sub, proj3, proj3, proj3)


def _natten_kernel(bias_ref, q_ref, k_ref, v_ref, o_ref, *, rows):
    kr = min(WIN_R, rows)
    nkeys = kr * GRID_W
    hpg = LANES // DB_H
    lane_head = lax.broadcasted_iota(jnp.int32, (GRID_W, LANES), 1) // DB_H

    def row_step(r, carry):
        rs = jnp.clip(r - kr // 2, 0, rows - kr)
        base = rs - r + (WIN_R - 1)
        qoff = pl.multiple_of(r * GRID_W, GRID_W)
        koff = pl.multiple_of(rs * GRID_W, GRID_W)
        qr = q_ref[0, pl.ds(qoff, GRID_W), :]
        zero = jnp.zeros_like(qr)
        qs = jnp.concatenate([jnp.where(lane_head == h, qr, zero) for h in range(hpg)], axis=0)
        kw = k_ref[0, pl.ds(koff, nkeys), :]
        vw = v_ref[0, pl.ds(koff, nkeys), :]
        s = lax.dot_general(qs, kw, (((1,), (1,)), ((), ())), preferred_element_type=F32)
        bias = jnp.concatenate([bias_ref[0, base + 2 * i] for i in range(kr // 2)], axis=1)
        s = s + bias
        m = jnp.max(s, axis=1, keepdims=True)
        p = jnp.exp(s - m).astype(BF16)
        pv = jnp.dot(p, jnp.concatenate([vw, jnp.ones_like(vw)], axis=1), preferred_element_type=F32)
        o = pv[:, :LANES] * (1.0 / pv[:, LANES:])
        out = jnp.zeros((GRID_W, LANES), F32)
        for h in range(hpg):
            out = out + jnp.where(lane_head == h, o[h * GRID_W:(h + 1) * GRID_W, :], 0.0)
        o_ref[0, pl.ds(qoff, GRID_W), :] = out.astype(BF16)
        return carry

    lax.fori_loop(0, rows, row_step, 0, unroll=16)


def _natten(proj3, bias_tab):
    b, seq, _ = proj3.shape
    rows = seq // GRID_W
    ng = HB * DB_H // LANES
    nro = bias_tab.shape[1]
    kern = functools.partial(_natten_kernel, rows=rows)
    cb = 3 * HA
    return pl.pallas_call(
        kern,
        grid=(b, ng),
        in_specs=[
            pl.BlockSpec((1, nro, bias_tab.shape[2], LANES), lambda i, g: (g, 0, 0, 0)),
            pl.BlockSpec((1, seq, LANES), lambda i, g: (i, 0, cb + g)),
            pl.BlockSpec((1, seq, LANES), lambda i, g: (i, 0, cb + ng + g)),
            pl.BlockSpec((1, seq, LANES), lambda i, g: (i, 0, cb + 2 * ng + g)),
        ],
        out_specs=pl.BlockSpec((1, seq, LANES), lambda i, g: (i, 0, g)),
        out_shape=jax.ShapeDtypeStruct((b, seq, HB * DB_H), BF16),
        compiler_params=_cparams(("parallel", "parallel")),
        name="natten",
    )(bias_tab, proj3, proj3, proj3)


def _natten_bias_table(rpb):
    q = np.arange(GRID_W)[:, None]
    kc = np.arange(GRID_W)[None, :]
    cs = np.clip(q - WIN_C // 2, 0, GRID_W - WIN_C)
    valid = (kc >= cs) & (kc < cs + WIN_C)
    col_off = np.clip(kc - q + (WIN_C - 1), 0, 2 * WIN_C - 2)
    ncol = 2 * WIN_C - 1
    nro = 2 * WIN_R - 2
    pick = np.zeros((2, ncol, GRID_W, 2 * GRID_W), np.float32)
    for j in range(2):
        pick[j, :, :, j * GRID_W:(j + 1) * GRID_W] = (np.arange(ncol)[:, None, None] == col_off[None]) & valid[None]
    negmask = np.tile(np.where(valid, 0.0, NEG).astype(np.float32), (1, 2))
    hpg = LANES // DB_H
    r4 = rpb.astype(F32).reshape(HB // hpg, hpg, 2 * WIN_R - 1, ncol)
    rp = jnp.stack([r4[:, :, :nro], r4[:, :, 1:]], axis=0)
    tab = jnp.einsum('jghrc,jcqk->grhqk', rp, jnp.asarray(pick), precision=lax.Precision.HIGHEST) + negmask
    return tab.reshape(HB // hpg, nro, hpg * GRID_W, LANES)


def _outproj_kernel(x_ref, oa_ref, ob_ref, wo_ref, gffn_ref, wr_ref, tri_ref,
                    h_ref, xn_ref, meta_ref, cnt_ref, run_scr):
    tm = x_ref.shape[0]
    da = oa_ref.shape[1]

    @pl.when(pl.program_id(0) == 0)
    def _():
        run_scr[...] = jnp.zeros_like(run_scr)

    attn = jnp.dot(oa_ref[...], wo_ref[:da, :], preferred_element_type=F32)
    attn = attn + jnp.dot(ob_ref[...], wo_ref[da:, :], preferred_element_type=F32)
    h = x_ref[...] + attn
    h_ref[...] = h
    xn = _rms_rows(h) * gffn_ref[...]
    _store_token_tiles(xn_ref, xn)

    xh = xn.astype(BF16)
    xl = (xn - xh.astype(F32)).astype(BF16)
    hh_hl = jnp.dot(xh, wr_ref[...], preferred_element_type=F32)
    lg = hh_hl[:, :LANES] + hh_hl[:, LANES:] + jnp.dot(xl, wr_ref[:, :LANES], preferred_element_type=F32)
    lane = lax.broadcasted_iota(jnp.int32, (tm, LANES), 1).astype(F32)
    ninf = -jnp.inf
    is_g = lane < N_GROUPS
    gl = jnp.where(is_g, lg, ninf)
    gmax = jnp.max(gl, axis=1, keepdims=True)
    gidx = jnp.min(jnp.where(gl == gmax, lane, float(LANES)), axis=1, keepdims=True)
    gsum = jnp.sum(jnp.where(is_g, jnp.exp(gl - gmax), 0.0), axis=1, keepdims=True)
    g_gate = 1.0 / gsum
    lo = E_LANE0 + gidx * EXPERTS_PER_GROUP
    is_e = (lane >= lo) & (lane < lo + EXPERTS_PER_GROUP)
    el = jnp.where(is_e, lg, ninf)
    l1 = jnp.max(el, axis=1, keepdims=True)
    i1 = jnp.min(jnp.where(el == l1, lane, float(LANES)), axis=1, keepdims=True)
    el2 = jnp.where(lane == i1, ninf, el)
    l2 = jnp.max(el2, axis=1, keepdims=True)
    i2 = jnp.min(jnp.where(el2 == l2, lane, float(LANES)), axis=1, keepdims=True)
    r = jnp.exp(l2 - l1)
    w1 = g_gate / (1.0 + r)
    w2 = w1 * r

    sel1 = lane == i1
    sel2 = lane == i2
    oh = jnp.where(sel1 | sel2, 1.0, 0.0)
    before = jnp.dot(tri_ref[...], oh.astype(BF16), preferred_element_type=F32) + run_scr[...]
    rank1 = jnp.sum(jnp.where(sel1, before, 0.0), axis=1, keepdims=True)
    rank2 = jnp.sum(jnp.where(sel2, before, 0.0), axis=1, keepdims=True)
    run = run_scr[...] + jnp.sum(oh, axis=0, keepdims=True)
    run_scr[...] = run
    cnt_ref[...] = run

    meta = jnp.where(lane == 0, i1 - E_LANE0, 0.0)
    meta = jnp.where(lane == 1, i2 - E_LANE0, meta)
    meta = jnp.where(lane == 2, rank1, meta)
    meta = jnp.where(lane == 3, rank2, meta)
    meta = jnp.where(lane == 4, w1, meta)
    meta = jnp.where(lane == 5, w2, meta)
    meta_ref[...] = meta


def _outproj(x2, oa2, ob2, wo_bf, g_ffn, w_router, tri, tm):
    t, d = x2.shape
    da = oa2.shape[1]
    const = lambda i: (0, 0)
    row = lambda i: (i, 0)
    return pl.pallas_call(
        _outproj_kernel,
        grid=(t // tm,),
        in_specs=[
            pl.BlockSpec((tm, d), row),
            pl.BlockSpec((tm, da), row),
            pl.BlockSpec((tm, d - da), row),
            pl.BlockSpec((d, d), const),
            pl.BlockSpec((1, d), const),
            pl.BlockSpec((d, 2 * LANES), const),
            pl.BlockSpec((tm, tm), const),
        ],
        out_specs=[
            pl.BlockSpec((tm, d), row),
            pl.BlockSpec((tm * SUBLANES, LANES), row),
            pl.BlockSpec((tm, LANES), row),
            pl.BlockSpec((1, LANES), const),
        ],
        out_shape=[
            jax.ShapeDtypeStruct((t, d), F32),
            jax.ShapeDtypeStruct((t * SUBLANES, LANES), F32),
            jax.ShapeDtypeStruct((t, LANES), F32),
            jax.ShapeDtypeStruct((1, LANES), F32),
        ],
        scratch_shapes=[pltpu.VMEM((1, LANES), F32)],
        compiler_params=_cparams(("arbitrary",)),
        name="outproj_router",
    )(x2, oa2, ob2, wo_bf, g_ffn, w_router, tri)


DMA_ISSUE_UNROLL = 8


def _experts_kernel(be_ref, nu_ref, tok_ref, xn_ref, w1_ref, w3_ref, w2_ref, ys_ref,
                    w1_scr, w3_scr, w2_scr, xbuf, sems):
    i = pl.program_id(0)
    n_used = nu_ref[0]

    def gather_block(blk, buf):
        def issue(r, c):
            tok = tok_ref[blk * MOE_BLOCK + r]
            src = xn_ref.at[pl.ds(pl.multiple_of(tok * SUBLANES, SUBLANES), SUBLANES), :]
            dst = xbuf.at[buf, pl.ds(pl.multiple_of(r * SUBLANES, SUBLANES), SUBLANES), :]
            pltpu.make_async_copy(src, dst, sems.at[buf]).start()
            return c
        lax.fori_loop(0, MOE_BLOCK, issue, 0, unroll=DMA_ISSUE_UNROLL)

    @pl.when((i == 0) & (n_used > 0))
    def _():
        gather_block(0, 0)

    @pl.when(i + 1 < n_used)
    def _():
        gather_block(i + 1, (i + 1) % 2)

    @pl.when(i < n_used)
    def _():
        prev = be_ref[jnp.maximum(i - 1, 0)]
        new_expert = (i == 0) | (be_ref[i] != prev)

        @pl.when(new_expert)
        def _():
            w1_scr[...] = w1_ref[0].astype(BF16)
            w3_scr[...] = w3_ref[0].astype(BF16)
            w2_scr[...] = w2_ref[0].astype(BF16)

        buf = i % 2
        pltpu.make_async_copy(xn_ref.at[pl.ds(0, MOE_BLOCK * SUBLANES), :], xbuf.at[buf], sems.at[buf]).wait()
        x = _load_token_tiles(xbuf.at[buf], MOE_BLOCK).astype(BF16)
        a = jnp.dot(x, w1_scr[...], preferred_element_type=F32)
        b = jnp.dot(x, w3_scr[...], preferred_element_type=F32)
        hdn = a * (1.0 / (1.0 + jnp.exp(-a))) * b
        _store_token_tiles(ys_ref, jnp.dot(hdn.astype(BF16), w2_scr[...], preferred_element_type=F32))

    @pl.when(i >= nu_ref[0])
    def _():
        ys_ref[...] = jnp.zeros_like(ys_ref)


def _experts(xn_tiles, block_e, n_used, row_tok, w1, w3, w2):
    p_len = row_tok.shape[0]
    n_blocks = p_len // MOE_BLOCK
    d, de = w1.shape[1], w1.shape[2]
    wsel = lambda i, be, nu, tok: (be[i], 0, 0)
    grid_spec = pltpu.PrefetchScalarGridSpec(
        num_scalar_prefetch=3,
        grid=(n_blocks,),
        in_specs=[
            pl.BlockSpec(memory_space=pl.ANY),
            pl.BlockSpec((1, d, de), wsel),
            pl.BlockSpec((1, d, de), wsel),
            pl.BlockSpec((1, de, d), wsel),
        ],
        out_specs=pl.BlockSpec((MOE_BLOCK * SUBLANES, LANES), lambda i, be, nu, tok: (i, 0)),
        scratch_shapes=[pltpu.VMEM((d, de), BF16), pltpu.VMEM((d, de), BF16), pltpu.VMEM((de, d), BF16),
                        pltpu.VMEM((2, MOE_BLOCK * SUBLANES, LANES), F32), pltpu.SemaphoreType.DMA((2,))],
    )
    return pl.pallas_call(
        _experts_kernel,
        grid_spec=grid_spec,
        out_shape=jax.ShapeDtypeStruct((p_len * SUBLANES, LANES), F32),
        compiler_params=_cparams(("arbitrary",)),
        name="experts",
    )(block_e, n_used, row_tok, xn_tiles, w1, w3, w2)


def _final_kernel(dest_ref, h_ref, meta_ref, p_ref, wple_ref, gple_ref, gplg_ref, wplg_ref, ys_ref,
                  out_ref, gbuf, sems):
    tm = h_ref.shape[0]
    step = pl.program_id(0)
    n_steps = pl.num_programs(0)
    t_total = n_steps * tm

    def issue_tile(tile, buf):
        def issue(r, c):
            roff = pl.multiple_of(r * SUBLANES, SUBLANES)
            for slot in range(TOP_K):
                d = dest_ref[slot * t_total + tile * tm + r]
                src = ys_ref.at[pl.ds(pl.multiple_of(d * SUBLANES, SUBLANES), SUBLANES), :]
                pltpu.make_async_copy(src, gbuf.at[buf, slot, pl.ds(roff, SUBLANES), :], sems.at[buf]).start()
            return c
        lax.fori_loop(0, tm, issue, 0, unroll=DMA_ISSUE_UNROLL)

    @pl.when(step == 0)
    def _():
        issue_tile(0, 0)

    @pl.when(step + 1 < n_steps)
    def _():
        issue_tile(step + 1, (step + 1) % 2)

    emb = jnp.dot(p_ref[...].astype(BF16), wple_ref[...], preferred_element_type=F32)
    emb = _rms_rows(emb) * gple_ref[...]

    buf = step % 2
    for slot in range(TOP_K):
        pltpu.make_async_copy(ys_ref.at[pl.ds(0, tm * SUBLANES), :], gbuf.at[buf, slot], sems.at[buf]).wait()

    meta = meta_ref[...]
    y1 = _load_token_tiles(gbuf.at[buf, 0], tm)
    y2 = _load_token_tiles(gbuf.at[buf, 1], tm)
    h = h_ref[...] + meta[:, 4:5] * y1 + meta[:, 5:6] * y2
    hn = (_rms_rows(h) * gplg_ref[...]).astype(BF16)
    z = jnp.dot(hn, wplg_ref[...], preferred_element_type=F32)
    out_ref[...] = h + emb * (1.0 / (1.0 + jnp.exp(-z)))


def _final(dest_flat, h2, meta, p2, wple_bf, g_ple, g_plg, wplg_bf, ys, tm):
    t, d = h2.shape
    dp = p2.shape[1]
    const = lambda i, dest: (0, 0)
    row = lambda i, dest: (i, 0)
    grid_spec = pltpu.PrefetchScalarGridSpec(
        num_scalar_prefetch=1,
        grid=(t // tm,),
        in_specs=[
            pl.BlockSpec((tm, d), row),
            pl.BlockSpec((tm, LANES), row),
            pl.BlockSpec((tm, dp), row),
            pl.BlockSpec((dp, d), const),
            pl.BlockSpec((1, d), const),
            pl.BlockSpec((1, d), const),
            pl.BlockSpec((d, d), const),
            pl.BlockSpec(memory_space=pl.ANY),
        ],
        out_specs=pl.BlockSpec((tm, d), row),
        scratch_shapes=[pltpu.VMEM((2, TOP_K, tm * SUBLANES, LANES), F32), pltpu.SemaphoreType.DMA((2,))],
    )
    return pl.pallas_call(
        _final_kernel,
        grid_spec=grid_spec,
        out_shape=jax.ShapeDtypeStruct((t, d), F32),
        compiler_params=_cparams(("arbitrary",)),
        name="combine_ple",
    )(dest_flat, h2, meta, p2, wple_bf, g_ple, g_plg, wplg_bf, ys)


def _block_diag_mean(seg):
    m = np.kron(np.eye(512 // seg), np.full((seg, seg), 1.0 / seg))
    return jnp.asarray(m, BF16)


def _rope_tables(seq):
    inv = ROPE_THETA ** (-jnp.arange(0, DA_QK, 2, dtype=F32) / DA_QK)
    ang = jnp.arange(seq, dtype=F32)[:, None] * inv[None, :]
    cos, sin = jnp.cos(ang), jnp.sin(ang)
    cos_t = jnp.tile(cos, (1, LANES // cos.shape[1]))
    sin_t = jnp.tile(jnp.concatenate([-sin, sin], axis=1), (1, LANES // (2 * sin.shape[1])))
    return cos_t, sin_t


def _tile_for(n, pref):
    while n % pref:
        pref //= 2
    return pref


def kernel(x, p, g_mix, w_in, g_qa, g_ka, lam_q1, lam_k1, lam_q2, lam_k2, g_sub, g_qb, g_kb, rpb, w_out,
           g_ffn, w_rg, w_re, w1, w3, w2, g_plg, w_plg, w_ple, g_ple):
    b, s, d = x.shape
    assert d == SUBLANES * LANES, "token-tile layout holds one token per (8, 128) f32 tile"
    t = b * s
    depth = w_in.shape[0]
    tm = _tile_for(s, 512)
    tmf = _tile_for(t, 256)
    n_slots = t * TOP_K
    n_blocks = n_slots // MOE_BLOCK + N_EXPERTS
    p_len = n_blocks * MOE_BLOCK

    bd64 = _block_diag_mean(DA_QK)
    bd32 = _block_diag_mean(DB_H)
    cos_t, sin_t = _rope_tables(s)
    tri = jnp.asarray(np.tril(np.ones((tm, tm)), -1), BF16)

    h = x.reshape(t, d)
    for i in range(depth):
        lam_init = 0.8 - 0.6 * math.exp(-0.3 * i)
        gsec = jnp.stack([
            jnp.tile(g_qa[i], 512 // DA_QK) * (DA_QK ** -0.5 * math.log2(math.e)),
            jnp.tile(g_ka[i], 512 // DA_QK),
            jnp.tile(g_qb[i], 512 // DB_H) * DB_H ** -0.5,
            jnp.tile(g_kb[i], 512 // DB_H),
        ]).astype(F32)
        proj = _inproj(h, g_mix[i][None, :], w_in[i].astype(BF16), bd64, bd32, gsec, cos_t, sin_t, s, tm)
        proj3 = proj.reshape(b, s, proj.shape[1])

        lam4 = jnp.stack([lam_q1[i], lam_k1[i], lam_q2[i], lam_k2[i]]).astype(F32)
        oa = _diffattn(proj3, lam4, g_sub[i][None, :].astype(F32), lam_init)
        ob = _natten(proj3, _natten_bias_table(rpb[i]))

        w_router = jnp.concatenate(
            [w_rg[i], w_re[i], jnp.zeros((d, LANES - N_GROUPS - N_EXPERTS), F32)], axis=1).astype(F32)
        wr_hi = w_router.astype(BF16)
        wr_lo = (w_router - wr_hi.astype(F32)).astype(BF16)
        h1, xn, meta, cnt = _outproj(h, oa.reshape(t, -1), ob.reshape(t, -1), w_out[i].astype(BF16),
                                     g_ffn[i][None, :], jnp.concatenate([wr_hi, wr_lo], axis=1), tri, tm)

        counts = cnt[0, E_LANE0:E_LANE0 + N_EXPERTS].astype(jnp.int32)
        padded = (counts + MOE_BLOCK - 1) // MOE_BLOCK * MOE_BLOCK
        pad_end = jnp.cumsum(padded)
        pad_start = pad_end - padded
        mt = meta[:, 0:2 * TOP_K].T.astype(jnp.int32)
        e_ids = jnp.arange(N_EXPERTS, dtype=jnp.int32)[None, :, None]
        start_of = jnp.sum(jnp.where(mt[:TOP_K, None, :] == e_ids, pad_start[None, :, None], 0), axis=1)
        dest = start_of + mt[TOP_K:]
        blk_start = jnp.arange(n_blocks, dtype=jnp.int32) * MOE_BLOCK
        block_e = jnp.minimum(jnp.sum(pad_end[None, :] <= blk_start[:, None], axis=1), N_EXPERTS - 1)
        block_e = block_e.astype(jnp.int32)
        n_used = (pad_end[-1:] // MOE_BLOCK).astype(jnp.int32)

        dest_flat = dest.reshape(TOP_K * t)

        tok_ids = jnp.tile(jnp.arange(t, dtype=jnp.int32), TOP_K)
        row_tok = jnp.zeros((p_len,), jnp.int32).at[dest_flat].set(tok_ids, unique_indices=True)
        ys = _experts(xn, block_e, n_used, row_tok, w1[i], w3[i], w2[i])
        h = _final(dest_flat, h1, meta, p[i].reshape(t, -1),
                   w_ple[i].astype(BF16), g_ple[i][None, :], g_plg[i][None, :], w_plg[i].astype(BF16), ys, tmf)
    return h.reshape(b, s, d)
```

```python
import functools
import math

import numpy as np
import jax
import jax.numpy as jnp
from jax import lax
from jax.experimental import pallas as pl
from jax.experimental.pallas import tpu as pltpu

F32 = jnp.float32
BF16 = jnp.bfloat16

LANES = 128
HA = 4
DA_QK = 64
DA_V = 128
HB = 16
DB_H = 32
GRID_W = 64
WIN_R = 8
WIN_C = 16
ROPE_THETA = 10000.0
EPS = 1e-6
NEG = -1e30
N_GROUPS = 4
EXPERTS_PER_GROUP = 8
N_EXPERTS = N_GROUPS * EXPERTS_PER_GROUP
TOP_K = 2
MOE_BLOCK = 256
E_LANE0 = N_GROUPS

VMEM_LIMIT = 56 * 1024 * 1024


def _cparams(sem):
    return pltpu.CompilerParams(dimension_semantics=sem, vmem_limit_bytes=VMEM_LIMIT)


def _rms_rows(x):
    return x * lax.rsqrt(jnp.mean(x * x, axis=-1, keepdims=True) + EPS)


SUBLANES = 8

def _store_token_tiles(ref, val, first=0):
    n = val.shape[0]
    for c in range(val.shape[1] // LANES):
        ref[pl.ds(first * SUBLANES + c, n, stride=SUBLANES), :] = val[:, c * LANES:(c + 1) * LANES]


def _load_token_tiles(ref, n, first=0):
    return jnp.concatenate(
        [ref[pl.ds(first * SUBLANES + c, n, stride=SUBLANES), :] for c in range(SUBLANES)], axis=1)


def _inproj_kernel(x_ref, gmix_ref, w_ref, bd64_ref, bd32_ref, gsec_ref, cos_ref, sin_ref, out_ref):
    tm = x_ref.shape[0]
    a = (_rms_rows(x_ref[...]) * gmix_ref[...]).astype(BF16)
    lane = lax.broadcasted_iota(jnp.int32, (tm, LANES), 1)
    first_half = (lane % 64) < 32
    cos = cos_ref[...]
    sin = sin_ref[...]
    for sec in range(6):
        y = jnp.dot(a, w_ref[:, sec * 512:(sec + 1) * 512], preferred_element_type=F32)
        if sec in (0, 1, 3, 4):
            bd = bd64_ref if sec < 2 else bd32_ref
            gi = {0: 0, 1: 1, 3: 2, 4: 3}[sec]
            ms = jnp.dot((y * y).astype(BF16), bd[...], preferred_element_type=F32)
            y = y * lax.rsqrt(ms + EPS) * gsec_ref[gi:gi + 1, :]
        for c in range(4):
            yc = y[:, c * LANES:(c + 1) * LANES]
            if sec < 2:
                rot = jnp.where(first_half, pltpu.roll(yc, 96, 1), pltpu.roll(yc, 32, 1))
                yc = yc * cos + rot * sin
            out_ref[:, sec * 512 + c * LANES: sec * 512 + (c + 1) * LANES] = yc.astype(BF16)


def _inproj(x2, g_mix, w_bf, bd64, bd32, gsec, cos_t, sin_t, seq, tm):
    t, d = x2.shape
    n = w_bf.shape[1]
    nsb = seq // tm
    const = lambda i: (0, 0)
    return pl.pallas_call(
        _inproj_kernel,
        grid=(t // tm,),
        in_specs=[
            pl.BlockSpec((tm, d), lambda i: (i, 0)),
            pl.BlockSpec((1, d), const),
            pl.BlockSpec((d, n), const),
            pl.BlockSpec((512, 512), const),
            pl.BlockSpec((512, 512), const),
            pl.BlockSpec((4, 512), const),
            pl.BlockSpec((tm, LANES), lambda i: (i % nsb, 0)),
            pl.BlockSpec((tm, LANES), lambda i: (i % nsb, 0)),
        ],
        out_specs=pl.BlockSpec((tm, n), lambda i: (i, 0)),
        out_shape=jax.ShapeDtypeStruct((t, n), BF16),
        compiler_params=_cparams(("parallel",)),
        name="inproj",
    )(x2, g_mix, w_bf, bd64, bd32, gsec, cos_t, sin_t)


def _diffattn_kernel(lam_ref, gsub_ref, q_ref, k_ref, v_ref, o_ref, qt_scr, vt_scr, sa_scr, sb_scr, *, seq, tq, tk,
                     lam_init):
    nq = seq // tq
    nk = seq // tk
    s1 = jnp.sum(lam_ref[0:1, :] * lam_ref[1:2, :], axis=1, keepdims=True)
    s2 = jnp.sum(lam_ref[2:3, :] * lam_ref[3:4, :], axis=1, keepdims=True)
    lam = jnp.exp(s1) - jnp.exp(s2) + lam_init

    def tr_body(c, carry):
        off = pl.multiple_of(c * LANES, LANES)
        qt_scr[:, pl.ds(off, LANES)] = q_ref[0, pl.ds(off, LANES), :].astype(F32).T.astype(BF16)
        vt_scr[:, pl.ds(off, LANES)] = v_ref[0, pl.ds(off, LANES), :].astype(F32).T.astype(BF16)
        return carry
    lax.fori_loop(0, seq // LANES, tr_body, 0)

    row = lax.broadcasted_iota(jnp.int32, (LANES, tq), 0)

    def scores(qi, s_scr):
        qoff = pl.multiple_of(qi * tq, tq)
        qt = qt_scr[:, pl.ds(qoff, tq)]
        zero = jnp.zeros_like(qt)
        qm = jnp.concatenate([jnp.where(row < DA_QK, qt, zero), jnp.where(row >= DA_QK, qt, zero)], axis=1)
        mx = None
        for j in range(nk):
            st = jnp.dot(k_ref[0, j * tk:(j + 1) * tk, :], qm, preferred_element_type=F32)
            s_scr[j * tk:(j + 1) * tk, :] = st
            cm = jnp.max(st, axis=0, keepdims=True)
            mx = cm if mx is None else jnp.maximum(mx, cm)
        return mx

    def finish(qi, s_scr, m):
        l = jnp.zeros((1, 2 * tq), F32)
        acc = jnp.zeros((DA_V, 2 * tq), F32)
        for j in range(nk):
            p = jnp.exp2(s_scr[j * tk:(j + 1) * tk, :] - m)
            l = l + jnp.sum(p, axis=0, keepdims=True)
            acc = acc + jnp.dot(vt_scr[:, j * tk:(j + 1) * tk], p.astype(BF16), preferred_element_type=F32)
        o = acc * (1.0 / l)
        ot = o[:, :tq] - lam * o[:, tq:]
        ot = ot * lax.rsqrt(jnp.mean(ot * ot, axis=0, keepdims=True) + EPS) * (1.0 - lam_init)
        qoff = pl.multiple_of(qi * tq, tq)
        o_ref[0, pl.ds(qoff, tq), :] = (ot.T * gsub_ref[...]).astype(BF16)

    def body(i, m_a):
        m_b = scores(2 * i + 1, sb_scr)
        finish(2 * i, sa_scr, m_a)
        m_a_next = scores(jnp.minimum(2 * i + 2, nq - 1), sa_scr)
        finish(2 * i + 1, sb_scr, m_b)
        return m_a_next

    lax.fori_loop(0, nq // 2, body, scores(0, sa_scr))


def _diffattn(proj3, lam4, g_sub, lam_init, tq=128, tk=256):
    b, seq, _ = proj3.shape
    kern = functools.partial(_diffattn_kernel, seq=seq, tq=tq, tk=tk, lam_init=lam_init)
    return pl.pallas_call(
        kern,
        grid=(b, HA),
        in_specs=[
            pl.BlockSpec((4, DA_QK), lambda i, h: (0, 0)),
            pl.BlockSpec((1, DA_V), lambda i, h: (0, 0)),
            pl.BlockSpec((1, seq, LANES), lambda i, h: (i, 0, h)),
            pl.BlockSpec((1, seq, LANES), lambda i, h: (i, 0, HA + h)),
            pl.BlockSpec((1, seq, LANES), lambda i, h: (i, 0, 2 * HA + h)),
        ],
        out_specs=pl.BlockSpec((1, seq, LANES), lambda i, h: (i, 0, h)),
        out_shape=jax.ShapeDtypeStruct((b, seq, HA * DA_V), BF16),
        scratch_shapes=[pltpu.VMEM((LANES, seq), BF16), pltpu.VMEM((LANES, seq), BF16),
                        pltpu.VMEM((seq, 2 * tq), F32), pltpu.VMEM((seq, 2 * tq), F32)],
        compiler_params=_cparams(("parallel", "parallel")),
        name="diffattn",
    )(lam4, g_sub, proj3, proj3, proj3)


def _natten_kernel(bias_ref, q_ref, k_ref, v_ref, o_ref, *, rows):
    kr = min(WIN_R, rows)
    nkeys = kr * GRID_W
    hpg = LANES // DB_H
    lane_head = lax.broadcasted_iota(jnp.int32, (GRID_W, LANES), 1) // DB_H

    def row_step(r, carry):
        rs = jnp.clip(r - kr // 2, 0, rows - kr)
        base = rs - r + (WIN_R - 1)
        qoff = pl.multiple_of(r * GRID_W, GRID_W)
        koff = pl.multiple_of(rs * GRID_W, GRID_W)
        qr = q_ref[0, pl.ds(qoff, GRID_W), :]
        zero = jnp.zeros_like(qr)
        qs = jnp.concatenate([jnp.where(lane_head == h, qr, zero) for h in range(hpg)], axis=0)
        kw = k_ref[0, pl.ds(koff, nkeys), :]
        vw = v_ref[0, pl.ds(koff, nkeys), :]
        s = lax.dot_general(qs, kw, (((1,), (1,)), ((), ())), preferred_element_type=F32)
        bias = jnp.concatenate([bias_ref[0, base + 2 * i] for i in range(kr // 2)], axis=1)
        s = s + bias
        m = jnp.max(s, axis=1, keepdims=True)
        p = jnp.exp(s - m).astype(BF16)
        pv = jnp.dot(p, jnp.concatenate([vw, jnp.ones_like(vw)], axis=1), preferred_element_type=F32)
        o = pv[:, :LANES] * (1.0 / pv[:, LANES:])
        out = jnp.zeros((GRID_W, LANES), F32)
        for h in range(hpg):
            out = out + jnp.where(lane_head == h, o[h * GRID_W:(h + 1) * GRID_W, :], 0.0)
        o_ref[0, pl.ds(qoff, GRID_W), :] = out.astype(BF16)
        return carry

    lax.fori_loop(0, rows, row_step, 0, unroll=16)


def _natten(proj3, bias_tab):
    b, seq, _ = proj3.shape
    rows = seq // GRID_W
    ng = HB * DB_H // LANES
    nro = bias_tab.shape[1]
    kern = functools.partial(_natten_kernel, rows=rows)
    cb = 3 * HA
    return pl.pallas_call(
        kern,
        grid=(b, ng),
        in_specs=[
            pl.BlockSpec((1, nro, bias_tab.shape[2], LANES), lambda i, g: (g, 0, 0, 0)),
            pl.BlockSpec((1, seq, LANES), lambda i, g: (i, 0, cb + g)),
            pl.BlockSpec((1, seq, LANES), lambda i, g: (i, 0, cb + ng + g)),
            pl.BlockSpec((1, seq, LANES), lambda i, g: (i, 0, cb + 2 * ng + g)),
        ],
        out_specs=pl.BlockSpec((1, seq, LANES), lambda i, g: (i, 0, g)),
        out_shape=jax.ShapeDtypeStruct((b, seq, HB * DB_H), BF16),
        compiler_params=_cparams(("parallel", "parallel")),
        name="natten",
    )(bias_tab, proj3, proj3, proj3)


def _natten_bias_table(rpb):
    q = np.arange(GRID_W)[:, None]
    kc = np.arange(GRID_W)[None, :]
    cs = np.clip(q - WIN_C // 2, 0, GRID_W - WIN_C)
    valid = (kc >= cs) & (kc < cs + WIN_C)
    col_off = np.clip(kc - q + (WIN_C - 1), 0, 2 * WIN_C - 2)
    ncol = 2 * WIN_C - 1
    nro = 2 * WIN_R - 2
    pick = np.zeros((2, ncol, GRID_W, 2 * GRID_W), np.float32)
    for j in range(2):
        pick[j, :, :, j * GRID_W:(j + 1) * GRID_W] = (np.arange(ncol)[:, None, None] == col_off[None]) & valid[None]
    negmask = np.tile(np.where(valid, 0.0, NEG).astype(np.float32), (1, 2))
    hpg = LANES // DB_H
    r4 = rpb.astype(F32).reshape(HB // hpg, hpg, 2 * WIN_R - 1, ncol)
    rp = jnp.stack([r4[:, :, :nro], r4[:, :, 1:]], axis=0)
    tab = jnp.einsum('jghrc,jcqk->grhqk', rp, jnp.asarray(pick), precision=lax.Precision.HIGHEST) + negmask
    return tab.reshape(HB // hpg, nro, hpg * GRID_W, LANES)


def _outproj_kernel(x_ref, oa_ref, ob_ref, wo_ref, gffn_ref, wr_ref, tri_ref,
                    h_ref, xn_ref, meta_ref, cnt_ref, run_scr):
    tm = x_ref.shape[0]
    da = oa_ref.shape[1]

    @pl.when(pl.program_id(0) == 0)
    def _():
        run_scr[...] = jnp.zeros_like(run_scr)

    attn = jnp.dot(oa_ref[...], wo_ref[:da, :], preferred_element_type=F32)
    attn = attn + jnp.dot(ob_ref[...], wo_ref[da:, :], preferred_element_type=F32)
    h = x_ref[...] + attn
    h_ref[...] = h
    xn = _rms_rows(h) * gffn_ref[...]
    _store_token_tiles(xn_ref, xn)

    xh = xn.astype(BF16)
    xl = (xn - xh.astype(F32)).astype(BF16)
    hh_hl = jnp.dot(xh, wr_ref[...], preferred_element_type=F32)
    lg = hh_hl[:, :LANES] + hh_hl[:, LANES:] + jnp.dot(xl, wr_ref[:, :LANES], preferred_element_type=F32)
    lane = lax.broadcasted_iota(jnp.int32, (tm, LANES), 1).astype(F32)
    ninf = -jnp.inf
    is_g = lane < N_GROUPS
    gl = jnp.where(is_g, lg, ninf)
    gmax = jnp.max(gl, axis=1, keepdims=True)
    gidx = jnp.min(jnp.where(gl == gmax, lane, float(LANES)), axis=1, keepdims=True)
    gsum = jnp.sum(jnp.where(is_g, jnp.exp(gl - gmax), 0.0), axis=1, keepdims=True)
    g_gate = 1.0 / gsum
    lo = E_LANE0 + gidx * EXPERTS_PER_GROUP
    is_e = (lane >= lo) & (lane < lo + EXPERTS_PER_GROUP)
    el = jnp.where(is_e, lg, ninf)
    l1 = jnp.max(el, axis=1, keepdims=True)
    i1 = jnp.min(jnp.where(el == l1, lane, float(LANES)), axis=1, keepdims=True)
    el2 = jnp.where(lane == i1, ninf, el)
    l2 = jnp.max(el2, axis=1, keepdims=True)
    i2 = jnp.min(jnp.where(el2 == l2, lane, float(LANES)), axis=1, keepdims=True)
    r = jnp.exp(l2 - l1)
    w1 = g_gate / (1.0 + r)
    w2 = w1 * r

    sel1 = lane == i1
    sel2 = lane == i2
    oh = jnp.where(sel1 | sel2, 1.0, 0.0)
    before = jnp.dot(tri_ref[...], oh.astype(BF16), preferred_element_type=F32) + run_scr[...]
    rank1 = jnp.sum(jnp.where(sel1, before, 0.0), axis=1, keepdims=True)
    rank2 = jnp.sum(jnp.where(sel2, before, 0.0), axis=1, keepdims=True)
    run = run_scr[...] + jnp.sum(oh, axis=0, keepdims=True)
    run_scr[...] = run
    cnt_ref[...] = run

    meta = jnp.where(lane == 0, i1 - E_LANE0, 0.0)
    meta = jnp.where(lane == 1, i2 - E_LANE0, meta)
    meta = jnp.where(lane == 2, rank1, meta)
    meta = jnp.where(lane == 3, rank2, meta)
    meta = jnp.where(lane == 4, w1, meta)
    meta = jnp.where(lane == 5, w2, meta)
    meta_ref[...] = meta


def _outproj(x2, oa2, ob2, wo_bf, g_ffn, w_router, tri, tm):
    t, d = x2.shape
    da = oa2.shape[1]
    const = lambda i: (0, 0)
    row = lambda i: (i, 0)
    return pl.pallas_call(
        _outproj_kernel,
        grid=(t // tm,),
        in_specs=[
            pl.BlockSpec((tm, d), row),
            pl.BlockSpec((tm, da), row),
            pl.BlockSpec((tm, d - da), row),
            pl.BlockSpec((d, d), const),
            pl.BlockSpec((1, d), const),
            pl.BlockSpec((d, 2 * LANES), const),
            pl.BlockSpec((tm, tm), const),
        ],
        out_specs=[
            pl.BlockSpec((tm, d), row),
            pl.BlockSpec((tm * SUBLANES, LANES), row),
            pl.BlockSpec((tm, LANES), row),
            pl.BlockSpec((1, LANES), const),
        ],
        out_shape=[
            jax.ShapeDtypeStruct((t, d), F32),
            jax.ShapeDtypeStruct((t * SUBLANES, LANES), F32),
            jax.ShapeDtypeStruct((t, LANES), F32),
            jax.ShapeDtypeStruct((1, LANES), F32),
        ],
        scratch_shapes=[pltpu.VMEM((1, LANES), F32)],
        compiler_params=_cparams(("arbitrary",)),
        name="outproj_router",
    )(x2, oa2, ob2, wo_bf, g_ffn, w_router, tri)


DMA_ISSUE_UNROLL = 8


def _dispatch_kernel(dest_ref, pend_ref, nu_ref, xn_ref, xs_ref, zbuf, sem, zsem):
    tm = xn_ref.shape[0] // SUBLANES
    t_total = pl.num_programs(0) * tm
    base = pl.program_id(0) * tm
    n_blocks = xs_ref.shape[0] // (MOE_BLOCK * SUBLANES)

    @pl.when(pl.program_id(0) == 0)
    def _():
        zbuf[...] = jnp.zeros_like(zbuf)

        def zero_block(blk):
            off = pl.multiple_of(blk * (MOE_BLOCK * SUBLANES), MOE_BLOCK * SUBLANES)
            return pltpu.make_async_copy(zbuf, xs_ref.at[pl.ds(off, MOE_BLOCK * SUBLANES), :], zsem)

        def for_each_zeroed_block(fn):
            def per_expert(e, c):
                end = pend_ref[e]
                start = jnp.where(e == 0, 0, pend_ref[jnp.maximum(e - 1, 0)])

                @pl.when(end > start)
                def _():
                    fn(zero_block(end // MOE_BLOCK - 1))
                return c
            lax.fori_loop(0, N_EXPERTS, per_expert, 0)

            def per_tail_block(blk, c):
                fn(zero_block(blk))
                return c
            lax.fori_loop(nu_ref[0], n_blocks, per_tail_block, 0)

        for_each_zeroed_block(lambda cp: cp.start())
        for_each_zeroed_block(lambda cp: cp.wait())

    def issue(r, c):
        src = xn_ref.at[pl.ds(pl.multiple_of(r * SUBLANES, SUBLANES), SUBLANES), :]
        for slot in range(TOP_K):
            d = dest_ref[slot * t_total + base + r]
            dst = xs_ref.at[pl.ds(pl.multiple_of(d * SUBLANES, SUBLANES), SUBLANES), :]
            pltpu.make_async_copy(src, dst, sem).start()
        return c
    lax.fori_loop(0, tm, issue, 0, unroll=DMA_ISSUE_UNROLL)

    for slot in range(TOP_K):
        pltpu.make_async_copy(xn_ref, xs_ref.at[pl.ds(0, tm * SUBLANES), :], sem).wait()


def _dispatch(xn_tiles, dest_flat, pad_end, n_used, p_len, tm):
    t = xn_tiles.shape[0] // SUBLANES
    grid_spec = pltpu.PrefetchScalarGridSpec(
        num_scalar_prefetch=3,
        grid=(t // tm,),
        in_specs=[pl.BlockSpec((tm * SUBLANES, LANES), lambda i, dest, pend, nu: (i, 0))],
        out_specs=pl.BlockSpec(memory_space=pl.ANY),
        scratch_shapes=[pltpu.VMEM((MOE_BLOCK * SUBLANES, LANES), F32),
                        pltpu.SemaphoreType.DMA(()), pltpu.SemaphoreType.DMA(())],
    )
    return pl.pallas_call(
        _dispatch_kernel,
        grid_spec=grid_spec,
        out_shape=jax.ShapeDtypeStruct((p_len * SUBLANES, LANES), F32),
        compiler_params=_cparams(("arbitrary",)),
        name="dispatch",
    )(dest_flat, pad_end, n_used, xn_tiles)


def _experts_kernel(be_ref, nu_ref, xs_ref, w1_ref, w3_ref, w2_ref, ys_ref, w1_scr, w3_scr, w2_scr):
    i = pl.program_id(0)

    @pl.when(i < nu_ref[0])
    def _():
        prev = be_ref[jnp.maximum(i - 1, 0)]
        new_expert = (i == 0) | (be_ref[i] != prev)

        @pl.when(new_expert)
        def _():
            w1_scr[...] = w1_ref[0].astype(BF16)
            w3_scr[...] = w3_ref[0].astype(BF16)
            w2_scr[...] = w2_ref[0].astype(BF16)

        x = _load_token_tiles(xs_ref, MOE_BLOCK).astype(BF16)
        a = jnp.dot(x, w1_scr[...], preferred_element_type=F32)
        b = jnp.dot(x, w3_scr[...], preferred_element_type=F32)
        hdn = a * (1.0 / (1.0 + jnp.exp(-a))) * b
        _store_token_tiles(ys_ref, jnp.dot(hdn.astype(BF16), w2_scr[...], preferred_element_type=F32))

    @pl.when(i >= nu_ref[0])
    def _():
        ys_ref[...] = jnp.zeros_like(ys_ref)


def _experts(xs, block_e, n_used, w1, w3, w2):
    p_len = xs.shape[0] // SUBLANES
    n_blocks = p_len // MOE_BLOCK
    d, de = w1.shape[1], w1.shape[2]
    blk = lambda i, be, nu: (jnp.maximum(jnp.minimum(i, nu[0] - 1), 0), 0)
    wsel = lambda i, be, nu: (be[i], 0, 0)
    grid_spec = pltpu.PrefetchScalarGridSpec(
        num_scalar_prefetch=2,
        grid=(n_blocks,),
        in_specs=[
            pl.BlockSpec((MOE_BLOCK * SUBLANES, LANES), blk),
            pl.BlockSpec((1, d, de), wsel),
            pl.BlockSpec((1, d, de), wsel),
            pl.BlockSpec((1, de, d), wsel),
        ],
        out_specs=pl.BlockSpec((MOE_BLOCK * SUBLANES, LANES), lambda i, be, nu: (i, 0)),
        scratch_shapes=[pltpu.VMEM((d, de), BF16), pltpu.VMEM((d, de), BF16), pltpu.VMEM((de, d), BF16)],
    )
    return pl.pallas_call(
        _experts_kernel,
        grid_spec=grid_spec,
        out_shape=jax.ShapeDtypeStruct((p_len * SUBLANES, LANES), F32),
        compiler_params=_cparams(("arbitrary",)),
        name="experts",
    )(block_e, n_used, xs, w1, w3, w2)


def _final_kernel(dest_ref, h_ref, meta_ref, p_ref, wple_ref, gple_ref, gplg_ref, wplg_ref, ys_ref,
                  out_ref, gbuf, sems):
    tm = h_ref.shape[0]
    step = pl.program_id(0)
    n_steps = pl.num_programs(0)
    t_total = n_steps * tm

    def issue_tile(tile, buf):
        def issue(r, c):
            roff = pl.multiple_of(r * SUBLANES, SUBLANES)
            for slot in range(TOP_K):
                d = dest_ref[slot * t_total + tile * tm + r]
                src = ys_ref.at[pl.ds(pl.multiple_of(d * SUBLANES, SUBLANES), SUBLANES), :]
                pltpu.make_async_copy(src, gbuf.at[buf, slot, pl.ds(roff, SUBLANES), :], sems.at[buf]).start()
            return c
        lax.fori_loop(0, tm, issue, 0, unroll=DMA_ISSUE_UNROLL)

    @pl.when(step == 0)
    def _():
        issue_tile(0, 0)

    @pl.when(step + 1 < n_steps)
    def _():
        issue_tile(step + 1, (step + 1) % 2)

    emb = jnp.dot(p_ref[...].astype(BF16), wple_ref[...], preferred_element_type=F32)
    emb = _rms_rows(emb) * gple_ref[...]

    buf = step % 2
    for slot in range(TOP_K):
        pltpu.make_async_copy(ys_ref.at[pl.ds(0, tm * SUBLANES), :], gbuf.at[buf, slot], sems.at[buf]).wait()

    meta = meta_ref[...]
    y1 = _load_token_tiles(gbuf.at[buf, 0], tm)
    y2 = _load_token_tiles(gbuf.at[buf, 1], tm)
    h = h_ref[...] + meta[:, 4:5] * y1 + meta[:, 5:6] * y2
    hn = (_rms_rows(h) * gplg_ref[...]).astype(BF16)
    z = jnp.dot(hn, wplg_ref[...], preferred_element_type=F32)
    out_ref[...] = h + emb * (1.0 / (1.0 + jnp.exp(-z)))


def _final(dest_flat, h2, meta, p2, wple_bf, g_ple, g_plg, wplg_bf, ys, tm):
    t, d = h2.shape
    dp = p2.shape[1]
    const = lambda i, dest: (0, 0)
    row = lambda i, dest: (i, 0)
    grid_spec = pltpu.PrefetchScalarGridSpec(
        num_scalar_prefetch=1,
        grid=(t // tm,),
        in_specs=[
            pl.BlockSpec((tm, d), row),
            pl.BlockSpec((tm, LANES), row),
            pl.BlockSpec((tm, dp), row),
            pl.BlockSpec((dp, d), const),
            pl.BlockSpec((1, d), const),
            pl.BlockSpec((1, d), const),
            pl.BlockSpec((d, d), const),
            pl.BlockSpec(memory_space=pl.ANY),
        ],
        out_specs=pl.BlockSpec((tm, d), row),
        scratch_shapes=[pltpu.VMEM((2, TOP_K, tm * SUBLANES, LANES), F32), pltpu.SemaphoreType.DMA((2,))],
    )
    return pl.pallas_call(
        _final_kernel,
        grid_spec=grid_spec,
        out_shape=jax.ShapeDtypeStruct((t, d), F32),
        compiler_params=_cparams(("arbitrary",)),
        name="combine_ple",
    )(dest_flat, h2, meta, p2, wple_bf, g_ple, g_plg, wplg_bf, ys)


def _block_diag_mean(seg):
    m = np.kron(np.eye(512 // seg), np.full((seg, seg), 1.0 / seg))
    return jnp.asarray(m, BF16)


def _rope_tables(seq):
    inv = ROPE_THETA ** (-jnp.arange(0, DA_QK, 2, dtype=F32) / DA_QK)
    ang = jnp.arange(seq, dtype=F32)[:, None] * inv[None, :]
    cos, sin = jnp.cos(ang), jnp.sin(ang)
    cos_t = jnp.tile(cos, (1, LANES // cos.shape[1]))
    sin_t = jnp.tile(jnp.concatenate([-sin, sin], axis=1), (1, LANES // (2 * sin.shape[1])))
    return cos_t, sin_t


def _tile_for(n, pref):
    while n % pref:
        pref //= 2
    return pref


def kernel(x, p, g_mix, w_in, g_qa, g_ka, lam_q1, lam_k1, lam_q2, lam_k2, g_sub, g_qb, g_kb, rpb, w_out,
           g_ffn, w_rg, w_re, w1, w3, w2, g_plg, w_plg, w_ple, g_ple):
    b, s, d = x.shape
    assert d == SUBLANES * LANES, "token-tile layout holds one token per (8, 128) f32 tile"
    t = b * s
    depth = w_in.shape[0]
    tm = _tile_for(s, 512)
    tmd = _tile_for(t, 512)
    tmf = _tile_for(t, 256)
    n_slots = t * TOP_K
    n_blocks = n_slots // MOE_BLOCK + N_EXPERTS
    p_len = n_blocks * MOE_BLOCK

    bd64 = _block_diag_mean(DA_QK)
    bd32 = _block_diag_mean(DB_H)
    cos_t, sin_t = _rope_tables(s)
    tri = jnp.asarray(np.tril(np.ones((tm, tm)), -1), BF16)

    h = x.reshape(t, d)
    for i in range(depth):
        lam_init = 0.8 - 0.6 * math.exp(-0.3 * i)
        gsec = jnp.stack([
            jnp.tile(g_qa[i], 512 // DA_QK) * (DA_QK ** -0.5 * math.log2(math.e)),
            jnp.tile(g_ka[i], 512 // DA_QK),
            jnp.tile(g_qb[i], 512 // DB_H) * DB_H ** -0.5,
            jnp.tile(g_kb[i], 512 // DB_H),
        ]).astype(F32)
        proj = _inproj(h, g_mix[i][None, :], w_in[i].astype(BF16), bd64, bd32, gsec, cos_t, sin_t, s, tm)
        proj3 = proj.reshape(b, s, proj.shape[1])

        lam4 = jnp.stack([lam_q1[i], lam_k1[i], lam_q2[i], lam_k2[i]]).astype(F32)
        oa = _diffattn(proj3, lam4, g_sub[i][None, :].astype(F32), lam_init)
        ob = _natten(proj3, _natten_bias_table(rpb[i]))

        w_router = jnp.concatenate(
            [w_rg[i], w_re[i], jnp.zeros((d, LANES - N_GROUPS - N_EXPERTS), F32)], axis=1).astype(F32)
        wr_hi = w_router.astype(BF16)
        wr_lo = (w_router - wr_hi.astype(F32)).astype(BF16)
        h1, xn, meta, cnt = _outproj(h, oa.reshape(t, -1), ob.reshape(t, -1), w_out[i].astype(BF16),
                                     g_ffn[i][None, :], jnp.concatenate([wr_hi, wr_lo], axis=1), tri, tm)

        counts = cnt[0, E_LANE0:E_LANE0 + N_EXPERTS].astype(jnp.int32)
        padded = (counts + MOE_BLOCK - 1) // MOE_BLOCK * MOE_BLOCK
        pad_end = jnp.cumsum(padded)
        pad_start = pad_end - padded
        mt = meta[:, 0:2 * TOP_K].T.astype(jnp.int32)
        e_ids = jnp.arange(N_EXPERTS, dtype=jnp.int32)[None, :, None]
        start_of = jnp.sum(jnp.where(mt[:TOP_K, None, :] == e_ids, pad_start[None, :, None], 0), axis=1)
        dest = start_of + mt[TOP_K:]
        blk_start = jnp.arange(n_blocks, dtype=jnp.int32) * MOE_BLOCK
        block_e = jnp.minimum(jnp.sum(pad_end[None, :] <= blk_start[:, None], axis=1), N_EXPERTS - 1)
        block_e = block_e.astype(jnp.int32)
        n_used = (pad_end[-1:] // MOE_BLOCK).astype(jnp.int32)

        dest_flat = dest.reshape(TOP_K * t)

        xs = _dispatch(xn, dest_flat, pad_end.astype(jnp.int32), n_used, p_len, tmd)
        ys = _experts(xs, block_e, n_used, w1[i], w3[i], w2[i])
        h = _final(dest_flat, h1, meta, p[i].reshape(t, -1),
                   w_ple[i].astype(BF16), g_ple[i][None, :], g_plg[i][None, :], w_plg[i].astype(BF16), ys, tmf)
    return h.reshape(b, s, d)
```

```python
import functools
import math

import numpy as np
import jax
import jax.numpy as jnp
from jax import lax
from jax.experimental import pallas as pl
from jax.experimental.pallas import tpu as pltpu

F32 = jnp.float32
BF16 = jnp.bfloat16

LANES = 128
HA = 4
DA_QK = 64
DA_V = 128
HB = 16
DB_H = 32
GRID_W = 64
WIN_R = 8
WIN_C = 16
ROPE_THETA = 10000.0
EPS = 1e-6
NEG = -1e30
N_GROUPS = 4
EXPERTS_PER_GROUP = 8
N_EXPERTS = N_GROUPS * EXPERTS_PER_GROUP
TOP_K = 2
MOE_BLOCK = 256
E_LANE0 = N_GROUPS

VMEM_LIMIT = 56 * 1024 * 1024


def _cparams(sem):
    return pltpu.CompilerParams(dimension_semantics=sem, vmem_limit_bytes=VMEM_LIMIT)


def _rms_rows(x):
    return x * lax.rsqrt(jnp.mean(x * x, axis=-1, keepdims=True) + EPS)


SUBLANES = 8

def _store_token_tiles(ref, val, first=0):
    n = val.shape[0]
    for c in range(val.shape[1] // LANES):
        ref[pl.ds(first * SUBLANES + c, n, stride=SUBLANES), :] = val[:, c * LANES:(c + 1) * LANES]


def _load_token_tiles(ref, n, first=0):
    return jnp.concatenate(
        [ref[pl.ds(first * SUBLANES + c, n, stride=SUBLANES), :] for c in range(SUBLANES)], axis=1)


def _inproj_kernel(x_ref, gmix_ref, w_ref, bd64_ref, bd32_ref, gsec_ref, cos_ref, sin_ref, out_ref):
    tm = x_ref.shape[0]
    a = (_rms_rows(x_ref[...]) * gmix_ref[...]).astype(BF16)
    lane = lax.broadcasted_iota(jnp.int32, (tm, LANES), 1)
    first_half = (lane % 64) < 32
    cos = cos_ref[...]
    sin = sin_ref[...]
    for sec in range(6):
        y = jnp.dot(a, w_ref[:, sec * 512:(sec + 1) * 512], preferred_element_type=F32)
        if sec in (0, 1, 3, 4):
            bd = bd64_ref if sec < 2 else bd32_ref
            gi = {0: 0, 1: 1, 3: 2, 4: 3}[sec]
            ms = jnp.dot((y * y).astype(BF16), bd[...], preferred_element_type=F32)
            y = y * lax.rsqrt(ms + EPS) * gsec_ref[gi:gi + 1, :]
        for c in range(4):
            yc = y[:, c * LANES:(c + 1) * LANES]
            if sec < 2:
                rot = jnp.where(first_half, pltpu.roll(yc, 96, 1), pltpu.roll(yc, 32, 1))
                yc = yc * cos + rot * sin
            out_ref[:, sec * 512 + c * LANES: sec * 512 + (c + 1) * LANES] = yc.astype(BF16)


def _inproj(x2, g_mix, w_bf, bd64, bd32, gsec, cos_t, sin_t, seq, tm):
    t, d = x2.shape
    n = w_bf.shape[1]
    nsb = seq // tm
    const = lambda i: (0, 0)
    return pl.pallas_call(
        _inproj_kernel,
        grid=(t // tm,),
        in_specs=[
            pl.BlockSpec((tm, d), lambda i: (i, 0)),
            pl.BlockSpec((1, d), const),
            pl.BlockSpec((d, n), const),
            pl.BlockSpec((512, 512), const),
            pl.BlockSpec((512, 512), const),
            pl.BlockSpec((4, 512), const),
            pl.BlockSpec((tm, LANES), lambda i: (i % nsb, 0)),
            pl.BlockSpec((tm, LANES), lambda i: (i % nsb, 0)),
        ],
        out_specs=pl.BlockSpec((tm, n), lambda i: (i, 0)),
        out_shape=jax.ShapeDtypeStruct((t, n), BF16),
        compiler_params=_cparams(("parallel",)),
        name="inproj",
    )(x2, g_mix, w_bf, bd64, bd32, gsec, cos_t, sin_t)


QBLOCK_PAIRS_PER_BODY = 4

def _diffattn_kernel(lam_ref, gsub_ref, q_ref, k_ref, v_ref, o_ref, qt_scr, vt_scr, sa_scr, sb_scr, *, seq, tq, tk,
                     lam_init):
    nq = seq // tq
    nk = seq // tk
    s1 = jnp.sum(lam_ref[0:1, :] * lam_ref[1:2, :], axis=1, keepdims=True)
    s2 = jnp.sum(lam_ref[2:3, :] * lam_ref[3:4, :], axis=1, keepdims=True)
    lam = jnp.exp(s1) - jnp.exp(s2) + lam_init

    def tr_body(c, carry):
        off = pl.multiple_of(c * LANES, LANES)
        qt_scr[:, pl.ds(off, LANES)] = q_ref[0, pl.ds(off, LANES), :].astype(F32).T.astype(BF16)
        vt_scr[:, pl.ds(off, LANES)] = v_ref[0, pl.ds(off, LANES), :].astype(F32).T.astype(BF16)
        return carry
    lax.fori_loop(0, seq // LANES, tr_body, 0)

    row = lax.broadcasted_iota(jnp.int32, (LANES, tq), 0)

    def scores(qi, s_scr):
        qoff = pl.multiple_of(qi * tq, tq)
        qt = qt_scr[:, pl.ds(qoff, tq)]
        zero = jnp.zeros_like(qt)
        qm = jnp.concatenate([jnp.where(row < DA_QK, qt, zero), jnp.where(row >= DA_QK, qt, zero)], axis=1)
        mx = None
        for j in range(nk):
            st = jnp.dot(k_ref[0, j * tk:(j + 1) * tk, :], qm, preferred_element_type=F32)
            s_scr[j * tk:(j + 1) * tk, :] = st
            cm = jnp.max(st, axis=0, keepdims=True)
            mx = cm if mx is None else jnp.maximum(mx, cm)
        return mx

    def finish(qi, s_scr, m):
        l = jnp.zeros((1, 2 * tq), F32)
        acc = jnp.zeros((DA_V, 2 * tq), F32)
        for j in range(nk):
            p = jnp.exp2(s_scr[j * tk:(j + 1) * tk, :] - m)
            l = l + jnp.sum(p, axis=0, keepdims=True)
            acc = acc + jnp.dot(vt_scr[:, j * tk:(j + 1) * tk], p.astype(BF16), preferred_element_type=F32)
        o = acc * (1.0 / l)
        ot = o[:, :tq] - lam * o[:, tq:]
        ot = ot * lax.rsqrt(jnp.mean(ot * ot, axis=0, keepdims=True) + EPS) * (1.0 - lam_init)
        qoff = pl.multiple_of(qi * tq, tq)
        o_ref[0, pl.ds(qoff, tq), :] = (ot.T * gsub_ref[...]).astype(BF16)

    def body(i, m_a):
        for pair in range(QBLOCK_PAIRS_PER_BODY):
            q0 = 2 * (QBLOCK_PAIRS_PER_BODY * i + pair)
            m_b = scores(q0 + 1, sb_scr)
            finish(q0, sa_scr, m_a)
            m_a = scores(jnp.minimum(q0 + 2, nq - 1), sa_scr)
            finish(q0 + 1, sb_scr, m_b)
        return m_a

    lax.fori_loop(0, nq // (2 * QBLOCK_PAIRS_PER_BODY), body, scores(0, sa_scr))


def _diffattn(proj3, lam4, g_sub, lam_init, tq=128, tk=512):
    b, seq, _ = proj3.shape
    kern = functools.partial(_diffattn_kernel, seq=seq, tq=tq, tk=tk, lam_init=lam_init)
    return pl.pallas_call(
        kern,
        grid=(b, HA),
        in_specs=[
            pl.BlockSpec((4, DA_QK), lambda i, h: (0, 0)),
            pl.BlockSpec((1, DA_V), lambda i, h: (0, 0)),
            pl.BlockSpec((1, seq, LANES), lambda i, h: (i, 0, h)),
            pl.BlockSpec((1, seq, LANES), lambda i, h: (i, 0, HA + h)),
            pl.BlockSpec((1, seq, LANES), lambda i, h: (i, 0, 2 * HA + h)),
        ],
        out_specs=pl.BlockSpec((1, seq, LANES), lambda i, h: (i, 0, h)),
        out_shape=jax.ShapeDtypeStruct((b, seq, HA * DA_V), BF16),
        scratch_shapes=[pltpu.VMEM((LANES, seq), BF16), pltpu.VMEM((LANES, seq), BF16),
                        pltpu.VMEM((seq, 2 * tq), F32), pltpu.VMEM((seq, 2 * tq), F32)],
        compiler_params=_cparams(("parallel", "parallel")),
        name="diffattn",
    )(lam4, g_sub, proj3, proj3, proj3)


def _natten_kernel(bias_ref, q_ref, k_ref, v_ref, o_ref, *, rows):
    kr = min(WIN_R, rows)
    nkeys = kr * GRID_W
    hpg = LANES // DB_H
    lane_head = lax.broadcasted_iota(jnp.int32, (GRID_W, LANES), 1) // DB_H

    def row_step(r, carry):
        rs = jnp.clip(r - kr // 2, 0, rows - kr)
        base = rs - r + (WIN_R - 1)
        qoff = pl.multiple_of(r * GRID_W, GRID_W)
        koff = pl.multiple_of(rs * GRID_W, GRID_W)
        qr = q_ref[0, pl.ds(qoff, GRID_W), :]
        zero = jnp.zeros_like(qr)
        qs = jnp.concatenate([jnp.where(lane_head == h, qr, zero) for h in range(hpg)], axis=0)
        kw = k_ref[0, pl.ds(koff, nkeys), :]
        vw = v_ref[0, pl.ds(koff, nkeys), :]
        s = lax.dot_general(qs, kw, (((1,), (1,)), ((), ())), preferred_element_type=F32)
        bias = jnp.concatenate([bias_ref[0, base + 2 * i] for i in range(kr // 2)], axis=1)
        s = s + bias
        m = jnp.max(s, axis=1, keepdims=True)
        p = jnp.exp(s - m).astype(BF16)
        pv = jnp.dot(p, jnp.concatenate([vw, jnp.ones_like(vw)], axis=1), preferred_element_type=F32)
        o = pv[:, :LANES] * (1.0 / pv[:, LANES:])
        out = jnp.zeros((GRID_W, LANES), F32)
        for h in range(hpg):
            out = out + jnp.where(lane_head == h, o[h * GRID_W:(h + 1) * GRID_W, :], 0.0)
        o_ref[0, pl.ds(qoff, GRID_W), :] = out.astype(BF16)
        return carry

    lax.fori_loop(0, rows, row_step, 0, unroll=16)


def _natten(proj3, bias_tab):
    b, seq, _ = proj3.shape
    rows = seq // GRID_W
    ng = HB * DB_H // LANES
    nro = bias_tab.shape[1]
    kern = functools.partial(_natten_kernel, rows=rows)
    cb = 3 * HA
    return pl.pallas_call(
        kern,
        grid=(b, ng),
        in_specs=[
            pl.BlockSpec((1, nro, bias_tab.shape[2], LANES), lambda i, g: (g, 0, 0, 0)),
            pl.BlockSpec((1, seq, LANES), lambda i, g: (i, 0, cb + g)),
            pl.BlockSpec((1, seq, LANES), lambda i, g: (i, 0, cb + ng + g)),
            pl.BlockSpec((1, seq, LANES), lambda i, g: (i, 0, cb + 2 * ng + g)),
        ],
        out_specs=pl.BlockSpec((1, seq, LANES), lambda i, g: (i, 0, g)),
        out_shape=jax.ShapeDtypeStruct((b, seq, HB * DB_H), BF16),
        compiler_params=_cparams(("parallel", "parallel")),
        name="natten",
    )(bias_tab, proj3, proj3, proj3)


def _natten_bias_table(rpb):
    q = np.arange(GRID_W)[:, None]
    kc = np.arange(GRID_W)[None, :]
    cs = np.clip(q - WIN_C // 2, 0, GRID_W - WIN_C)
    valid = (kc >= cs) & (kc < cs + WIN_C)
    col_off = np.clip(kc - q + (WIN_C - 1), 0, 2 * WIN_C - 2)
    ncol = 2 * WIN_C - 1
    nro = 2 * WIN_R - 2
    pick = np.zeros((2, ncol, GRID_W, 2 * GRID_W), np.float32)
    for j in range(2):
        pick[j, :, :, j * GRID_W:(j + 1) * GRID_W] = (np.arange(ncol)[:, None, None] == col_off[None]) & valid[None]
    negmask = np.tile(np.where(valid, 0.0, NEG).astype(np.float32), (1, 2))
    hpg = LANES // DB_H
    r4 = rpb.astype(F32).reshape(HB // hpg, hpg, 2 * WIN_R - 1, ncol)
    rp = jnp.stack([r4[:, :, :nro], r4[:, :, 1:]], axis=0)
    tab = jnp.einsum('jghrc,jcqk->grhqk', rp, jnp.asarray(pick), precision=lax.Precision.HIGHEST) + negmask
    return tab.reshape(HB // hpg, nro, hpg * GRID_W, LANES)


def _outproj_kernel(x_ref, oa_ref, ob_ref, wo_ref, gffn_ref, wr_ref, tri_ref,
                    h_ref, xn_ref, meta_ref, meta_t_ref, cnt_ref, run_scr):
    tm = x_ref.shape[0]
    da = oa_ref.shape[1]

    @pl.when(pl.program_id(0) == 0)
    def _():
        run_scr[...] = jnp.zeros_like(run_scr)

    attn = jnp.dot(oa_ref[...], wo_ref[:da, :], preferred_element_type=F32)
    attn = attn + jnp.dot(ob_ref[...], wo_ref[da:, :], preferred_element_type=F32)
    h = x_ref[...] + attn
    h_ref[...] = h
    xn = _rms_rows(h) * gffn_ref[...]
    _store_token_tiles(xn_ref, xn)

    xh = xn.astype(BF16)
    xl = (xn - xh.astype(F32)).astype(BF16)
    hh_hl = jnp.dot(xh, wr_ref[...], preferred_element_type=F32)
    lg = hh_hl[:, :LANES] + hh_hl[:, LANES:] + jnp.dot(xl, wr_ref[:, :LANES], preferred_element_type=F32)
    lane = lax.broadcasted_iota(jnp.int32, (tm, LANES), 1).astype(F32)
    ninf = -jnp.inf
    is_g = lane < N_GROUPS
    gl = jnp.where(is_g, lg, ninf)
    gmax = jnp.max(gl, axis=1, keepdims=True)
    gidx = jnp.min(jnp.where(gl == gmax, lane, float(LANES)), axis=1, keepdims=True)
    gsum = jnp.sum(jnp.where(is_g, jnp.exp(gl - gmax), 0.0), axis=1, keepdims=True)
    g_gate = 1.0 / gsum
    lo = E_LANE0 + gidx * EXPERTS_PER_GROUP
    is_e = (lane >= lo) & (lane < lo + EXPERTS_PER_GROUP)
    el = jnp.where(is_e, lg, ninf)
    l1 = jnp.max(el, axis=1, keepdims=True)
    i1 = jnp.min(jnp.where(el == l1, lane, float(LANES)), axis=1, keepdims=True)
    el2 = jnp.where(lane == i1, ninf, el)
    l2 = jnp.max(el2, axis=1, keepdims=True)
    i2 = jnp.min(jnp.where(el2 == l2, lane, float(LANES)), axis=1, keepdims=True)
    r = jnp.exp(l2 - l1)
    w1 = g_gate / (1.0 + r)
    w2 = w1 * r

    sel1 = lane == i1
    sel2 = lane == i2
    oh = jnp.where(sel1 | sel2, 1.0, 0.0)
    before = jnp.dot(tri_ref[...], oh.astype(BF16), preferred_element_type=F32) + run_scr[...]
    rank1 = jnp.sum(jnp.where(sel1, before, 0.0), axis=1, keepdims=True)
    rank2 = jnp.sum(jnp.where(sel2, before, 0.0), axis=1, keepdims=True)
    run = run_scr[...] + jnp.sum(oh, axis=0, keepdims=True)
    run_scr[...] = run
    cnt_ref[...] = run

    meta = jnp.where(lane == 0, i1 - E_LANE0, 0.0)
    meta = jnp.where(lane == 1, i2 - E_LANE0, meta)
    meta = jnp.where(lane == 2, rank1, meta)
    meta = jnp.where(lane == 3, rank2, meta)
    meta = jnp.where(lane == 4, w1, meta)
    meta = jnp.where(lane == 5, w2, meta)
    meta_ref[...] = meta
    meta_t_ref[...] = meta.T[:SUBLANES, :]


def _outproj(x2, oa2, ob2, wo_bf, g_ffn, w_router, tri, tm):
    t, d = x2.shape
    da = oa2.shape[1]
    const = lambda i: (0, 0)
    row = lambda i: (i, 0)
    return pl.pallas_call(
        _outproj_kernel,
        grid=(t // tm,),
        in_specs=[
            pl.BlockSpec((tm, d), row),
            pl.BlockSpec((tm, da), row),
            pl.BlockSpec((tm, d - da), row),
            pl.BlockSpec((d, d), const),
            pl.BlockSpec((1, d), const),
            pl.BlockSpec((d, 2 * LANES), const),
            pl.BlockSpec((tm, tm), const),
        ],
        out_specs=[
            pl.BlockSpec((tm, d), row),
            pl.BlockSpec((tm * SUBLANES, LANES), row),
            pl.BlockSpec((tm, LANES), row),
            pl.BlockSpec((SUBLANES, tm), lambda i: (0, i)),
            pl.BlockSpec((1, LANES), const),
        ],
        out_shape=[
            jax.ShapeDtypeStruct((t, d), F32),
            jax.ShapeDtypeStruct((t * SUBLANES, LANES), F32),
            jax.ShapeDtypeStruct((t, LANES), F32),
            jax.ShapeDtypeStruct((SUBLANES, t), F32),
            jax.ShapeDtypeStruct((1, LANES), F32),
        ],
        scratch_shapes=[pltpu.VMEM((1, LANES), F32)],
        compiler_params=_cparams(("arbitrary",)),
        name="outproj_router",
    )(x2, oa2, ob2, wo_bf, g_ffn, w_router, tri)


DMA_ISSUE_UNROLL = 8


def _dispatch_kernel(dest_ref, pend_ref, nu_ref, xn_ref, xs_ref, zbuf, sem, zsem):
    tm = xn_ref.shape[0] // SUBLANES
    t_total = pl.num_programs(0) * tm
    base = pl.program_id(0) * tm
    n_blocks = xs_ref.shape[0] // (MOE_BLOCK * SUBLANES)

    @pl.when(pl.program_id(0) == 0)
    def _():
        zbuf[...] = jnp.zeros_like(zbuf)

        def zero_block(blk):
            off = pl.multiple_of(blk * (MOE_BLOCK * SUBLANES), MOE_BLOCK * SUBLANES)
            return pltpu.make_async_copy(zbuf, xs_ref.at[pl.ds(off, MOE_BLOCK * SUBLANES), :], zsem)

        def for_each_zeroed_block(fn):
            def per_expert(e, c):
                end = pend_ref[e]
                start = jnp.where(e == 0, 0, pend_ref[jnp.maximum(e - 1, 0)])

                @pl.when(end > start)
                def _():
                    fn(zero_block(end // MOE_BLOCK - 1))
                return c
            lax.fori_loop(0, N_EXPERTS, per_expert, 0)

            def per_tail_block(blk, c):
                fn(zero_block(blk))
                return c
            lax.fori_loop(nu_ref[0], n_blocks, per_tail_block, 0)

        for_each_zeroed_block(lambda cp: cp.start())
        for_each_zeroed_block(lambda cp: cp.wait())

    def issue(r, c):
        src = xn_ref.at[pl.ds(pl.multiple_of(r * SUBLANES, SUBLANES), SUBLANES), :]
        for slot in range(TOP_K):
            d = dest_ref[slot * t_total + base + r]
            dst = xs_ref.at[pl.ds(pl.multiple_of(d * SUBLANES, SUBLANES), SUBLANES), :]
            pltpu.make_async_copy(src, dst, sem).start()
        return c
    lax.fori_loop(0, tm, issue, 0, unroll=DMA_ISSUE_UNROLL)

    for slot in range(TOP_K):
        pltpu.make_async_copy(xn_ref, xs_ref.at[pl.ds(0, tm * SUBLANES), :], sem).wait()


def _dispatch(xn_tiles, dest_flat, pad_end, n_used, p_len, tm):
    t = xn_tiles.shape[0] // SUBLANES
    grid_spec = pltpu.PrefetchScalarGridSpec(
        num_scalar_prefetch=3,
        grid=(t // tm,),
        in_specs=[pl.BlockSpec((tm * SUBLANES, LANES), lambda i, dest, pend, nu: (i, 0))],
        out_specs=pl.BlockSpec(memory_space=pl.ANY),
        scratch_shapes=[pltpu.VMEM((MOE_BLOCK * SUBLANES, LANES), F32),
                        pltpu.SemaphoreType.DMA(()), pltpu.SemaphoreType.DMA(())],
    )
    return pl.pallas_call(
        _dispatch_kernel,
        grid_spec=grid_spec,
        out_shape=jax.ShapeDtypeStruct((p_len * SUBLANES, LANES), F32),
        compiler_params=_cparams(("arbitrary",)),
        name="dispatch",
    )(dest_flat, pad_end, n_used, xn_tiles)


def _experts_kernel(be_ref, nu_ref, nxt_ref, ord_ref, xs_ref, w1_hbm, w3_hbm, w2_hbm, ys_ref,
                    w1_scr, w3_scr, w2_scr, w1_buf, w3_buf, w2_buf, wsems):
    i = pl.program_id(0)

    def weight_copies(e, slot):
        return (pltpu.make_async_copy(w1_hbm.at[e], w1_buf.at[slot], wsems.at[slot, 0]),
                pltpu.make_async_copy(w3_hbm.at[e], w3_buf.at[slot], wsems.at[slot, 1]),
                pltpu.make_async_copy(w2_hbm.at[e], w2_buf.at[slot], wsems.at[slot, 2]))

    @pl.when(i < nu_ref[0])
    def _():
        e = be_ref[i]
        prev = be_ref[jnp.maximum(i - 1, 0)]
        slot = ord_ref[e] % 2

        @pl.when(i == 0)
        def _():
            for cp in weight_copies(e, slot):
                cp.start()

        @pl.when((i == 0) | (e != prev))
        def _():
            for cp in weight_copies(e, slot):
                cp.wait()
            nxt = nxt_ref[e]

            @pl.when(nxt >= 0)
            def _():
                for cp in weight_copies(nxt, 1 - slot):
                    cp.start()
            w1_scr[...] = w1_buf[slot].astype(BF16)
            w3_scr[...] = w3_buf[slot].astype(BF16)
            w2_scr[...] = w2_buf[slot].astype(BF16)

        x = _load_token_tiles(xs_ref, MOE_BLOCK).astype(BF16)
        a = jnp.dot(x, w1_scr[...], preferred_element_type=F32)
        b = jnp.dot(x, w3_scr[...], preferred_element_type=F32)
        hdn = a * (1.0 / (1.0 + jnp.exp(-a))) * b
        _store_token_tiles(ys_ref, jnp.dot(hdn.astype(BF16), w2_scr[...], preferred_element_type=F32))

    @pl.when(i >= nu_ref[0])
    def _():
        ys_ref[...] = jnp.zeros_like(ys_ref)


def _experts(xs, block_e, n_used, next_expert, expert_ord, w1, w3, w2):
    p_len = xs.shape[0] // SUBLANES
    n_blocks = p_len // MOE_BLOCK
    d, de = w1.shape[1], w1.shape[2]
    blk = lambda i, be, nu, nxt, od: (jnp.maximum(jnp.minimum(i, nu[0] - 1), 0), 0)
    hbm = pl.BlockSpec(memory_space=pl.ANY)
    grid_spec = pltpu.PrefetchScalarGridSpec(
        num_scalar_prefetch=4,
        grid=(n_blocks,),
        in_specs=[pl.BlockSpec((MOE_BLOCK * SUBLANES, LANES), blk), hbm, hbm, hbm],
        out_specs=pl.BlockSpec((MOE_BLOCK * SUBLANES, LANES), lambda i, be, nu, nxt, od: (i, 0)),
        scratch_shapes=[pltpu.VMEM((d, de), BF16), pltpu.VMEM((d, de), BF16), pltpu.VMEM((de, d), BF16),
                        pltpu.VMEM((2, d, de), F32), pltpu.VMEM((2, d, de), F32), pltpu.VMEM((2, de, d), F32),
                        pltpu.SemaphoreType.DMA((2, 3))],
    )
    return pl.pallas_call(
        _experts_kernel,
        grid_spec=grid_spec,
        out_shape=jax.ShapeDtypeStruct((p_len * SUBLANES, LANES), F32),
        compiler_params=_cparams(("arbitrary",)),
        name="experts",
    )(block_e, n_used, next_expert, expert_ord, xs, w1, w3, w2)


def _final_kernel(dest_ref, h_ref, meta_ref, p_ref, wple_ref, gple_ref, gplg_ref, wplg_ref, ys_ref,
                  out_ref, gbuf, sems):
    tm = h_ref.shape[0]
    step = pl.program_id(0)
    n_steps = pl.num_programs(0)
    t_total = n_steps * tm

    def issue_tile(tile, buf):
        def issue(r, c):
            roff = pl.multiple_of(r * SUBLANES, SUBLANES)
            for slot in range(TOP_K):
                d = dest_ref[slot * t_total + tile * tm + r]
                src = ys_ref.at[pl.ds(pl.multiple_of(d * SUBLANES, SUBLANES), SUBLANES), :]
                pltpu.make_async_copy(src, gbuf.at[buf, slot, pl.ds(roff, SUBLANES), :], sems.at[buf]).start()
            return c
        lax.fori_loop(0, tm, issue, 0, unroll=DMA_ISSUE_UNROLL)

    @pl.when(step == 0)
    def _():
        issue_tile(0, 0)

    @pl.when(step + 1 < n_steps)
    def _():
        issue_tile(step + 1, (step + 1) % 2)

    emb = jnp.dot(p_ref[...].astype(BF16), wple_ref[...], preferred_element_type=F32)
    emb = _rms_rows(emb) * gple_ref[...]

    buf = step % 2
    for slot in range(TOP_K):
        pltpu.make_async_copy(ys_ref.at[pl.ds(0, tm * SUBLANES), :], gbuf.at[buf, slot], sems.at[buf]).wait()

    meta = meta_ref[...]
    y1 = _load_token_tiles(gbuf.at[buf, 0], tm)
    y2 = _load_token_tiles(gbuf.at[buf, 1], tm)
    h = h_ref[...] + meta[:, 4:5] * y1 + meta[:, 5:6] * y2
    hn = (_rms_rows(h) * gplg_ref[...]).astype(BF16)
    z = jnp.dot(hn, wplg_ref[...], preferred_element_type=F32)
    out_ref[...] = h + emb * (1.0 / (1.0 + jnp.exp(-z)))


def _final(dest_flat, h2, meta, p2, wple_bf, g_ple, g_plg, wplg_bf, ys, tm):
    t, d = h2.shape
    dp = p2.shape[1]
    const = lambda i, dest: (0, 0)
    row = lambda i, dest: (i, 0)
    grid_spec = pltpu.PrefetchScalarGridSpec(
        num_scalar_prefetch=1,
        grid=(t // tm,),
        in_specs=[
            pl.BlockSpec((tm, d), row),
            pl.BlockSpec((tm, LANES), row),
            pl.BlockSpec((tm, dp), row),
            pl.BlockSpec((dp, d), const),
            pl.BlockSpec((1, d), const),
            pl.BlockSpec((1, d), const),
            pl.BlockSpec((d, d), const),
            pl.BlockSpec(memory_space=pl.ANY),
        ],
        out_specs=pl.BlockSpec((tm, d), row),
        scratch_shapes=[pltpu.VMEM((2, TOP_K, tm * SUBLANES, LANES), F32), pltpu.SemaphoreType.DMA((2,))],
    )
    return pl.pallas_call(
        _final_kernel,
        grid_spec=grid_spec,
        out_shape=jax.ShapeDtypeStruct((t, d), F32),
        compiler_params=_cparams(("arbitrary",)),
        name="combine_ple",
    )(dest_flat, h2, meta, p2, wple_bf, g_ple, g_plg, wplg_bf, ys)


def _block_diag_mean(seg):
    m = np.kron(np.eye(512 // seg), np.full((seg, seg), 1.0 / seg))
    return jnp.asarray(m, BF16)


def _rope_tables(seq):
    inv = ROPE_THETA ** (-jnp.arange(0, DA_QK, 2, dtype=F32) / DA_QK)
    ang = jnp.arange(seq, dtype=F32)[:, None] * inv[None, :]
    cos, sin = jnp.cos(ang), jnp.sin(ang)
    cos_t = jnp.tile(cos, (1, LANES // cos.shape[1]))
    sin_t = jnp.tile(jnp.concatenate([-sin, sin], axis=1), (1, LANES // (2 * sin.shape[1])))
    return cos_t, sin_t


def _tile_for(n, pref):
    while n % pref:
        pref //= 2
    return pref


def kernel(x, p, g_mix, w_in, g_qa, g_ka, lam_q1, lam_k1, lam_q2, lam_k2, g_sub, g_qb, g_kb, rpb, w_out,
           g_ffn, w_rg, w_re, w1, w3, w2, g_plg, w_plg, w_ple, g_ple):
    b, s, d = x.shape
    assert d == SUBLANES * LANES, "token-tile layout holds one token per (8, 128) f32 tile"
    t = b * s
    depth = w_in.shape[0]
    tm = _tile_for(s, 512)
    tmd = _tile_for(t, 512)
    tmf = _tile_for(t, 256)
    n_slots = t * TOP_K
    n_blocks = n_slots // MOE_BLOCK + N_EXPERTS
    p_len = n_blocks * MOE_BLOCK

    bd64 = _block_diag_mean(DA_QK)
    bd32 = _block_diag_mean(DB_H)
    cos_t, sin_t = _rope_tables(s)
    tri = jnp.asarray(np.tril(np.ones((tm, tm)), -1), BF16)

    h = x.reshape(t, d)
    for i in range(depth):
        lam_init = 0.8 - 0.6 * math.exp(-0.3 * i)
        gsec = jnp.stack([
            jnp.tile(g_qa[i], 512 // DA_QK) * (DA_QK ** -0.5 * math.log2(math.e)),
            jnp.tile(g_ka[i], 512 // DA_QK),
            jnp.tile(g_qb[i], 512 // DB_H) * DB_H ** -0.5,
            jnp.tile(g_kb[i], 512 // DB_H),
        ]).astype(F32)
        proj = _inproj(h, g_mix[i][None, :], w_in[i].astype(BF16), bd64, bd32, gsec, cos_t, sin_t, s, tm)
        proj3 = proj.reshape(b, s, proj.shape[1])

        lam4 = jnp.stack([lam_q1[i], lam_k1[i], lam_q2[i], lam_k2[i]]).astype(F32)
        oa = _diffattn(proj3, lam4, g_sub[i][None, :].astype(F32), lam_init)
        ob = _natten(proj3, _natten_bias_table(rpb[i]))

        w_router = jnp.concatenate(
            [w_rg[i], w_re[i], jnp.zeros((d, LANES - N_GROUPS - N_EXPERTS), F32)], axis=1).astype(F32)
        wr_hi = w_router.astype(BF16)
        wr_lo = (w_router - wr_hi.astype(F32)).astype(BF16)
        h1, xn, meta, meta_t, cnt = _outproj(h, oa.reshape(t, -1), ob.reshape(t, -1), w_out[i].astype(BF16),
                                             g_ffn[i][None, :], jnp.concatenate([wr_hi, wr_lo], axis=1), tri, tm)

        counts = cnt[0, E_LANE0:E_LANE0 + N_EXPERTS].astype(jnp.int32)
        padded = (counts + MOE_BLOCK - 1) // MOE_BLOCK * MOE_BLOCK
        pad_end = jnp.cumsum(padded)
        pad_start = pad_end - padded
        mt = meta_t[0:2 * TOP_K].astype(jnp.int32)
        e_ids = jnp.arange(N_EXPERTS, dtype=jnp.int32)[None, :, None]
        start_of = jnp.sum(jnp.where(mt[:TOP_K, None, :] == e_ids, pad_start[None, :, None], 0), axis=1)
        dest = start_of + mt[TOP_K:]
        blk_start = jnp.arange(n_blocks, dtype=jnp.int32) * MOE_BLOCK
        block_e = jnp.minimum(jnp.sum(pad_end[None, :] <= blk_start[:, None], axis=1), N_EXPERTS - 1)
        block_e = block_e.astype(jnp.int32)
        n_used = (pad_end[-1:] // MOE_BLOCK).astype(jnp.int32)

        dest_flat = dest.reshape(TOP_K * t)

        xs = _dispatch(xn, dest_flat, pad_end.astype(jnp.int32), n_used, p_len, tmd)
        in_use = counts > 0
        expert_ord = (jnp.cumsum(in_use) - 1).astype(jnp.int32)
        ids = jnp.arange(N_EXPERTS, dtype=jnp.int32)
        later = jnp.where((ids[None, :] > ids[:, None]) & in_use[None, :], ids[None, :], N_EXPERTS)
        next_expert = jnp.min(later, axis=1)
        next_expert = jnp.where(next_expert == N_EXPERTS, -1, next_expert).astype(jnp.int32)
        ys = _experts(xs, block_e, n_used, next_expert, expert_ord, w1[i], w3[i], w2[i])
        h = _final(dest_flat, h1, meta, p[i].reshape(t, -1),
                   w_ple[i].astype(BF16), g_ple[i][None, :], g_plg[i][None, :], w_plg[i].astype(BF16), ys, tmf)
    return h.reshape(b, s, d)
```

```python
import functools
import math

import numpy as np
import jax
import jax.numpy as jnp
from jax import lax
from jax.experimental import pallas as pl
from jax.experimental.pallas import tpu as pltpu

F32 = jnp.float32
BF16 = jnp.bfloat16

LANES = 128
HA = 4
DA_QK = 64
DA_V = 128
HB = 16
DB_H = 32
GRID_W = 64
WIN_R = 8
WIN_C = 16
ROPE_THETA = 10000.0
EPS = 1e-6
NEG = -1e30
N_GROUPS = 4
EXPERTS_PER_GROUP = 8
N_EXPERTS = N_GROUPS * EXPERTS_PER_GROUP
TOP_K = 2
MOE_BLOCK = 256
E_LANE0 = N_GROUPS

VMEM_LIMIT = 56 * 1024 * 1024


def _cparams(sem):
    return pltpu.CompilerParams(dimension_semantics=sem, vmem_limit_bytes=VMEM_LIMIT)


def _rms_rows(x):
    return x * lax.rsqrt(jnp.mean(x * x, axis=-1, keepdims=True) + EPS)


SUBLANES = 8

def _store_token_tiles(ref, val, first=0):
    n = val.shape[0]
    for c in range(val.shape[1] // LANES):
        ref[pl.ds(first * SUBLANES + c, n, stride=SUBLANES), :] = val[:, c * LANES:(c + 1) * LANES]


def _load_token_tiles(ref, n, first=0):
    return jnp.concatenate(
        [ref[pl.ds(first * SUBLANES + c, n, stride=SUBLANES), :] for c in range(SUBLANES)], axis=1)


def _inproj_kernel(x_ref, gmix_ref, w_ref, bd64_ref, bd32_ref, gsec_ref, cos_ref, sin_ref, out_ref):
    tm = x_ref.shape[0]
    a = (_rms_rows(x_ref[...]) * gmix_ref[...]).astype(BF16)
    lane = lax.broadcasted_iota(jnp.int32, (tm, LANES), 1)
    first_half = (lane % 64) < 32
    cos = cos_ref[...]
    sin = sin_ref[...]
    for sec in range(6):
        y = jnp.dot(a, w_ref[:, sec * 512:(sec + 1) * 512], preferred_element_type=F32)
        if sec in (0, 1, 3, 4):
            bd = bd64_ref if sec < 2 else bd32_ref
            gi = {0: 0, 1: 1, 3: 2, 4: 3}[sec]
            ms = jnp.dot((y * y).astype(BF16), bd[...], preferred_element_type=F32)
            y = y * lax.rsqrt(ms + EPS) * gsec_ref[gi:gi + 1, :]
        for c in range(4):
            yc = y[:, c * LANES:(c + 1) * LANES]
            if sec < 2:
                rot = jnp.where(first_half, pltpu.roll(yc, 96, 1), pltpu.roll(yc, 32, 1))
                yc = yc * cos + rot * sin
            out_ref[:, sec * 512 + c * LANES: sec * 512 + (c + 1) * LANES] = yc.astype(BF16)


def _inproj(x2, g_mix, w_bf, bd64, bd32, gsec, cos_t, sin_t, seq, tm):
    t, d = x2.shape
    n = w_bf.shape[1]
    nsb = seq // tm
    const = lambda i: (0, 0)
    return pl.pallas_call(
        _inproj_kernel,
        grid=(t // tm,),
        in_specs=[
            pl.BlockSpec((tm, d), lambda i: (i, 0)),
            pl.BlockSpec((1, d), const),
            pl.BlockSpec((d, n), const),
            pl.BlockSpec((512, 512), const),
            pl.BlockSpec((512, 512), const),
            pl.BlockSpec((4, 512), const),
            pl.BlockSpec((tm, LANES), lambda i: (i % nsb, 0)),
            pl.BlockSpec((tm, LANES), lambda i: (i % nsb, 0)),
        ],
        out_specs=pl.BlockSpec((tm, n), lambda i: (i, 0)),
        out_shape=jax.ShapeDtypeStruct((t, n), BF16),
        compiler_params=_cparams(("parallel",)),
        name="inproj",
    )(x2, g_mix, w_bf, bd64, bd32, gsec, cos_t, sin_t)


QBLOCK_PAIRS_PER_BODY = 4

def _diffattn_kernel(lam_ref, gsub_ref, q_ref, k_ref, v_ref, o_ref, qt_scr, vt_scr, sa_scr, sb_scr, *, seq, tq, tk,
                     lam_init):
    nq = seq // tq
    nk = seq // tk
    s1 = jnp.sum(lam_ref[0:1, :] * lam_ref[1:2, :], axis=1, keepdims=True)
    s2 = jnp.sum(lam_ref[2:3, :] * lam_ref[3:4, :], axis=1, keepdims=True)
    lam = jnp.exp(s1) - jnp.exp(s2) + lam_init

    def tr_body(c, carry):
        off = pl.multiple_of(c * LANES, LANES)
        qt_scr[:, pl.ds(off, LANES)] = q_ref[0, pl.ds(off, LANES), :].astype(F32).T.astype(BF16)
        vt_scr[:, pl.ds(off, LANES)] = v_ref[0, pl.ds(off, LANES), :].astype(F32).T.astype(BF16)
        return carry
    lax.fori_loop(0, seq // LANES, tr_body, 0)

    row = lax.broadcasted_iota(jnp.int32, (LANES, tq), 0)

    def scores(qi, s_scr):
        qoff = pl.multiple_of(qi * tq, tq)
        qt = qt_scr[:, pl.ds(qoff, tq)]
        zero = jnp.zeros_like(qt)
        qm = jnp.concatenate([jnp.where(row < DA_QK, qt, zero), jnp.where(row >= DA_QK, qt, zero)], axis=1)
        mx = None
        for j in range(nk):
            st = jnp.dot(k_ref[0, j * tk:(j + 1) * tk, :], qm, preferred_element_type=F32)
            s_scr[j * tk:(j + 1) * tk, :] = st
            cm = jnp.max(st, axis=0, keepdims=True)
            mx = cm if mx is None else jnp.maximum(mx, cm)
        return mx

    def finish(qi, s_scr, m):
        l = jnp.zeros((1, 2 * tq), F32)
        acc = jnp.zeros((DA_V, 2 * tq), F32)
        for j in range(nk):
            p = jnp.exp2(s_scr[j * tk:(j + 1) * tk, :] - m)
            l = l + jnp.sum(p, axis=0, keepdims=True)
            acc = acc + jnp.dot(vt_scr[:, j * tk:(j + 1) * tk], p.astype(BF16), preferred_element_type=F32)
        o = acc * (1.0 / l)
        ot = o[:, :tq] - lam * o[:, tq:]
        ot = ot * lax.rsqrt(jnp.mean(ot * ot, axis=0, keepdims=True) + EPS) * (1.0 - lam_init)
        qoff = pl.multiple_of(qi * tq, tq)
        o_ref[0, pl.ds(qoff, tq), :] = (ot.T * gsub_ref[...]).astype(BF16)

    def body(i, m_a):
        for pair in range(QBLOCK_PAIRS_PER_BODY):
            q0 = 2 * (QBLOCK_PAIRS_PER_BODY * i + pair)
            m_b = scores(q0 + 1, sb_scr)
            finish(q0, sa_scr, m_a)
            m_a = scores(jnp.minimum(q0 + 2, nq - 1), sa_scr)
            finish(q0 + 1, sb_scr, m_b)
        return m_a

    lax.fori_loop(0, nq // (2 * QBLOCK_PAIRS_PER_BODY), body, scores(0, sa_scr))


def _diffattn(proj3, lam4, g_sub, lam_init, tq=128, tk=512):
    b, seq, _ = proj3.shape
    kern = functools.partial(_diffattn_kernel, seq=seq, tq=tq, tk=tk, lam_init=lam_init)
    return pl.pallas_call(
        kern,
        grid=(b, HA),
        in_specs=[
            pl.BlockSpec((4, DA_QK), lambda i, h: (0, 0)),
            pl.BlockSpec((1, DA_V), lambda i, h: (0, 0)),
            pl.BlockSpec((1, seq, LANES), lambda i, h: (i, 0, h)),
            pl.BlockSpec((1, seq, LANES), lambda i, h: (i, 0, HA + h)),
            pl.BlockSpec((1, seq, LANES), lambda i, h: (i, 0, 2 * HA + h)),
        ],
        out_specs=pl.BlockSpec((1, seq, LANES), lambda i, h: (i, 0, h)),
        out_shape=jax.ShapeDtypeStruct((b, seq, HA * DA_V), BF16),
        scratch_shapes=[pltpu.VMEM((LANES, seq), BF16), pltpu.VMEM((LANES, seq), BF16),
                        pltpu.VMEM((seq, 2 * tq), F32), pltpu.VMEM((seq, 2 * tq), F32)],
        compiler_params=_cparams(("parallel", "parallel")),
        name="diffattn",
    )(lam4, g_sub, proj3, proj3, proj3)


def _natten_kernel(bias_ref, q_ref, k_ref, v_ref, o_ref, *, rows):
    kr = min(WIN_R, rows)
    nkeys = kr * GRID_W
    hpg = LANES // DB_H
    lane_head = lax.broadcasted_iota(jnp.int32, (GRID_W, LANES), 1) // DB_H

    def row_step(r, carry):
        rs = jnp.clip(r - kr // 2, 0, rows - kr)
        base = rs - r + (WIN_R - 1)
        qoff = pl.multiple_of(r * GRID_W, GRID_W)
        koff = pl.multiple_of(rs * GRID_W, GRID_W)
        qr = q_ref[0, pl.ds(qoff, GRID_W), :]
        zero = jnp.zeros_like(qr)
        qs = jnp.concatenate([jnp.where(lane_head == h, qr, zero) for h in range(hpg)], axis=0)
        kw = k_ref[0, pl.ds(koff, nkeys), :]
        vw = v_ref[0, pl.ds(koff, nkeys), :]
        s = lax.dot_general(qs, kw, (((1,), (1,)), ((), ())), preferred_element_type=F32)
        bias = jnp.concatenate([bias_ref[0, base + 2 * i] for i in range(kr // 2)], axis=1)
        s = s + bias
        m = jnp.max(s, axis=1, keepdims=True)
        p = jnp.exp(s - m).astype(BF16)
        pv = jnp.dot(p, jnp.concatenate([vw, jnp.ones_like(vw)], axis=1), preferred_element_type=F32)
        o = pv[:, :LANES] * (1.0 / pv[:, LANES:])
        out = jnp.zeros((GRID_W, LANES), F32)
        for h in range(hpg):
            out = out + jnp.where(lane_head == h, o[h * GRID_W:(h + 1) * GRID_W, :], 0.0)
        o_ref[0, pl.ds(qoff, GRID_W), :] = out.astype(BF16)
        return carry

    lax.fori_loop(0, rows, row_step, 0, unroll=16)


def _natten(proj3, bias_tab):
    b, seq, _ = proj3.shape
    rows = seq // GRID_W
    ng = HB * DB_H // LANES
    nro = bias_tab.shape[1]
    kern = functools.partial(_natten_kernel, rows=rows)
    cb = 3 * HA
    return pl.pallas_call(
        kern,
        grid=(b, ng),
        in_specs=[
            pl.BlockSpec((1, nro, bias_tab.shape[2], LANES), lambda i, g: (g, 0, 0, 0)),
            pl.BlockSpec((1, seq, LANES), lambda i, g: (i, 0, cb + g)),
            pl.BlockSpec((1, seq, LANES), lambda i, g: (i, 0, cb + ng + g)),
            pl.BlockSpec((1, seq, LANES), lambda i, g: (i, 0, cb + 2 * ng + g)),
        ],
        out_specs=pl.BlockSpec((1, seq, LANES), lambda i, g: (i, 0, g)),
        out_shape=jax.ShapeDtypeStruct((b, seq, HB * DB_H), BF16),
        compiler_params=_cparams(("parallel", "parallel")),
        name="natten",
    )(bias_tab, proj3, proj3, proj3)


def _natten_bias_table(rpb):
    q = np.arange(GRID_W)[:, None]
    kc = np.arange(GRID_W)[None, :]
    cs = np.clip(q - WIN_C // 2, 0, GRID_W - WIN_C)
    valid = (kc >= cs) & (kc < cs + WIN_C)
    col_off = np.clip(kc - q + (WIN_C - 1), 0, 2 * WIN_C - 2)
    ncol = 2 * WIN_C - 1
    nro = 2 * WIN_R - 2
    pick = np.zeros((2, ncol, GRID_W, 2 * GRID_W), np.float32)
    for j in range(2):
        pick[j, :, :, j * GRID_W:(j + 1) * GRID_W] = (np.arange(ncol)[:, None, None] == col_off[None]) & valid[None]
    negmask = np.tile(np.where(valid, 0.0, NEG).astype(np.float32), (1, 2))
    hpg = LANES // DB_H
    r4 = rpb.astype(F32).reshape(HB // hpg, hpg, 2 * WIN_R - 1, ncol)
    rp = jnp.stack([r4[:, :, :nro], r4[:, :, 1:]], axis=0)
    tab = jnp.einsum('jghrc,jcqk->grhqk', rp, jnp.asarray(pick), precision=lax.Precision.HIGHEST) + negmask
    return tab.reshape(HB // hpg, nro, hpg * GRID_W, LANES)


def _outproj_kernel(x_ref, oa_ref, ob_ref, wo_ref, gffn_ref, wr_ref, tri_ref,
                    h_ref, xn_ref, meta_ref, meta_t_ref, cnt_ref, run_scr):
    tm = x_ref.shape[0]
    da = oa_ref.shape[1]

    @pl.when(pl.program_id(0) == 0)
    def _():
        run_scr[...] = jnp.zeros_like(run_scr)

    attn = jnp.dot(oa_ref[...], wo_ref[:da, :], preferred_element_type=F32)
    attn = attn + jnp.dot(ob_ref[...], wo_ref[da:, :], preferred_element_type=F32)
    h = x_ref[...] + attn
    h_ref[...] = h
    xn = _rms_rows(h) * gffn_ref[...]
    _store_token_tiles(xn_ref, xn)

    xh = xn.astype(BF16)
    xl = (xn - xh.astype(F32)).astype(BF16)
    hh_hl = jnp.dot(xh, wr_ref[...], preferred_element_type=F32)
    lg = hh_hl[:, :LANES] + hh_hl[:, LANES:] + jnp.dot(xl, wr_ref[:, :LANES], preferred_element_type=F32)
    lane = lax.broadcasted_iota(jnp.int32, (tm, LANES), 1).astype(F32)
    ninf = -jnp.inf
    is_g = lane < N_GROUPS
    gl = jnp.where(is_g, lg, ninf)
    gmax = jnp.max(gl, axis=1, keepdims=True)
    gidx = jnp.min(jnp.where(gl == gmax, lane, float(LANES)), axis=1, keepdims=True)
    gsum = jnp.sum(jnp.where(is_g, jnp.exp(gl - gmax), 0.0), axis=1, keepdims=True)
    g_gate = 1.0 / gsum
    lo = E_LANE0 + gidx * EXPERTS_PER_GROUP
    is_e = (lane >= lo) & (lane < lo + EXPERTS_PER_GROUP)
    el = jnp.where(is_e, lg, ninf)
    l1 = jnp.max(el, axis=1, keepdims=True)
    i1 = jnp.min(jnp.where(el == l1, lane, float(LANES)), axis=1, keepdims=True)
    el2 = jnp.where(lane == i1, ninf, el)
    l2 = jnp.max(el2, axis=1, keepdims=True)
    i2 = jnp.min(jnp.where(el2 == l2, lane, float(LANES)), axis=1, keepdims=True)
    r = jnp.exp(l2 - l1)
    w1 = g_gate / (1.0 + r)
    w2 = w1 * r

    sel1 = lane == i1
    sel2 = lane == i2
    oh = jnp.where(sel1 | sel2, 1.0, 0.0)
    before = jnp.dot(tri_ref[...], oh.astype(BF16), preferred_element_type=F32) + run_scr[...]
    rank1 = jnp.sum(jnp.where(sel1, before, 0.0), axis=1, keepdims=True)
    rank2 = jnp.sum(jnp.where(sel2, before, 0.0), axis=1, keepdims=True)
    run = run_scr[...] + jnp.sum(oh, axis=0, keepdims=True)
    run_scr[...] = run
    cnt_ref[...] = run

    meta = jnp.where(lane == 0, i1 - E_LANE0, 0.0)
    meta = jnp.where(lane == 1, i2 - E_LANE0, meta)
    meta = jnp.where(lane == 2, rank1, meta)
    meta = jnp.where(lane == 3, rank2, meta)
    meta = jnp.where(lane == 4, w1, meta)
    meta = jnp.where(lane == 5, w2, meta)
    meta_ref[...] = meta
    meta_t_ref[...] = meta.T[:SUBLANES, :]


def _outproj(x2, oa2, ob2, wo_bf, g_ffn, w_router, tri, tm):
    t, d = x2.shape
    da = oa2.shape[1]
    const = lambda i: (0, 0)
    row = lambda i: (i, 0)
    return pl.pallas_call(
        _outproj_kernel,
        grid=(t // tm,),
        in_specs=[
            pl.BlockSpec((tm, d), row),
            pl.BlockSpec((tm, da), row),
            pl.BlockSpec((tm, d - da), row),
            pl.BlockSpec((d, d), const),
            pl.BlockSpec((1, d), const),
            pl.BlockSpec((d, 2 * LANES), const),
            pl.BlockSpec((tm, tm), const),
        ],
        out_specs=[
            pl.BlockSpec((tm, d), row),
            pl.BlockSpec((tm * SUBLANES, LANES), row),
            pl.BlockSpec((tm, LANES), row),
            pl.BlockSpec((SUBLANES, tm), lambda i: (0, i)),
            pl.BlockSpec((1, LANES), const),
        ],
        out_shape=[
            jax.ShapeDtypeStruct((t, d), F32),
            jax.ShapeDtypeStruct((t * SUBLANES, LANES), F32),
            jax.ShapeDtypeStruct((t, LANES), F32),
            jax.ShapeDtypeStruct((SUBLANES, t), F32),
            jax.ShapeDtypeStruct((1, LANES), F32),
        ],
        scratch_shapes=[pltpu.VMEM((1, LANES), F32)],
        compiler_params=_cparams(("arbitrary",)),
        name="outproj_router",
    )(x2, oa2, ob2, wo_bf, g_ffn, w_router, tri)


DMA_ISSUE_UNROLL = 8
DMA_PRIORITIES = 2


def _dispatch_kernel(dest_ref, pend_ref, nu_ref, xn_ref, xs_ref, zbuf, sem, zsem):
    tm = xn_ref.shape[0] // SUBLANES
    t_total = pl.num_programs(0) * tm
    base = pl.program_id(0) * tm
    n_blocks = xs_ref.shape[0] // (MOE_BLOCK * SUBLANES)

    @pl.when(pl.program_id(0) == 0)
    def _():
        zbuf[...] = jnp.zeros_like(zbuf)

        def zero_block(blk):
            off = pl.multiple_of(blk * (MOE_BLOCK * SUBLANES), MOE_BLOCK * SUBLANES)
            return pltpu.make_async_copy(zbuf, xs_ref.at[pl.ds(off, MOE_BLOCK * SUBLANES), :], zsem)

        def for_each_zeroed_block(fn):
            def per_expert(e, c):
                end = pend_ref[e]
                start = jnp.where(e == 0, 0, pend_ref[jnp.maximum(e - 1, 0)])

                @pl.when(end > start)
                def _():
                    fn(zero_block(end // MOE_BLOCK - 1))
                return c
            lax.fori_loop(0, N_EXPERTS, per_expert, 0)

            def per_tail_block(blk, c):
                fn(zero_block(blk))
                return c
            lax.fori_loop(nu_ref[0], n_blocks, per_tail_block, 0)

        for_each_zeroed_block(lambda cp: cp.start())
        for_each_zeroed_block(lambda cp: cp.wait())

    def issue(r, c):
        src = xn_ref.at[pl.ds(pl.multiple_of(r * SUBLANES, SUBLANES), SUBLANES), :]
        for slot in range(TOP_K):
            d = dest_ref[slot * t_total + base + r]
            dst = xs_ref.at[pl.ds(pl.multiple_of(d * SUBLANES, SUBLANES), SUBLANES), :]
            pltpu.make_async_copy(src, dst, sem).start(priority=slot % DMA_PRIORITIES)
        return c
    lax.fori_loop(0, tm, issue, 0, unroll=DMA_ISSUE_UNROLL)

    for slot in range(TOP_K):
        pltpu.make_async_copy(xn_ref, xs_ref.at[pl.ds(0, tm * SUBLANES), :], sem).wait()


def _dispatch(xn_tiles, dest_flat, pad_end, n_used, p_len, tm):
    t = xn_tiles.shape[0] // SUBLANES
    grid_spec = pltpu.PrefetchScalarGridSpec(
        num_scalar_prefetch=3,
        grid=(t // tm,),
        in_specs=[pl.BlockSpec((tm * SUBLANES, LANES), lambda i, dest, pend, nu: (i, 0))],
        out_specs=pl.BlockSpec(memory_space=pl.ANY),
        scratch_shapes=[pltpu.VMEM((MOE_BLOCK * SUBLANES, LANES), F32),
                        pltpu.SemaphoreType.DMA(()), pltpu.SemaphoreType.DMA(())],
    )
    return pl.pallas_call(
        _dispatch_kernel,
        grid_spec=grid_spec,
        out_shape=jax.ShapeDtypeStruct((p_len * SUBLANES, LANES), F32),
        compiler_params=_cparams(("arbitrary",)),
        name="dispatch",
    )(dest_flat, pad_end, n_used, xn_tiles)


def _experts_kernel(be_ref, nu_ref, nxt_ref, ord_ref, xs_ref, w1_hbm, w3_hbm, w2_hbm, ys_ref,
                    w1_scr, w3_scr, w2_scr, w1_buf, w3_buf, w2_buf, wsems):
    i = pl.program_id(0)

    def weight_copies(e, slot):
        return (pltpu.make_async_copy(w1_hbm.at[e], w1_buf.at[slot], wsems.at[slot, 0]),
                pltpu.make_async_copy(w3_hbm.at[e], w3_buf.at[slot], wsems.at[slot, 1]),
                pltpu.make_async_copy(w2_hbm.at[e], w2_buf.at[slot], wsems.at[slot, 2]))

    @pl.when(i < nu_ref[0])
    def _():
        e = be_ref[i]
        prev = be_ref[jnp.maximum(i - 1, 0)]
        slot = ord_ref[e] % 2

        @pl.when(i == 0)
        def _():
            for cp in weight_copies(e, slot):
                cp.start()

        @pl.when((i == 0) | (e != prev))
        def _():
            for cp in weight_copies(e, slot):
                cp.wait()
            nxt = nxt_ref[e]

            @pl.when(nxt >= 0)
            def _():
                for cp in weight_copies(nxt, 1 - slot):
                    cp.start()
            w1_scr[...] = w1_buf[slot].astype(BF16)
            w3_scr[...] = w3_buf[slot].astype(BF16)
            w2_scr[...] = w2_buf[slot].astype(BF16)

        x = _load_token_tiles(xs_ref, MOE_BLOCK).astype(BF16)
        a = jnp.dot(x, w1_scr[...], preferred_element_type=F32)
        b = jnp.dot(x, w3_scr[...], preferred_element_type=F32)
        hdn = a * (1.0 / (1.0 + jnp.exp(-a))) * b
        _store_token_tiles(ys_ref, jnp.dot(hdn.astype(BF16), w2_scr[...], preferred_element_type=F32))

    @pl.when(i >= nu_ref[0])
    def _():
        ys_ref[...] = jnp.zeros_like(ys_ref)


def _experts(xs, block_e, n_used, next_expert, expert_ord, w1, w3, w2):
    p_len = xs.shape[0] // SUBLANES
    n_blocks = p_len // MOE_BLOCK
    d, de = w1.shape[1], w1.shape[2]
    blk = lambda i, be, nu, nxt, od: (jnp.maximum(jnp.minimum(i, nu[0] - 1), 0), 0)
    hbm = pl.BlockSpec(memory_space=pl.ANY)
    grid_spec = pltpu.PrefetchScalarGridSpec(
        num_scalar_prefetch=4,
        grid=(n_blocks,),
        in_specs=[pl.BlockSpec((MOE_BLOCK * SUBLANES, LANES), blk), hbm, hbm, hbm],
        out_specs=pl.BlockSpec((MOE_BLOCK * SUBLANES, LANES), lambda i, be, nu, nxt, od: (i, 0)),
        scratch_shapes=[pltpu.VMEM((d, de), BF16), pltpu.VMEM((d, de), BF16), pltpu.VMEM((de, d), BF16),
                        pltpu.VMEM((2, d, de), F32), pltpu.VMEM((2, d, de), F32), pltpu.VMEM((2, de, d), F32),
                        pltpu.SemaphoreType.DMA((2, 3))],
    )
    return pl.pallas_call(
        _experts_kernel,
        grid_spec=grid_spec,
        out_shape=jax.ShapeDtypeStruct((p_len * SUBLANES, LANES), F32),
        compiler_params=_cparams(("arbitrary",)),
        name="experts",
    )(block_e, n_used, next_expert, expert_ord, xs, w1, w3, w2)


def _final_kernel(dest_ref, h_ref, meta_ref, p_ref, wple_ref, gple_ref, gplg_ref, wplg_ref, ys_ref,
                  out_ref, gbuf, sems):
    tm = h_ref.shape[0]
    step = pl.program_id(0)
    n_steps = pl.num_programs(0)
    t_total = n_steps * tm

    def issue_tile(tile, buf):
        def issue(r, c):
            roff = pl.multiple_of(r * SUBLANES, SUBLANES)
            for slot in range(TOP_K):
                d = dest_ref[slot * t_total + tile * tm + r]
                src = ys_ref.at[pl.ds(pl.multiple_of(d * SUBLANES, SUBLANES), SUBLANES), :]
                pltpu.make_async_copy(src, gbuf.at[buf, slot, pl.ds(roff, SUBLANES), :],
                                      sems.at[buf]).start(priority=slot % DMA_PRIORITIES)
            return c
        lax.fori_loop(0, tm, issue, 0, unroll=DMA_ISSUE_UNROLL)

    @pl.when(step == 0)
    def _():
        issue_tile(0, 0)

    @pl.when(step + 1 < n_steps)
    def _():
        issue_tile(step + 1, (step + 1) % 2)

    emb = jnp.dot(p_ref[...].astype(BF16), wple_ref[...], preferred_element_type=F32)
    emb = _rms_rows(emb) * gple_ref[...]

    buf = step % 2
    for slot in range(TOP_K):
        pltpu.make_async_copy(ys_ref.at[pl.ds(0, tm * SUBLANES), :], gbuf.at[buf, slot], sems.at[buf]).wait()

    meta = meta_ref[...]
    y1 = _load_token_tiles(gbuf.at[buf, 0], tm)
    y2 = _load_token_tiles(gbuf.at[buf, 1], tm)
    h = h_ref[...] + meta[:, 4:5] * y1 + meta[:, 5:6] * y2
    hn = (_rms_rows(h) * gplg_ref[...]).astype(BF16)
    z = jnp.dot(hn, wplg_ref[...], preferred_element_type=F32)
    out_ref[...] = h + emb * (1.0 / (1.0 + jnp.exp(-z)))


def _final(dest_flat, h2, meta, p2, wple_bf, g_ple, g_plg, wplg_bf, ys, tm):
    t, d = h2.shape
    dp = p2.shape[1]
    const = lambda i, dest: (0, 0)
    row = lambda i, dest: (i, 0)
    grid_spec = pltpu.PrefetchScalarGridSpec(
        num_scalar_prefetch=1,
        grid=(t // tm,),
        in_specs=[
            pl.BlockSpec((tm, d), row),
            pl.BlockSpec((tm, LANES), row),
            pl.BlockSpec((tm, dp), row),
            pl.BlockSpec((dp, d), const),
            pl.BlockSpec((1, d), const),
            pl.BlockSpec((1, d), const),
            pl.BlockSpec((d, d), const),
            pl.BlockSpec(memory_space=pl.ANY),
        ],
        out_specs=pl.BlockSpec((tm, d), row),
        scratch_shapes=[pltpu.VMEM((2, TOP_K, tm * SUBLANES, LANES), F32), pltpu.SemaphoreType.DMA((2,))],
    )
    return pl.pallas_call(
        _final_kernel,
        grid_spec=grid_spec,
        out_shape=jax.ShapeDtypeStruct((t, d), F32),
        compiler_params=_cparams(("arbitrary",)),
        name="combine_ple",
    )(dest_flat, h2, meta, p2, wple_bf, g_ple, g_plg, wplg_bf, ys)


def _block_diag_mean(seg):
    m = np.kron(np.eye(512 // seg), np.full((seg, seg), 1.0 / seg))
    return jnp.asarray(m, BF16)


def _rope_tables(seq):
    inv = ROPE_THETA ** (-jnp.arange(0, DA_QK, 2, dtype=F32) / DA_QK)
    ang = jnp.arange(seq, dtype=F32)[:, None] * inv[None, :]
    cos, sin = jnp.cos(ang), jnp.sin(ang)
    cos_t = jnp.tile(cos, (1, LANES // cos.shape[1]))
    sin_t = jnp.tile(jnp.concatenate([-sin, sin], axis=1), (1, LANES // (2 * sin.shape[1])))
    return cos_t, sin_t


def _tile_for(n, pref):
    while n % pref:
        pref //= 2
    return pref


def kernel(x, p, g_mix, w_in, g_qa, g_ka, lam_q1, lam_k1, lam_q2, lam_k2, g_sub, g_qb, g_kb, rpb, w_out,
           g_ffn, w_rg, w_re, w1, w3, w2, g_plg, w_plg, w_ple, g_ple):
    b, s, d = x.shape
    assert d == SUBLANES * LANES, "token-tile layout holds one token per (8, 128) f32 tile"
    t = b * s
    depth = w_in.shape[0]
    tm = _tile_for(s, 512)
    tmd = _tile_for(t, 512)
    tmf = _tile_for(t, 256)
    n_slots = t * TOP_K
    n_blocks = n_slots // MOE_BLOCK + N_EXPERTS
    p_len = n_blocks * MOE_BLOCK

    bd64 = _block_diag_mean(DA_QK)
    bd32 = _block_diag_mean(DB_H)
    cos_t, sin_t = _rope_tables(s)
    tri = jnp.asarray(np.tril(np.ones((tm, tm)), -1), BF16)

    h = x.reshape(t, d)
    for i in range(depth):
        lam_init = 0.8 - 0.6 * math.exp(-0.3 * i)
        gsec = jnp.stack([
            jnp.tile(g_qa[i], 512 // DA_QK) * (DA_QK ** -0.5 * math.log2(math.e)),
            jnp.tile(g_ka[i], 512 // DA_QK),
            jnp.tile(g_qb[i], 512 // DB_H) * DB_H ** -0.5,
            jnp.tile(g_kb[i], 512 // DB_H),
        ]).astype(F32)
        proj = _inproj(h, g_mix[i][None, :], w_in[i].astype(BF16), bd64, bd32, gsec, cos_t, sin_t, s, tm)
        proj3 = proj.reshape(b, s, proj.shape[1])

        lam4 = jnp.stack([lam_q1[i], lam_k1[i], lam_q2[i], lam_k2[i]]).astype(F32)
        oa = _diffattn(proj3, lam4, g_sub[i][None, :].astype(F32), lam_init)
        ob = _natten(proj3, _natten_bias_table(rpb[i]))

        w_router = jnp.concatenate(
            [w_rg[i], w_re[i], jnp.zeros((d, LANES - N_GROUPS - N_EXPERTS), F32)], axis=1).astype(F32)
        wr_hi = w_router.astype(BF16)
        wr_lo = (w_router - wr_hi.astype(F32)).astype(BF16)
        h1, xn, meta, meta_t, cnt = _outproj(h, oa.reshape(t, -1), ob.reshape(t, -1), w_out[i].astype(BF16),
                                             g_ffn[i][None, :], jnp.concatenate([wr_hi, wr_lo], axis=1), tri, tm)

        counts = cnt[0, E_LANE0:E_LANE0 + N_EXPERTS].astype(jnp.int32)
        padded = (counts + MOE_BLOCK - 1) // MOE_BLOCK * MOE_BLOCK
        pad_end = jnp.cumsum(padded)
        pad_start = pad_end - padded
        mt = meta_t[0:2 * TOP_K].astype(jnp.int32)
        e_ids = jnp.arange(N_EXPERTS, dtype=jnp.int32)[None, :, None]
        start_of = jnp.sum(jnp.where(mt[:TOP_K, None, :] == e_ids, pad_start[None, :, None], 0), axis=1)
        dest = start_of + mt[TOP_K:]
        blk_start = jnp.arange(n_blocks, dtype=jnp.int32) * MOE_BLOCK
        block_e = jnp.minimum(jnp.sum(pad_end[None, :] <= blk_start[:, None], axis=1), N_EXPERTS - 1)
        block_e = block_e.astype(jnp.int32)
        n_used = (pad_end[-1:] // MOE_BLOCK).astype(jnp.int32)

        dest_flat = dest.reshape(TOP_K * t)

        xs = _dispatch(xn, dest_flat, pad_end.astype(jnp.int32), n_used, p_len, tmd)
        in_use = counts > 0
        expert_ord = (jnp.cumsum(in_use) - 1).astype(jnp.int32)
        ids = jnp.arange(N_EXPERTS, dtype=jnp.int32)
        later = jnp.where((ids[None, :] > ids[:, None]) & in_use[None, :], ids[None, :], N_EXPERTS)
        next_expert = jnp.min(later, axis=1)
        next_expert = jnp.where(next_expert == N_EXPERTS, -1, next_expert).astype(jnp.int32)
        ys = _experts(xs, block_e, n_used, next_expert, expert_ord, w1[i], w3[i], w2[i])
        h = _final(dest_flat, h1, meta, p[i].reshape(t, -1),
                   w_ple[i].astype(BF16), g_ple[i][None, :], g_plg[i][None, :], w_plg[i].astype(BF16), ys, tmf)
    return h.reshape(b, s, d)
```

```python
import functools
import math

import numpy as np
import jax
import jax.numpy as jnp
from jax import lax
from jax.experimental import pallas as pl
from jax.experimental.pallas import tpu as pltpu

F32 = jnp.float32
BF16 = jnp.bfloat16

LANES = 128
HA = 4
DA_QK = 64
DA_V = 128
HB = 16
DB_H = 32
GRID_W = 64
WIN_R = 8
WIN_C = 16
ROPE_THETA = 10000.0
EPS = 1e-6
NEG = -1e30
N_GROUPS = 4
EXPERTS_PER_GROUP = 8
N_EXPERTS = N_GROUPS * EXPERTS_PER_GROUP
TOP_K = 2
MOE_BLOCK = 256
E_LANE0 = N_GROUPS

VMEM_LIMIT = 56 * 1024 * 1024


def _cparams(sem):
    return pltpu.CompilerParams(dimension_semantics=sem, vmem_limit_bytes=VMEM_LIMIT)


def _rms_rows(x):
    return x * lax.rsqrt(jnp.mean(x * x, axis=-1, keepdims=True) + EPS)


SUBLANES = 8

def _store_token_tiles(ref, val, first=0):
    n = val.shape[0]
    for c in range(val.shape[1] // LANES):
        ref[pl.ds(first * SUBLANES + c, n, stride=SUBLANES), :] = val[:, c * LANES:(c + 1) * LANES]


def _load_token_tiles(ref, n, first=0):
    return jnp.concatenate(
        [ref[pl.ds(first * SUBLANES + c, n, stride=SUBLANES), :] for c in range(SUBLANES)], axis=1)


def _inproj_kernel(x_ref, gmix_ref, w_ref, bd64_ref, bd32_ref, gsec_ref, cos_ref, sin_ref, out_ref):
    tm = x_ref.shape[0]
    a = (_rms_rows(x_ref[...]) * gmix_ref[...]).astype(BF16)
    lane = lax.broadcasted_iota(jnp.int32, (tm, LANES), 1)
    first_half = (lane % 64) < 32
    cos = cos_ref[...]
    sin = sin_ref[...]
    for sec in range(6):
        y = jnp.dot(a, w_ref[:, sec * 512:(sec + 1) * 512], preferred_element_type=F32)
        if sec in (0, 1, 3, 4):
            bd = bd64_ref if sec < 2 else bd32_ref
            gi = {0: 0, 1: 1, 3: 2, 4: 3}[sec]
            ms = jnp.dot((y * y).astype(BF16), bd[...], preferred_element_type=F32)
            y = y * lax.rsqrt(ms + EPS) * gsec_ref[gi:gi + 1, :]
        for c in range(4):
            yc = y[:, c * LANES:(c + 1) * LANES]
            if sec < 2:
                rot = jnp.where(first_half, pltpu.roll(yc, 96, 1), pltpu.roll(yc, 32, 1))
                yc = yc * cos + rot * sin
            out_ref[:, sec * 512 + c * LANES: sec * 512 + (c + 1) * LANES] = yc.astype(BF16)


def _inproj(x2, g_mix, w_bf, bd64, bd32, gsec, cos_t, sin_t, seq, tm):
    t, d = x2.shape
    n = w_bf.shape[1]
    nsb = seq // tm
    const = lambda i: (0, 0)
    return pl.pallas_call(
        _inproj_kernel,
        grid=(t // tm,),
        in_specs=[
            pl.BlockSpec((tm, d), lambda i: (i, 0)),
            pl.BlockSpec((1, d), const),
            pl.BlockSpec((d, n), const),
            pl.BlockSpec((512, 512), const),
            pl.BlockSpec((512, 512), const),
            pl.BlockSpec((4, 512), const),
            pl.BlockSpec((tm, LANES), lambda i: (i % nsb, 0)),
            pl.BlockSpec((tm, LANES), lambda i: (i % nsb, 0)),
        ],
        out_specs=pl.BlockSpec((tm, n), lambda i: (i, 0)),
        out_shape=jax.ShapeDtypeStruct((t, n), BF16),
        compiler_params=_cparams(("parallel",)),
        name="inproj",
    )(x2, g_mix, w_bf, bd64, bd32, gsec, cos_t, sin_t)


QBLOCK_PAIRS_PER_BODY = 4

def _diffattn_kernel(lam_ref, gsub_ref, q_ref, k_ref, v_ref, o_ref, qt_scr, vt_scr, sa_scr, sb_scr, *, seq, tq, tk,
                     lam_init):
    nq = seq // tq
    nk = seq // tk
    s1 = jnp.sum(lam_ref[0:1, :] * lam_ref[1:2, :], axis=1, keepdims=True)
    s2 = jnp.sum(lam_ref[2:3, :] * lam_ref[3:4, :], axis=1, keepdims=True)
    lam = jnp.exp(s1) - jnp.exp(s2) + lam_init

    def tr_body(c, carry):
        off = pl.multiple_of(c * LANES, LANES)
        qt_scr[:, pl.ds(off, LANES)] = q_ref[0, pl.ds(off, LANES), :].astype(F32).T.astype(BF16)
        vt_scr[:, pl.ds(off, LANES)] = v_ref[0, pl.ds(off, LANES), :].astype(F32).T.astype(BF16)
        return carry
    lax.fori_loop(0, seq // LANES, tr_body, 0)

    row = lax.broadcasted_iota(jnp.int32, (LANES, tq), 0)

    def scores(qi, s_scr):
        qoff = pl.multiple_of(qi * tq, tq)
        qt = qt_scr[:, pl.ds(qoff, tq)]
        zero = jnp.zeros_like(qt)
        qm = jnp.concatenate([jnp.where(row < DA_QK, qt, zero), jnp.where(row >= DA_QK, qt, zero)], axis=1)
        mx = None
        for j in range(nk):
            st = jnp.dot(k_ref[0, j * tk:(j + 1) * tk, :], qm, preferred_element_type=F32)
            s_scr[j * tk:(j + 1) * tk, :] = st
            cm = jnp.max(st, axis=0, keepdims=True)
            mx = cm if mx is None else jnp.maximum(mx, cm)
        return mx

    def finish(qi, s_scr, m):
        l = jnp.zeros((1, 2 * tq), F32)
        acc = jnp.zeros((DA_V, 2 * tq), F32)
        for j in range(nk):
            p = jnp.exp2(s_scr[j * tk:(j + 1) * tk, :] - m)
            l = l + jnp.sum(p, axis=0, keepdims=True)
            acc = acc + jnp.dot(vt_scr[:, j * tk:(j + 1) * tk], p.astype(BF16), preferred_element_type=F32)
        o = acc * (1.0 / l)
        ot = o[:, :tq] - lam * o[:, tq:]
        ot = ot * lax.rsqrt(jnp.mean(ot * ot, axis=0, keepdims=True) + EPS) * (1.0 - lam_init)
        qoff = pl.multiple_of(qi * tq, tq)
        o_ref[0, pl.ds(qoff, tq), :] = (ot.T * gsub_ref[...]).astype(BF16)

    def body(i, m_a):
        for pair in range(QBLOCK_PAIRS_PER_BODY):
            q0 = 2 * (QBLOCK_PAIRS_PER_BODY * i + pair)
            m_b = scores(q0 + 1, sb_scr)
            finish(q0, sa_scr, m_a)
            m_a = scores(jnp.minimum(q0 + 2, nq - 1), sa_scr)
            finish(q0 + 1, sb_scr, m_b)
        return m_a

    lax.fori_loop(0, nq // (2 * QBLOCK_PAIRS_PER_BODY), body, scores(0, sa_scr))


def _diffattn(proj3, lam4, g_sub, lam_init, tq=128, tk=512):
    b, seq, _ = proj3.shape
    kern = functools.partial(_diffattn_kernel, seq=seq, tq=tq, tk=tk, lam_init=lam_init)
    return pl.pallas_call(
        kern,
        grid=(b, HA),
        in_specs=[
            pl.BlockSpec((4, DA_QK), lambda i, h: (0, 0)),
            pl.BlockSpec((1, DA_V), lambda i, h: (0, 0)),
            pl.BlockSpec((1, seq, LANES), lambda i, h: (i, 0, h)),
            pl.BlockSpec((1, seq, LANES), lambda i, h: (i, 0, HA + h)),
            pl.BlockSpec((1, seq, LANES), lambda i, h: (i, 0, 2 * HA + h)),
        ],
        out_specs=pl.BlockSpec((1, seq, LANES), lambda i, h: (i, 0, h)),
        out_shape=jax.ShapeDtypeStruct((b, seq, HA * DA_V), BF16),
        scratch_shapes=[pltpu.VMEM((LANES, seq), BF16), pltpu.VMEM((LANES, seq), BF16),
                        pltpu.VMEM((seq, 2 * tq), F32), pltpu.VMEM((seq, 2 * tq), F32)],
        compiler_params=_cparams(("parallel", "parallel")),
        name="diffattn",
    )(lam4, g_sub, proj3, proj3, proj3)


def _natten_kernel(bias_ref, q_ref, k_ref, v_ref, o_ref, *, rows):
    kr = min(WIN_R, rows)
    nkeys = kr * GRID_W
    hpg = LANES // DB_H
    lane_head = lax.broadcasted_iota(jnp.int32, (GRID_W, LANES), 1) // DB_H

    def row_step(r, carry):
        rs = jnp.clip(r - kr // 2, 0, rows - kr)
        base = rs - r + (WIN_R - 1)
        qoff = pl.multiple_of(r * GRID_W, GRID_W)
        koff = pl.multiple_of(rs * GRID_W, GRID_W)
        qr = q_ref[0, pl.ds(qoff, GRID_W), :]
        zero = jnp.zeros_like(qr)
        qs = jnp.concatenate([jnp.where(lane_head == h, qr, zero) for h in range(hpg)], axis=0)
        kw = k_ref[0, pl.ds(koff, nkeys), :]
        vw = v_ref[0, pl.ds(koff, nkeys), :]
        s = lax.dot_general(qs, kw, (((1,), (1,)), ((), ())), preferred_element_type=F32)
        bias = jnp.concatenate([bias_ref[0, base + 2 * i] for i in range(kr // 2)], axis=1)
        s = s + bias
        m = jnp.max(s, axis=1, keepdims=True)
        p = jnp.exp(s - m).astype(BF16)
        pv = jnp.dot(p, jnp.concatenate([vw, jnp.ones_like(vw)], axis=1), preferred_element_type=F32)
        o = pv[:, :LANES] * (1.0 / pv[:, LANES:])
        out = jnp.zeros((GRID_W, LANES), F32)
        for h in range(hpg):
            out = out + jnp.where(lane_head == h, o[h * GRID_W:(h + 1) * GRID_W, :], 0.0)
        o_ref[0, pl.ds(qoff, GRID_W), :] = out.astype(BF16)
        return carry

    lax.fori_loop(0, rows, row_step, 0, unroll=16)


def _natten(proj3, bias_tab):
    b, seq, _ = proj3.shape
    rows = seq // GRID_W
    ng = HB * DB_H // LANES
    nro = bias_tab.shape[1]
    kern = functools.partial(_natten_kernel, rows=rows)
    cb = 3 * HA
    return pl.pallas_call(
        kern,
        grid=(b, ng),
        in_specs=[
            pl.BlockSpec((1, nro, bias_tab.shape[2], LANES), lambda i, g: (g, 0, 0, 0)),
            pl.BlockSpec((1, seq, LANES), lambda i, g: (i, 0, cb + g)),
            pl.BlockSpec((1, seq, LANES), lambda i, g: (i, 0, cb + ng + g)),
            pl.BlockSpec((1, seq, LANES), lambda i, g: (i, 0, cb + 2 * ng + g)),
        ],
        out_specs=pl.BlockSpec((1, seq, LANES), lambda i, g: (i, 0, g)),
        out_shape=jax.ShapeDtypeStruct((b, seq, HB * DB_H), BF16),
        compiler_params=_cparams(("parallel", "parallel")),
        name="natten",
    )(bias_tab, proj3, proj3, proj3)


def _natten_bias_table(rpb):
    q = np.arange(GRID_W)[:, None]
    kc = np.arange(GRID_W)[None, :]
    cs = np.clip(q - WIN_C // 2, 0, GRID_W - WIN_C)
    valid = (kc >= cs) & (kc < cs + WIN_C)
    col_off = np.clip(kc - q + (WIN_C - 1), 0, 2 * WIN_C - 2)
    ncol = 2 * WIN_C - 1
    nro = 2 * WIN_R - 2
    pick = np.zeros((2, ncol, GRID_W, 2 * GRID_W), np.float32)
    for j in range(2):
        pick[j, :, :, j * GRID_W:(j + 1) * GRID_W] = (np.arange(ncol)[:, None, None] == col_off[None]) & valid[None]
    negmask = np.tile(np.where(valid, 0.0, NEG).astype(np.float32), (1, 2))
    hpg = LANES // DB_H
    r4 = rpb.astype(F32).reshape(HB // hpg, hpg, 2 * WIN_R - 1, ncol)
    rp = jnp.stack([r4[:, :, :nro], r4[:, :, 1:]], axis=0)
    tab = jnp.einsum('jghrc,jcqk->grhqk', rp, jnp.asarray(pick), precision=lax.Precision.HIGHEST) + negmask
    return tab.reshape(HB // hpg, nro, hpg * GRID_W, LANES)


def _outproj_kernel(x_ref, oa_ref, ob_ref, wo_ref, gffn_ref, wr_ref, tri_ref,
                    h_ref, xn_ref, meta_ref, meta_t_ref, cnt_ref, run_scr):
    tm = x_ref.shape[0]
    da = oa_ref.shape[1]

    @pl.when(pl.program_id(0) == 0)
    def _():
        run_scr[...] = jnp.zeros_like(run_scr)

    attn = jnp.dot(oa_ref[...], wo_ref[:da, :], preferred_element_type=F32)
    attn = attn + jnp.dot(ob_ref[...], wo_ref[da:, :], preferred_element_type=F32)
    h = x_ref[...] + attn
    h_ref[...] = h
    xn = _rms_rows(h) * gffn_ref[...]
    _store_token_tiles(xn_ref, xn)

    xh = xn.astype(BF16)
    xl = (xn - xh.astype(F32)).astype(BF16)
    hh_hl = jnp.dot(xh, wr_ref[...], preferred_element_type=F32)
    lg = hh_hl[:, :LANES] + hh_hl[:, LANES:] + jnp.dot(xl, wr_ref[:, :LANES], preferred_element_type=F32)
    lane = lax.broadcasted_iota(jnp.int32, (tm, LANES), 1).astype(F32)
    ninf = -jnp.inf
    is_g = lane < N_GROUPS
    gl = jnp.where(is_g, lg, ninf)
    gmax = jnp.max(gl, axis=1, keepdims=True)
    gidx = jnp.min(jnp.where(gl == gmax, lane, float(LANES)), axis=1, keepdims=True)
    gsum = jnp.sum(jnp.where(is_g, jnp.exp(gl - gmax), 0.0), axis=1, keepdims=True)
    g_gate = 1.0 / gsum
    lo = E_LANE0 + gidx * EXPERTS_PER_GROUP
    is_e = (lane >= lo) & (lane < lo + EXPERTS_PER_GROUP)
    el = jnp.where(is_e, lg, ninf)
    l1 = jnp.max(el, axis=1, keepdims=True)
    i1 = jnp.min(jnp.where(el == l1, lane, float(LANES)), axis=1, keepdims=True)
    el2 = jnp.where(lane == i1, ninf, el)
    l2 = jnp.max(el2, axis=1, keepdims=True)
    i2 = jnp.min(jnp.where(el2 == l2, lane, float(LANES)), axis=1, keepdims=True)
    r = jnp.exp(l2 - l1)
    w1 = g_gate / (1.0 + r)
    w2 = w1 * r

    sel1 = lane == i1
    sel2 = lane == i2
    oh = jnp.where(sel1 | sel2, 1.0, 0.0)
    before = jnp.dot(tri_ref[...], oh.astype(BF16), preferred_element_type=F32) + run_scr[...]
    rank1 = jnp.sum(jnp.where(sel1, before, 0.0), axis=1, keepdims=True)
    rank2 = jnp.sum(jnp.where(sel2, before, 0.0), axis=1, keepdims=True)
    run = run_scr[...] + jnp.sum(oh, axis=0, keepdims=True)
    run_scr[...] = run
    cnt_ref[...] = run

    meta = jnp.where(lane == 0, i1 - E_LANE0, 0.0)
    meta = jnp.where(lane == 1, i2 - E_LANE0, meta)
    meta = jnp.where(lane == 2, rank1, meta)
    meta = jnp.where(lane == 3, rank2, meta)
    meta = jnp.where(lane == 4, w1, meta)
    meta = jnp.where(lane == 5, w2, meta)
    meta_ref[...] = meta
    meta_t_ref[...] = meta.T[:SUBLANES, :]


def _outproj(x2, oa2, ob2, wo_bf, g_ffn, w_router, tri, tm):
    t, d = x2.shape
    da = oa2.shape[1]
    const = lambda i: (0, 0)
    row = lambda i: (i, 0)
    return pl.pallas_call(
        _outproj_kernel,
        grid=(t // tm,),
        in_specs=[
            pl.BlockSpec((tm, d), row),
            pl.BlockSpec((tm, da), row),
            pl.BlockSpec((tm, d - da), row),
            pl.BlockSpec((d, d), const),
            pl.BlockSpec((1, d), const),
            pl.BlockSpec((d, 2 * LANES), const),
            pl.BlockSpec((tm, tm), const),
        ],
        out_specs=[
            pl.BlockSpec((tm, d), row),
            pl.BlockSpec((tm * SUBLANES, LANES), row),
            pl.BlockSpec((tm, LANES), row),
            pl.BlockSpec((SUBLANES, tm), lambda i: (0, i)),
            pl.BlockSpec((1, LANES), const),
        ],
        out_shape=[
            jax.ShapeDtypeStruct((t, d), F32),
            jax.ShapeDtypeStruct((t * SUBLANES, LANES), F32),
            jax.ShapeDtypeStruct((t, LANES), F32),
            jax.ShapeDtypeStruct((SUBLANES, t), F32),
            jax.ShapeDtypeStruct((1, LANES), F32),
        ],
        scratch_shapes=[pltpu.VMEM((1, LANES), F32)],
        compiler_params=_cparams(("arbitrary",)),
        name="outproj_router",
    )(x2, oa2, ob2, wo_bf, g_ffn, w_router, tri)


DMA_ISSUE_UNROLL = 8
DMA_PRIORITIES = 2


def _dispatch_kernel(dest_ref, pend_ref, nu_ref, xn_ref, xs_ref, zbuf, sem, zsem):
    tm = xn_ref.shape[0] // SUBLANES
    t_total = pl.num_programs(0) * tm
    base = pl.program_id(0) * tm
    n_blocks = xs_ref.shape[0] // (MOE_BLOCK * SUBLANES)

    @pl.when(pl.program_id(0) == 0)
    def _():
        zbuf[...] = jnp.zeros_like(zbuf)

        def zero_block(blk):
            off = pl.multiple_of(blk * (MOE_BLOCK * SUBLANES), MOE_BLOCK * SUBLANES)
            return pltpu.make_async_copy(zbuf, xs_ref.at[pl.ds(off, MOE_BLOCK * SUBLANES), :], zsem)

        def for_each_zeroed_block(fn):
            def per_expert(e, c):
                end = pend_ref[e]
                start = jnp.where(e == 0, 0, pend_ref[jnp.maximum(e - 1, 0)])

                @pl.when(end > start)
                def _():
                    fn(zero_block(end // MOE_BLOCK - 1))
                return c
            lax.fori_loop(0, N_EXPERTS, per_expert, 0)

            def per_tail_block(blk, c):
                fn(zero_block(blk))
                return c
            lax.fori_loop(nu_ref[0], n_blocks, per_tail_block, 0)

        for_each_zeroed_block(lambda cp: cp.start())
        for_each_zeroed_block(lambda cp: cp.wait())

    def issue(r, c):
        src = xn_ref.at[pl.ds(pl.multiple_of(r * SUBLANES, SUBLANES), SUBLANES), :]
        for slot in range(TOP_K):
            d = dest_ref[slot * t_total + base + r]
            dst = xs_ref.at[pl.ds(pl.multiple_of(d * SUBLANES, SUBLANES), SUBLANES), :]
            pltpu.make_async_copy(src, dst, sem).start(priority=slot % DMA_PRIORITIES)
        return c
    lax.fori_loop(0, tm, issue, 0, unroll=DMA_ISSUE_UNROLL)

    for slot in range(TOP_K):
        pltpu.make_async_copy(xn_ref, xs_ref.at[pl.ds(0, tm * SUBLANES), :], sem).wait()


def _dispatch(xn_tiles, dest_flat, pad_end, n_used, p_len, tm):
    t = xn_tiles.shape[0] // SUBLANES
    grid_spec = pltpu.PrefetchScalarGridSpec(
        num_scalar_prefetch=3,
        grid=(t // tm,),
        in_specs=[pl.BlockSpec((tm * SUBLANES, LANES), lambda i, dest, pend, nu: (i, 0))],
        out_specs=pl.BlockSpec(memory_space=pl.ANY),
        scratch_shapes=[pltpu.VMEM((MOE_BLOCK * SUBLANES, LANES), F32),
                        pltpu.SemaphoreType.DMA(()), pltpu.SemaphoreType.DMA(())],
    )
    return pl.pallas_call(
        _dispatch_kernel,
        grid_spec=grid_spec,
        out_shape=jax.ShapeDtypeStruct((p_len * SUBLANES, LANES), F32),
        compiler_params=_cparams(("arbitrary",)),
        name="dispatch",
    )(dest_flat, pad_end, n_used, xn_tiles)


def _experts_kernel(be_ref, nu_ref, nxt_ref, ord_ref, xs_ref, w1_hbm, w3_hbm, w2_hbm, ys_ref,
                    w1_scr, w3_scr, w2_scr, w1_buf, w3_buf, w2_buf, wsems):
    i = pl.program_id(0)

    def weight_copies(e, slot):
        return (pltpu.make_async_copy(w1_hbm.at[e], w1_buf.at[slot], wsems.at[slot, 0]),
                pltpu.make_async_copy(w3_hbm.at[e], w3_buf.at[slot], wsems.at[slot, 1]),
                pltpu.make_async_copy(w2_hbm.at[e], w2_buf.at[slot], wsems.at[slot, 2]))

    @pl.when(i < nu_ref[0])
    def _():
        e = be_ref[i]
        prev = be_ref[jnp.maximum(i - 1, 0)]
        slot = ord_ref[e] % 2

        @pl.when(i == 0)
        def _():
            for cp in weight_copies(e, slot):
                cp.start()

        @pl.when((i == 0) | (e != prev))
        def _():
            for cp in weight_copies(e, slot):
                cp.wait()
            nxt = nxt_ref[e]

            @pl.when(nxt >= 0)
            def _():
                for cp in weight_copies(nxt, 1 - slot):
                    cp.start()
            w1_scr[...] = w1_buf[slot].astype(BF16)
            w3_scr[...] = w3_buf[slot].astype(BF16)
            w2_scr[...] = w2_buf[slot].astype(BF16)

        x = _load_token_tiles(xs_ref, MOE_BLOCK).astype(BF16)
        a = jnp.dot(x, w1_scr[...], preferred_element_type=F32)
        b = jnp.dot(x, w3_scr[...], preferred_element_type=F32)
        hdn = a * (1.0 / (1.0 + jnp.exp(-a))) * b
        _store_token_tiles(ys_ref, jnp.dot(hdn.astype(BF16), w2_scr[...], preferred_element_type=F32))

    @pl.when(i >= nu_ref[0])
    def _():
        ys_ref[...] = jnp.zeros_like(ys_ref)


def _experts(xs, block_e, n_used, next_expert, expert_ord, w1, w3, w2):
    p_len = xs.shape[0] // SUBLANES
    n_blocks = p_len // MOE_BLOCK
    d, de = w1.shape[1], w1.shape[2]
    blk = lambda i, be, nu, nxt, od: (jnp.maximum(jnp.minimum(i, nu[0] - 1), 0), 0)
    hbm = pl.BlockSpec(memory_space=pl.ANY)
    grid_spec = pltpu.PrefetchScalarGridSpec(
        num_scalar_prefetch=4,
        grid=(n_blocks,),
        in_specs=[pl.BlockSpec((MOE_BLOCK * SUBLANES, LANES), blk), hbm, hbm, hbm],
        out_specs=pl.BlockSpec((MOE_BLOCK * SUBLANES, LANES), lambda i, be, nu, nxt, od: (i, 0)),
        scratch_shapes=[pltpu.VMEM((d, de), BF16), pltpu.VMEM((d, de), BF16), pltpu.VMEM((de, d), BF16),
                        pltpu.VMEM((2, d, de), F32), pltpu.VMEM((2, d, de), F32), pltpu.VMEM((2, de, d), F32),
                        pltpu.SemaphoreType.DMA((2, 3))],
    )
    return pl.pallas_call(
        _experts_kernel,
        grid_spec=grid_spec,
        out_shape=jax.ShapeDtypeStruct((p_len * SUBLANES, LANES), F32),
        compiler_params=_cparams(("arbitrary",)),
        name="experts",
    )(block_e, n_used, next_expert, expert_ord, xs, w1, w3, w2)


def _final_kernel(dest_ref, h_ref, meta_ref, p_ref, wple_ref, gple_ref, gplg_ref, wplg_ref, ys_ref,
                  out_ref, gbuf_a, gbuf_b, sems):
    th = h_ref.shape[0] // 2
    step = pl.program_id(0)
    n_tiles = 2 * pl.num_programs(0)
    t_total = n_tiles * th

    def gather_rows(tile, gbuf, sem, unrolled):
        def issue(r, c):
            for slot in range(TOP_K):
                d = dest_ref[slot * t_total + tile * th + r]
                src = ys_ref.at[pl.ds(pl.multiple_of(d * SUBLANES, SUBLANES), SUBLANES), :]
                dst = gbuf.at[slot, pl.ds(pl.multiple_of(r * SUBLANES, SUBLANES), SUBLANES), :]
                pltpu.make_async_copy(src, dst, sem).start(priority=slot % DMA_PRIORITIES)
            return c
        if unrolled:
            for r in range(th):
                issue(r, 0)
        else:
            lax.fori_loop(0, th, issue, 0, unroll=DMA_ISSUE_UNROLL)

    def wait_rows(gbuf, sem):
        for slot in range(TOP_K):
            pltpu.make_async_copy(ys_ref.at[pl.ds(0, th * SUBLANES), :], gbuf.at[slot], sem).wait()

    def combine(half, gbuf):
        rows = slice(half * th, (half + 1) * th)
        emb = jnp.dot(p_ref[rows, :].astype(BF16), wple_ref[...], preferred_element_type=F32)
        emb = _rms_rows(emb) * gple_ref[...]
        meta = meta_ref[rows, :]
        y1 = _load_token_tiles(gbuf.at[0], th)
        y2 = _load_token_tiles(gbuf.at[1], th)
        h = h_ref[rows, :] + meta[:, 4:5] * y1 + meta[:, 5:6] * y2
        hn = (_rms_rows(h) * gplg_ref[...]).astype(BF16)
        z = jnp.dot(hn, wplg_ref[...], preferred_element_type=F32)
        out_ref[rows, :] = h + emb * (1.0 / (1.0 + jnp.exp(-z)))

    @pl.when(step == 0)
    def _():
        gather_rows(0, gbuf_a, sems.at[0], unrolled=False)

    wait_rows(gbuf_a, sems.at[0])
    gather_rows(2 * step + 1, gbuf_b, sems.at[1], unrolled=True)
    combine(0, gbuf_a)
    wait_rows(gbuf_b, sems.at[1])
    gather_rows(jnp.minimum(2 * step + 2, n_tiles - 1), gbuf_a, sems.at[0], unrolled=True)
    combine(1, gbuf_b)

    @pl.when(step == pl.num_programs(0) - 1)
    def _():
        wait_rows(gbuf_a, sems.at[0])


def _final(dest_flat, h2, meta, p2, wple_bf, g_ple, g_plg, wplg_bf, ys, tm):
    t, d = h2.shape
    dp = p2.shape[1]
    const = lambda i, dest: (0, 0)
    row = lambda i, dest: (i, 0)
    grid_spec = pltpu.PrefetchScalarGridSpec(
        num_scalar_prefetch=1,
        grid=(t // tm,),
        in_specs=[
            pl.BlockSpec((tm, d), row),
            pl.BlockSpec((tm, LANES), row),
            pl.BlockSpec((tm, dp), row),
            pl.BlockSpec((dp, d), const),
            pl.BlockSpec((1, d), const),
            pl.BlockSpec((1, d), const),
            pl.BlockSpec((d, d), const),
            pl.BlockSpec(memory_space=pl.ANY),
        ],
        out_specs=pl.BlockSpec((tm, d), row),
        scratch_shapes=[pltpu.VMEM((TOP_K, tm // 2 * SUBLANES, LANES), F32),
                        pltpu.VMEM((TOP_K, tm // 2 * SUBLANES, LANES), F32), pltpu.SemaphoreType.DMA((2,))],
    )
    return pl.pallas_call(
        _final_kernel,
        grid_spec=grid_spec,
        out_shape=jax.ShapeDtypeStruct((t, d), F32),
        compiler_params=_cparams(("arbitrary",)),
        name="combine_ple",
    )(dest_flat, h2, meta, p2, wple_bf, g_ple, g_plg, wplg_bf, ys)


def _block_diag_mean(seg):
    m = np.kron(np.eye(512 // seg), np.full((seg, seg), 1.0 / seg))
    return jnp.asarray(m, BF16)


def _rope_tables(seq):
    inv = ROPE_THETA ** (-jnp.arange(0, DA_QK, 2, dtype=F32) / DA_QK)
    ang = jnp.arange(seq, dtype=F32)[:, None] * inv[None, :]
    cos, sin = jnp.cos(ang), jnp.sin(ang)
    cos_t = jnp.tile(cos, (1, LANES // cos.shape[1]))
    sin_t = jnp.tile(jnp.concatenate([-sin, sin], axis=1), (1, LANES // (2 * sin.shape[1])))
    return cos_t, sin_t


def _tile_for(n, pref):
    while n % pref:
        pref //= 2
    return pref


def kernel(x, p, g_mix, w_in, g_qa, g_ka, lam_q1, lam_k1, lam_q2, lam_k2, g_sub, g_qb, g_kb, rpb, w_out,
           g_ffn, w_rg, w_re, w1, w3, w2, g_plg, w_plg, w_ple, g_ple):
    b, s, d = x.shape
    assert d == SUBLANES * LANES, "token-tile layout holds one token per (8, 128) f32 tile"
    t = b * s
    depth = w_in.shape[0]
    tm = _tile_for(s, 512)
    tmd = _tile_for(t, 512)
    tmf = _tile_for(t, 512)
    n_slots = t * TOP_K
    n_blocks = n_slots // MOE_BLOCK + N_EXPERTS
    p_len = n_blocks * MOE_BLOCK

    bd64 = _block_diag_mean(DA_QK)
    bd32 = _block_diag_mean(DB_H)
    cos_t, sin_t = _rope_tables(s)
    tri = jnp.asarray(np.tril(np.ones((tm, tm)), -1), BF16)

    h = x.reshape(t, d)
    for i in range(depth):
        lam_init = 0.8 - 0.6 * math.exp(-0.3 * i)
        gsec = jnp.stack([
            jnp.tile(g_qa[i], 512 // DA_QK) * (DA_QK ** -0.5 * math.log2(math.e)),
            jnp.tile(g_ka[i], 512 // DA_QK),
            jnp.tile(g_qb[i], 512 // DB_H) * DB_H ** -0.5,
            jnp.tile(g_kb[i], 512 // DB_H),
        ]).astype(F32)
        proj = _inproj(h, g_mix[i][None, :], w_in[i].astype(BF16), bd64, bd32, gsec, cos_t, sin_t, s, tm)
        proj3 = proj.reshape(b, s, proj.shape[1])

        lam4 = jnp.stack([lam_q1[i], lam_k1[i], lam_q2[i], lam_k2[i]]).astype(F32)
        oa = _diffattn(proj3, lam4, g_sub[i][None, :].astype(F32), lam_init)
        ob = _natten(proj3, _natten_bias_table(rpb[i]))

        w_router = jnp.concatenate(
            [w_rg[i], w_re[i], jnp.zeros((d, LANES - N_GROUPS - N_EXPERTS), F32)], axis=1).astype(F32)
        wr_hi = w_router.astype(BF16)
        wr_lo = (w_router - wr_hi.astype(F32)).astype(BF16)
        h1, xn, meta, meta_t, cnt = _outproj(h, oa.reshape(t, -1), ob.reshape(t, -1), w_out[i].astype(BF16),
                                             g_ffn[i][None, :], jnp.concatenate([wr_hi, wr_lo], axis=1), tri, tm)

        counts = cnt[0, E_LANE0:E_LANE0 + N_EXPERTS].astype(jnp.int32)
        padded = (counts + MOE_BLOCK - 1) // MOE_BLOCK * MOE_BLOCK
        pad_end = jnp.cumsum(padded)
        pad_start = pad_end - padded
        mt = meta_t[0:2 * TOP_K].astype(jnp.int32)
        e_ids = jnp.arange(N_EXPERTS, dtype=jnp.int32)[None, :, None]
        start_of = jnp.sum(jnp.where(mt[:TOP_K, None, :] == e_ids, pad_start[None, :, None], 0), axis=1)
        dest = start_of + mt[TOP_K:]
        blk_start = jnp.arange(n_blocks, dtype=jnp.int32) * MOE_BLOCK
        block_e = jnp.minimum(jnp.sum(pad_end[None, :] <= blk_start[:, None], axis=1), N_EXPERTS - 1)
        block_e = block_e.astype(jnp.int32)
        n_used = (pad_end[-1:] // MOE_BLOCK).astype(jnp.int32)

        dest_flat = dest.reshape(TOP_K * t)

        xs = _dispatch(xn, dest_flat, pad_end.astype(jnp.int32), n_used, p_len, tmd)
        in_use = counts > 0
        expert_ord = (jnp.cumsum(in_use) - 1).astype(jnp.int32)
        ids = jnp.arange(N_EXPERTS, dtype=jnp.int32)
        later = jnp.where((ids[None, :] > ids[:, None]) & in_use[None, :], ids[None, :], N_EXPERTS)
        next_expert = jnp.min(later, axis=1)
        next_expert = jnp.where(next_expert == N_EXPERTS, -1, next_expert).astype(jnp.int32)
        ys = _experts(xs, block_e, n_used, next_expert, expert_ord, w1[i], w3[i], w2[i])
        h = _final(dest_flat, h1, meta, p[i].reshape(t, -1),
                   w_ple[i].astype(BF16), g_ple[i][None, :], g_plg[i][None, :], w_plg[i].astype(BF16), ys, tmf)
    return h.reshape(b, s, d)
```

```python
import functools
import math

import numpy as np
import jax
import jax.numpy as jnp
from jax import lax
from jax.experimental import pallas as pl
from jax.experimental.pallas import tpu as pltpu

F32 = jnp.float32
BF16 = jnp.bfloat16

LANES = 128
HA = 4
DA_QK = 64
DA_V = 128
HB = 16
DB_H = 32
GRID_W = 64
WIN_R = 8
WIN_C = 16
ROPE_THETA = 10000.0
EPS = 1e-6
NEG = -1e30
N_GROUPS = 4
EXPERTS_PER_GROUP = 8
N_EXPERTS = N_GROUPS * EXPERTS_PER_GROUP
TOP_K = 2
MOE_BLOCK = 256
E_LANE0 = N_GROUPS

VMEM_LIMIT = 56 * 1024 * 1024


def _cparams(sem):
    return pltpu.CompilerParams(dimension_semantics=sem, vmem_limit_bytes=VMEM_LIMIT)


def _rms_rows(x):
    return x * lax.rsqrt(jnp.mean(x * x, axis=-1, keepdims=True) + EPS)


SUBLANES = 8

def _store_token_tiles(ref, val, first=0):
    n = val.shape[0]
    for c in range(val.shape[1] // LANES):
        ref[pl.ds(first * SUBLANES + c, n, stride=SUBLANES), :] = val[:, c * LANES:(c + 1) * LANES]


def _load_token_tiles(ref, n, first=0):
    return jnp.concatenate(
        [ref[pl.ds(first * SUBLANES + c, n, stride=SUBLANES), :] for c in range(SUBLANES)], axis=1)


def _inproj_kernel(x_ref, gmix_ref, w_ref, bd64_ref, bd32_ref, gsec_ref, cos_ref, sin_ref, out_ref, qvt_ref):
    tm = x_ref.shape[0]
    a = (_rms_rows(x_ref[...]) * gmix_ref[...]).astype(BF16)
    lane = lax.broadcasted_iota(jnp.int32, (tm, LANES), 1)
    first_half = (lane % 64) < 32
    cos = cos_ref[...]
    sin = sin_ref[...]
    out_col = {1: 0, 3: 1, 4: 2, 5: 3}
    qvt_row = {0: 0, 2: 1}
    for sec in range(6):
        y = jnp.dot(a, w_ref[:, sec * 512:(sec + 1) * 512], preferred_element_type=F32)
        if sec in (0, 1, 3, 4):
            bd = bd64_ref if sec < 2 else bd32_ref
            gi = {0: 0, 1: 1, 3: 2, 4: 3}[sec]
            ms = jnp.dot((y * y).astype(BF16), bd[...], preferred_element_type=F32)
            y = y * lax.rsqrt(ms + EPS) * gsec_ref[gi:gi + 1, :]
        for c in range(4):
            yc = y[:, c * LANES:(c + 1) * LANES]
            if sec < 2:
                rot = jnp.where(first_half, pltpu.roll(yc, 96, 1), pltpu.roll(yc, 32, 1))
                yc = yc * cos + rot * sin
            if sec in qvt_row:
                r0 = qvt_row[sec] * 512 + c * LANES
                qvt_ref[0, r0:r0 + LANES, :] = yc.T.astype(BF16)
            else:
                c0 = out_col[sec] * 512 + c * LANES
                out_ref[:, c0:c0 + LANES] = yc.astype(BF16)


def _inproj(x2, g_mix, w_bf, bd64, bd32, gsec, cos_t, sin_t, seq, tm):
    t, d = x2.shape
    n = w_bf.shape[1]
    nsb = seq // tm
    n_tok_major = 4 * 512
    n_feat_major = 2 * 512
    const = lambda i: (0, 0)
    return pl.pallas_call(
        _inproj_kernel,
        grid=(t // tm,),
        in_specs=[
            pl.BlockSpec((tm, d), lambda i: (i, 0)),
            pl.BlockSpec((1, d), const),
            pl.BlockSpec((d, n), const),
            pl.BlockSpec((512, 512), const),
            pl.BlockSpec((512, 512), const),
            pl.BlockSpec((4, 512), const),
            pl.BlockSpec((tm, LANES), lambda i: (i % nsb, 0)),
            pl.BlockSpec((tm, LANES), lambda i: (i % nsb, 0)),
        ],
        out_specs=[pl.BlockSpec((tm, n_tok_major), lambda i: (i, 0)),
                   pl.BlockSpec((1, n_feat_major, tm), lambda i: (i // nsb, 0, i % nsb))],
        out_shape=[jax.ShapeDtypeStruct((t, n_tok_major), BF16),
                   jax.ShapeDtypeStruct((t // seq, n_feat_major, seq), BF16)],
        compiler_params=_cparams(("parallel",)),
        name="inproj",
    )(x2, g_mix, w_bf, bd64, bd32, gsec, cos_t, sin_t)


QBLOCK_PAIRS_PER_BODY = 4

def _diffattn_kernel(lam_ref, gsub_ref, qt_ref, k_ref, vt_ref, o_ref, sa_scr, sb_scr, *, seq, tq, tk, lam_init):
    nq = seq // tq
    nk = seq // tk
    s1 = jnp.sum(lam_ref[0:1, :] * lam_ref[1:2, :], axis=1, keepdims=True)
    s2 = jnp.sum(lam_ref[2:3, :] * lam_ref[3:4, :], axis=1, keepdims=True)
    lam = jnp.exp(s1) - jnp.exp(s2) + lam_init

    row = lax.broadcasted_iota(jnp.int32, (LANES, tq), 0)

    def scores(qi, s_scr):
        qoff = pl.multiple_of(qi * tq, tq)
        qt = qt_ref[0, :, pl.ds(qoff, tq)]
        zero = jnp.zeros_like(qt)
        qm = jnp.concatenate([jnp.where(row < DA_QK, qt, zero), jnp.where(row >= DA_QK, qt, zero)], axis=1)
        mx = None
        for j in range(nk):
            st = jnp.dot(k_ref[0, j * tk:(j + 1) * tk, :], qm, preferred_element_type=F32)
            s_scr[j * tk:(j + 1) * tk, :] = st
            cm = jnp.max(st, axis=0, keepdims=True)
            mx = cm if mx is None else jnp.maximum(mx, cm)
        return mx

    def finish(qi, s_scr, m):
        l = jnp.zeros((1, 2 * tq), F32)
        acc = jnp.zeros((DA_V, 2 * tq), F32)
        for j in range(nk):
            p = jnp.exp2(s_scr[j * tk:(j + 1) * tk, :] - m)
            l = l + jnp.sum(p, axis=0, keepdims=True)
            acc = acc + jnp.dot(vt_ref[0, :, j * tk:(j + 1) * tk], p.astype(BF16), preferred_element_type=F32)
        o = acc * (1.0 / l)
        ot = o[:, :tq] - lam * o[:, tq:]
        ot = ot * lax.rsqrt(jnp.mean(ot * ot, axis=0, keepdims=True) + EPS) * (1.0 - lam_init)
        qoff = pl.multiple_of(qi * tq, tq)
        o_ref[0, pl.ds(qoff, tq), :] = (ot.T * gsub_ref[...]).astype(BF16)

    def body(i, m_a):
        for pair in range(QBLOCK_PAIRS_PER_BODY):
            q0 = 2 * (QBLOCK_PAIRS_PER_BODY * i + pair)
            m_b = scores(q0 + 1, sb_scr)
            finish(q0, sa_scr, m_a)
            m_a = scores(jnp.minimum(q0 + 2, nq - 1), sa_scr)
            finish(q0 + 1, sb_scr, m_b)
        return m_a

    lax.fori_loop(0, nq // (2 * QBLOCK_PAIRS_PER_BODY), body, scores(0, sa_scr))


def _diffattn(proj3, qvt, lam4, g_sub, lam_init, tq=128, tk=512):
    b, seq, _ = proj3.shape
    kern = functools.partial(_diffattn_kernel, seq=seq, tq=tq, tk=tk, lam_init=lam_init)
    return pl.pallas_call(
        kern,
        grid=(b, HA),
        in_specs=[
            pl.BlockSpec((4, DA_QK), lambda i, h: (0, 0)),
            pl.BlockSpec((1, DA_V), lambda i, h: (0, 0)),
            pl.BlockSpec((1, LANES, seq), lambda i, h: (i, h, 0)),
            pl.BlockSpec((1, seq, LANES), lambda i, h: (i, 0, h)),
            pl.BlockSpec((1, LANES, seq), lambda i, h: (i, HA + h, 0)),
        ],
        out_specs=pl.BlockSpec((1, seq, LANES), lambda i, h: (i, 0, h)),
        out_shape=jax.ShapeDtypeStruct((b, seq, HA * DA_V), BF16),
        scratch_shapes=[pltpu.VMEM((seq, 2 * tq), F32), pltpu.VMEM((seq, 2 * tq), F32)],
        compiler_params=_cparams(("parallel", "parallel")),
        name="diffattn",
    )(lam4, g_sub, qvt, proj3, qvt)


def _natten_kernel(bias_ref, q_ref, k_ref, v_ref, o_ref, *, rows):
    kr = min(WIN_R, rows)
    nkeys = kr * GRID_W
    hpg = LANES // DB_H
    lane_head = lax.broadcasted_iota(jnp.int32, (GRID_W, LANES), 1) // DB_H

    def row_step(r, carry):
        rs = jnp.clip(r - kr // 2, 0, rows - kr)
        base = rs - r + (WIN_R - 1)
        qoff = pl.multiple_of(r * GRID_W, GRID_W)
        koff = pl.multiple_of(rs * GRID_W, GRID_W)
        qr = q_ref[0, pl.ds(qoff, GRID_W), :]
        zero = jnp.zeros_like(qr)
        qs = jnp.concatenate([jnp.where(lane_head == h, qr, zero) for h in range(hpg)], axis=0)
        kw = k_ref[0, pl.ds(koff, nkeys), :]
        vw = v_ref[0, pl.ds(koff, nkeys), :]
        s = lax.dot_general(qs, kw, (((1,), (1,)), ((), ())), preferred_element_type=F32)
        bias = jnp.concatenate([bias_ref[0, base + 2 * i] for i in range(kr // 2)], axis=1)
        s = s + bias
        m = jnp.max(s, axis=1, keepdims=True)
        p = jnp.exp(s - m).astype(BF16)
        pv = jnp.dot(p, jnp.concatenate([vw, jnp.ones_like(vw)], axis=1), preferred_element_type=F32)
        o = pv[:, :LANES] * (1.0 / pv[:, LANES:])
        out = jnp.zeros((GRID_W, LANES), F32)
        for h in range(hpg):
            out = out + jnp.where(lane_head == h, o[h * GRID_W:(h + 1) * GRID_W, :], 0.0)
        o_ref[0, pl.ds(qoff, GRID_W), :] = out.astype(BF16)
        return carry

    lax.fori_loop(0, rows, row_step, 0, unroll=16)


def _natten(proj3, bias_tab):
    b, seq, _ = proj3.shape
    rows = seq // GRID_W
    ng = HB * DB_H // LANES
    nro = bias_tab.shape[1]
    kern = functools.partial(_natten_kernel, rows=rows)
    cb = HA
    return pl.pallas_call(
        kern,
        grid=(b, ng),
        in_specs=[
            pl.BlockSpec((1, nro, bias_tab.shape[2], LANES), lambda i, g: (g, 0, 0, 0)),
            pl.BlockSpec((1, seq, LANES), lambda i, g: (i, 0, cb + g)),
            pl.BlockSpec((1, seq, LANES), lambda i, g: (i, 0, cb + ng + g)),
            pl.BlockSpec((1, seq, LANES), lambda i, g: (i, 0, cb + 2 * ng + g)),
        ],
        out_specs=pl.BlockSpec((1, seq, LANES), lambda i, g: (i, 0, g)),
        out_shape=jax.ShapeDtypeStruct((b, seq, HB * DB_H), BF16),
        compiler_params=_cparams(("parallel", "parallel")),
        name="natten",
    )(bias_tab, proj3, proj3, proj3)


def _natten_bias_table(rpb):
    q = np.arange(GRID_W)[:, None]
    kc = np.arange(GRID_W)[None, :]
    cs = np.clip(q - WIN_C // 2, 0, GRID_W - WIN_C)
    valid = (kc >= cs) & (kc < cs + WIN_C)
    col_off = np.clip(kc - q + (WIN_C - 1), 0, 2 * WIN_C - 2)
    ncol = 2 * WIN_C - 1
    nro = 2 * WIN_R - 2
    pick = np.zeros((2, ncol, GRID_W, 2 * GRID_W), np.float32)
    for j in range(2):
        pick[j, :, :, j * GRID_W:(j + 1) * GRID_W] = (np.arange(ncol)[:, None, None] == col_off[None]) & valid[None]
    negmask = np.tile(np.where(valid, 0.0, NEG).astype(np.float32), (1, 2))
    hpg = LANES // DB_H
    r4 = rpb.astype(F32).reshape(HB // hpg, hpg, 2 * WIN_R - 1, ncol)
    rp = jnp.stack([r4[:, :, :nro], r4[:, :, 1:]], axis=0)
    tab = jnp.einsum('jghrc,jcqk->grhqk', rp, jnp.asarray(pick), precision=lax.Precision.HIGHEST) + negmask
    return tab.reshape(HB // hpg, nro, hpg * GRID_W, LANES)


def _outproj_kernel(x_ref, oa_ref, ob_ref, wo_ref, gffn_ref, wr_ref, tri_ref,
                    h_ref, xn_ref, meta_ref, meta_t_ref, cnt_ref, run_scr):
    tm = x_ref.shape[0]
    da = oa_ref.shape[1]

    @pl.when(pl.program_id(0) == 0)
    def _():
        run_scr[...] = jnp.zeros_like(run_scr)

    attn = jnp.dot(oa_ref[...], wo_ref[:da, :], preferred_element_type=F32)
    attn = attn + jnp.dot(ob_ref[...], wo_ref[da:, :], preferred_element_type=F32)
    h = x_ref[...] + attn
    h_ref[...] = h
    xn = _rms_rows(h) * gffn_ref[...]
    _store_token_tiles(xn_ref, xn)

    xh = xn.astype(BF16)
    xl = (xn - xh.astype(F32)).astype(BF16)
    hh_hl = jnp.dot(xh, wr_ref[...], preferred_element_type=F32)
    lg = hh_hl[:, :LANES] + hh_hl[:, LANES:] + jnp.dot(xl, wr_ref[:, :LANES], preferred_element_type=F32)
    lane = lax.broadcasted_iota(jnp.int32, (tm, LANES), 1).astype(F32)
    ninf = -jnp.inf
    is_g = lane < N_GROUPS
    gl = jnp.where(is_g, lg, ninf)
    gmax = jnp.max(gl, axis=1, keepdims=True)
    gidx = jnp.min(jnp.where(gl == gmax, lane, float(LANES)), axis=1, keepdims=True)
    gsum = jnp.sum(jnp.where(is_g, jnp.exp(gl - gmax), 0.0), axis=1, keepdims=True)
    g_gate = 1.0 / gsum
    lo = E_LANE0 + gidx * EXPERTS_PER_GROUP
    is_e = (lane >= lo) & (lane < lo + EXPERTS_PER_GROUP)
    el = jnp.where(is_e, lg, ninf)
    l1 = jnp.max(el, axis=1, keepdims=True)
    i1 = jnp.min(jnp.where(el == l1, lane, float(LANES)), axis=1, keepdims=True)
    el2 = jnp.where(lane == i1, ninf, el)
    l2 = jnp.max(el2, axis=1, keepdims=True)
    i2 = jnp.min(jnp.where(el2 == l2, lane, float(LANES)), axis=1, keepdims=True)
    r = jnp.exp(l2 - l1)
    w1 = g_gate / (1.0 + r)
    w2 = w1 * r

    sel1 = lane == i1
    sel2 = lane == i2
    oh = jnp.where(sel1 | sel2, 1.0, 0.0)
    before = jnp.dot(tri_ref[...], oh.astype(BF16), preferred_element_type=F32) + run_scr[...]
    rank1 = jnp.sum(jnp.where(sel1, before, 0.0), axis=1, keepdims=True)
    rank2 = jnp.sum(jnp.where(sel2, before, 0.0), axis=1, keepdims=True)
    run = run_scr[...] + jnp.sum(oh, axis=0, keepdims=True)
    run_scr[...] = run
    cnt_ref[...] = run

    meta = jnp.where(lane == 0, i1 - E_LANE0, 0.0)
    meta = jnp.where(lane == 1, i2 - E_LANE0, meta)
    meta = jnp.where(lane == 2, rank1, meta)
    meta = jnp.where(lane == 3, rank2, meta)
    meta = jnp.where(lane == 4, w1, meta)
    meta = jnp.where(lane == 5, w2, meta)
    meta_ref[...] = meta
    meta_t_ref[...] = meta.T[:SUBLANES, :]


def _outproj(x2, oa2, ob2, wo_bf, g_ffn, w_router, tri, tm):
    t, d = x2.shape
    da = oa2.shape[1]
    const = lambda i: (0, 0)
    row = lambda i: (i, 0)
    return pl.pallas_call(
        _outproj_kernel,
        grid=(t // tm,),
        in_specs=[
            pl.BlockSpec((tm, d), row),
            pl.BlockSpec((tm, da), row),
            pl.BlockSpec((tm, d - da), row),
            pl.BlockSpec((d, d), const),
            pl.BlockSpec((1, d), const),
            pl.BlockSpec((d, 2 * LANES), const),
            pl.BlockSpec((tm, tm), const),
        ],
        out_specs=[
            pl.BlockSpec((tm, d), row),
            pl.BlockSpec((tm * SUBLANES, LANES), row),
            pl.BlockSpec((tm, LANES), row),
            pl.BlockSpec((SUBLANES, tm), lambda i: (0, i)),
            pl.BlockSpec((1, LANES), const),
        ],
        out_shape=[
            jax.ShapeDtypeStruct((t, d), F32),
            jax.ShapeDtypeStruct((t * SUBLANES, LANES), F32),
            jax.ShapeDtypeStruct((t, LANES), F32),
            jax.ShapeDtypeStruct((SUBLANES, t), F32),
            jax.ShapeDtypeStruct((1, LANES), F32),
        ],
        scratch_shapes=[pltpu.VMEM((1, LANES), F32)],
        compiler_params=_cparams(("arbitrary",)),
        name="outproj_router",
    )(x2, oa2, ob2, wo_bf, g_ffn, w_router, tri)


DMA_ISSUE_UNROLL = 8
DMA_PRIORITIES = 2


def _dispatch_kernel(dest_ref, pend_ref, nu_ref, xn_ref, xs_ref, zbuf, sem, zsem):
    tm = xn_ref.shape[0] // SUBLANES
    t_total = pl.num_programs(0) * tm
    base = pl.program_id(0) * tm
    n_blocks = xs_ref.shape[0] // (MOE_BLOCK * SUBLANES)

    @pl.when(pl.program_id(0) == 0)
    def _():
        zbuf[...] = jnp.zeros_like(zbuf)

        def zero_block(blk):
            off = pl.multiple_of(blk * (MOE_BLOCK * SUBLANES), MOE_BLOCK * SUBLANES)
            return pltpu.make_async_copy(zbuf, xs_ref.at[pl.ds(off, MOE_BLOCK * SUBLANES), :], zsem)

        def for_each_zeroed_block(fn):
            def per_expert(e, c):
                end = pend_ref[e]
                start = jnp.where(e == 0, 0, pend_ref[jnp.maximum(e - 1, 0)])

                @pl.when(end > start)
                def _():
                    fn(zero_block(end // MOE_BLOCK - 1))
                return c
            lax.fori_loop(0, N_EXPERTS, per_expert, 0)

            def per_tail_block(blk, c):
                fn(zero_block(blk))
                return c
            lax.fori_loop(nu_ref[0], n_blocks, per_tail_block, 0)

        for_each_zeroed_block(lambda cp: cp.start())
        for_each_zeroed_block(lambda cp: cp.wait())

    def issue(r, c):
        src = xn_ref.at[pl.ds(pl.multiple_of(r * SUBLANES, SUBLANES), SUBLANES), :]
        for slot in range(TOP_K):
            d = dest_ref[slot * t_total + base + r]
            dst = xs_ref.at[pl.ds(pl.multiple_of(d * SUBLANES, SUBLANES), SUBLANES), :]
            pltpu.make_async_copy(src, dst, sem).start(priority=slot % DMA_PRIORITIES)
        return c
    lax.fori_loop(0, tm, issue, 0, unroll=DMA_ISSUE_UNROLL)

    for slot in range(TOP_K):
        pltpu.make_async_copy(xn_ref, xs_ref.at[pl.ds(0, tm * SUBLANES), :], sem).wait()


def _dispatch(xn_tiles, dest_flat, pad_end, n_used, p_len, tm):
    t = xn_tiles.shape[0] // SUBLANES
    grid_spec = pltpu.PrefetchScalarGridSpec(
        num_scalar_prefetch=3,
        grid=(t // tm,),
        in_specs=[pl.BlockSpec((tm * SUBLANES, LANES), lambda i, dest, pend, nu: (i, 0))],
        out_specs=pl.BlockSpec(memory_space=pl.ANY),
        scratch_shapes=[pltpu.VMEM((MOE_BLOCK * SUBLANES, LANES), F32),
                        pltpu.SemaphoreType.DMA(()), pltpu.SemaphoreType.DMA(())],
    )
    return pl.pallas_call(
        _dispatch_kernel,
        grid_spec=grid_spec,
        out_shape=jax.ShapeDtypeStruct((p_len * SUBLANES, LANES), F32),
        compiler_params=_cparams(("arbitrary",)),
        name="dispatch",
    )(dest_flat, pad_end, n_used, xn_tiles)


def _experts_kernel(be_ref, nu_ref, nxt_ref, ord_ref, xs_ref, w1_hbm, w3_hbm, w2_hbm, ys_ref,
                    w1_scr, w3_scr, w2_scr, w1_buf, w3_buf, w2_buf, wsems):
    i = pl.program_id(0)

    def weight_copies(e, slot):
        return (pltpu.make_async_copy(w1_hbm.at[e], w1_buf.at[slot], wsems.at[slot, 0]),
                pltpu.make_async_copy(w3_hbm.at[e], w3_buf.at[slot], wsems.at[slot, 1]),
                pltpu.make_async_copy(w2_hbm.at[e], w2_buf.at[slot], wsems.at[slot, 2]))

    @pl.when(i < nu_ref[0])
    def _():
        e = be_ref[i]
        prev = be_ref[jnp.maximum(i - 1, 0)]
        slot = ord_ref[e] % 2

        @pl.when(i == 0)
        def _():
            for cp in weight_copies(e, slot):
                cp.start()

        @pl.when((i == 0) | (e != prev))
        def _():
            for cp in weight_copies(e, slot):
                cp.wait()
            nxt = nxt_ref[e]

            @pl.when(nxt >= 0)
            def _():
                for cp in weight_copies(nxt, 1 - slot):
                    cp.start()
            w1_scr[...] = w1_buf[slot].astype(BF16)
            w3_scr[...] = w3_buf[slot].astype(BF16)
            w2_scr[...] = w2_buf[slot].astype(BF16)

        x = _load_token_tiles(xs_ref, MOE_BLOCK).astype(BF16)
        a = jnp.dot(x, w1_scr[...], preferred_element_type=F32)
        b = jnp.dot(x, w3_scr[...], preferred_element_type=F32)
        hdn = a * (1.0 / (1.0 + jnp.exp(-a))) * b
        _store_token_tiles(ys_ref, jnp.dot(hdn.astype(BF16), w2_scr[...], preferred_element_type=F32))

    @pl.when(i >= nu_ref[0])
    def _():
        ys_ref[...] = jnp.zeros_like(ys_ref)


def _experts(xs, block_e, n_used, next_expert, expert_ord, w1, w3, w2):
    p_len = xs.shape[0] // SUBLANES
    n_blocks = p_len // MOE_BLOCK
    d, de = w1.shape[1], w1.shape[2]
    blk = lambda i, be, nu, nxt, od: (jnp.maximum(jnp.minimum(i, nu[0] - 1), 0), 0)
    hbm = pl.BlockSpec(memory_space=pl.ANY)
    grid_spec = pltpu.PrefetchScalarGridSpec(
        num_scalar_prefetch=4,
        grid=(n_blocks,),
        in_specs=[pl.BlockSpec((MOE_BLOCK * SUBLANES, LANES), blk), hbm, hbm, hbm],
        out_specs=pl.BlockSpec((MOE_BLOCK * SUBLANES, LANES), lambda i, be, nu, nxt, od: (i, 0)),
        scratch_shapes=[pltpu.VMEM((d, de), BF16), pltpu.VMEM((d, de), BF16), pltpu.VMEM((de, d), BF16),
                        pltpu.VMEM((2, d, de), F32), pltpu.VMEM((2, d, de), F32), pltpu.VMEM((2, de, d), F32),
                        pltpu.SemaphoreType.DMA((2, 3))],
    )
    return pl.pallas_call(
        _experts_kernel,
        grid_spec=grid_spec,
        out_shape=jax.ShapeDtypeStruct((p_len * SUBLANES, LANES), F32),
        compiler_params=_cparams(("arbitrary",)),
        name="experts",
    )(block_e, n_used, next_expert, expert_ord, xs, w1, w3, w2)


def _final_kernel(dest_ref, h_ref, meta_ref, p_ref, wple_ref, gple_ref, gplg_ref, wplg_ref, ys_ref,
                  out_ref, gbuf_a, gbuf_b, sems):
    th = h_ref.shape[0] // 2
    step = pl.program_id(0)
    n_tiles = 2 * pl.num_programs(0)
    t_total = n_tiles * th

    def gather_rows(tile, gbuf, sem, unrolled):
        def issue(r, c):
            for slot in range(TOP_K):
                d = dest_ref[slot * t_total + tile * th + r]
                src = ys_ref.at[pl.ds(pl.multiple_of(d * SUBLANES, SUBLANES), SUBLANES), :]
                dst = gbuf.at[slot, pl.ds(pl.multiple_of(r * SUBLANES, SUBLANES), SUBLANES), :]
                pltpu.make_async_copy(src, dst, sem).start(priority=slot % DMA_PRIORITIES)
            return c
        if unrolled:
            for r in range(th):
                issue(r, 0)
        else:
            lax.fori_loop(0, th, issue, 0, unroll=DMA_ISSUE_UNROLL)

    def wait_rows(gbuf, sem):
        for slot in range(TOP_K):
            pltpu.make_async_copy(ys_ref.at[pl.ds(0, th * SUBLANES), :], gbuf.at[slot], sem).wait()

    def combine(half, gbuf):
        rows = slice(half * th, (half + 1) * th)
        emb = jnp.dot(p_ref[rows, :].astype(BF16), wple_ref[...], preferred_element_type=F32)
        emb = _rms_rows(emb) * gple_ref[...]
        meta = meta_ref[rows, :]
        y1 = _load_token_tiles(gbuf.at[0], th)
        y2 = _load_token_tiles(gbuf.at[1], th)
        h = h_ref[rows, :] + meta[:, 4:5] * y1 + meta[:, 5:6] * y2
        hn = (_rms_rows(h) * gplg_ref[...]).astype(BF16)
        z = jnp.dot(hn, wplg_ref[...], preferred_element_type=F32)
        out_ref[rows, :] = h + emb * (1.0 / (1.0 + jnp.exp(-z)))

    @pl.when(step == 0)
    def _():
        gather_rows(0, gbuf_a, sems.at[0], unrolled=False)

    wait_rows(gbuf_a, sems.at[0])
    gather_rows(2 * step + 1, gbuf_b, sems.at[1], unrolled=True)
    combine(0, gbuf_a)
    wait_rows(gbuf_b, sems.at[1])
    gather_rows(jnp.minimum(2 * step + 2, n_tiles - 1), gbuf_a, sems.at[0], unrolled=True)
    combine(1, gbuf_b)

    @pl.when(step == pl.num_programs(0) - 1)
    def _():
        wait_rows(gbuf_a, sems.at[0])


def _final(dest_flat, h2, meta, p2, wple_bf, g_ple, g_plg, wplg_bf, ys, tm):
    t, d = h2.shape
    dp = p2.shape[1]
    const = lambda i, dest: (0, 0)
    row = lambda i, dest: (i, 0)
    grid_spec = pltpu.PrefetchScalarGridSpec(
        num_scalar_prefetch=1,
        grid=(t // tm,),
        in_specs=[
            pl.BlockSpec((tm, d), row),
            pl.BlockSpec((tm, LANES), row),
            pl.BlockSpec((tm, dp), row),
            pl.BlockSpec((dp, d), const),
            pl.BlockSpec((1, d), const),
            pl.BlockSpec((1, d), const),
            pl.BlockSpec((d, d), const),
            pl.BlockSpec(memory_space=pl.ANY),
        ],
        out_specs=pl.BlockSpec((tm, d), row),
        scratch_shapes=[pltpu.VMEM((TOP_K, tm // 2 * SUBLANES, LANES), F32),
                        pltpu.VMEM((TOP_K, tm // 2 * SUBLANES, LANES), F32), pltpu.SemaphoreType.DMA((2,))],
    )
    return pl.pallas_call(
        _final_kernel,
        grid_spec=grid_spec,
        out_shape=jax.ShapeDtypeStruct((t, d), F32),
        compiler_params=_cparams(("arbitrary",)),
        name="combine_ple",
    )(dest_flat, h2, meta, p2, wple_bf, g_ple, g_plg, wplg_bf, ys)


def _block_diag_mean(seg):
    m = np.kron(np.eye(512 // seg), np.full((seg, seg), 1.0 / seg))
    return jnp.asarray(m, BF16)


def _rope_tables(seq):
    inv = ROPE_THETA ** (-jnp.arange(0, DA_QK, 2, dtype=F32) / DA_QK)
    ang = jnp.arange(seq, dtype=F32)[:, None] * inv[None, :]
    cos, sin = jnp.cos(ang), jnp.sin(ang)
    cos_t = jnp.tile(cos, (1, LANES // cos.shape[1]))
    sin_t = jnp.tile(jnp.concatenate([-sin, sin], axis=1), (1, LANES // (2 * sin.shape[1])))
    return cos_t, sin_t


def _tile_for(n, pref):
    while n % pref:
        pref //= 2
    return pref


def kernel(x, p, g_mix, w_in, g_qa, g_ka, lam_q1, lam_k1, lam_q2, lam_k2, g_sub, g_qb, g_kb, rpb, w_out,
           g_ffn, w_rg, w_re, w1, w3, w2, g_plg, w_plg, w_ple, g_ple):
    b, s, d = x.shape
    assert d == SUBLANES * LANES, "token-tile layout holds one token per (8, 128) f32 tile"
    t = b * s
    depth = w_in.shape[0]
    tm = _tile_for(s, 512)
    tmd = _tile_for(t, 512)
    tmf = _tile_for(t, 512)
    n_slots = t * TOP_K
    n_blocks = n_slots // MOE_BLOCK + N_EXPERTS
    p_len = n_blocks * MOE_BLOCK

    bd64 = _block_diag_mean(DA_QK)
    bd32 = _block_diag_mean(DB_H)
    cos_t, sin_t = _rope_tables(s)
    tri = jnp.asarray(np.tril(np.ones((tm, tm)), -1), BF16)

    h = x.reshape(t, d)
    for i in range(depth):
        lam_init = 0.8 - 0.6 * math.exp(-0.3 * i)
        gsec = jnp.stack([
            jnp.tile(g_qa[i], 512 // DA_QK) * (DA_QK ** -0.5 * math.log2(math.e)),
            jnp.tile(g_ka[i], 512 // DA_QK),
            jnp.tile(g_qb[i], 512 // DB_H) * DB_H ** -0.5,
            jnp.tile(g_kb[i], 512 // DB_H),
        ]).astype(F32)
        proj, qvt = _inproj(h, g_mix[i][None, :], w_in[i].astype(BF16), bd64, bd32, gsec, cos_t, sin_t, s, tm)
        proj3 = proj.reshape(b, s, proj.shape[1])

        lam4 = jnp.stack([lam_q1[i], lam_k1[i], lam_q2[i], lam_k2[i]]).astype(F32)
        oa = _diffattn(proj3, qvt, lam4, g_sub[i][None, :].astype(F32), lam_init)
        ob = _natten(proj3, _natten_bias_table(rpb[i]))

        w_router = jnp.concatenate(
            [w_rg[i], w_re[i], jnp.zeros((d, LANES - N_GROUPS - N_EXPERTS), F32)], axis=1).astype(F32)
        wr_hi = w_router.astype(BF16)
        wr_lo = (w_router - wr_hi.astype(F32)).astype(BF16)
        h1, xn, meta, meta_t, cnt = _outproj(h, oa.reshape(t, -1), ob.reshape(t, -1), w_out[i].astype(BF16),
                                             g_ffn[i][None, :], jnp.concatenate([wr_hi, wr_lo], axis=1), tri, tm)

        counts = cnt[0, E_LANE0:E_LANE0 + N_EXPERTS].astype(jnp.int32)
        padded = (counts + MOE_BLOCK - 1) // MOE_BLOCK * MOE_BLOCK
        pad_end = jnp.cumsum(padded)
        pad_start = pad_end - padded
        mt = meta_t[0:2 * TOP_K].astype(jnp.int32)
        e_ids = jnp.arange(N_EXPERTS, dtype=jnp.int32)[None, :, None]
        start_of = jnp.sum(jnp.where(mt[:TOP_K, None, :] == e_ids, pad_start[None, :, None], 0), axis=1)
        dest = start_of + mt[TOP_K:]
        blk_start = jnp.arange(n_blocks, dtype=jnp.int32) * MOE_BLOCK
        block_e = jnp.minimum(jnp.sum(pad_end[None, :] <= blk_start[:, None], axis=1), N_EXPERTS - 1)
        block_e = block_e.astype(jnp.int32)
        n_used = (pad_end[-1:] // MOE_BLOCK).astype(jnp.int32)

        dest_flat = dest.reshape(TOP_K * t)

        xs = _dispatch(xn, dest_flat, pad_end.astype(jnp.int32), n_used, p_len, tmd)
        in_use = counts > 0
        expert_ord = (jnp.cumsum(in_use) - 1).astype(jnp.int32)
        ids = jnp.arange(N_EXPERTS, dtype=jnp.int32)
        later = jnp.where((ids[None, :] > ids[:, None]) & in_use[None, :], ids[None, :], N_EXPERTS)
        next_expert = jnp.min(later, axis=1)
        next_expert = jnp.where(next_expert == N_EXPERTS, -1, next_expert).astype(jnp.int32)
        ys = _experts(xs, block_e, n_used, next_expert, expert_ord, w1[i], w3[i], w2[i])
        h = _final(dest_flat, h1, meta, p[i].reshape(t, -1),
                   w_ple[i].astype(BF16), g_ple[i][None, :], g_plg[i][None, :], w_plg[i].astype(BF16), ys, tmf)
    return h.reshape(b, s, d)
```

```python
import functools
import math

import numpy as np
import jax
import jax.numpy as jnp
from jax import lax
from jax.experimental import pallas as pl
from jax.experimental.pallas import tpu as pltpu

F32 = jnp.float32
BF16 = jnp.bfloat16
U32 = jnp.uint32

LANES = 128
HA = 4
DA_QK = 64
DA_V = 128
HB = 16
DB_H = 32
GRID_W = 64
WIN_R = 8
WIN_C = 16
ROPE_THETA = 10000.0
EPS = 1e-6
NEG = -1e30
N_GROUPS = 4
EXPERTS_PER_GROUP = 8
N_EXPERTS = N_GROUPS * EXPERTS_PER_GROUP
TOP_K = 2
MOE_BLOCK = 256
E_LANE0 = N_GROUPS

VMEM_LIMIT = 56 * 1024 * 1024


def _cparams(sem):
    return pltpu.CompilerParams(dimension_semantics=sem, vmem_limit_bytes=VMEM_LIMIT)


def _rms_rows(x):
    return x * lax.rsqrt(jnp.mean(x * x, axis=-1, keepdims=True) + EPS)


SUBLANES = 8

TOKEN_ROWS = 4
HIGH_HALF = 0xFFFF0000


def _pack_bf16_pairs(x):
    m = x.shape[1] // 2
    bits = pltpu.bitcast(x.astype(BF16).astype(F32), U32)
    return (bits[:, :m] >> 16) | (bits[:, m:] & jnp.uint32(HIGH_HALF))


def _unpack_bf16_pairs(u):
    lo = pltpu.bitcast(u << 16, F32)
    hi = pltpu.bitcast(u & jnp.uint32(HIGH_HALF), F32)
    return jnp.concatenate([lo, hi], axis=1)


def _store_token_tiles(ref, val, first=0):
    n = val.shape[0]
    for c in range(TOKEN_ROWS):
        ref[pl.ds(first * TOKEN_ROWS + c, n, stride=TOKEN_ROWS), :] = val[:, c * LANES:(c + 1) * LANES]


def _load_token_tiles(ref, n, first=0):
    return jnp.concatenate(
        [ref[pl.ds(first * TOKEN_ROWS + c, n, stride=TOKEN_ROWS), :] for c in range(TOKEN_ROWS)], axis=1)


def _inproj_kernel(x_ref, gmix_ref, w_ref, bd64_ref, bd32_ref, gsec_ref, cos_ref, sin_ref, out_ref, qvt_ref):
    tm = x_ref.shape[0]
    a = (_rms_rows(x_ref[...]) * gmix_ref[...]).astype(BF16)
    lane = lax.broadcasted_iota(jnp.int32, (tm, LANES), 1)
    first_half = (lane % 64) < 32
    cos = cos_ref[...]
    sin = sin_ref[...]
    out_col = {1: 0, 3: 1, 4: 2, 5: 3}
    qvt_row = {0: 0, 2: 1}
    for sec in range(6):
        y = jnp.dot(a, w_ref[:, sec * 512:(sec + 1) * 512], preferred_element_type=F32)
        if sec in (0, 1, 3, 4):
            bd = bd64_ref if sec < 2 else bd32_ref
            gi = {0: 0, 1: 1, 3: 2, 4: 3}[sec]
            ms = jnp.dot((y * y).astype(BF16), bd[...], preferred_element_type=F32)
            y = y * lax.rsqrt(ms + EPS) * gsec_ref[gi:gi + 1, :]
        for c in range(4):
            yc = y[:, c * LANES:(c + 1) * LANES]
            if sec < 2:
                rot = jnp.where(first_half, pltpu.roll(yc, 96, 1), pltpu.roll(yc, 32, 1))
                yc = yc * cos + rot * sin
            if sec in qvt_row:
                r0 = qvt_row[sec] * 512 + c * LANES
                qvt_ref[0, r0:r0 + LANES, :] = yc.T.astype(BF16)
            else:
                c0 = out_col[sec] * 512 + c * LANES
                out_ref[:, c0:c0 + LANES] = yc.astype(BF16)


def _inproj(x2, g_mix, w_bf, bd64, bd32, gsec, cos_t, sin_t, seq, tm):
    t, d = x2.shape
    n = w_bf.shape[1]
    nsb = seq // tm
    n_tok_major = 4 * 512
    n_feat_major = 2 * 512
    const = lambda i: (0, 0)
    return pl.pallas_call(
        _inproj_kernel,
        grid=(t // tm,),
        in_specs=[
            pl.BlockSpec((tm, d), lambda i: (i, 0)),
            pl.BlockSpec((1, d), const),
            pl.BlockSpec((d, n), const),
            pl.BlockSpec((512, 512), const),
            pl.BlockSpec((512, 512), const),
            pl.BlockSpec((4, 512), const),
            pl.BlockSpec((tm, LANES), lambda i: (i % nsb, 0)),
            pl.BlockSpec((tm, LANES), lambda i: (i % nsb, 0)),
        ],
        out_specs=[pl.BlockSpec((tm, n_tok_major), lambda i: (i, 0)),
                   pl.BlockSpec((1, n_feat_major, tm), lambda i: (i // nsb, 0, i % nsb))],
        out_shape=[jax.ShapeDtypeStruct((t, n_tok_major), BF16),
                   jax.ShapeDtypeStruct((t // seq, n_feat_major, seq), BF16)],
        compiler_params=_cparams(("parallel",)),
        name="inproj",
    )(x2, g_mix, w_bf, bd64, bd32, gsec, cos_t, sin_t)


QBLOCK_PAIRS_PER_BODY = 4

def _diffattn_kernel(lam_ref, gsub_ref, qt_ref, k_ref, vt_ref, o_ref, sa_scr, sb_scr, *, seq, tq, tk, lam_init):
    nq = seq // tq
    nk = seq // tk
    s1 = jnp.sum(lam_ref[0:1, :] * lam_ref[1:2, :], axis=1, keepdims=True)
    s2 = jnp.sum(lam_ref[2:3, :] * lam_ref[3:4, :], axis=1, keepdims=True)
    lam = jnp.exp(s1) - jnp.exp(s2) + lam_init

    row = lax.broadcasted_iota(jnp.int32, (LANES, tq), 0)

    def scores(qi, s_scr):
        qoff = pl.multiple_of(qi * tq, tq)
        qt = qt_ref[0, :, pl.ds(qoff, tq)]
        zero = jnp.zeros_like(qt)
        qm = jnp.concatenate([jnp.where(row < DA_QK, qt, zero), jnp.where(row >= DA_QK, qt, zero)], axis=1)
        mx = None
        for j in range(nk):
            st = jnp.dot(k_ref[0, j * tk:(j + 1) * tk, :], qm, preferred_element_type=F32)
            s_scr[j * tk:(j + 1) * tk, :] = st
            cm = jnp.max(st, axis=0, keepdims=True)
            mx = cm if mx is None else jnp.maximum(mx, cm)
        return mx

    def finish(qi, s_scr, m):
        l = jnp.zeros((1, 2 * tq), F32)
        acc = jnp.zeros((DA_V, 2 * tq), F32)
        for j in range(nk):
            p = jnp.exp2(s_scr[j * tk:(j + 1) * tk, :] - m)
            l = l + jnp.sum(p, axis=0, keepdims=True)
            acc = acc + jnp.dot(vt_ref[0, :, j * tk:(j + 1) * tk], p.astype(BF16), preferred_element_type=F32)
        o = acc * (1.0 / l)
        ot = o[:, :tq] - lam * o[:, tq:]
        ot = ot * lax.rsqrt(jnp.mean(ot * ot, axis=0, keepdims=True) + EPS) * (1.0 - lam_init)
        qoff = pl.multiple_of(qi * tq, tq)
        o_ref[0, pl.ds(qoff, tq), :] = (ot.T * gsub_ref[...]).astype(BF16)

    def body(i, m_a):
        for pair in range(QBLOCK_PAIRS_PER_BODY):
            q0 = 2 * (QBLOCK_PAIRS_PER_BODY * i + pair)
            m_b = scores(q0 + 1, sb_scr)
            finish(q0, sa_scr, m_a)
            m_a = scores(jnp.minimum(q0 + 2, nq - 1), sa_scr)
            finish(q0 + 1, sb_scr, m_b)
        return m_a

    lax.fori_loop(0, nq // (2 * QBLOCK_PAIRS_PER_BODY), body, scores(0, sa_scr))


def _diffattn(proj3, qvt, lam4, g_sub, lam_init, tq=128, tk=512):
    b, seq, _ = proj3.shape
    kern = functools.partial(_diffattn_kernel, seq=seq, tq=tq, tk=tk, lam_init=lam_init)
    return pl.pallas_call(
        kern,
        grid=(b, HA),
        in_specs=[
            pl.BlockSpec((4, DA_QK), lambda i, h: (0, 0)),
            pl.BlockSpec((1, DA_V), lambda i, h: (0, 0)),
            pl.BlockSpec((1, LANES, seq), lambda i, h: (i, h, 0)),
            pl.BlockSpec((1, seq, LANES), lambda i, h: (i, 0, h)),
            pl.BlockSpec((1, LANES, seq), lambda i, h: (i, HA + h, 0)),
        ],
        out_specs=pl.BlockSpec((1, seq, LANES), lambda i, h: (i, 0, h)),
        out_shape=jax.ShapeDtypeStruct((b, seq, HA * DA_V), BF16),
        scratch_shapes=[pltpu.VMEM((seq, 2 * tq), F32), pltpu.VMEM((seq, 2 * tq), F32)],
        compiler_params=_cparams(("parallel", "parallel")),
        name="diffattn",
    )(lam4, g_sub, qvt, proj3, qvt)


def _natten_kernel(bias_ref, q_ref, k_ref, v_ref, o_ref, *, rows):
    kr = min(WIN_R, rows)
    nkeys = kr * GRID_W
    hpg = LANES // DB_H
    lane_head = lax.broadcasted_iota(jnp.int32, (GRID_W, LANES), 1) // DB_H

    def row_step(r, carry):
        rs = jnp.clip(r - kr // 2, 0, rows - kr)
        base = rs - r + (WIN_R - 1)
        qoff = pl.multiple_of(r * GRID_W, GRID_W)
        koff = pl.multiple_of(rs * GRID_W, GRID_W)
        qr = q_ref[0, pl.ds(qoff, GRID_W), :]
        zero = jnp.zeros_like(qr)
        qs = jnp.concatenate([jnp.where(lane_head == h, qr, zero) for h in range(hpg)], axis=0)
        kw = k_ref[0, pl.ds(koff, nkeys), :]
        vw = v_ref[0, pl.ds(koff, nkeys), :]
        s = lax.dot_general(qs, kw, (((1,), (1,)), ((), ())), preferred_element_type=F32)
        bias = jnp.concatenate([bias_ref[0, base + 2 * i] for i in range(kr // 2)], axis=1)
        s = s + bias
        m = jnp.max(s, axis=1, keepdims=True)
        p = jnp.exp(s - m).astype(BF16)
        pv = jnp.dot(p, jnp.concatenate([vw, jnp.ones_like(vw)], axis=1), preferred_element_type=F32)
        o = pv[:, :LANES] * (1.0 / pv[:, LANES:])
        out = jnp.zeros((GRID_W, LANES), F32)
        for h in range(hpg):
            out = out + jnp.where(lane_head == h, o[h * GRID_W:(h + 1) * GRID_W, :], 0.0)
        o_ref[0, pl.ds(qoff, GRID_W), :] = out.astype(BF16)
        return carry

    lax.fori_loop(0, rows, row_step, 0, unroll=16)


def _natten(proj3, bias_tab):
    b, seq, _ = proj3.shape
    rows = seq // GRID_W
    ng = HB * DB_H // LANES
    nro = bias_tab.shape[1]
    kern = functools.partial(_natten_kernel, rows=rows)
    cb = HA
    return pl.pallas_call(
        kern,
        grid=(b, ng),
        in_specs=[
            pl.BlockSpec((1, nro, bias_tab.shape[2], LANES), lambda i, g: (g, 0, 0, 0)),
            pl.BlockSpec((1, seq, LANES), lambda i, g: (i, 0, cb + g)),
            pl.BlockSpec((1, seq, LANES), lambda i, g: (i, 0, cb + ng + g)),
            pl.BlockSpec((1, seq, LANES), lambda i, g: (i, 0, cb + 2 * ng + g)),
        ],
        out_specs=pl.BlockSpec((1, seq, LANES), lambda i, g: (i, 0, g)),
        out_shape=jax.ShapeDtypeStruct((b, seq, HB * DB_H), BF16),
        compiler_params=_cparams(("parallel", "parallel")),
        name="natten",
    )(bias_tab, proj3, proj3, proj3)


def _natten_bias_table(rpb):
    q = np.arange(GRID_W)[:, None]
    kc = np.arange(GRID_W)[None, :]
    cs = np.clip(q - WIN_C // 2, 0, GRID_W - WIN_C)
    valid = (kc >= cs) & (kc < cs + WIN_C)
    col_off = np.clip(kc - q + (WIN_C - 1), 0, 2 * WIN_C - 2)
    ncol = 2 * WIN_C - 1
    nro = 2 * WIN_R - 2
    pick = np.zeros((2, ncol, GRID_W, 2 * GRID_W), np.float32)
    for j in range(2):
        pick[j, :, :, j * GRID_W:(j + 1) * GRID_W] = (np.arange(ncol)[:, None, None] == col_off[None]) & valid[None]
    negmask = np.tile(np.where(valid, 0.0, NEG).astype(np.float32), (1, 2))
    hpg = LANES // DB_H
    r4 = rpb.astype(F32).reshape(HB // hpg, hpg, 2 * WIN_R - 1, ncol)
    rp = jnp.stack([r4[:, :, :nro], r4[:, :, 1:]], axis=0)
    tab = jnp.einsum('jghrc,jcqk->grhqk', rp, jnp.asarray(pick), precision=lax.Precision.HIGHEST) + negmask
    return tab.reshape(HB // hpg, nro, hpg * GRID_W, LANES)


def _outproj_kernel(x_ref, oa_ref, ob_ref, wo_ref, gffn_ref, wr_ref, tri_ref,
                    h_ref, xn_ref, meta_ref, meta_t_ref, cnt_ref, run_scr):
    tm = x_ref.shape[0]
    da = oa_ref.shape[1]

    @pl.when(pl.program_id(0) == 0)
    def _():
        run_scr[...] = jnp.zeros_like(run_scr)

    attn = jnp.dot(oa_ref[...], wo_ref[:da, :], preferred_element_type=F32)
    attn = attn + jnp.dot(ob_ref[...], wo_ref[da:, :], preferred_element_type=F32)
    h = x_ref[...] + attn
    h_ref[...] = h
    xn = _rms_rows(h) * gffn_ref[...]
    _store_token_tiles(xn_ref, _pack_bf16_pairs(xn))

    xh = xn.astype(BF16)
    xl = (xn - xh.astype(F32)).astype(BF16)
    hh_hl = jnp.dot(xh, wr_ref[...], preferred_element_type=F32)
    lg = hh_hl[:, :LANES] + hh_hl[:, LANES:] + jnp.dot(xl, wr_ref[:, :LANES], preferred_element_type=F32)
    lane = lax.broadcasted_iota(jnp.int32, (tm, LANES), 1).astype(F32)
    ninf = -jnp.inf
    is_g = lane < N_GROUPS
    gl = jnp.where(is_g, lg, ninf)
    gmax = jnp.max(gl, axis=1, keepdims=True)
    gidx = jnp.min(jnp.where(gl == gmax, lane, float(LANES)), axis=1, keepdims=True)
    gsum = jnp.sum(jnp.where(is_g, jnp.exp(gl - gmax), 0.0), axis=1, keepdims=True)
    g_gate = 1.0 / gsum
    lo = E_LANE0 + gidx * EXPERTS_PER_GROUP
    is_e = (lane >= lo) & (lane < lo + EXPERTS_PER_GROUP)
    el = jnp.where(is_e, lg, ninf)
    l1 = jnp.max(el, axis=1, keepdims=True)
    i1 = jnp.min(jnp.where(el == l1, lane, float(LANES)), axis=1, keepdims=True)
    el2 = jnp.where(lane == i1, ninf, el)
    l2 = jnp.max(el2, axis=1, keepdims=True)
    i2 = jnp.min(jnp.where(el2 == l2, lane, float(LANES)), axis=1, keepdims=True)
    r = jnp.exp(l2 - l1)
    w1 = g_gate / (1.0 + r)
    w2 = w1 * r

    sel1 = lane == i1
    sel2 = lane == i2
    oh = jnp.where(sel1 | sel2, 1.0, 0.0)
    before = jnp.dot(tri_ref[...], oh.astype(BF16), preferred_element_type=F32) + run_scr[...]
    rank1 = jnp.sum(jnp.where(sel1, before, 0.0), axis=1, keepdims=True)
    rank2 = jnp.sum(jnp.where(sel2, before, 0.0), axis=1, keepdims=True)
    run = run_scr[...] + jnp.sum(oh, axis=0, keepdims=True)
    run_scr[...] = run
    cnt_ref[...] = run

    meta = jnp.where(lane == 0, i1 - E_LANE0, 0.0)
    meta = jnp.where(lane == 1, i2 - E_LANE0, meta)
    meta = jnp.where(lane == 2, rank1, meta)
    meta = jnp.where(lane == 3, rank2, meta)
    meta = jnp.where(lane == 4, w1, meta)
    meta = jnp.where(lane == 5, w2, meta)
    meta_ref[...] = meta
    meta_t_ref[...] = meta.T[:SUBLANES, :]


def _outproj(x2, oa2, ob2, wo_bf, g_ffn, w_router, tri, tm):
    t, d = x2.shape
    da = oa2.shape[1]
    const = lambda i: (0, 0)
    row = lambda i: (i, 0)
    return pl.pallas_call(
        _outproj_kernel,
        grid=(t // tm,),
        in_specs=[
            pl.BlockSpec((tm, d), row),
            pl.BlockSpec((tm, da), row),
            pl.BlockSpec((tm, d - da), row),
            pl.BlockSpec((d, d), const),
            pl.BlockSpec((1, d), const),
            pl.BlockSpec((d, 2 * LANES), const),
            pl.BlockSpec((tm, tm), const),
        ],
        out_specs=[
            pl.BlockSpec((tm, d), row),
            pl.BlockSpec((tm * TOKEN_ROWS, LANES), row),
            pl.BlockSpec((tm, LANES), row),
            pl.BlockSpec((SUBLANES, tm), lambda i: (0, i)),
            pl.BlockSpec((1, LANES), const),
        ],
        out_shape=[
            jax.ShapeDtypeStruct((t, d), F32),
            jax.ShapeDtypeStruct((t * TOKEN_ROWS, LANES), U32),
            jax.ShapeDtypeStruct((t, LANES), F32),
            jax.ShapeDtypeStruct((SUBLANES, t), F32),
            jax.ShapeDtypeStruct((1, LANES), F32),
        ],
        scratch_shapes=[pltpu.VMEM((1, LANES), F32)],
        compiler_params=_cparams(("arbitrary",)),
        name="outproj_router",
    )(x2, oa2, ob2, wo_bf, g_ffn, w_router, tri)


DMA_ISSUE_UNROLL = 8
DMA_PRIORITIES = 2


def _dispatch_kernel(dest_ref, pend_ref, nu_ref, xn_ref, xs_ref, zbuf, sem, zsem):
    tm = xn_ref.shape[0] // TOKEN_ROWS
    t_total = pl.num_programs(0) * tm
    base = pl.program_id(0) * tm
    n_blocks = xs_ref.shape[0] // (MOE_BLOCK * TOKEN_ROWS)

    @pl.when(pl.program_id(0) == 0)
    def _():
        zbuf[...] = jnp.zeros_like(zbuf)

        def zero_block(blk):
            off = pl.multiple_of(blk * (MOE_BLOCK * TOKEN_ROWS), MOE_BLOCK * TOKEN_ROWS)
            return pltpu.make_async_copy(zbuf, xs_ref.at[pl.ds(off, MOE_BLOCK * TOKEN_ROWS), :], zsem)

        def for_each_zeroed_block(fn):
            def per_expert(e, c):
                end = pend_ref[e]
                start = jnp.where(e == 0, 0, pend_ref[jnp.maximum(e - 1, 0)])

                @pl.when(end > start)
                def _():
                    fn(zero_block(end // MOE_BLOCK - 1))
                return c
            lax.fori_loop(0, N_EXPERTS, per_expert, 0)

            def per_tail_block(blk, c):
                fn(zero_block(blk))
                return c
            lax.fori_loop(nu_ref[0], n_blocks, per_tail_block, 0)

        for_each_zeroed_block(lambda cp: cp.start())
        for_each_zeroed_block(lambda cp: cp.wait())

    def issue(r, c):
        src = xn_ref.at[pl.ds(pl.multiple_of(r * TOKEN_ROWS, TOKEN_ROWS), TOKEN_ROWS), :]
        for slot in range(TOP_K):
            d = dest_ref[slot * t_total + base + r]
            dst = xs_ref.at[pl.ds(pl.multiple_of(d * TOKEN_ROWS, TOKEN_ROWS), TOKEN_ROWS), :]
            pltpu.make_async_copy(src, dst, sem).start(priority=slot % DMA_PRIORITIES)
        return c
    lax.fori_loop(0, tm, issue, 0, unroll=DMA_ISSUE_UNROLL)

    for slot in range(TOP_K):
        pltpu.make_async_copy(xn_ref, xs_ref.at[pl.ds(0, tm * TOKEN_ROWS), :], sem).wait()


def _dispatch(xn_tiles, dest_flat, pad_end, n_used, p_len, tm):
    t = xn_tiles.shape[0] // TOKEN_ROWS
    grid_spec = pltpu.PrefetchScalarGridSpec(
        num_scalar_prefetch=3,
        grid=(t // tm,),
        in_specs=[pl.BlockSpec((tm * TOKEN_ROWS, LANES), lambda i, dest, pend, nu: (i, 0))],
        out_specs=pl.BlockSpec(memory_space=pl.ANY),
        scratch_shapes=[pltpu.VMEM((MOE_BLOCK * TOKEN_ROWS, LANES), U32),
                        pltpu.SemaphoreType.DMA(()), pltpu.SemaphoreType.DMA(())],
    )
    return pl.pallas_call(
        _dispatch_kernel,
        grid_spec=grid_spec,
        out_shape=jax.ShapeDtypeStruct((p_len * TOKEN_ROWS, LANES), U32),
        compiler_params=_cparams(("arbitrary",)),
        name="dispatch",
    )(dest_flat, pad_end, n_used, xn_tiles)


def _experts_kernel(be_ref, nu_ref, nxt_ref, ord_ref, xs_ref, w1_hbm, w3_hbm, w2_hbm, ys_ref,
                    w1_scr, w3_scr, w2_scr, w1_buf, w3_buf, w2_buf, wsems):
    i = pl.program_id(0)

    def weight_copies(e, slot):
        return (pltpu.make_async_copy(w1_hbm.at[e], w1_buf.at[slot], wsems.at[slot, 0]),
                pltpu.make_async_copy(w3_hbm.at[e], w3_buf.at[slot], wsems.at[slot, 1]),
                pltpu.make_async_copy(w2_hbm.at[e], w2_buf.at[slot], wsems.at[slot, 2]))

    @pl.when(i < nu_ref[0])
    def _():
        e = be_ref[i]
        prev = be_ref[jnp.maximum(i - 1, 0)]
        slot = ord_ref[e] % 2

        @pl.when(i == 0)
        def _():
            for cp in weight_copies(e, slot):
                cp.start()

        @pl.when((i == 0) | (e != prev))
        def _():
            for cp in weight_copies(e, slot):
                cp.wait()
            nxt = nxt_ref[e]

            @pl.when(nxt >= 0)
            def _():
                for cp in weight_copies(nxt, 1 - slot):
                    cp.start()
            w1_scr[...] = w1_buf[slot].astype(BF16)
            w3_scr[...] = w3_buf[slot].astype(BF16)
            w2_scr[...] = w2_buf[slot].astype(BF16)

        x = _unpack_bf16_pairs(_load_token_tiles(xs_ref, MOE_BLOCK)).astype(BF16)
        a = jnp.dot(x, w1_scr[...], preferred_element_type=F32)
        b = jnp.dot(x, w3_scr[...], preferred_element_type=F32)
        hdn = a * (1.0 / (1.0 + jnp.exp(-a))) * b
        y = jnp.dot(hdn.astype(BF16), w2_scr[...], preferred_element_type=F32)
        _store_token_tiles(ys_ref, _pack_bf16_pairs(y))

    @pl.when(i >= nu_ref[0])
    def _():
        ys_ref[...] = jnp.zeros_like(ys_ref)


def _experts(xs, block_e, n_used, next_expert, expert_ord, w1, w3, w2):
    p_len = xs.shape[0] // TOKEN_ROWS
    n_blocks = p_len // MOE_BLOCK
    d, de = w1.shape[1], w1.shape[2]
    blk = lambda i, be, nu, nxt, od: (jnp.maximum(jnp.minimum(i, nu[0] - 1), 0), 0)
    hbm = pl.BlockSpec(memory_space=pl.ANY)
    grid_spec = pltpu.PrefetchScalarGridSpec(
        num_scalar_prefetch=4,
        grid=(n_blocks,),
        in_specs=[pl.BlockSpec((MOE_BLOCK * TOKEN_ROWS, LANES), blk), hbm, hbm, hbm],
        out_specs=pl.BlockSpec((MOE_BLOCK * TOKEN_ROWS, LANES), lambda i, be, nu, nxt, od: (i, 0)),
        scratch_shapes=[pltpu.VMEM((d, de), BF16), pltpu.VMEM((d, de), BF16), pltpu.VMEM((de, d), BF16),
                        pltpu.VMEM((2, d, de), F32), pltpu.VMEM((2, d, de), F32), pltpu.VMEM((2, de, d), F32),
                        pltpu.SemaphoreType.DMA((2, 3))],
    )
    return pl.pallas_call(
        _experts_kernel,
        grid_spec=grid_spec,
        out_shape=jax.ShapeDtypeStruct((p_len * TOKEN_ROWS, LANES), U32),
        compiler_params=_cparams(("arbitrary",)),
        name="experts",
    )(block_e, n_used, next_expert, expert_ord, xs, w1, w3, w2)


def _final_kernel(dest_ref, h_ref, meta_ref, p_ref, wple_ref, gple_ref, gplg_ref, wplg_ref, ys_ref,
                  out_ref, gbuf_a, gbuf_b, sems):
    th = h_ref.shape[0] // 2
    step = pl.program_id(0)
    n_tiles = 2 * pl.num_programs(0)
    t_total = n_tiles * th

    def gather_rows(tile, gbuf, sem, unrolled):
        def issue(r, c):
            for slot in range(TOP_K):
                d = dest_ref[slot * t_total + tile * th + r]
                src = ys_ref.at[pl.ds(pl.multiple_of(d * TOKEN_ROWS, TOKEN_ROWS), TOKEN_ROWS), :]
                dst = gbuf.at[slot, pl.ds(pl.multiple_of(r * TOKEN_ROWS, TOKEN_ROWS), TOKEN_ROWS), :]
                pltpu.make_async_copy(src, dst, sem).start(priority=slot % DMA_PRIORITIES)
            return c
        if unrolled:
            for r in range(th):
                issue(r, 0)
        else:
            lax.fori_loop(0, th, issue, 0, unroll=DMA_ISSUE_UNROLL)

    def wait_rows(gbuf, sem):
        for slot in range(TOP_K):
            pltpu.make_async_copy(ys_ref.at[pl.ds(0, th * TOKEN_ROWS), :], gbuf.at[slot], sem).wait()

    def combine(half, gbuf):
        rows = slice(half * th, (half + 1) * th)
        emb = jnp.dot(p_ref[rows, :].astype(BF16), wple_ref[...], preferred_element_type=F32)
        emb = _rms_rows(emb) * gple_ref[...]
        meta = meta_ref[rows, :]
        y1 = _unpack_bf16_pairs(_load_token_tiles(gbuf.at[0], th))
        y2 = _unpack_bf16_pairs(_load_token_tiles(gbuf.at[1], th))
        h = h_ref[rows, :] + meta[:, 4:5] * y1 + meta[:, 5:6] * y2
        hn = (_rms_rows(h) * gplg_ref[...]).astype(BF16)
        z = jnp.dot(hn, wplg_ref[...], preferred_element_type=F32)
        out_ref[rows, :] = h + emb * (1.0 / (1.0 + jnp.exp(-z)))

    @pl.when(step == 0)
    def _():
        gather_rows(0, gbuf_a, sems.at[0], unrolled=False)

    wait_rows(gbuf_a, sems.at[0])
    gather_rows(2 * step + 1, gbuf_b, sems.at[1], unrolled=True)
    combine(0, gbuf_a)
    wait_rows(gbuf_b, sems.at[1])
    gather_rows(jnp.minimum(2 * step + 2, n_tiles - 1), gbuf_a, sems.at[0], unrolled=True)
    combine(1, gbuf_b)

    @pl.when(step == pl.num_programs(0) - 1)
    def _():
        wait_rows(gbuf_a, sems.at[0])


def _final(dest_flat, h2, meta, p2, wple_bf, g_ple, g_plg, wplg_bf, ys, tm):
    t, d = h2.shape
    dp = p2.shape[1]
    const = lambda i, dest: (0, 0)
    row = lambda i, dest: (i, 0)
    grid_spec = pltpu.PrefetchScalarGridSpec(
        num_scalar_prefetch=1,
        grid=(t // tm,),
        in_specs=[
            pl.BlockSpec((tm, d), row),
            pl.BlockSpec((tm, LANES), row),
            pl.BlockSpec((tm, dp), row),
            pl.BlockSpec((dp, d), const),
            pl.BlockSpec((1, d), const),
            pl.BlockSpec((1, d), const),
            pl.BlockSpec((d, d), const),
            pl.BlockSpec(memory_space=pl.ANY),
        ],
        out_specs=pl.BlockSpec((tm, d), row),
        scratch_shapes=[pltpu.VMEM((TOP_K, tm // 2 * TOKEN_ROWS, LANES), U32),
                        pltpu.VMEM((TOP_K, tm // 2 * TOKEN_ROWS, LANES), U32), pltpu.SemaphoreType.DMA((2,))],
    )
    return pl.pallas_call(
        _final_kernel,
        grid_spec=grid_spec,
        out_shape=jax.ShapeDtypeStruct((t, d), F32),
        compiler_params=_cparams(("arbitrary",)),
        name="combine_ple",
    )(dest_flat, h2, meta, p2, wple_bf, g_ple, g_plg, wplg_bf, ys)


def _block_diag_mean(seg):
    m = np.kron(np.eye(512 // seg), np.full((seg, seg), 1.0 / seg))
    return jnp.asarray(m, BF16)


def _rope_tables(seq):
    inv = ROPE_THETA ** (-jnp.arange(0, DA_QK, 2, dtype=F32) / DA_QK)
    ang = jnp.arange(seq, dtype=F32)[:, None] * inv[None, :]
    cos, sin = jnp.cos(ang), jnp.sin(ang)
    cos_t = jnp.tile(cos, (1, LANES // cos.shape[1]))
    sin_t = jnp.tile(jnp.concatenate([-sin, sin], axis=1), (1, LANES // (2 * sin.shape[1])))
    return cos_t, sin_t


def _tile_for(n, pref):
    while n % pref:
        pref //= 2
    return pref


def kernel(x, p, g_mix, w_in, g_qa, g_ka, lam_q1, lam_k1, lam_q2, lam_k2, g_sub, g_qb, g_kb, rpb, w_out,
           g_ffn, w_rg, w_re, w1, w3, w2, g_plg, w_plg, w_ple, g_ple):
    b, s, d = x.shape
    assert d == 2 * TOKEN_ROWS * LANES, "token-tile layout packs one token into TOKEN_ROWS rows of 128 uint32"
    t = b * s
    depth = w_in.shape[0]
    tm = _tile_for(s, 512)
    tmd = _tile_for(t, 512)
    tmf = _tile_for(t, 512)
    n_slots = t * TOP_K
    n_blocks = n_slots // MOE_BLOCK + N_EXPERTS
    p_len = n_blocks * MOE_BLOCK

    bd64 = _block_diag_mean(DA_QK)
    bd32 = _block_diag_mean(DB_H)
    cos_t, sin_t = _rope_tables(s)
    tri = jnp.asarray(np.tril(np.ones((tm, tm)), -1), BF16)

    h = x.reshape(t, d)
    for i in range(depth):
        lam_init = 0.8 - 0.6 * math.exp(-0.3 * i)
        gsec = jnp.stack([
            jnp.tile(g_qa[i], 512 // DA_QK) * (DA_QK ** -0.5 * math.log2(math.e)),
            jnp.tile(g_ka[i], 512 // DA_QK),
            jnp.tile(g_qb[i], 512 // DB_H) * DB_H ** -0.5,
            jnp.tile(g_kb[i], 512 // DB_H),
        ]).astype(F32)
        proj, qvt = _inproj(h, g_mix[i][None, :], w_in[i].astype(BF16), bd64, bd32, gsec, cos_t, sin_t, s, tm)
        proj3 = proj.reshape(b, s, proj.shape[1])

        lam4 = jnp.stack([lam_q1[i], lam_k1[i], lam_q2[i], lam_k2[i]]).astype(F32)
        oa = _diffattn(proj3, qvt, lam4, g_sub[i][None, :].astype(F32), lam_init)
        ob = _natten(proj3, _natten_bias_table(rpb[i]))

        w_router = jnp.concatenate(
            [w_rg[i], w_re[i], jnp.zeros((d, LANES - N_GROUPS - N_EXPERTS), F32)], axis=1).astype(F32)
        wr_hi = w_router.astype(BF16)
        wr_lo = (w_router - wr_hi.astype(F32)).astype(BF16)
        h1, xn, meta, meta_t, cnt = _outproj(h, oa.reshape(t, -1), ob.reshape(t, -1), w_out[i].astype(BF16),
                                             g_ffn[i][None, :], jnp.concatenate([wr_hi, wr_lo], axis=1), tri, tm)

        counts = cnt[0, E_LANE0:E_LANE0 + N_EXPERTS].astype(jnp.int32)
        padded = (counts + MOE_BLOCK - 1) // MOE_BLOCK * MOE_BLOCK
        pad_end = jnp.cumsum(padded)
        pad_start = pad_end - padded
        mt = meta_t[0:2 * TOP_K].astype(jnp.int32)
        e_ids = jnp.arange(N_EXPERTS, dtype=jnp.int32)[None, :, None]
        start_of = jnp.sum(jnp.where(mt[:TOP_K, None, :] == e_ids, pad_start[None, :, None], 0), axis=1)
        dest = start_of + mt[TOP_K:]
        blk_start = jnp.arange(n_blocks, dtype=jnp.int32) * MOE_BLOCK
        block_e = jnp.minimum(jnp.sum(pad_end[None, :] <= blk_start[:, None], axis=1), N_EXPERTS - 1)
        block_e = block_e.astype(jnp.int32)
        n_used = (pad_end[-1:] // MOE_BLOCK).astype(jnp.int32)

        dest_flat = dest.reshape(TOP_K * t)

        xs = _dispatch(xn, dest_flat, pad_end.astype(jnp.int32), n_used, p_len, tmd)
        in_use = counts > 0
        expert_ord = (jnp.cumsum(in_use) - 1).astype(jnp.int32)
        ids = jnp.arange(N_EXPERTS, dtype=jnp.int32)
        later = jnp.where((ids[None, :] > ids[:, None]) & in_use[None, :], ids[None, :], N_EXPERTS)
        next_expert = jnp.min(later, axis=1)
        next_expert = jnp.where(next_expert == N_EXPERTS, -1, next_expert).astype(jnp.int32)
        ys = _experts(xs, block_e, n_used, next_expert, expert_ord, w1[i], w3[i], w2[i])
        h = _final(dest_flat, h1, meta, p[i].reshape(t, -1),
                   w_ple[i].astype(BF16), g_ple[i][None, :], g_plg[i][None, :], w_plg[i].astype(BF16), ys, tmf)
    return h.reshape(b, s, d)
```

```python
import functools
import math

import numpy as np
import jax
import jax.numpy as jnp
from jax import lax
from jax.experimental import pallas as pl
from jax.experimental.pallas import tpu as pltpu

F32 = jnp.float32
BF16 = jnp.bfloat16
U32 = jnp.uint32

LANES = 128
HA = 4
DA_QK = 64
DA_V = 128
HB = 16
DB_H = 32
GRID_W = 64
WIN_R = 8
WIN_C = 16
ROPE_THETA = 10000.0
EPS = 1e-6
NEG = -1e30
N_GROUPS = 4
EXPERTS_PER_GROUP = 8
N_EXPERTS = N_GROUPS * EXPERTS_PER_GROUP
TOP_K = 2
MOE_BLOCK = 256
E_LANE0 = N_GROUPS

VMEM_LIMIT = 56 * 1024 * 1024


def _cparams(sem):
    return pltpu.CompilerParams(dimension_semantics=sem, vmem_limit_bytes=VMEM_LIMIT)


def _rms_rows(x):
    return x * lax.rsqrt(jnp.mean(x * x, axis=-1, keepdims=True) + EPS)


SUBLANES = 8

TOKEN_ROWS = 4
HIGH_HALF = 0xFFFF0000


def _pack_bf16_pairs(x):
    m = x.shape[1] // 2
    bits = pltpu.bitcast(x.astype(BF16).astype(F32), U32)
    return (bits[:, :m] >> 16) | (bits[:, m:] & jnp.uint32(HIGH_HALF))


def _unpack_bf16_pairs(u):
    lo = pltpu.bitcast(u << 16, F32)
    hi = pltpu.bitcast(u & jnp.uint32(HIGH_HALF), F32)
    return jnp.concatenate([lo, hi], axis=1)


def _store_token_tiles(ref, val, first=0):
    n = val.shape[0]
    for c in range(TOKEN_ROWS):
        ref[pl.ds(first * TOKEN_ROWS + c, n, stride=TOKEN_ROWS), :] = val[:, c * LANES:(c + 1) * LANES]


def _load_token_tiles(ref, n, first=0):
    return jnp.concatenate(
        [ref[pl.ds(first * TOKEN_ROWS + c, n, stride=TOKEN_ROWS), :] for c in range(TOKEN_ROWS)], axis=1)


def _inproj_kernel(x_ref, gmix_ref, w_ref, bd64_ref, bd32_ref, gsec_ref, cos_ref, sin_ref, out_ref, qvt_ref):
    tm = x_ref.shape[0]
    a = (_rms_rows(x_ref[...]) * gmix_ref[...]).astype(BF16)
    lane = lax.broadcasted_iota(jnp.int32, (tm, LANES), 1)
    first_half = (lane % 64) < 32
    cos = cos_ref[...]
    sin = sin_ref[...]
    out_col = {1: 0, 3: 1, 4: 2, 5: 3}
    qvt_row = {0: 0, 2: 1}
    for sec in range(6):
        y = jnp.dot(a, w_ref[:, sec * 512:(sec + 1) * 512], preferred_element_type=F32)
        if sec in (0, 1, 3, 4):
            bd = bd64_ref if sec < 2 else bd32_ref
            gi = {0: 0, 1: 1, 3: 2, 4: 3}[sec]
            y2 = (y * y).astype(BF16)
            ms = jnp.concatenate([jnp.dot(y2[:, half * 256:(half + 1) * 256], bd[...], preferred_element_type=F32)
                                  for half in range(2)], axis=1)
            y = y * lax.rsqrt(ms + EPS) * gsec_ref[gi:gi + 1, :]
        for c in range(4):
            yc = y[:, c * LANES:(c + 1) * LANES]
            if sec < 2:
                rot = jnp.where(first_half, pltpu.roll(yc, 96, 1), pltpu.roll(yc, 32, 1))
                yc = yc * cos + rot * sin
            if sec in qvt_row:
                r0 = qvt_row[sec] * 512 + c * LANES
                qvt_ref[0, r0:r0 + LANES, :] = yc.T.astype(BF16)
            else:
                c0 = out_col[sec] * 512 + c * LANES
                out_ref[:, c0:c0 + LANES] = yc.astype(BF16)


def _inproj(x2, g_mix, w_bf, bd64, bd32, gsec, cos_t, sin_t, seq, tm):
    t, d = x2.shape
    n = w_bf.shape[1]
    nsb = seq // tm
    n_tok_major = 4 * 512
    n_feat_major = 2 * 512
    const = lambda i: (0, 0)
    return pl.pallas_call(
        _inproj_kernel,
        grid=(t // tm,),
        in_specs=[
            pl.BlockSpec((tm, d), lambda i: (i, 0)),
            pl.BlockSpec((1, d), const),
            pl.BlockSpec((d, n), const),
            pl.BlockSpec((256, 256), const),
            pl.BlockSpec((256, 256), const),
            pl.BlockSpec((4, 512), const),
            pl.BlockSpec((tm, LANES), lambda i: (i % nsb, 0)),
            pl.BlockSpec((tm, LANES), lambda i: (i % nsb, 0)),
        ],
        out_specs=[pl.BlockSpec((tm, n_tok_major), lambda i: (i, 0)),
                   pl.BlockSpec((1, n_feat_major, tm), lambda i: (i // nsb, 0, i % nsb))],
        out_shape=[jax.ShapeDtypeStruct((t, n_tok_major), BF16),
                   jax.ShapeDtypeStruct((t // seq, n_feat_major, seq), BF16)],
        compiler_params=_cparams(("parallel",)),
        name="inproj",
    )(x2, g_mix, w_bf, bd64, bd32, gsec, cos_t, sin_t)


QBLOCK_PAIRS_PER_BODY = 4

def _diffattn_kernel(lam_ref, gsub_ref, qt_ref, k_ref, vt_ref, o_ref, sa_scr, sb_scr, *, seq, tq, tk, lam_init):
    nq = seq // tq
    nk = seq // tk
    s1 = jnp.sum(lam_ref[0:1, :] * lam_ref[1:2, :], axis=1, keepdims=True)
    s2 = jnp.sum(lam_ref[2:3, :] * lam_ref[3:4, :], axis=1, keepdims=True)
    lam = jnp.exp(s1) - jnp.exp(s2) + lam_init

    row = lax.broadcasted_iota(jnp.int32, (LANES, tq), 0)

    def scores(qi, s_scr):
        qoff = pl.multiple_of(qi * tq, tq)
        qt = qt_ref[0, :, pl.ds(qoff, tq)]
        zero = jnp.zeros_like(qt)
        qm = jnp.concatenate([jnp.where(row < DA_QK, qt, zero), jnp.where(row >= DA_QK, qt, zero)], axis=1)
        mx = None
        for j in range(nk):
            st = jnp.dot(k_ref[0, j * tk:(j + 1) * tk, :], qm, preferred_element_type=F32)
            s_scr[j * tk:(j + 1) * tk, :] = st
            cm = jnp.max(st, axis=0, keepdims=True)
            mx = cm if mx is None else jnp.maximum(mx, cm)
        return mx

    def finish(qi, s_scr, m):
        l = jnp.zeros((1, 2 * tq), F32)
        acc = jnp.zeros((DA_V, 2 * tq), F32)
        for j in range(nk):
            p = jnp.exp2(s_scr[j * tk:(j + 1) * tk, :] - m)
            l = l + jnp.sum(p, axis=0, keepdims=True)
            acc = acc + jnp.dot(vt_ref[0, :, j * tk:(j + 1) * tk], p.astype(BF16), preferred_element_type=F32)
        o = acc * (1.0 / l)
        ot = o[:, :tq] - lam * o[:, tq:]
        ot = ot * lax.rsqrt(jnp.mean(ot * ot, axis=0, keepdims=True) + EPS) * (1.0 - lam_init)
        qoff = pl.multiple_of(qi * tq, tq)
        o_ref[0, pl.ds(qoff, tq), :] = (ot.T * gsub_ref[...]).astype(BF16)

    def body(i, m_a):
        for pair in range(QBLOCK_PAIRS_PER_BODY):
            q0 = 2 * (QBLOCK_PAIRS_PER_BODY * i + pair)
            m_b = scores(q0 + 1, sb_scr)
            finish(q0, sa_scr, m_a)
            m_a = scores(jnp.minimum(q0 + 2, nq - 1), sa_scr)
            finish(q0 + 1, sb_scr, m_b)
        return m_a

    lax.fori_loop(0, nq // (2 * QBLOCK_PAIRS_PER_BODY), body, scores(0, sa_scr))


def _diffattn(proj3, qvt, lam4, g_sub, lam_init, tq=128, tk=512):
    b, seq, _ = proj3.shape
    kern = functools.partial(_diffattn_kernel, seq=seq, tq=tq, tk=tk, lam_init=lam_init)
    return pl.pallas_call(
        kern,
        grid=(b, HA),
        in_specs=[
            pl.BlockSpec((4, DA_QK), lambda i, h: (0, 0)),
            pl.BlockSpec((1, DA_V), lambda i, h: (0, 0)),
            pl.BlockSpec((1, LANES, seq), lambda i, h: (i, h, 0)),
            pl.BlockSpec((1, seq, LANES), lambda i, h: (i, 0, h)),
            pl.BlockSpec((1, LANES, seq), lambda i, h: (i, HA + h, 0)),
        ],
        out_specs=pl.BlockSpec((1, seq, LANES), lambda i, h: (i, 0, h)),
        out_shape=jax.ShapeDtypeStruct((b, seq, HA * DA_V), BF16),
        scratch_shapes=[pltpu.VMEM((seq, 2 * tq), F32), pltpu.VMEM((seq, 2 * tq), F32)],
        compiler_params=_cparams(("parallel", "parallel")),
        name="diffattn",
    )(lam4, g_sub, qvt, proj3, qvt)


def _natten_kernel(bias_ref, q_ref, k_ref, v_ref, o_ref, *, rows):
    kr = min(WIN_R, rows)
    nkeys = kr * GRID_W
    hpg = LANES // DB_H
    lane_head = lax.broadcasted_iota(jnp.int32, (GRID_W, LANES), 1) // DB_H

    def row_step(r, carry):
        rs = jnp.clip(r - kr // 2, 0, rows - kr)
        base = rs - r + (WIN_R - 1)
        qoff = pl.multiple_of(r * GRID_W, GRID_W)
        koff = pl.multiple_of(rs * GRID_W, GRID_W)
        qr = q_ref[0, pl.ds(qoff, GRID_W), :]
        zero = jnp.zeros_like(qr)
        qs = jnp.concatenate([jnp.where(lane_head == h, qr, zero) for h in range(hpg)], axis=0)
        kw = k_ref[0, pl.ds(koff, nkeys), :]
        vw = v_ref[0, pl.ds(koff, nkeys), :]
        s = lax.dot_general(qs, kw, (((1,), (1,)), ((), ())), preferred_element_type=F32)
        bias = jnp.concatenate([bias_ref[0, base + 2 * i] for i in range(kr // 2)], axis=1)
        s = s + bias
        m = jnp.max(s, axis=1, keepdims=True)
        p = jnp.exp(s - m).astype(BF16)
        pv = jnp.dot(p, jnp.concatenate([vw, jnp.ones_like(vw)], axis=1), preferred_element_type=F32)
        o = pv[:, :LANES] * (1.0 / pv[:, LANES:])
        out = jnp.zeros((GRID_W, LANES), F32)
        for h in range(hpg):
            out = out + jnp.where(lane_head == h, o[h * GRID_W:(h + 1) * GRID_W, :], 0.0)
        o_ref[0, pl.ds(qoff, GRID_W), :] = out.astype(BF16)
        return carry

    lax.fori_loop(0, rows, row_step, 0, unroll=min(rows, 32))


def _natten(proj3, bias_tab):
    b, seq, _ = proj3.shape
    rows = seq // GRID_W
    ng = HB * DB_H // LANES
    nro = bias_tab.shape[1]
    kern = functools.partial(_natten_kernel, rows=rows)
    cb = HA
    return pl.pallas_call(
        kern,
        grid=(b, ng),
        in_specs=[
            pl.BlockSpec((1, nro, bias_tab.shape[2], LANES), lambda i, g: (g, 0, 0, 0)),
            pl.BlockSpec((1, seq, LANES), lambda i, g: (i, 0, cb + g)),
            pl.BlockSpec((1, seq, LANES), lambda i, g: (i, 0, cb + ng + g)),
            pl.BlockSpec((1, seq, LANES), lambda i, g: (i, 0, cb + 2 * ng + g)),
        ],
        out_specs=pl.BlockSpec((1, seq, LANES), lambda i, g: (i, 0, g)),
        out_shape=jax.ShapeDtypeStruct((b, seq, HB * DB_H), BF16),
        compiler_params=_cparams(("parallel", "parallel")),
        name="natten",
    )(bias_tab, proj3, proj3, proj3)


def _natten_bias_table(rpb):
    q = np.arange(GRID_W)[:, None]
    kc = np.arange(GRID_W)[None, :]
    cs = np.clip(q - WIN_C // 2, 0, GRID_W - WIN_C)
    valid = (kc >= cs) & (kc < cs + WIN_C)
    col_off = np.clip(kc - q + (WIN_C - 1), 0, 2 * WIN_C - 2)
    ncol = 2 * WIN_C - 1
    nro = 2 * WIN_R - 2
    pick = np.zeros((2, ncol, GRID_W, 2 * GRID_W), np.float32)
    for j in range(2):
        pick[j, :, :, j * GRID_W:(j + 1) * GRID_W] = (np.arange(ncol)[:, None, None] == col_off[None]) & valid[None]
    negmask = np.tile(np.where(valid, 0.0, NEG).astype(np.float32), (1, 2))
    hpg = LANES // DB_H
    r4 = rpb.astype(F32).reshape(HB // hpg, hpg, 2 * WIN_R - 1, ncol)
    rp = jnp.stack([r4[:, :, :nro], r4[:, :, 1:]], axis=0)
    tab = jnp.einsum('jghrc,jcqk->grhqk', rp, jnp.asarray(pick), precision=lax.Precision.HIGHEST) + negmask
    return tab.reshape(HB // hpg, nro, hpg * GRID_W, LANES)


def _outproj_kernel(x_ref, oa_ref, ob_ref, wo_ref, gffn_ref, wr_ref, tri_ref,
                    h_ref, xn_ref, meta_ref, meta_t_ref, cnt_ref, run_scr):
    tm = x_ref.shape[0]
    da = oa_ref.shape[1]

    @pl.when(pl.program_id(0) == 0)
    def _():
        run_scr[...] = jnp.zeros_like(run_scr)

    attn = jnp.dot(oa_ref[...], wo_ref[:da, :], preferred_element_type=F32)
    attn = attn + jnp.dot(ob_ref[...], wo_ref[da:, :], preferred_element_type=F32)
    h = x_ref[...] + attn
    h_ref[...] = h
    xn = _rms_rows(h) * gffn_ref[...]
    _store_token_tiles(xn_ref, _pack_bf16_pairs(xn))

    xh = xn.astype(BF16)
    xl = (xn - xh.astype(F32)).astype(BF16)
    hh_hl = jnp.dot(xh, wr_ref[...], preferred_element_type=F32)
    lg = hh_hl[:, :LANES] + hh_hl[:, LANES:] + jnp.dot(xl, wr_ref[:, :LANES], preferred_element_type=F32)
    lane = lax.broadcasted_iota(jnp.int32, (tm, LANES), 1).astype(F32)
    ninf = -jnp.inf
    is_g = lane < N_GROUPS
    gl = jnp.where(is_g, lg, ninf)
    gmax = jnp.max(gl, axis=1, keepdims=True)
    gidx = jnp.min(jnp.where(gl == gmax, lane, float(LANES)), axis=1, keepdims=True)
    gsum = jnp.sum(jnp.where(is_g, jnp.exp(gl - gmax), 0.0), axis=1, keepdims=True)
    g_gate = 1.0 / gsum
    lo = E_LANE0 + gidx * EXPERTS_PER_GROUP
    is_e = (lane >= lo) & (lane < lo + EXPERTS_PER_GROUP)
    el = jnp.where(is_e, lg, ninf)
    l1 = jnp.max(el, axis=1, keepdims=True)
    i1 = jnp.min(jnp.where(el == l1, lane, float(LANES)), axis=1, keepdims=True)
    el2 = jnp.where(lane == i1, ninf, el)
    l2 = jnp.max(el2, axis=1, keepdims=True)
    i2 = jnp.min(jnp.where(el2 == l2, lane, float(LANES)), axis=1, keepdims=True)
    r = jnp.exp(l2 - l1)
    w1 = g_gate / (1.0 + r)
    w2 = w1 * r

    sel1 = lane == i1
    sel2 = lane == i2
    oh = jnp.where(sel1 | sel2, 1.0, 0.0)
    before = jnp.dot(tri_ref[...], oh.astype(BF16), preferred_element_type=F32) + run_scr[...]
    rank1 = jnp.sum(jnp.where(sel1, before, 0.0), axis=1, keepdims=True)
    rank2 = jnp.sum(jnp.where(sel2, before, 0.0), axis=1, keepdims=True)
    run = run_scr[...] + jnp.sum(oh, axis=0, keepdims=True)
    run_scr[...] = run
    cnt_ref[...] = run

    meta = jnp.where(lane == 0, i1 - E_LANE0, 0.0)
    meta = jnp.where(lane == 1, i2 - E_LANE0, meta)
    meta = jnp.where(lane == 2, rank1, meta)
    meta = jnp.where(lane == 3, rank2, meta)
    meta = jnp.where(lane == 4, w1, meta)
    meta = jnp.where(lane == 5, w2, meta)
    meta_ref[...] = meta
    meta_t_ref[...] = meta.T[:SUBLANES, :]


def _outproj(x2, oa2, ob2, wo_bf, g_ffn, w_router, tri, tm):
    t, d = x2.shape
    da = oa2.shape[1]
    const = lambda i: (0, 0)
    row = lambda i: (i, 0)
    return pl.pallas_call(
        _outproj_kernel,
        grid=(t // tm,),
        in_specs=[
            pl.BlockSpec((tm, d), row),
            pl.BlockSpec((tm, da), row),
            pl.BlockSpec((tm, d - da), row),
            pl.BlockSpec((d, d), const),
            pl.BlockSpec((1, d), const),
            pl.BlockSpec((d, 2 * LANES), const),
            pl.BlockSpec((tm, tm), const),
        ],
        out_specs=[
            pl.BlockSpec((tm, d), row),
            pl.BlockSpec((tm * TOKEN_ROWS, LANES), row),
            pl.BlockSpec((tm, LANES), row),
            pl.BlockSpec((SUBLANES, tm), lambda i: (0, i)),
            pl.BlockSpec((1, LANES), const),
        ],
        out_shape=[
            jax.ShapeDtypeStruct((t, d), F32),
            jax.ShapeDtypeStruct((t * TOKEN_ROWS, LANES), U32),
            jax.ShapeDtypeStruct((t, LANES), F32),
            jax.ShapeDtypeStruct((SUBLANES, t), F32),
            jax.ShapeDtypeStruct((1, LANES), F32),
        ],
        scratch_shapes=[pltpu.VMEM((1, LANES), F32)],
        compiler_params=_cparams(("arbitrary",)),
        name="outproj_router",
    )(x2, oa2, ob2, wo_bf, g_ffn, w_router, tri)


DMA_ISSUE_UNROLL = 8
DMA_PRIORITIES = 2


def _dispatch_kernel(dest_ref, pend_ref, nu_ref, xn_ref, xs_ref, zbuf, sem, zsem):
    tm = xn_ref.shape[0] // TOKEN_ROWS
    t_total = pl.num_programs(0) * tm
    base = pl.program_id(0) * tm
    n_blocks = xs_ref.shape[0] // (MOE_BLOCK * TOKEN_ROWS)

    @pl.when(pl.program_id(0) == 0)
    def _():
        zbuf[...] = jnp.zeros_like(zbuf)

        def zero_block(blk):
            off = pl.multiple_of(blk * (MOE_BLOCK * TOKEN_ROWS), MOE_BLOCK * TOKEN_ROWS)
            return pltpu.make_async_copy(zbuf, xs_ref.at[pl.ds(off, MOE_BLOCK * TOKEN_ROWS), :], zsem)

        def for_each_zeroed_block(fn):
            def per_expert(e, c):
                end = pend_ref[e]
                start = jnp.where(e == 0, 0, pend_ref[jnp.maximum(e - 1, 0)])

                @pl.when(end > start)
                def _():
                    fn(zero_block(end // MOE_BLOCK - 1))
                return c
            lax.fori_loop(0, N_EXPERTS, per_expert, 0)

            def per_tail_block(blk, c):
                fn(zero_block(blk))
                return c
            lax.fori_loop(nu_ref[0], n_blocks, per_tail_block, 0)

        for_each_zeroed_block(lambda cp: cp.start())
        for_each_zeroed_block(lambda cp: cp.wait())

    def issue(r, c):
        src = xn_ref.at[pl.ds(pl.multiple_of(r * TOKEN_ROWS, TOKEN_ROWS), TOKEN_ROWS), :]
        for slot in range(TOP_K):
            d = dest_ref[slot * t_total + base + r]
            dst = xs_ref.at[pl.ds(pl.multiple_of(d * TOKEN_ROWS, TOKEN_ROWS), TOKEN_ROWS), :]
            pltpu.make_async_copy(src, dst, sem).start(priority=slot % DMA_PRIORITIES)
        return c
    lax.fori_loop(0, tm, issue, 0, unroll=DMA_ISSUE_UNROLL)

    for slot in range(TOP_K):
        pltpu.make_async_copy(xn_ref, xs_ref.at[pl.ds(0, tm * TOKEN_ROWS), :], sem).wait()


def _dispatch(xn_tiles, dest_flat, pad_end, n_used, p_len, tm):
    t = xn_tiles.shape[0] // TOKEN_ROWS
    grid_spec = pltpu.PrefetchScalarGridSpec(
        num_scalar_prefetch=3,
        grid=(t // tm,),
        in_specs=[pl.BlockSpec((tm * TOKEN_ROWS, LANES), lambda i, dest, pend, nu: (i, 0))],
        out_specs=pl.BlockSpec(memory_space=pl.ANY),
        scratch_shapes=[pltpu.VMEM((MOE_BLOCK * TOKEN_ROWS, LANES), U32),
                        pltpu.SemaphoreType.DMA(()), pltpu.SemaphoreType.DMA(())],
    )
    return pl.pallas_call(
        _dispatch_kernel,
        grid_spec=grid_spec,
        out_shape=jax.ShapeDtypeStruct((p_len * TOKEN_ROWS, LANES), U32),
        compiler_params=_cparams(("arbitrary",)),
        name="dispatch",
    )(dest_flat, pad_end, n_used, xn_tiles)


def _experts_kernel(be_ref, nu_ref, nxt_ref, ord_ref, xs_ref, w1_hbm, w3_hbm, w2_hbm, ys_ref,
                    w1_scr, w3_scr, w2_scr, w1_buf, w3_buf, w2_buf, wsems):
    i = pl.program_id(0)

    def weight_copies(e, slot):
        return (pltpu.make_async_copy(w1_hbm.at[e], w1_buf.at[slot], wsems.at[slot, 0]),
                pltpu.make_async_copy(w3_hbm.at[e], w3_buf.at[slot], wsems.at[slot, 1]),
                pltpu.make_async_copy(w2_hbm.at[e], w2_buf.at[slot], wsems.at[slot, 2]))

    @pl.when(i < nu_ref[0])
    def _():
        e = be_ref[i]
        prev = be_ref[jnp.maximum(i - 1, 0)]
        slot = ord_ref[e] % 2

        @pl.when(i == 0)
        def _():
            for cp in weight_copies(e, slot):
                cp.start()

        @pl.when((i == 0) | (e != prev))
        def _():
            for cp in weight_copies(e, slot):
                cp.wait()
            nxt = nxt_ref[e]

            @pl.when(nxt >= 0)
            def _():
                for cp in weight_copies(nxt, 1 - slot):
                    cp.start()
            w1_scr[...] = w1_buf[slot].astype(BF16)
            w3_scr[...] = w3_buf[slot].astype(BF16)
            w2_scr[...] = w2_buf[slot].astype(BF16)

        x = _unpack_bf16_pairs(_load_token_tiles(xs_ref, MOE_BLOCK)).astype(BF16)
        a = jnp.dot(x, w1_scr[...], preferred_element_type=F32)
        b = jnp.dot(x, w3_scr[...], preferred_element_type=F32)
        hdn = a * (1.0 / (1.0 + jnp.exp(-a))) * b
        y = jnp.dot(hdn.astype(BF16), w2_scr[...], preferred_element_type=F32)
        _store_token_tiles(ys_ref, _pack_bf16_pairs(y))

    @pl.when(i >= nu_ref[0])
    def _():
        ys_ref[...] = jnp.zeros_like(ys_ref)


def _experts(xs, block_e, n_used, next_expert, expert_ord, w1, w3, w2):
    p_len = xs.shape[0] // TOKEN_ROWS
    n_blocks = p_len // MOE_BLOCK
    d, de = w1.shape[1], w1.shape[2]
    blk = lambda i, be, nu, nxt, od: (jnp.maximum(jnp.minimum(i, nu[0] - 1), 0), 0)
    hbm = pl.BlockSpec(memory_space=pl.ANY)
    grid_spec = pltpu.PrefetchScalarGridSpec(
        num_scalar_prefetch=4,
        grid=(n_blocks,),
        in_specs=[pl.BlockSpec((MOE_BLOCK * TOKEN_ROWS, LANES), blk), hbm, hbm, hbm],
        out_specs=pl.BlockSpec((MOE_BLOCK * TOKEN_ROWS, LANES), lambda i, be, nu, nxt, od: (i, 0)),
        scratch_shapes=[pltpu.VMEM((d, de), BF16), pltpu.VMEM((d, de), BF16), pltpu.VMEM((de, d), BF16),
                        pltpu.VMEM((2, d, de), F32), pltpu.VMEM((2, d, de), F32), pltpu.VMEM((2, de, d), F32),
                        pltpu.SemaphoreType.DMA((2, 3))],
    )
    return pl.pallas_call(
        _experts_kernel,
        grid_spec=grid_spec,
        out_shape=jax.ShapeDtypeStruct((p_len * TOKEN_ROWS, LANES), U32),
        compiler_params=_cparams(("arbitrary",)),
        name="experts",
    )(block_e, n_used, next_expert, expert_ord, xs, w1, w3, w2)


def _final_kernel(dest_ref, h_ref, meta_ref, p_ref, wple_ref, gple_ref, gplg_ref, wplg_ref, ys_ref,
                  out_ref, gbuf_a, gbuf_b, sems):
    th = h_ref.shape[0] // 2
    step = pl.program_id(0)
    n_tiles = 2 * pl.num_programs(0)
    t_total = n_tiles * th

    def gather_rows(tile, gbuf, sem, unrolled):
        def issue(r, c):
            for slot in range(TOP_K):
                d = dest_ref[slot * t_total + tile * th + r]
                src = ys_ref.at[pl.ds(pl.multiple_of(d * TOKEN_ROWS, TOKEN_ROWS), TOKEN_ROWS), :]
                dst = gbuf.at[slot, pl.ds(pl.multiple_of(r * TOKEN_ROWS, TOKEN_ROWS), TOKEN_ROWS), :]
                pltpu.make_async_copy(src, dst, sem).start(priority=slot % DMA_PRIORITIES)
            return c
        if unrolled:
            for r in range(th):
                issue(r, 0)
        else:
            lax.fori_loop(0, th, issue, 0, unroll=DMA_ISSUE_UNROLL)

    def wait_rows(gbuf, sem):
        for slot in range(TOP_K):
            pltpu.make_async_copy(ys_ref.at[pl.ds(0, th * TOKEN_ROWS), :], gbuf.at[slot], sem).wait()

    def combine(half, gbuf):
        rows = slice(half * th, (half + 1) * th)
        emb = jnp.dot(p_ref[rows, :].astype(BF16), wple_ref[...], preferred_element_type=F32)
        emb = _rms_rows(emb) * gple_ref[...]
        meta = meta_ref[rows, :]
        y1 = _unpack_bf16_pairs(_load_token_tiles(gbuf.at[0], th))
        y2 = _unpack_bf16_pairs(_load_token_tiles(gbuf.at[1], th))
        h = h_ref[rows, :] + meta[:, 4:5] * y1 + meta[:, 5:6] * y2
        hn = (_rms_rows(h) * gplg_ref[...]).astype(BF16)
        z = jnp.dot(hn, wplg_ref[...], preferred_element_type=F32)
        out_ref[rows, :] = h + emb * (1.0 / (1.0 + jnp.exp(-z)))

    @pl.when(step == 0)
    def _():
        gather_rows(0, gbuf_a, sems.at[0], unrolled=False)

    wait_rows(gbuf_a, sems.at[0])
    gather_rows(2 * step + 1, gbuf_b, sems.at[1], unrolled=True)
    combine(0, gbuf_a)
    wait_rows(gbuf_b, sems.at[1])
    gather_rows(jnp.minimum(2 * step + 2, n_tiles - 1), gbuf_a, sems.at[0], unrolled=True)
    combine(1, gbuf_b)

    @pl.when(step == pl.num_programs(0) - 1)
    def _():
        wait_rows(gbuf_a, sems.at[0])


def _final(dest_flat, h2, meta, p2, wple_bf, g_ple, g_plg, wplg_bf, ys, tm):
    t, d = h2.shape
    dp = p2.shape[1]
    const = lambda i, dest: (0, 0)
    row = lambda i, dest: (i, 0)
    grid_spec = pltpu.PrefetchScalarGridSpec(
        num_scalar_prefetch=1,
        grid=(t // tm,),
        in_specs=[
            pl.BlockSpec((tm, d), row),
            pl.BlockSpec((tm, LANES), row),
            pl.BlockSpec((tm, dp), row),
            pl.BlockSpec((dp, d), const),
            pl.BlockSpec((1, d), const),
            pl.BlockSpec((1, d), const),
            pl.BlockSpec((d, d), const),
            pl.BlockSpec(memory_space=pl.ANY),
        ],
        out_specs=pl.BlockSpec((tm, d), row),
        scratch_shapes=[pltpu.VMEM((TOP_K, tm // 2 * TOKEN_ROWS, LANES), U32),
                        pltpu.VMEM((TOP_K, tm // 2 * TOKEN_ROWS, LANES), U32), pltpu.SemaphoreType.DMA((2,))],
    )
    return pl.pallas_call(
        _final_kernel,
        grid_spec=grid_spec,
        out_shape=jax.ShapeDtypeStruct((t, d), F32),
        compiler_params=_cparams(("arbitrary",)),
        name="combine_ple",
    )(dest_flat, h2, meta, p2, wple_bf, g_ple, g_plg, wplg_bf, ys)


def _block_diag_mean(seg):
    m = np.kron(np.eye(256 // seg), np.full((seg, seg), 1.0 / seg))
    return jnp.asarray(m, BF16)


def _rope_tables(seq):
    f32 = np.float32
    inv = np.power(f32(ROPE_THETA), -np.arange(0, DA_QK, 2, dtype=f32) / f32(DA_QK)).astype(f32)
    ang = (np.arange(seq, dtype=f32)[:, None] * inv[None, :]).astype(f32)
    cos, sin = np.cos(ang).astype(f32), np.sin(ang).astype(f32)
    cos_t = np.tile(cos, (1, LANES // cos.shape[1]))
    sin_t = np.tile(np.concatenate([-sin, sin], axis=1), (1, LANES // (2 * sin.shape[1])))
    return jnp.asarray(cos_t), jnp.asarray(sin_t)


def _tile_for(n, pref):
    while n % pref:
        pref //= 2
    return pref


def kernel(x, p, g_mix, w_in, g_qa, g_ka, lam_q1, lam_k1, lam_q2, lam_k2, g_sub, g_qb, g_kb, rpb, w_out,
           g_ffn, w_rg, w_re, w1, w3, w2, g_plg, w_plg, w_ple, g_ple):
    b, s, d = x.shape
    assert d == 2 * TOKEN_ROWS * LANES, "token-tile layout packs one token into TOKEN_ROWS rows of 128 uint32"
    t = b * s
    depth = w_in.shape[0]
    tm = _tile_for(s, 512)
    tmd = _tile_for(t, 512)
    tmf = _tile_for(t, 512)
    n_slots = t * TOP_K
    n_blocks = n_slots // MOE_BLOCK + N_EXPERTS
    p_len = n_blocks * MOE_BLOCK

    bd64 = _block_diag_mean(DA_QK)
    bd32 = _block_diag_mean(DB_H)
    cos_t, sin_t = _rope_tables(s)
    tri = jnp.asarray(np.tril(np.ones((tm, tm)), -1), BF16)

    h = x.reshape(t, d)
    for i in range(depth):
        lam_init = 0.8 - 0.6 * math.exp(-0.3 * i)
        gsec = jnp.stack([
            jnp.tile(g_qa[i], 512 // DA_QK) * (DA_QK ** -0.5 * math.log2(math.e)),
            jnp.tile(g_ka[i], 512 // DA_QK),
            jnp.tile(g_qb[i], 512 // DB_H) * DB_H ** -0.5,
            jnp.tile(g_kb[i], 512 // DB_H),
        ]).astype(F32)
        proj, qvt = _inproj(h, g_mix[i][None, :], w_in[i].astype(BF16), bd64, bd32, gsec, cos_t, sin_t, s, tm)
        proj3 = proj.reshape(b, s, proj.shape[1])

        lam4 = jnp.stack([lam_q1[i], lam_k1[i], lam_q2[i], lam_k2[i]]).astype(F32)
        oa = _diffattn(proj3, qvt, lam4, g_sub[i][None, :].astype(F32), lam_init)
        ob = _natten(proj3, _natten_bias_table(rpb[i]))

        w_router = jnp.concatenate(
            [w_rg[i], w_re[i], jnp.zeros((d, LANES - N_GROUPS - N_EXPERTS), F32)], axis=1).astype(F32)
        wr_hi = w_router.astype(BF16)
        wr_lo = (w_router - wr_hi.astype(F32)).astype(BF16)
        h1, xn, meta, meta_t, cnt = _outproj(h, oa.reshape(t, -1), ob.reshape(t, -1), w_out[i].astype(BF16),
                                             g_ffn[i][None, :], jnp.concatenate([wr_hi, wr_lo], axis=1), tri, tm)

        counts = cnt[0, E_LANE0:E_LANE0 + N_EXPERTS].astype(jnp.int32)
        padded = (counts + MOE_BLOCK - 1) // MOE_BLOCK * MOE_BLOCK
        pad_end = jnp.cumsum(padded)
        pad_start = pad_end - padded
        mt = meta_t[0:2 * TOP_K].astype(jnp.int32)
        e_ids = jnp.arange(N_EXPERTS, dtype=jnp.int32)[None, :, None]
        start_of = jnp.sum(jnp.where(mt[:TOP_K, None, :] == e_ids, pad_start[None, :, None], 0), axis=1)
        dest = start_of + mt[TOP_K:]
        blk_start = jnp.arange(n_blocks, dtype=jnp.int32) * MOE_BLOCK
        block_e = jnp.minimum(jnp.sum(pad_end[None, :] <= blk_start[:, None], axis=1), N_EXPERTS - 1)
        block_e = block_e.astype(jnp.int32)
        n_used = (pad_end[-1:] // MOE_BLOCK).astype(jnp.int32)

        dest_flat = dest.reshape(TOP_K * t)

        xs = _dispatch(xn, dest_flat, pad_end.astype(jnp.int32), n_used, p_len, tmd)
        in_use = counts > 0
        expert_ord = (jnp.cumsum(in_use) - 1).astype(jnp.int32)
        ids = jnp.arange(N_EXPERTS, dtype=jnp.int32)
        later = jnp.where((ids[None, :] > ids[:, None]) & in_use[None, :], ids[None, :], N_EXPERTS)
        next_expert = jnp.min(later, axis=1)
        next_expert = jnp.where(next_expert == N_EXPERTS, -1, next_expert).astype(jnp.int32)
        ys = _experts(xs, block_e, n_used, next_expert, expert_ord, w1[i], w3[i], w2[i])
        h = _final(dest_flat, h1, meta, p[i].reshape(t, -1),
                   w_ple[i].astype(BF16), g_ple[i][None, :], g_plg[i][None, :], w_plg[i].astype(BF16), ys, tmf)
    return h.reshape(b, s, d)
```

```python
import functools
import math

import numpy as np
import jax
import jax.numpy as jnp
from jax import lax
from jax.experimental import pallas as pl
from jax.experimental.pallas import tpu as pltpu

F32 = jnp.float32
BF16 = jnp.bfloat16
U32 = jnp.uint32

LANES = 128
HA = 4
DA_QK = 64
DA_V = 128
HB = 16
DB_H = 32
GRID_W = 64
WIN_R = 8
WIN_C = 16
ROPE_THETA = 10000.0
EPS = 1e-6
NEG = -1e30
N_GROUPS = 4
EXPERTS_PER_GROUP = 8
N_EXPERTS = N_GROUPS * EXPERTS_PER_GROUP
TOP_K = 2
MOE_BLOCK = 256
E_LANE0 = N_GROUPS

VMEM_LIMIT = 56 * 1024 * 1024


def _cparams(sem):
    return pltpu.CompilerParams(dimension_semantics=sem, vmem_limit_bytes=VMEM_LIMIT)


def _rms_rows(x):
    return x * lax.rsqrt(jnp.mean(x * x, axis=-1, keepdims=True) + EPS)


SUBLANES = 8

TOKEN_ROWS = 4
HIGH_HALF = 0xFFFF0000


def _pack_bf16_pairs(x):
    m = x.shape[1] // 2
    bits = pltpu.bitcast(x.astype(BF16).astype(F32), U32)
    return (bits[:, :m] >> 16) | (bits[:, m:] & jnp.uint32(HIGH_HALF))


def _unpack_bf16_pairs(u):
    lo = pltpu.bitcast(u << 16, F32)
    hi = pltpu.bitcast(u & jnp.uint32(HIGH_HALF), F32)
    return jnp.concatenate([lo, hi], axis=1)


def _store_token_tiles(ref, val, first=0):
    n = val.shape[0]
    for c in range(TOKEN_ROWS):
        ref[pl.ds(first * TOKEN_ROWS + c, n, stride=TOKEN_ROWS), :] = val[:, c * LANES:(c + 1) * LANES]


def _load_token_tiles(ref, n, first=0):
    return jnp.concatenate(
        [ref[pl.ds(first * TOKEN_ROWS + c, n, stride=TOKEN_ROWS), :] for c in range(TOKEN_ROWS)], axis=1)


def _inproj_kernel(x_ref, gmix_ref, w_ref, bd64_ref, bd32_ref, gsec_ref, cos_ref, sin_ref, out_ref, qvt_ref):
    tm = x_ref.shape[0]
    a = (_rms_rows(x_ref[...]) * gmix_ref[...]).astype(BF16)
    lane = lax.broadcasted_iota(jnp.int32, (tm, LANES), 1)
    first_half = (lane % 64) < 32
    cos = cos_ref[...]
    sin = sin_ref[...]
    out_col = {1: 0, 3: 1, 4: 2, 5: 3}
    qvt_row = {0: 0, 2: 1}
    for sec in range(6):
        y = jnp.dot(a, w_ref[:, sec * 512:(sec + 1) * 512], preferred_element_type=F32)
        if sec in (0, 1, 3, 4):
            bd = bd64_ref if sec < 2 else bd32_ref
            gi = {0: 0, 1: 1, 3: 2, 4: 3}[sec]
            y2 = (y * y).astype(BF16)
            ms = jnp.concatenate([jnp.dot(y2[:, half * 256:(half + 1) * 256], bd[...], preferred_element_type=F32)
                                  for half in range(2)], axis=1)
            y = y * lax.rsqrt(ms + EPS) * gsec_ref[gi:gi + 1, :]
        for c in range(4):
            yc = y[:, c * LANES:(c + 1) * LANES]
            if sec < 2:
                rot = jnp.where(first_half, pltpu.roll(yc, 96, 1), pltpu.roll(yc, 32, 1))
                yc = yc * cos + rot * sin
            if sec in qvt_row:
                r0 = qvt_row[sec] * 512 + c * LANES
                qvt_ref[0, r0:r0 + LANES, :] = yc.T.astype(BF16)
            else:
                c0 = out_col[sec] * 512 + c * LANES
                out_ref[:, c0:c0 + LANES] = yc.astype(BF16)


def _inproj(x2, g_mix, w_bf, bd64, bd32, gsec, cos_t, sin_t, seq, tm):
    t, d = x2.shape
    n = w_bf.shape[1]
    nsb = seq // tm
    n_tok_major = 4 * 512
    n_feat_major = 2 * 512
    const = lambda i: (0, 0)
    return pl.pallas_call(
        _inproj_kernel,
        grid=(t // tm,),
        in_specs=[
            pl.BlockSpec((tm, d), lambda i: (i, 0)),
            pl.BlockSpec((1, d), const),
            pl.BlockSpec((d, n), const),
            pl.BlockSpec((256, 256), const),
            pl.BlockSpec((256, 256), const),
            pl.BlockSpec((4, 512), const),
            pl.BlockSpec((tm, LANES), lambda i: (i % nsb, 0)),
            pl.BlockSpec((tm, LANES), lambda i: (i % nsb, 0)),
        ],
        out_specs=[pl.BlockSpec((tm, n_tok_major), lambda i: (i, 0)),
                   pl.BlockSpec((1, n_feat_major, tm), lambda i: (i // nsb, 0, i % nsb))],
        out_shape=[jax.ShapeDtypeStruct((t, n_tok_major), BF16),
                   jax.ShapeDtypeStruct((t // seq, n_feat_major, seq), BF16)],
        compiler_params=_cparams(("parallel",)),
        name="inproj",
    )(x2, g_mix, w_bf, bd64, bd32, gsec, cos_t, sin_t)


QBLOCK_PAIRS_PER_BODY = 4

def _diffattn_kernel(lam_ref, gsub_ref, qt_ref, k_ref, vt_ref, o_ref, sa_scr, sb_scr, *, seq, tq, tk, lam_init):
    nq = seq // tq
    nk = seq // tk
    s1 = jnp.sum(lam_ref[0:1, :] * lam_ref[1:2, :], axis=1, keepdims=True)
    s2 = jnp.sum(lam_ref[2:3, :] * lam_ref[3:4, :], axis=1, keepdims=True)
    lam = jnp.exp(s1) - jnp.exp(s2) + lam_init

    row = lax.broadcasted_iota(jnp.int32, (LANES, tq), 0)

    def scores(qi, s_scr):
        qoff = pl.multiple_of(qi * tq, tq)
        qt = qt_ref[0, :, pl.ds(qoff, tq)]
        zero = jnp.zeros_like(qt)
        qm = jnp.concatenate([jnp.where(row < DA_QK, qt, zero), jnp.where(row >= DA_QK, qt, zero)], axis=1)
        mx = None
        for j in range(nk):
            st = jnp.dot(k_ref[0, j * tk:(j + 1) * tk, :], qm, preferred_element_type=F32)
            s_scr[j * tk:(j + 1) * tk, :] = st
            cm = jnp.max(st, axis=0, keepdims=True)
            mx = cm if mx is None else jnp.maximum(mx, cm)
        return mx

    def finish(qi, s_scr, m):
        l = jnp.zeros((1, 2 * tq), F32)
        acc = jnp.zeros((DA_V, 2 * tq), F32)
        for j in range(nk):
            p = jnp.exp2(s_scr[j * tk:(j + 1) * tk, :] - m)
            l = l + jnp.sum(p, axis=0, keepdims=True)
            acc = acc + jnp.dot(vt_ref[0, :, j * tk:(j + 1) * tk], p.astype(BF16), preferred_element_type=F32)
        o = acc * (1.0 / l)
        ot = o[:, :tq] - lam * o[:, tq:]
        ot = ot * lax.rsqrt(jnp.mean(ot * ot, axis=0, keepdims=True) + EPS) * (1.0 - lam_init)
        qoff = pl.multiple_of(qi * tq, tq)
        o_ref[0, pl.ds(qoff, tq), :] = (ot.T * gsub_ref[...]).astype(BF16)

    def body(i, m_a):
        for pair in range(QBLOCK_PAIRS_PER_BODY):
            q0 = 2 * (QBLOCK_PAIRS_PER_BODY * i + pair)
            m_b = scores(q0 + 1, sb_scr)
            finish(q0, sa_scr, m_a)
            m_a = scores(jnp.minimum(q0 + 2, nq - 1), sa_scr)
            finish(q0 + 1, sb_scr, m_b)
        return m_a

    lax.fori_loop(0, nq // (2 * QBLOCK_PAIRS_PER_BODY), body, scores(0, sa_scr))


def _diffattn(proj3, qvt, lam4, g_sub, lam_init, tq=128, tk=512):
    b, seq, _ = proj3.shape
    kern = functools.partial(_diffattn_kernel, seq=seq, tq=tq, tk=tk, lam_init=lam_init)
    return pl.pallas_call(
        kern,
        grid=(b, HA),
        in_specs=[
            pl.BlockSpec((4, DA_QK), lambda i, h: (0, 0)),
            pl.BlockSpec((1, DA_V), lambda i, h: (0, 0)),
            pl.BlockSpec((1, LANES, seq), lambda i, h: (i, h, 0)),
            pl.BlockSpec((1, seq, LANES), lambda i, h: (i, 0, h)),
            pl.BlockSpec((1, LANES, seq), lambda i, h: (i, HA + h, 0)),
        ],
        out_specs=pl.BlockSpec((1, seq, LANES), lambda i, h: (i, 0, h)),
        out_shape=jax.ShapeDtypeStruct((b, seq, HA * DA_V), BF16),
        scratch_shapes=[pltpu.VMEM((seq, 2 * tq), F32), pltpu.VMEM((seq, 2 * tq), F32)],
        compiler_params=_cparams(("parallel", "parallel")),
        name="diffattn",
    )(lam4, g_sub, qvt, proj3, qvt)


def _natten_kernel(bias_ref, q_ref, k_ref, v_ref, o_ref, *, rows):
    kr = min(WIN_R, rows)
    nkeys = kr * GRID_W
    hpg = LANES // DB_H
    lane_head = lax.broadcasted_iota(jnp.int32, (GRID_W, LANES), 1) // DB_H

    def row_step(r, carry):
        rs = jnp.clip(r - kr // 2, 0, rows - kr)
        base = rs - r + (WIN_R - 1)
        qoff = pl.multiple_of(r * GRID_W, GRID_W)
        koff = pl.multiple_of(rs * GRID_W, GRID_W)
        qr = q_ref[0, pl.ds(qoff, GRID_W), :]
        zero = jnp.zeros_like(qr)
        qs = jnp.concatenate([jnp.where(lane_head == h, qr, zero) for h in range(hpg)], axis=0)
        kw = k_ref[0, pl.ds(koff, nkeys), :]
        vw = v_ref[0, pl.ds(koff, nkeys), :]
        s = lax.dot_general(qs, kw, (((1,), (1,)), ((), ())), preferred_element_type=F32)
        bias = jnp.concatenate([bias_ref[0, base + 2 * i] for i in range(kr // 2)], axis=1)
        s = s + bias
        m = jnp.max(s, axis=1, keepdims=True)
        p = jnp.exp(s - m).astype(BF16)
        pv = jnp.dot(p, jnp.concatenate([vw, jnp.ones_like(vw)], axis=1), preferred_element_type=F32)
        o = pv[:, :LANES] * (1.0 / pv[:, LANES:])
        out = jnp.zeros((GRID_W, LANES), F32)
        for h in range(hpg):
            out = out + jnp.where(lane_head == h, o[h * GRID_W:(h + 1) * GRID_W, :], 0.0)
        o_ref[0, pl.ds(qoff, GRID_W), :] = out.astype(BF16)
        return carry

    lax.fori_loop(0, rows, row_step, 0, unroll=min(rows, 32))


def _natten(proj3, bias_tab):
    b, seq, _ = proj3.shape
    rows = seq // GRID_W
    ng = HB * DB_H // LANES
    nro = bias_tab.shape[1]
    kern = functools.partial(_natten_kernel, rows=rows)
    cb = HA
    return pl.pallas_call(
        kern,
        grid=(b, ng),
        in_specs=[
            pl.BlockSpec((1, nro, bias_tab.shape[2], LANES), lambda i, g: (g, 0, 0, 0)),
            pl.BlockSpec((1, seq, LANES), lambda i, g: (i, 0, cb + g)),
            pl.BlockSpec((1, seq, LANES), lambda i, g: (i, 0, cb + ng + g)),
            pl.BlockSpec((1, seq, LANES), lambda i, g: (i, 0, cb + 2 * ng + g)),
        ],
        out_specs=pl.BlockSpec((1, seq, LANES), lambda i, g: (i, 0, g)),
        out_shape=jax.ShapeDtypeStruct((b, seq, HB * DB_H), BF16),
        compiler_params=_cparams(("parallel", "parallel")),
        name="natten",
    )(bias_tab, proj3, proj3, proj3)


def _natten_bias_table(rpb):
    q = np.arange(GRID_W)[:, None]
    kc = np.arange(GRID_W)[None, :]
    cs = np.clip(q - WIN_C // 2, 0, GRID_W - WIN_C)
    valid = (kc >= cs) & (kc < cs + WIN_C)
    col_off = np.clip(kc - q + (WIN_C - 1), 0, 2 * WIN_C - 2)
    ncol = 2 * WIN_C - 1
    nro = 2 * WIN_R - 2
    pick = np.zeros((2, ncol, GRID_W, 2 * GRID_W), np.float32)
    for j in range(2):
        pick[j, :, :, j * GRID_W:(j + 1) * GRID_W] = (np.arange(ncol)[:, None, None] == col_off[None]) & valid[None]
    negmask = np.tile(np.where(valid, 0.0, NEG).astype(np.float32), (1, 2))
    hpg = LANES // DB_H
    r4 = rpb.astype(F32).reshape(HB // hpg, hpg, 2 * WIN_R - 1, ncol)
    rp = jnp.stack([r4[:, :, :nro], r4[:, :, 1:]], axis=0)
    tab = jnp.einsum('jghrc,jcqk->grhqk', rp, jnp.asarray(pick), precision=lax.Precision.HIGHEST) + negmask
    return tab.reshape(HB // hpg, nro, hpg * GRID_W, LANES)


def _outproj_kernel(x_ref, oa_ref, ob_ref, wo_ref, gffn_ref, wr_ref, tri_ref,
                    h_ref, xn_ref, meta_ref, meta_t_ref, cnt_ref, run_scr):
    tm = x_ref.shape[0]
    da = oa_ref.shape[1]

    @pl.when(pl.program_id(0) == 0)
    def _():
        run_scr[...] = jnp.zeros_like(run_scr)

    attn = jnp.dot(oa_ref[...], wo_ref[:da, :], preferred_element_type=F32)
    attn = attn + jnp.dot(ob_ref[...], wo_ref[da:, :], preferred_element_type=F32)
    h = x_ref[...] + attn
    h_ref[...] = h
    xn = _rms_rows(h) * gffn_ref[...]
    _store_token_tiles(xn_ref, _pack_bf16_pairs(xn))

    xh = xn.astype(BF16)
    xl = (xn - xh.astype(F32)).astype(BF16)
    hh_hl = jnp.dot(xh, wr_ref[...], preferred_element_type=F32)
    lg = hh_hl[:, :LANES] + hh_hl[:, LANES:] + jnp.dot(xl, wr_ref[:, :LANES], preferred_element_type=F32)
    lane = lax.broadcasted_iota(jnp.int32, (tm, LANES), 1).astype(F32)
    ninf = -jnp.inf
    is_g = lane < N_GROUPS
    gl = jnp.where(is_g, lg, ninf)
    gmax = jnp.max(gl, axis=1, keepdims=True)
    gidx = jnp.min(jnp.where(gl == gmax, lane, float(LANES)), axis=1, keepdims=True)
    gsum = jnp.sum(jnp.where(is_g, jnp.exp(gl - gmax), 0.0), axis=1, keepdims=True)
    g_gate = 1.0 / gsum
    lo = E_LANE0 + gidx * EXPERTS_PER_GROUP
    is_e = (lane >= lo) & (lane < lo + EXPERTS_PER_GROUP)
    el = jnp.where(is_e, lg, ninf)
    l1 = jnp.max(el, axis=1, keepdims=True)
    i1 = jnp.min(jnp.where(el == l1, lane, float(LANES)), axis=1, keepdims=True)
    el2 = jnp.where(lane == i1, ninf, el)
    l2 = jnp.max(el2, axis=1, keepdims=True)
    i2 = jnp.min(jnp.where(el2 == l2, lane, float(LANES)), axis=1, keepdims=True)
    r = jnp.exp(l2 - l1)
    w1 = g_gate / (1.0 + r)
    w2 = w1 * r

    sel1 = lane == i1
    sel2 = lane == i2
    oh = jnp.where(sel1 | sel2, 1.0, 0.0)
    before = jnp.dot(tri_ref[...], oh.astype(BF16), preferred_element_type=F32) + run_scr[...]
    rank1 = jnp.sum(jnp.where(sel1, before, 0.0), axis=1, keepdims=True)
    rank2 = jnp.sum(jnp.where(sel2, before, 0.0), axis=1, keepdims=True)
    run = run_scr[...] + jnp.sum(oh, axis=0, keepdims=True)
    run_scr[...] = run
    cnt_ref[...] = run

    meta = jnp.where(lane == 0, i1 - E_LANE0, 0.0)
    meta = jnp.where(lane == 1, i2 - E_LANE0, meta)
    meta = jnp.where(lane == 2, rank1, meta)
    meta = jnp.where(lane == 3, rank2, meta)
    meta = jnp.where(lane == 4, w1, meta)
    meta = jnp.where(lane == 5, w2, meta)
    meta_ref[...] = meta
    meta_t_ref[...] = meta.T[:SUBLANES, :]


def _outproj(x2, oa2, ob2, wo_bf, g_ffn, w_router, tri, tm):
    t, d = x2.shape
    da = oa2.shape[1]
    const = lambda i: (0, 0)
    row = lambda i: (i, 0)
    return pl.pallas_call(
        _outproj_kernel,
        grid=(t // tm,),
        in_specs=[
            pl.BlockSpec((tm, d), row),
            pl.BlockSpec((tm, da), row),
            pl.BlockSpec((tm, d - da), row),
            pl.BlockSpec((d, d), const),
            pl.BlockSpec((1, d), const),
            pl.BlockSpec((d, 2 * LANES), const),
            pl.BlockSpec((tm, tm), const),
        ],
        out_specs=[
            pl.BlockSpec((tm, d), row),
            pl.BlockSpec((tm * TOKEN_ROWS, LANES), row),
            pl.BlockSpec((tm, LANES), row),
            pl.BlockSpec((SUBLANES, tm), lambda i: (0, i)),
            pl.BlockSpec((1, LANES), const),
        ],
        out_shape=[
            jax.ShapeDtypeStruct((t, d), F32),
            jax.ShapeDtypeStruct((t * TOKEN_ROWS, LANES), U32),
            jax.ShapeDtypeStruct((t, LANES), F32),
            jax.ShapeDtypeStruct((SUBLANES, t), F32),
            jax.ShapeDtypeStruct((1, LANES), F32),
        ],
        scratch_shapes=[pltpu.VMEM((1, LANES), F32)],
        compiler_params=_cparams(("arbitrary",)),
        name="outproj_router",
    )(x2, oa2, ob2, wo_bf, g_ffn, w_router, tri)


DMA_ISSUE_UNROLL = 8
DMA_PRIORITIES = 2


def _dispatch_kernel(dest_ref, pend_ref, nu_ref, xn_ref, xs_ref, zbuf, sem, zsem):
    tm = xn_ref.shape[0] // TOKEN_ROWS
    t_total = pl.num_programs(0) * tm
    base = pl.program_id(0) * tm
    n_blocks = xs_ref.shape[0] // (MOE_BLOCK * TOKEN_ROWS)

    @pl.when(pl.program_id(0) == 0)
    def _():
        zbuf[...] = jnp.zeros_like(zbuf)

        def zero_block(blk):
            off = pl.multiple_of(blk * (MOE_BLOCK * TOKEN_ROWS), MOE_BLOCK * TOKEN_ROWS)
            return pltpu.make_async_copy(zbuf, xs_ref.at[pl.ds(off, MOE_BLOCK * TOKEN_ROWS), :], zsem)

        def for_each_zeroed_block(fn):
            def per_expert(e, c):
                end = pend_ref[e]
                start = jnp.where(e == 0, 0, pend_ref[jnp.maximum(e - 1, 0)])

                @pl.when(end > start)
                def _():
                    fn(zero_block(end // MOE_BLOCK - 1))
                return c
            lax.fori_loop(0, N_EXPERTS, per_expert, 0)

            def per_tail_block(blk, c):
                fn(zero_block(blk))
                return c
            lax.fori_loop(nu_ref[0], n_blocks, per_tail_block, 0)

        for_each_zeroed_block(lambda cp: cp.start())
        for_each_zeroed_block(lambda cp: cp.wait())

    def issue(r, c):
        src = xn_ref.at[pl.ds(pl.multiple_of(r * TOKEN_ROWS, TOKEN_ROWS), TOKEN_ROWS), :]
        for slot in range(TOP_K):
            d = dest_ref[slot * t_total + base + r]
            dst = xs_ref.at[pl.ds(pl.multiple_of(d * TOKEN_ROWS, TOKEN_ROWS), TOKEN_ROWS), :]
            pltpu.make_async_copy(src, dst, sem).start(priority=slot % DMA_PRIORITIES)
        return c
    lax.fori_loop(0, tm, issue, 0, unroll=DMA_ISSUE_UNROLL)

    for slot in range(TOP_K):
        pltpu.make_async_copy(xn_ref, xs_ref.at[pl.ds(0, tm * TOKEN_ROWS), :], sem).wait()


def _dispatch(xn_tiles, dest_flat, pad_end, n_used, p_len, tm):
    t = xn_tiles.shape[0] // TOKEN_ROWS
    grid_spec = pltpu.PrefetchScalarGridSpec(
        num_scalar_prefetch=3,
        grid=(t // tm,),
        in_specs=[pl.BlockSpec((tm * TOKEN_ROWS, LANES), lambda i, dest, pend, nu: (i, 0))],
        out_specs=pl.BlockSpec(memory_space=pl.ANY),
        scratch_shapes=[pltpu.VMEM((MOE_BLOCK * TOKEN_ROWS, LANES), U32),
                        pltpu.SemaphoreType.DMA(()), pltpu.SemaphoreType.DMA(())],
    )
    return pl.pallas_call(
        _dispatch_kernel,
        grid_spec=grid_spec,
        out_shape=jax.ShapeDtypeStruct((p_len * TOKEN_ROWS, LANES), U32),
        compiler_params=_cparams(("arbitrary",)),
        name="dispatch",
    )(dest_flat, pad_end, n_used, xn_tiles)


def _experts_kernel(be_ref, nu_ref, nxt_ref, ord_ref, xs_ref, w1_hbm, w3_hbm, w2_hbm, ys_ref,
                    w1_scr, w3_scr, w2_scr, w1_buf, w3_buf, w2_buf, wsems):
    first = 2 * pl.program_id(0)
    second = first + 1
    n_used = nu_ref[0]

    def weight_copies(e, slot):
        return (pltpu.make_async_copy(w1_hbm.at[e], w1_buf.at[slot], wsems.at[slot, 0]),
                pltpu.make_async_copy(w3_hbm.at[e], w3_buf.at[slot], wsems.at[slot, 1]),
                pltpu.make_async_copy(w2_hbm.at[e], w2_buf.at[slot], wsems.at[slot, 2]))

    def switch_to(e):
        slot = ord_ref[e] % 2
        for cp in weight_copies(e, slot):
            cp.wait()
        nxt = nxt_ref[e]

        @pl.when(nxt >= 0)
        def _():
            for cp in weight_copies(nxt, 1 - slot):
                cp.start()
        w1_scr[...] = w1_buf[slot].astype(BF16)
        w3_scr[...] = w3_buf[slot].astype(BF16)
        w2_scr[...] = w2_buf[slot].astype(BF16)

    def gated_mlp(row0, n_rows):
        x = _unpack_bf16_pairs(_load_token_tiles(xs_ref, n_rows, row0)).astype(BF16)
        a = jnp.dot(x, w1_scr[...], preferred_element_type=F32)
        b = jnp.dot(x, w3_scr[...], preferred_element_type=F32)
        hdn = a * (1.0 / (1.0 + jnp.exp(-a))) * b
        y = jnp.dot(hdn.astype(BF16), w2_scr[...], preferred_element_type=F32)
        _store_token_tiles(ys_ref, _pack_bf16_pairs(y), row0)

    def zero_rows(row0, n_rows):
        ys_ref[row0 * TOKEN_ROWS:(row0 + n_rows) * TOKEN_ROWS, :] = jnp.zeros((n_rows * TOKEN_ROWS, LANES), U32)

    @pl.when(first < n_used)
    def _():
        e_first = be_ref[first]
        e_before = be_ref[jnp.maximum(first - 1, 0)]

        @pl.when(first == 0)
        def _():
            for cp in weight_copies(e_first, ord_ref[e_first] % 2):
                cp.start()

        @pl.when((first == 0) | (e_first != e_before))
        def _():
            switch_to(e_first)

        second_used = second < n_used
        e_second = be_ref[second]
        same_expert = second_used & (e_second == e_first)

        @pl.when(same_expert)
        def _():
            gated_mlp(0, 2 * MOE_BLOCK)

        @pl.when(jnp.logical_not(same_expert))
        def _():
            gated_mlp(0, MOE_BLOCK)

            @pl.when(second_used)
            def _():
                switch_to(e_second)
                gated_mlp(MOE_BLOCK, MOE_BLOCK)

            @pl.when(jnp.logical_not(second_used))
            def _():
                zero_rows(MOE_BLOCK, MOE_BLOCK)

    @pl.when(first >= n_used)
    def _():
        zero_rows(0, 2 * MOE_BLOCK)


def _experts(xs, block_e, n_used, next_expert, expert_ord, w1, w3, w2):
    p_len = xs.shape[0] // TOKEN_ROWS
    n_blocks = p_len // MOE_BLOCK
    assert n_blocks % 2 == 0
    d, de = w1.shape[1], w1.shape[2]
    blk = lambda i, be, nu, nxt, od: (jnp.maximum(jnp.minimum(i, (nu[0] - 1) // 2), 0), 0)
    hbm = pl.BlockSpec(memory_space=pl.ANY)
    grid_spec = pltpu.PrefetchScalarGridSpec(
        num_scalar_prefetch=4,
        grid=(n_blocks // 2,),
        in_specs=[pl.BlockSpec((2 * MOE_BLOCK * TOKEN_ROWS, LANES), blk), hbm, hbm, hbm],
        out_specs=pl.BlockSpec((2 * MOE_BLOCK * TOKEN_ROWS, LANES), lambda i, be, nu, nxt, od: (i, 0)),
        scratch_shapes=[pltpu.VMEM((d, de), BF16), pltpu.VMEM((d, de), BF16), pltpu.VMEM((de, d), BF16),
                        pltpu.VMEM((2, d, de), F32), pltpu.VMEM((2, d, de), F32), pltpu.VMEM((2, de, d), F32),
                        pltpu.SemaphoreType.DMA((2, 3))],
    )
    return pl.pallas_call(
        _experts_kernel,
        grid_spec=grid_spec,
        out_shape=jax.ShapeDtypeStruct((p_len * TOKEN_ROWS, LANES), U32),
        compiler_params=_cparams(("arbitrary",)),
        name="experts",
    )(block_e, n_used, next_expert, expert_ord, xs, w1, w3, w2)


def _final_kernel(dest_ref, h_ref, meta_ref, p_ref, wple_ref, gple_ref, gplg_ref, wplg_ref, ys_ref,
                  out_ref, gbuf_a, gbuf_b, sems):
    th = h_ref.shape[0] // 2
    step = pl.program_id(0)
    n_tiles = 2 * pl.num_programs(0)
    t_total = n_tiles * th

    def gather_rows(tile, gbuf, sem, unrolled):
        def issue(r, c):
            for slot in range(TOP_K):
                d = dest_ref[slot * t_total + tile * th + r]
                src = ys_ref.at[pl.ds(pl.multiple_of(d * TOKEN_ROWS, TOKEN_ROWS), TOKEN_ROWS), :]
                dst = gbuf.at[slot, pl.ds(pl.multiple_of(r * TOKEN_ROWS, TOKEN_ROWS), TOKEN_ROWS), :]
                pltpu.make_async_copy(src, dst, sem).start(priority=slot % DMA_PRIORITIES)
            return c
        if unrolled:
            for r in range(th):
                issue(r, 0)
        else:
            lax.fori_loop(0, th, issue, 0, unroll=DMA_ISSUE_UNROLL)

    def wait_rows(gbuf, sem):
        for slot in range(TOP_K):
            pltpu.make_async_copy(ys_ref.at[pl.ds(0, th * TOKEN_ROWS), :], gbuf.at[slot], sem).wait()

    def combine(half, gbuf):
        rows = slice(half * th, (half + 1) * th)
        emb = jnp.dot(p_ref[rows, :].astype(BF16), wple_ref[...], preferred_element_type=F32)
        emb = _rms_rows(emb) * gple_ref[...]
        meta = meta_ref[rows, :]
        y1 = _unpack_bf16_pairs(_load_token_tiles(gbuf.at[0], th))
        y2 = _unpack_bf16_pairs(_load_token_tiles(gbuf.at[1], th))
        h = h_ref[rows, :] + meta[:, 4:5] * y1 + meta[:, 5:6] * y2
        hn = (_rms_rows(h) * gplg_ref[...]).astype(BF16)
        z = jnp.dot(hn, wplg_ref[...], preferred_element_type=F32)
        out_ref[rows, :] = h + emb * (1.0 / (1.0 + jnp.exp(-z)))

    @pl.when(step == 0)
    def _():
        gather_rows(0, gbuf_a, sems.at[0], unrolled=False)

    wait_rows(gbuf_a, sems.at[0])
    gather_rows(2 * step + 1, gbuf_b, sems.at[1], unrolled=True)
    combine(0, gbuf_a)
    wait_rows(gbuf_b, sems.at[1])
    gather_rows(jnp.minimum(2 * step + 2, n_tiles - 1), gbuf_a, sems.at[0], unrolled=True)
    combine(1, gbuf_b)

    @pl.when(step == pl.num_programs(0) - 1)
    def _():
        wait_rows(gbuf_a, sems.at[0])


def _final(dest_flat, h2, meta, p2, wple_bf, g_ple, g_plg, wplg_bf, ys, tm):
    t, d = h2.shape
    dp = p2.shape[1]
    const = lambda i, dest: (0, 0)
    row = lambda i, dest: (i, 0)
    grid_spec = pltpu.PrefetchScalarGridSpec(
        num_scalar_prefetch=1,
        grid=(t // tm,),
        in_specs=[
            pl.BlockSpec((tm, d), row),
            pl.BlockSpec((tm, LANES), row),
            pl.BlockSpec((tm, dp), row),
            pl.BlockSpec((dp, d), const),
            pl.BlockSpec((1, d), const),
            pl.BlockSpec((1, d), const),
            pl.BlockSpec((d, d), const),
            pl.BlockSpec(memory_space=pl.ANY),
        ],
        out_specs=pl.BlockSpec((tm, d), row),
        scratch_shapes=[pltpu.VMEM((TOP_K, tm // 2 * TOKEN_ROWS, LANES), U32),
                        pltpu.VMEM((TOP_K, tm // 2 * TOKEN_ROWS, LANES), U32), pltpu.SemaphoreType.DMA((2,))],
    )
    return pl.pallas_call(
        _final_kernel,
        grid_spec=grid_spec,
        out_shape=jax.ShapeDtypeStruct((t, d), F32),
        compiler_params=_cparams(("arbitrary",)),
        name="combine_ple",
    )(dest_flat, h2, meta, p2, wple_bf, g_ple, g_plg, wplg_bf, ys)


def _block_diag_mean(seg):
    m = np.kron(np.eye(256 // seg), np.full((seg, seg), 1.0 / seg))
    return jnp.asarray(m, BF16)


def _rope_tables(seq):
    f32 = np.float32
    inv = np.power(f32(ROPE_THETA), -np.arange(0, DA_QK, 2, dtype=f32) / f32(DA_QK)).astype(f32)
    ang = (np.arange(seq, dtype=f32)[:, None] * inv[None, :]).astype(f32)
    cos, sin = np.cos(ang).astype(f32), np.sin(ang).astype(f32)
    cos_t = np.tile(cos, (1, LANES // cos.shape[1]))
    sin_t = np.tile(np.concatenate([-sin, sin], axis=1), (1, LANES // (2 * sin.shape[1])))
    return jnp.asarray(cos_t), jnp.asarray(sin_t)


def _tile_for(n, pref):
    while n % pref:
        pref //= 2
    return pref


def kernel(x, p, g_mix, w_in, g_qa, g_ka, lam_q1, lam_k1, lam_q2, lam_k2, g_sub, g_qb, g_kb, rpb, w_out,
           g_ffn, w_rg, w_re, w1, w3, w2, g_plg, w_plg, w_ple, g_ple):
    b, s, d = x.shape
    assert d == 2 * TOKEN_ROWS * LANES, "token-tile layout packs one token into TOKEN_ROWS rows of 128 uint32"
    t = b * s
    depth = w_in.shape[0]
    tm = _tile_for(s, 512)
    tmd = _tile_for(t, 512)
    tmf = _tile_for(t, 512)
    n_slots = t * TOP_K
    n_blocks = n_slots // MOE_BLOCK + N_EXPERTS
    p_len = n_blocks * MOE_BLOCK

    bd64 = _block_diag_mean(DA_QK)
    bd32 = _block_diag_mean(DB_H)
    cos_t, sin_t = _rope_tables(s)
    tri = jnp.asarray(np.tril(np.ones((tm, tm)), -1), BF16)

    h = x.reshape(t, d)
    for i in range(depth):
        lam_init = 0.8 - 0.6 * math.exp(-0.3 * i)
        gsec = jnp.stack([
            jnp.tile(g_qa[i], 512 // DA_QK) * (DA_QK ** -0.5 * math.log2(math.e)),
            jnp.tile(g_ka[i], 512 // DA_QK),
            jnp.tile(g_qb[i], 512 // DB_H) * DB_H ** -0.5,
            jnp.tile(g_kb[i], 512 // DB_H),
        ]).astype(F32)
        proj, qvt = _inproj(h, g_mix[i][None, :], w_in[i].astype(BF16), bd64, bd32, gsec, cos_t, sin_t, s, tm)
        proj3 = proj.reshape(b, s, proj.shape[1])

        lam4 = jnp.stack([lam_q1[i], lam_k1[i], lam_q2[i], lam_k2[i]]).astype(F32)
        oa = _diffattn(proj3, qvt, lam4, g_sub[i][None, :].astype(F32), lam_init)
        ob = _natten(proj3, _natten_bias_table(rpb[i]))

        w_router = jnp.concatenate(
            [w_rg[i], w_re[i], jnp.zeros((d, LANES - N_GROUPS - N_EXPERTS), F32)], axis=1).astype(F32)
        wr_hi = w_router.astype(BF16)
        wr_lo = (w_router - wr_hi.astype(F32)).astype(BF16)
        h1, xn, meta, meta_t, cnt = _outproj(h, oa.reshape(t, -1), ob.reshape(t, -1), w_out[i].astype(BF16),
                                             g_ffn[i][None, :], jnp.concatenate([wr_hi, wr_lo], axis=1), tri, tm)

        counts = cnt[0, E_LANE0:E_LANE0 + N_EXPERTS].astype(jnp.int32)
        padded = (counts + MOE_BLOCK - 1) // MOE_BLOCK * MOE_BLOCK
        pad_end = jnp.cumsum(padded)
        pad_start = pad_end - padded
        mt = meta_t[0:2 * TOP_K].astype(jnp.int32)
        e_ids = jnp.arange(N_EXPERTS, dtype=jnp.int32)[None, :, None]
        start_of = jnp.sum(jnp.where(mt[:TOP_K, None, :] == e_ids, pad_start[None, :, None], 0), axis=1)
        dest = start_of + mt[TOP_K:]
        blk_start = jnp.arange(n_blocks, dtype=jnp.int32) * MOE_BLOCK
        block_e = jnp.minimum(jnp.sum(pad_end[None, :] <= blk_start[:, None], axis=1), N_EXPERTS - 1)
        block_e = block_e.astype(jnp.int32)
        n_used = (pad_end[-1:] // MOE_BLOCK).astype(jnp.int32)

        dest_flat = dest.reshape(TOP_K * t)

        xs = _dispatch(xn, dest_flat, pad_end.astype(jnp.int32), n_used, p_len, tmd)
        in_use = counts > 0
        expert_ord = (jnp.cumsum(in_use) - 1).astype(jnp.int32)
        ids = jnp.arange(N_EXPERTS, dtype=jnp.int32)
        later = jnp.where((ids[None, :] > ids[:, None]) & in_use[None, :], ids[None, :], N_EXPERTS)
        next_expert = jnp.min(later, axis=1)
        next_expert = jnp.where(next_expert == N_EXPERTS, -1, next_expert).astype(jnp.int32)
        ys = _experts(xs, block_e, n_used, next_expert, expert_ord, w1[i], w3[i], w2[i])
        h = _final(dest_flat, h1, meta, p[i].reshape(t, -1),
                   w_ple[i].astype(BF16), g_ple[i][None, :], g_plg[i][None, :], w_plg[i].astype(BF16), ys, tmf)
    return h.reshape(b, s, d)
```

```python
import functools
import math

import numpy as np
import jax
import jax.numpy as jnp
from jax import lax
from jax.experimental import pallas as pl
from jax.experimental.pallas import tpu as pltpu

F32 = jnp.float32
BF16 = jnp.bfloat16
U32 = jnp.uint32

LANES = 128
HA = 4
DA_QK = 64
DA_V = 128
HB = 16
DB_H = 32
GRID_W = 64
WIN_R = 8
WIN_C = 16
ROPE_THETA = 10000.0
EPS = 1e-6
NEG = -1e30
N_GROUPS = 4
EXPERTS_PER_GROUP = 8
N_EXPERTS = N_GROUPS * EXPERTS_PER_GROUP
TOP_K = 2
MOE_BLOCK = 256
E_LANE0 = N_GROUPS

VMEM_LIMIT = 56 * 1024 * 1024


def _cparams(sem):
    return pltpu.CompilerParams(dimension_semantics=sem, vmem_limit_bytes=VMEM_LIMIT)


def _rms_rows(x):
    return x * lax.rsqrt(jnp.mean(x * x, axis=-1, keepdims=True) + EPS)


SUBLANES = 8

TOKEN_ROWS = 4
HIGH_HALF = 0xFFFF0000


def _pack_bf16_pairs(x):
    m = x.shape[1] // 2
    bits = pltpu.bitcast(x.astype(BF16).astype(F32), U32)
    return (bits[:, :m] >> 16) | (bits[:, m:] & jnp.uint32(HIGH_HALF))


def _unpack_bf16_pairs(u):
    lo = pltpu.bitcast(u << 16, F32)
    hi = pltpu.bitcast(u & jnp.uint32(HIGH_HALF), F32)
    return jnp.concatenate([lo, hi], axis=1)


def _store_token_tiles(ref, val, first=0):
    n = val.shape[0]
    for c in range(TOKEN_ROWS):
        ref[pl.ds(first * TOKEN_ROWS + c, n, stride=TOKEN_ROWS), :] = val[:, c * LANES:(c + 1) * LANES]


def _load_token_tiles(ref, n, first=0):
    return jnp.concatenate(
        [ref[pl.ds(first * TOKEN_ROWS + c, n, stride=TOKEN_ROWS), :] for c in range(TOKEN_ROWS)], axis=1)


def _inproj_kernel(x_ref, gmix_ref, w_ref, bd64_ref, bd32_ref, gsec_ref, cos_ref, sin_ref, out_ref, qvt_ref):
    tm = x_ref.shape[0]
    a = (_rms_rows(x_ref[...]) * gmix_ref[...]).astype(BF16)
    lane = lax.broadcasted_iota(jnp.int32, (tm, LANES), 1)
    first_half = (lane % 64) < 32
    cos = cos_ref[...]
    sin = sin_ref[...]
    out_col = {1: 0, 3: 1, 4: 2, 5: 3}
    qvt_row = {0: 0, 2: 1}
    for sec in range(6):
        y = jnp.dot(a, w_ref[:, sec * 512:(sec + 1) * 512], preferred_element_type=F32)
        if sec in (0, 1, 3, 4):
            bd = bd64_ref if sec < 2 else bd32_ref
            gi = {0: 0, 1: 1, 3: 2, 4: 3}[sec]
            y2 = (y * y).astype(BF16)
            ms = jnp.concatenate([jnp.dot(y2[:, half * 256:(half + 1) * 256], bd[...], preferred_element_type=F32)
                                  for half in range(2)], axis=1)
            y = y * lax.rsqrt(ms + EPS) * gsec_ref[gi:gi + 1, :]
        for c in range(4):
            yc = y[:, c * LANES:(c + 1) * LANES]
            if sec < 2:
                rot = jnp.where(first_half, pltpu.roll(yc, 96, 1), pltpu.roll(yc, 32, 1))
                yc = yc * cos + rot * sin
            if sec in qvt_row:
                r0 = qvt_row[sec] * 512 + c * LANES
                qvt_ref[0, r0:r0 + LANES, :] = yc.T.astype(BF16)
            else:
                c0 = out_col[sec] * 512 + c * LANES
                out_ref[:, c0:c0 + LANES] = yc.astype(BF16)


def _inproj(x2, g_mix, w_bf, bd64, bd32, gsec, cos_t, sin_t, seq, tm):
    t, d = x2.shape
    n = w_bf.shape[1]
    nsb = seq // tm
    n_tok_major = 4 * 512
    n_feat_major = 2 * 512
    const = lambda i: (0, 0)
    return pl.pallas_call(
        _inproj_kernel,
        grid=(t // tm,),
        in_specs=[
            pl.BlockSpec((tm, d), lambda i: (i, 0)),
            pl.BlockSpec((1, d), const),
            pl.BlockSpec((d, n), const),
            pl.BlockSpec((256, 256), const),
            pl.BlockSpec((256, 256), const),
            pl.BlockSpec((4, 512), const),
            pl.BlockSpec((tm, LANES), lambda i: (i % nsb, 0)),
            pl.BlockSpec((tm, LANES), lambda i: (i % nsb, 0)),
        ],
        out_specs=[pl.BlockSpec((tm, n_tok_major), lambda i: (i, 0)),
                   pl.BlockSpec((1, n_feat_major, tm), lambda i: (i // nsb, 0, i % nsb))],
        out_shape=[jax.ShapeDtypeStruct((t, n_tok_major), BF16),
                   jax.ShapeDtypeStruct((t // seq, n_feat_major, seq), BF16)],
        compiler_params=_cparams(("parallel",)),
        name="inproj",
    )(x2, g_mix, w_bf, bd64, bd32, gsec, cos_t, sin_t)


QBLOCK_PAIRS_PER_BODY = 4

def _diffattn_kernel(lam_ref, gsub_ref, qt_ref, k_ref, vt_ref, o_ref, sa_scr, sb_scr, *, seq, tq, tk, lam_init):
    nq = seq // tq
    nk = seq // tk
    s1 = jnp.sum(lam_ref[0:1, :] * lam_ref[1:2, :], axis=1, keepdims=True)
    s2 = jnp.sum(lam_ref[2:3, :] * lam_ref[3:4, :], axis=1, keepdims=True)
    lam = jnp.exp(s1) - jnp.exp(s2) + lam_init

    row = lax.broadcasted_iota(jnp.int32, (LANES, tq), 0)

    def scores(qi, s_scr):
        qoff = pl.multiple_of(qi * tq, tq)
        qt = qt_ref[0, :, pl.ds(qoff, tq)]
        zero = jnp.zeros_like(qt)
        qm = jnp.concatenate([jnp.where(row < DA_QK, qt, zero), jnp.where(row >= DA_QK, qt, zero)], axis=1)
        mx = None
        for j in range(nk):
            st = jnp.dot(k_ref[0, j * tk:(j + 1) * tk, :], qm, preferred_element_type=F32)
            s_scr[j * tk:(j + 1) * tk, :] = st
            cm = jnp.max(st, axis=0, keepdims=True)
            mx = cm if mx is None else jnp.maximum(mx, cm)
        return mx

    def finish(qi, s_scr, m):
        l = jnp.zeros((1, 2 * tq), F32)
        acc = jnp.zeros((DA_V, 2 * tq), F32)
        for j in range(nk):
            p = jnp.exp2(s_scr[j * tk:(j + 1) * tk, :] - m)
            l = l + jnp.sum(p, axis=0, keepdims=True)
            acc = acc + jnp.dot(vt_ref[0, :, j * tk:(j + 1) * tk], p.astype(BF16), preferred_element_type=F32)
        o = acc * (1.0 / l)
        ot = o[:, :tq] - lam * o[:, tq:]
        ot = ot * lax.rsqrt(jnp.mean(ot * ot, axis=0, keepdims=True) + EPS) * (1.0 - lam_init)
        qoff = pl.multiple_of(qi * tq, tq)
        o_ref[0, pl.ds(qoff, tq), :] = (ot.T * gsub_ref[...]).astype(BF16)

    def body(i, m_a):
        for pair in range(QBLOCK_PAIRS_PER_BODY):
            q0 = 2 * (QBLOCK_PAIRS_PER_BODY * i + pair)
            m_b = scores(q0 + 1, sb_scr)
            finish(q0, sa_scr, m_a)
            m_a = scores(jnp.minimum(q0 + 2, nq - 1), sa_scr)
            finish(q0 + 1, sb_scr, m_b)
        return m_a

    lax.fori_loop(0, nq // (2 * QBLOCK_PAIRS_PER_BODY), body, scores(0, sa_scr))


def _diffattn(proj3, qvt, lam4, g_sub, lam_init, tq=128, tk=512):
    b, seq, _ = proj3.shape
    kern = functools.partial(_diffattn_kernel, seq=seq, tq=tq, tk=tk, lam_init=lam_init)
    return pl.pallas_call(
        kern,
        grid=(b, HA),
        in_specs=[
            pl.BlockSpec((4, DA_QK), lambda i, h: (0, 0)),
            pl.BlockSpec((1, DA_V), lambda i, h: (0, 0)),
            pl.BlockSpec((1, LANES, seq), lambda i, h: (i, h, 0)),
            pl.BlockSpec((1, seq, LANES), lambda i, h: (i, 0, h)),
            pl.BlockSpec((1, LANES, seq), lambda i, h: (i, HA + h, 0)),
        ],
        out_specs=pl.BlockSpec((1, seq, LANES), lambda i, h: (i, 0, h)),
        out_shape=jax.ShapeDtypeStruct((b, seq, HA * DA_V), BF16),
        scratch_shapes=[pltpu.VMEM((seq, 2 * tq), F32), pltpu.VMEM((seq, 2 * tq), F32)],
        compiler_params=_cparams(("parallel", "parallel")),
        name="diffattn",
    )(lam4, g_sub, qvt, proj3, qvt)


def _natten_kernel(bias_ref, q_ref, k_ref, v_ref, o_ref, *, rows):
    kr = min(WIN_R, rows)
    nkeys = kr * GRID_W
    hpg = LANES // DB_H
    lane_head = lax.broadcasted_iota(jnp.int32, (GRID_W, LANES), 1) // DB_H

    def row_step(r, carry):
        rs = jnp.clip(r - kr // 2, 0, rows - kr)
        base = rs - r + (WIN_R - 1)
        qoff = pl.multiple_of(r * GRID_W, GRID_W)
        koff = pl.multiple_of(rs * GRID_W, GRID_W)
        qr = q_ref[0, pl.ds(qoff, GRID_W), :]
        zero = jnp.zeros_like(qr)
        qs = jnp.concatenate([jnp.where(lane_head == h, qr, zero) for h in range(hpg)], axis=0)
        kw = k_ref[0, pl.ds(koff, nkeys), :]
        vw = v_ref[0, pl.ds(koff, nkeys), :]
        s = lax.dot_general(qs, kw, (((1,), (1,)), ((), ())), preferred_element_type=F32)
        bias = jnp.concatenate([bias_ref[0, base + 2 * i] for i in range(kr // 2)], axis=1)
        s = s + bias
        m = jnp.max(s, axis=1, keepdims=True)
        p = jnp.exp(s - m).astype(BF16)
        pv = jnp.dot(p, jnp.concatenate([vw, jnp.ones_like(vw)], axis=1), preferred_element_type=F32)
        o = pv[:, :LANES] * (1.0 / pv[:, LANES:])
        out = jnp.zeros((GRID_W, LANES), F32)
        for h in range(hpg):
            out = out + jnp.where(lane_head == h, o[h * GRID_W:(h + 1) * GRID_W, :], 0.0)
        o_ref[0, pl.ds(qoff, GRID_W), :] = out.astype(BF16)
        return carry

    lax.fori_loop(0, rows, row_step, 0, unroll=min(rows, 32))


def _natten(proj3, bias_tab):
    b, seq, _ = proj3.shape
    rows = seq // GRID_W
    ng = HB * DB_H // LANES
    nro = bias_tab.shape[1]
    kern = functools.partial(_natten_kernel, rows=rows)
    cb = HA
    return pl.pallas_call(
        kern,
        grid=(b, ng),
        in_specs=[
            pl.BlockSpec((1, nro, bias_tab.shape[2], LANES), lambda i, g: (g, 0, 0, 0)),
            pl.BlockSpec((1, seq, LANES), lambda i, g: (i, 0, cb + g)),
            pl.BlockSpec((1, seq, LANES), lambda i, g: (i, 0, cb + ng + g)),
            pl.BlockSpec((1, seq, LANES), lambda i, g: (i, 0, cb + 2 * ng + g)),
        ],
        out_specs=pl.BlockSpec((1, seq, LANES), lambda i, g: (i, 0, g)),
        out_shape=jax.ShapeDtypeStruct((b, seq, HB * DB_H), BF16),
        compiler_params=_cparams(("parallel", "parallel")),
        name="natten",
    )(bias_tab, proj3, proj3, proj3)


def _natten_bias_table(rpb):
    q = np.arange(GRID_W)[:, None]
    kc = np.arange(GRID_W)[None, :]
    cs = np.clip(q - WIN_C // 2, 0, GRID_W - WIN_C)
    valid = (kc >= cs) & (kc < cs + WIN_C)
    col_off = np.clip(kc - q + (WIN_C - 1), 0, 2 * WIN_C - 2)
    ncol = 2 * WIN_C - 1
    nro = 2 * WIN_R - 2
    pick = np.zeros((2, ncol, GRID_W, 2 * GRID_W), np.float32)
    for j in range(2):
        pick[j, :, :, j * GRID_W:(j + 1) * GRID_W] = (np.arange(ncol)[:, None, None] == col_off[None]) & valid[None]
    negmask = np.tile(np.where(valid, 0.0, NEG).astype(np.float32), (1, 2))
    hpg = LANES // DB_H
    r4 = rpb.astype(F32).reshape(HB // hpg, hpg, 2 * WIN_R - 1, ncol)
    rp = jnp.stack([r4[:, :, :nro], r4[:, :, 1:]], axis=0)
    tab = jnp.einsum('jghrc,jcqk->grhqk', rp, jnp.asarray(pick), precision=lax.Precision.HIGHEST) + negmask
    return tab.reshape(HB // hpg, nro, hpg * GRID_W, LANES)


def _outproj_kernel(x_ref, oa_ref, ob_ref, wo_ref, gffn_ref, wr_ref, tri_ref,
                    h_ref, xn_ref, meta_ref, meta_t_ref, cnt_ref, run_scr):
    tm = x_ref.shape[0]
    da = oa_ref.shape[1]

    @pl.when(pl.program_id(0) == 0)
    def _():
        run_scr[...] = jnp.zeros_like(run_scr)

    attn = jnp.dot(oa_ref[...], wo_ref[:da, :], preferred_element_type=F32)
    attn = attn + jnp.dot(ob_ref[...], wo_ref[da:, :], preferred_element_type=F32)
    h = x_ref[...] + attn
    h_ref[...] = h
    xn = _rms_rows(h) * gffn_ref[...]
    _store_token_tiles(xn_ref, _pack_bf16_pairs(xn))

    xh = xn.astype(BF16)
    xl = (xn - xh.astype(F32)).astype(BF16)
    hh_hl = jnp.dot(xh, wr_ref[...], preferred_element_type=F32)
    lg = hh_hl[:, :LANES] + hh_hl[:, LANES:] + jnp.dot(xl, wr_ref[:, :LANES], preferred_element_type=F32)
    lane = lax.broadcasted_iota(jnp.int32, (tm, LANES), 1).astype(F32)
    ninf = -jnp.inf
    is_g = lane < N_GROUPS
    gl = jnp.where(is_g, lg, ninf)
    gmax = jnp.max(gl, axis=1, keepdims=True)
    gidx = jnp.min(jnp.where(gl == gmax, lane, float(LANES)), axis=1, keepdims=True)
    gsum = jnp.sum(jnp.where(is_g, jnp.exp(gl - gmax), 0.0), axis=1, keepdims=True)
    g_gate = 1.0 / gsum
    lo = E_LANE0 + gidx * EXPERTS_PER_GROUP
    is_e = (lane >= lo) & (lane < lo + EXPERTS_PER_GROUP)
    el = jnp.where(is_e, lg, ninf)
    l1 = jnp.max(el, axis=1, keepdims=True)
    i1 = jnp.min(jnp.where(el == l1, lane, float(LANES)), axis=1, keepdims=True)
    el2 = jnp.where(lane == i1, ninf, el)
    l2 = jnp.max(el2, axis=1, keepdims=True)
    i2 = jnp.min(jnp.where(el2 == l2, lane, float(LANES)), axis=1, keepdims=True)
    r = jnp.exp(l2 - l1)
    w1 = g_gate / (1.0 + r)
    w2 = w1 * r

    sel1 = lane == i1
    sel2 = lane == i2
    oh = jnp.where(sel1 | sel2, 1.0, 0.0)
    before = jnp.dot(tri_ref[...], oh.astype(BF16), preferred_element_type=F32) + run_scr[...]
    rank1 = jnp.sum(jnp.where(sel1, before, 0.0), axis=1, keepdims=True)
    rank2 = jnp.sum(jnp.where(sel2, before, 0.0), axis=1, keepdims=True)
    run = run_scr[...] + jnp.sum(oh, axis=0, keepdims=True)
    run_scr[...] = run
    cnt_ref[...] = run

    meta = jnp.where(lane == 0, i1 - E_LANE0, 0.0)
    meta = jnp.where(lane == 1, i2 - E_LANE0, meta)
    meta = jnp.where(lane == 2, rank1, meta)
    meta = jnp.where(lane == 3, rank2, meta)
    meta = jnp.where(lane == 4, w1, meta)
    meta = jnp.where(lane == 5, w2, meta)
    meta_ref[...] = meta
    meta_t_ref[...] = meta.T[:SUBLANES, :]


def _outproj(x2, oa2, ob2, wo_bf, g_ffn, w_router, tri, tm):
    t, d = x2.shape
    da = oa2.shape[1]
    const = lambda i: (0, 0)
    row = lambda i: (i, 0)
    return pl.pallas_call(
        _outproj_kernel,
        grid=(t // tm,),
        in_specs=[
            pl.BlockSpec((tm, d), row),
            pl.BlockSpec((tm, da), row),
            pl.BlockSpec((tm, d - da), row),
            pl.BlockSpec((d, d), const),
            pl.BlockSpec((1, d), const),
            pl.BlockSpec((d, 2 * LANES), const),
            pl.BlockSpec((tm, tm), const),
        ],
        out_specs=[
            pl.BlockSpec((tm, d), row),
            pl.BlockSpec((tm * TOKEN_ROWS, LANES), row),
            pl.BlockSpec((tm, LANES), row),
            pl.BlockSpec((SUBLANES, tm), lambda i: (0, i)),
            pl.BlockSpec((1, LANES), const),
        ],
        out_shape=[
            jax.ShapeDtypeStruct((t, d), F32),
            jax.ShapeDtypeStruct((t * TOKEN_ROWS, LANES), U32),
            jax.ShapeDtypeStruct((t, LANES), F32),
            jax.ShapeDtypeStruct((SUBLANES, t), F32),
            jax.ShapeDtypeStruct((1, LANES), F32),
        ],
        scratch_shapes=[pltpu.VMEM((1, LANES), F32)],
        compiler_params=_cparams(("arbitrary",)),
        name="outproj_router",
    )(x2, oa2, ob2, wo_bf, g_ffn, w_router, tri)


DMA_ISSUE_UNROLL = 8
DMA_PRIORITIES = 2


def _dispatch_kernel(dest_ref, pend_ref, nu_ref, xn_ref, xs_ref, src_ref, zbuf, ibuf, sem, zsem):
    tm = xn_ref.shape[0] // TOKEN_ROWS
    t_total = pl.num_programs(0) * tm
    base = pl.program_id(0) * tm
    n_blocks = xs_ref.shape[0] // (MOE_BLOCK * TOKEN_ROWS)

    @pl.when(pl.program_id(0) == 0)
    def _():
        ibuf[...] = jnp.full(ibuf.shape, -1, jnp.int32)
        fill = pltpu.make_async_copy(ibuf, src_ref, zsem)
        fill.start()
        fill.wait()
        zbuf[...] = jnp.zeros_like(zbuf)

        def zero_block(blk):
            off = pl.multiple_of(blk * (MOE_BLOCK * TOKEN_ROWS), MOE_BLOCK * TOKEN_ROWS)
            return pltpu.make_async_copy(zbuf, xs_ref.at[pl.ds(off, MOE_BLOCK * TOKEN_ROWS), :], zsem)

        def for_each_zeroed_block(fn):
            def per_expert(e, c):
                end = pend_ref[e]
                start = jnp.where(e == 0, 0, pend_ref[jnp.maximum(e - 1, 0)])

                @pl.when(end > start)
                def _():
                    fn(zero_block(end // MOE_BLOCK - 1))
                return c
            lax.fori_loop(0, N_EXPERTS, per_expert, 0)

            def per_tail_block(blk, c):
                fn(zero_block(blk))
                return c
            lax.fori_loop(nu_ref[0], n_blocks, per_tail_block, 0)

        for_each_zeroed_block(lambda cp: cp.start())
        for_each_zeroed_block(lambda cp: cp.wait())

    def issue(group, c):
        toks = [group * DMA_ISSUE_UNROLL + g for g in range(DMA_ISSUE_UNROLL)]
        dests = [[dest_ref[slot * t_total + base + r] for slot in range(TOP_K)] for r in toks]
        for r, ds in zip(toks, dests):
            src = xn_ref.at[pl.ds(pl.multiple_of(r * TOKEN_ROWS, TOKEN_ROWS), TOKEN_ROWS), :]
            for slot, d in enumerate(ds):
                dst = xs_ref.at[pl.ds(pl.multiple_of(d * TOKEN_ROWS, TOKEN_ROWS), TOKEN_ROWS), :]
                pltpu.make_async_copy(src, dst, sem).start(priority=slot % DMA_PRIORITIES)
        for r, ds in zip(toks, dests):
            for slot, d in enumerate(ds):
                src_ref[d] = slot * t_total + base + r
        return c
    lax.fori_loop(0, tm // DMA_ISSUE_UNROLL, issue, 0)

    for slot in range(TOP_K):
        pltpu.make_async_copy(xn_ref, xs_ref.at[pl.ds(0, tm * TOKEN_ROWS), :], sem).wait()


def _dispatch(xn_tiles, dest_flat, pad_end, n_used, p_len, tm):
    t = xn_tiles.shape[0] // TOKEN_ROWS
    grid_spec = pltpu.PrefetchScalarGridSpec(
        num_scalar_prefetch=3,
        grid=(t // tm,),
        in_specs=[pl.BlockSpec((tm * TOKEN_ROWS, LANES), lambda i, dest, pend, nu: (i, 0))],
        out_specs=[pl.BlockSpec(memory_space=pl.ANY), pl.BlockSpec(memory_space=pltpu.SMEM)],
        scratch_shapes=[pltpu.VMEM((MOE_BLOCK * TOKEN_ROWS, LANES), U32),
                        pltpu.VMEM((p_len,), jnp.int32),
                        pltpu.SemaphoreType.DMA(()), pltpu.SemaphoreType.DMA(())],
    )
    return pl.pallas_call(
        _dispatch_kernel,
        grid_spec=grid_spec,
        out_shape=[jax.ShapeDtypeStruct((p_len * TOKEN_ROWS, LANES), U32),
                   jax.ShapeDtypeStruct((p_len,), jnp.int32)],
        compiler_params=_cparams(("arbitrary",)),
        name="dispatch",
    )(dest_flat, pad_end, n_used, xn_tiles)


def _experts_kernel(be_ref, nu_ref, nxt_ref, ord_ref, src_ref, xs_ref, w1_hbm, w3_hbm, w2_hbm, y2_ref,
                    w1_scr, w3_scr, w2_scr, w1_buf, w3_buf, w2_buf, ybuf, wsems, ysem):
    step = pl.program_id(0)
    first = 2 * step
    second = first + 1
    n_used = nu_ref[0]
    cur = step % 2
    pair_rows = 2 * MOE_BLOCK
    n_slot_rows = y2_ref.shape[0] // TOKEN_ROWS - pair_rows

    def send_previous(unrolled):
        prev_base = jnp.maximum(step - 1, 0) * pair_rows

        def send(r, priority):
            s = src_ref[prev_base + r]
            row = jnp.where((step > 0) & (s >= 0), s, n_slot_rows + r)
            src = ybuf.at[1 - cur, pl.ds(pl.multiple_of(r * TOKEN_ROWS, TOKEN_ROWS), TOKEN_ROWS), :]
            dst = y2_ref.at[pl.ds(pl.multiple_of(row * TOKEN_ROWS, TOKEN_ROWS), TOKEN_ROWS), :]
            pltpu.make_async_copy(src, dst, ysem).start(priority=priority)

        if unrolled:
            for r in range(pair_rows):
                send(r, r % DMA_PRIORITIES)
        else:
            def send_group(j, c):
                for k in range(DMA_PRIORITIES):
                    send(j * DMA_PRIORITIES + k, k)
                return c
            lax.fori_loop(0, pair_rows // DMA_PRIORITIES, send_group, 0, unroll=DMA_ISSUE_UNROLL // DMA_PRIORITIES)

    @pl.when(step == 0)
    def _():
        ybuf[1] = jnp.zeros(ybuf.shape[1:], U32)

    def weight_copies(e, slot):
        return (pltpu.make_async_copy(w1_hbm.at[e], w1_buf.at[slot], wsems.at[slot, 0]),
                pltpu.make_async_copy(w3_hbm.at[e], w3_buf.at[slot], wsems.at[slot, 1]),
                pltpu.make_async_copy(w2_hbm.at[e], w2_buf.at[slot], wsems.at[slot, 2]))

    def switch_to(e):
        slot = ord_ref[e] % 2
        for cp in weight_copies(e, slot):
            cp.wait()
        nxt = nxt_ref[e]

        @pl.when(nxt >= 0)
        def _():
            for cp in weight_copies(nxt, 1 - slot):
                cp.start()
        w1_scr[...] = w1_buf[slot].astype(BF16)
        w3_scr[...] = w3_buf[slot].astype(BF16)
        w2_scr[...] = w2_buf[slot].astype(BF16)

    def gated_mlp(row0, n_rows):
        x = _unpack_bf16_pairs(_load_token_tiles(xs_ref, n_rows, row0)).astype(BF16)
        a = jnp.dot(x, w1_scr[...], preferred_element_type=F32)
        b = jnp.dot(x, w3_scr[...], preferred_element_type=F32)
        hdn = a * (1.0 / (1.0 + jnp.exp(-a))) * b
        y = jnp.dot(hdn.astype(BF16), w2_scr[...], preferred_element_type=F32)
        _store_token_tiles(ybuf.at[cur], _pack_bf16_pairs(y), row0)

    def zero_rows(row0, n_rows):
        ybuf[cur, row0 * TOKEN_ROWS:(row0 + n_rows) * TOKEN_ROWS, :] = jnp.zeros((n_rows * TOKEN_ROWS, LANES), U32)

    @pl.when(first < n_used)
    def _():
        e_first = be_ref[first]
        e_before = be_ref[jnp.maximum(first - 1, 0)]

        @pl.when(first == 0)
        def _():
            for cp in weight_copies(e_first, ord_ref[e_first] % 2):
                cp.start()

        @pl.when((first == 0) | (e_first != e_before))
        def _():
            switch_to(e_first)

        second_used = second < n_used
        e_second = be_ref[second]
        same_expert = second_used & (e_second == e_first)

        @pl.when(same_expert)
        def _():
            send_previous(unrolled=True)
            gated_mlp(0, 2 * MOE_BLOCK)

        @pl.when(jnp.logical_not(same_expert))
        def _():
            send_previous(unrolled=True)
            gated_mlp(0, MOE_BLOCK)

            @pl.when(second_used)
            def _():
                switch_to(e_second)
                gated_mlp(MOE_BLOCK, MOE_BLOCK)

            @pl.when(jnp.logical_not(second_used))
            def _():
                zero_rows(MOE_BLOCK, MOE_BLOCK)

    @pl.when(first >= n_used)
    def _():
        send_previous(unrolled=False)
        zero_rows(0, 2 * MOE_BLOCK)

    pltpu.make_async_copy(ybuf.at[1 - cur], y2_ref.at[pl.ds(0, pair_rows * TOKEN_ROWS), :], ysem).wait()


def _experts(xs, row_src, block_e, n_used, next_expert, expert_ord, w1, w3, w2, n_slot_rows):
    p_len = xs.shape[0] // TOKEN_ROWS
    n_blocks = p_len // MOE_BLOCK
    assert n_blocks % 2 == 0
    n_pairs = n_blocks // 2
    d, de = w1.shape[1], w1.shape[2]
    blk = lambda i, be, nu, nxt, od, src: (jnp.maximum(jnp.minimum(i, (nu[0] - 1) // 2), 0), 0)
    hbm = pl.BlockSpec(memory_space=pl.ANY)
    grid_spec = pltpu.PrefetchScalarGridSpec(
        num_scalar_prefetch=5,
        grid=(n_pairs + 1,),
        in_specs=[pl.BlockSpec((2 * MOE_BLOCK * TOKEN_ROWS, LANES), blk), hbm, hbm, hbm],
        out_specs=hbm,
        scratch_shapes=[pltpu.VMEM((d, de), BF16), pltpu.VMEM((d, de), BF16), pltpu.VMEM((de, d), BF16),
                        pltpu.VMEM((2, d, de), F32), pltpu.VMEM((2, d, de), F32), pltpu.VMEM((2, de, d), F32),
                        pltpu.VMEM((2, 2 * MOE_BLOCK * TOKEN_ROWS, LANES), U32),
                        pltpu.SemaphoreType.DMA((2, 3)), pltpu.SemaphoreType.DMA(())],
    )
    return pl.pallas_call(
        _experts_kernel,
        grid_spec=grid_spec,
        out_shape=jax.ShapeDtypeStruct(((n_slot_rows + 2 * MOE_BLOCK) * TOKEN_ROWS, LANES), U32),
        compiler_params=_cparams(("arbitrary",)),
        name="experts",
    )(block_e, n_used, next_expert, expert_ord, row_src, xs, w1, w3, w2)


def _final_kernel(h_ref, meta_ref, p_ref, wple_ref, gple_ref, gplg_ref, wplg_ref, y_first_ref, y_second_ref, out_ref):
    tm = h_ref.shape[0]
    emb = jnp.dot(p_ref[...].astype(BF16), wple_ref[...], preferred_element_type=F32)
    emb = _rms_rows(emb) * gple_ref[...]
    meta = meta_ref[...]
    y1 = _unpack_bf16_pairs(_load_token_tiles(y_first_ref, tm))
    y2 = _unpack_bf16_pairs(_load_token_tiles(y_second_ref, tm))
    h = h_ref[...] + meta[:, 4:5] * y1 + meta[:, 5:6] * y2
    hn = (_rms_rows(h) * gplg_ref[...]).astype(BF16)
    z = jnp.dot(hn, wplg_ref[...], preferred_element_type=F32)
    out_ref[...] = h + emb * (1.0 / (1.0 + jnp.exp(-z)))


def _final(h2, meta, p2, wple_bf, g_ple, g_plg, wplg_bf, y_slots, tm):
    t, d = h2.shape
    dp = p2.shape[1]
    nt = t // tm
    const = lambda i: (0, 0)
    row = lambda i: (i, 0)
    return pl.pallas_call(
        _final_kernel,
        grid=(nt,),
        in_specs=[
            pl.BlockSpec((tm, d), row),
            pl.BlockSpec((tm, LANES), row),
            pl.BlockSpec((tm, dp), row),
            pl.BlockSpec((dp, d), const),
            pl.BlockSpec((1, d), const),
            pl.BlockSpec((1, d), const),
            pl.BlockSpec((d, d), const),
            pl.BlockSpec((tm * TOKEN_ROWS, LANES), row),
            pl.BlockSpec((tm * TOKEN_ROWS, LANES), lambda i: (nt + i, 0)),
        ],
        out_specs=pl.BlockSpec((tm, d), row),
        out_shape=jax.ShapeDtypeStruct((t, d), F32),
        compiler_params=_cparams(("parallel",)),
        name="combine_ple",
    )(h2, meta, p2, wple_bf, g_ple, g_plg, wplg_bf, y_slots, y_slots)


def _block_diag_mean(seg):
    m = np.kron(np.eye(256 // seg), np.full((seg, seg), 1.0 / seg))
    return jnp.asarray(m, BF16)


def _rope_tables(seq):
    f32 = np.float32
    inv = np.power(f32(ROPE_THETA), -np.arange(0, DA_QK, 2, dtype=f32) / f32(DA_QK)).astype(f32)
    ang = (np.arange(seq, dtype=f32)[:, None] * inv[None, :]).astype(f32)
    cos, sin = np.cos(ang).astype(f32), np.sin(ang).astype(f32)
    cos_t = np.tile(cos, (1, LANES // cos.shape[1]))
    sin_t = np.tile(np.concatenate([-sin, sin], axis=1), (1, LANES // (2 * sin.shape[1])))
    return jnp.asarray(cos_t), jnp.asarray(sin_t)


def _tile_for(n, pref):
    while n % pref:
        pref //= 2
    return pref


def kernel(x, p, g_mix, w_in, g_qa, g_ka, lam_q1, lam_k1, lam_q2, lam_k2, g_sub, g_qb, g_kb, rpb, w_out,
           g_ffn, w_rg, w_re, w1, w3, w2, g_plg, w_plg, w_ple, g_ple):
    b, s, d = x.shape
    assert d == 2 * TOKEN_ROWS * LANES, "token-tile layout packs one token into TOKEN_ROWS rows of 128 uint32"
    t = b * s
    depth = w_in.shape[0]
    tm = _tile_for(s, 512)
    tmd = _tile_for(t, 512)
    tmf = _tile_for(t, 512)
    n_slots = t * TOP_K
    n_blocks = n_slots // MOE_BLOCK + N_EXPERTS
    p_len = n_blocks * MOE_BLOCK

    bd64 = _block_diag_mean(DA_QK)
    bd32 = _block_diag_mean(DB_H)
    cos_t, sin_t = _rope_tables(s)
    tri = jnp.asarray(np.tril(np.ones((tm, tm)), -1), BF16)

    h = x.reshape(t, d)
    for i in range(depth):
        lam_init = 0.8 - 0.6 * math.exp(-0.3 * i)
        gsec = jnp.stack([
            jnp.tile(g_qa[i], 512 // DA_QK) * (DA_QK ** -0.5 * math.log2(math.e)),
            jnp.tile(g_ka[i], 512 // DA_QK),
            jnp.tile(g_qb[i], 512 // DB_H) * DB_H ** -0.5,
            jnp.tile(g_kb[i], 512 // DB_H),
        ]).astype(F32)
        proj, qvt = _inproj(h, g_mix[i][None, :], w_in[i].astype(BF16), bd64, bd32, gsec, cos_t, sin_t, s, tm)
        proj3 = proj.reshape(b, s, proj.shape[1])

        lam4 = jnp.stack([lam_q1[i], lam_k1[i], lam_q2[i], lam_k2[i]]).astype(F32)
        oa = _diffattn(proj3, qvt, lam4, g_sub[i][None, :].astype(F32), lam_init)
        ob = _natten(proj3, _natten_bias_table(rpb[i]))

        w_router = jnp.concatenate(
            [w_rg[i], w_re[i], jnp.zeros((d, LANES - N_GROUPS - N_EXPERTS), F32)], axis=1).astype(F32)
        wr_hi = w_router.astype(BF16)
        wr_lo = (w_router - wr_hi.astype(F32)).astype(BF16)
        h1, xn, meta, meta_t, cnt = _outproj(h, oa.reshape(t, -1), ob.reshape(t, -1), w_out[i].astype(BF16),
                                             g_ffn[i][None, :], jnp.concatenate([wr_hi, wr_lo], axis=1), tri, tm)

        counts = cnt[0, E_LANE0:E_LANE0 + N_EXPERTS].astype(jnp.int32)
        padded = (counts + MOE_BLOCK - 1) // MOE_BLOCK * MOE_BLOCK
        pad_end = jnp.cumsum(padded)
        pad_start = pad_end - padded
        mt = meta_t[0:2 * TOP_K].astype(jnp.int32)
        e_ids = jnp.arange(N_EXPERTS, dtype=jnp.int32)[None, :, None]
        start_of = jnp.sum(jnp.where(mt[:TOP_K, None, :] == e_ids, pad_start[None, :, None], 0), axis=1)
        dest = start_of + mt[TOP_K:]
        blk_start = jnp.arange(n_blocks, dtype=jnp.int32) * MOE_BLOCK
        block_e = jnp.minimum(jnp.sum(pad_end[None, :] <= blk_start[:, None], axis=1), N_EXPERTS - 1)
        block_e = block_e.astype(jnp.int32)
        n_used = (pad_end[-1:] // MOE_BLOCK).astype(jnp.int32)

        dest_flat = dest.reshape(TOP_K * t)

        xs, row_src = _dispatch(xn, dest_flat, pad_end.astype(jnp.int32), n_used, p_len, tmd)
        in_use = counts > 0
        expert_ord = (jnp.cumsum(in_use) - 1).astype(jnp.int32)
        ids = jnp.arange(N_EXPERTS, dtype=jnp.int32)
        later = jnp.where((ids[None, :] > ids[:, None]) & in_use[None, :], ids[None, :], N_EXPERTS)
        next_expert = jnp.min(later, axis=1)
        next_expert = jnp.where(next_expert == N_EXPERTS, -1, next_expert).astype(jnp.int32)
        y_slots = _experts(xs, row_src.reshape(p_len), block_e, n_used, next_expert, expert_ord,
                           w1[i], w3[i], w2[i], TOP_K * t)
        h = _final(h1, meta, p[i].reshape(t, -1), w_ple[i].astype(BF16), g_ple[i][None, :], g_plg[i][None, :],
                   w_plg[i].astype(BF16), y_slots, tmf)
    return h.reshape(b, s, d)
```

```python
import functools
import math

import numpy as np
import jax
import jax.numpy as jnp
from jax import lax
from jax.experimental import pallas as pl
from jax.experimental.pallas import tpu as pltpu

F32 = jnp.float32
BF16 = jnp.bfloat16
U32 = jnp.uint32

LANES = 128
HA = 4
DA_QK = 64
DA_V = 128
HB = 16
DB_H = 32
GRID_W = 64
WIN_R = 8
WIN_C = 16
ROPE_THETA = 10000.0
EPS = 1e-6
NEG = -1e30
N_GROUPS = 4
EXPERTS_PER_GROUP = 8
N_EXPERTS = N_GROUPS * EXPERTS_PER_GROUP
TOP_K = 2
MOE_BLOCK = 256
E_LANE0 = N_GROUPS

VMEM_LIMIT = 56 * 1024 * 1024


def _cparams(sem):
    return pltpu.CompilerParams(dimension_semantics=sem, vmem_limit_bytes=VMEM_LIMIT)


def _rms_rows(x):
    return x * lax.rsqrt(jnp.mean(x * x, axis=-1, keepdims=True) + EPS)


SUBLANES = 8

TOKEN_ROWS = 4
HIGH_HALF = 0xFFFF0000


def _pack_bf16_pairs(x):
    m = x.shape[1] // 2
    bits = pltpu.bitcast(x.astype(BF16).astype(F32), U32)
    return (bits[:, :m] >> 16) | (bits[:, m:] & jnp.uint32(HIGH_HALF))


def _unpack_bf16_pairs(u):
    lo = pltpu.bitcast(u << 16, F32)
    hi = pltpu.bitcast(u & jnp.uint32(HIGH_HALF), F32)
    return jnp.concatenate([lo, hi], axis=1)


def _store_token_tiles(ref, val, first=0):
    n = val.shape[0]
    for c in range(TOKEN_ROWS):
        ref[pl.ds(first * TOKEN_ROWS + c, n, stride=TOKEN_ROWS), :] = val[:, c * LANES:(c + 1) * LANES]


def _load_token_tiles(ref, n, first=0):
    return jnp.concatenate(
        [ref[pl.ds(first * TOKEN_ROWS + c, n, stride=TOKEN_ROWS), :] for c in range(TOKEN_ROWS)], axis=1)


def _inproj_kernel(x_ref, gmix_ref, w_ref, bd64_ref, bd32_ref, gsec_ref, cos_ref, sin_ref, out_ref, qvt_ref):
    tm = x_ref.shape[0]
    a = (_rms_rows(x_ref[...]) * gmix_ref[...]).astype(BF16)
    lane = lax.broadcasted_iota(jnp.int32, (tm, LANES), 1)
    first_half = (lane % 64) < 32
    cos = cos_ref[...]
    sin = sin_ref[...]
    out_col = {1: 0, 3: 1, 4: 2, 5: 3}
    qvt_row = {0: 0, 2: 1}
    for sec in range(6):
        y = jnp.dot(a, w_ref[:, sec * 512:(sec + 1) * 512], preferred_element_type=F32)
        if sec in (0, 1, 3, 4):
            bd = bd64_ref if sec < 2 else bd32_ref
            gi = {0: 0, 1: 1, 3: 2, 4: 3}[sec]
            y2 = (y * y).astype(BF16)
            ms = jnp.concatenate([jnp.dot(y2[:, half * 256:(half + 1) * 256], bd[...], preferred_element_type=F32)
                                  for half in range(2)], axis=1)
            y = y * lax.rsqrt(ms + EPS) * gsec_ref[gi:gi + 1, :]
        for c in range(4):
            yc = y[:, c * LANES:(c + 1) * LANES]
            if sec < 2:
                rot = jnp.where(first_half, pltpu.roll(yc, 96, 1), pltpu.roll(yc, 32, 1))
                yc = yc * cos + rot * sin
            if sec in qvt_row:
                r0 = qvt_row[sec] * 512 + c * LANES
                qvt_ref[0, r0:r0 + LANES, :] = yc.T.astype(BF16)
            else:
                c0 = out_col[sec] * 512 + c * LANES
                out_ref[:, c0:c0 + LANES] = yc.astype(BF16)


def _inproj(x2, g_mix, w_bf, bd64, bd32, gsec, cos_t, sin_t, seq, tm):
    t, d = x2.shape
    n = w_bf.shape[1]
    nsb = seq // tm
    n_tok_major = 4 * 512
    n_feat_major = 2 * 512
    const = lambda i: (0, 0)
    return pl.pallas_call(
        _inproj_kernel,
        grid=(t // tm,),
        in_specs=[
            pl.BlockSpec((tm, d), lambda i: (i, 0)),
            pl.BlockSpec((1, d), const),
            pl.BlockSpec((d, n), const),
            pl.BlockSpec((256, 256), const),
            pl.BlockSpec((256, 256), const),
            pl.BlockSpec((4, 512), const),
            pl.BlockSpec((tm, LANES), lambda i: (i % nsb, 0)),
            pl.BlockSpec((tm, LANES), lambda i: (i % nsb, 0)),
        ],
        out_specs=[pl.BlockSpec((tm, n_tok_major), lambda i: (i, 0)),
                   pl.BlockSpec((1, n_feat_major, tm), lambda i: (i // nsb, 0, i % nsb))],
        out_shape=[jax.ShapeDtypeStruct((t, n_tok_major), BF16),
                   jax.ShapeDtypeStruct((t // seq, n_feat_major, seq), BF16)],
        compiler_params=_cparams(("parallel",)),
        name="inproj",
    )(x2, g_mix, w_bf, bd64, bd32, gsec, cos_t, sin_t)


QBLOCK_PAIRS_PER_BODY = 4

def _diffattn_kernel(lam_ref, gsub_ref, qt_ref, k_ref, vt_ref, o_ref, sa_scr, sb_scr, *, seq, tq, tk, lam_init):
    nq = seq // tq
    nk = seq // tk
    s1 = jnp.sum(lam_ref[0:1, :] * lam_ref[1:2, :], axis=1, keepdims=True)
    s2 = jnp.sum(lam_ref[2:3, :] * lam_ref[3:4, :], axis=1, keepdims=True)
    lam = jnp.exp(s1) - jnp.exp(s2) + lam_init

    row = lax.broadcasted_iota(jnp.int32, (LANES, tq), 0)

    def scores(qi, s_scr):
        qoff = pl.multiple_of(qi * tq, tq)
        qt = qt_ref[0, :, pl.ds(qoff, tq)]
        zero = jnp.zeros_like(qt)
        qm = jnp.concatenate([jnp.where(row < DA_QK, qt, zero), jnp.where(row >= DA_QK, qt, zero)], axis=1)
        mx = None
        for j in range(nk):
            st = jnp.dot(k_ref[0, j * tk:(j + 1) * tk, :], qm, preferred_element_type=F32)
            s_scr[j * tk:(j + 1) * tk, :] = st
            cm = jnp.max(st, axis=0, keepdims=True)
            mx = cm if mx is None else jnp.maximum(mx, cm)
        return mx

    def finish(qi, s_scr, m):
        l = jnp.zeros((1, 2 * tq), F32)
        acc = jnp.zeros((DA_V, 2 * tq), F32)
        for j in range(nk):
            p = jnp.exp2(s_scr[j * tk:(j + 1) * tk, :] - m)
            l = l + jnp.sum(p, axis=0, keepdims=True)
            acc = acc + jnp.dot(vt_ref[0, :, j * tk:(j + 1) * tk], p.astype(BF16), preferred_element_type=F32)
        o = acc * (1.0 / l)
        ot = o[:, :tq] - lam * o[:, tq:]
        ot = ot * lax.rsqrt(jnp.mean(ot * ot, axis=0, keepdims=True) + EPS) * (1.0 - lam_init)
        qoff = pl.multiple_of(qi * tq, tq)
        o_ref[0, pl.ds(qoff, tq), :] = (ot.T * gsub_ref[...]).astype(BF16)

    def body(i, m_a):
        for pair in range(QBLOCK_PAIRS_PER_BODY):
            q0 = 2 * (QBLOCK_PAIRS_PER_BODY * i + pair)
            m_b = scores(q0 + 1, sb_scr)
            finish(q0, sa_scr, m_a)
            m_a = scores(jnp.minimum(q0 + 2, nq - 1), sa_scr)
            finish(q0 + 1, sb_scr, m_b)
        return m_a

    lax.fori_loop(0, nq // (2 * QBLOCK_PAIRS_PER_BODY), body, scores(0, sa_scr))


def _diffattn(proj3, qvt, lam4, g_sub, lam_init, tq=128, tk=512):
    b, seq, _ = proj3.shape
    kern = functools.partial(_diffattn_kernel, seq=seq, tq=tq, tk=tk, lam_init=lam_init)
    return pl.pallas_call(
        kern,
        grid=(b, HA),
        in_specs=[
            pl.BlockSpec((4, DA_QK), lambda i, h: (0, 0)),
            pl.BlockSpec((1, DA_V), lambda i, h: (0, 0)),
            pl.BlockSpec((1, LANES, seq), lambda i, h: (i, h, 0)),
            pl.BlockSpec((1, seq, LANES), lambda i, h: (i, 0, h)),
            pl.BlockSpec((1, LANES, seq), lambda i, h: (i, HA + h, 0)),
        ],
        out_specs=pl.BlockSpec((1, seq, LANES), lambda i, h: (i, 0, h)),
        out_shape=jax.ShapeDtypeStruct((b, seq, HA * DA_V), BF16),
        scratch_shapes=[pltpu.VMEM((seq, 2 * tq), F32), pltpu.VMEM((seq, 2 * tq), F32)],
        compiler_params=_cparams(("parallel", "parallel")),
        name="diffattn",
    )(lam4, g_sub, qvt, proj3, qvt)


def _natten_kernel(bias_ref, q_ref, k_ref, v_ref, o_ref, *, rows):
    kr = min(WIN_R, rows)
    nkeys = kr * GRID_W
    hpg = LANES // DB_H
    lane_head = lax.broadcasted_iota(jnp.int32, (GRID_W, LANES), 1) // DB_H

    def row_step(r, carry):
        rs = jnp.clip(r - kr // 2, 0, rows - kr)
        base = rs - r + (WIN_R - 1)
        qoff = pl.multiple_of(r * GRID_W, GRID_W)
        koff = pl.multiple_of(rs * GRID_W, GRID_W)
        qr = q_ref[0, pl.ds(qoff, GRID_W), :]
        zero = jnp.zeros_like(qr)
        qs = jnp.concatenate([jnp.where(lane_head == h, qr, zero) for h in range(hpg)], axis=0)
        kw = k_ref[0, pl.ds(koff, nkeys), :]
        vw = v_ref[0, pl.ds(koff, nkeys), :]
        s = lax.dot_general(qs, kw, (((1,), (1,)), ((), ())), preferred_element_type=F32)
        bias = jnp.concatenate([bias_ref[0, base + 2 * i] for i in range(kr // 2)], axis=1)
        s = s + bias
        m = jnp.max(s, axis=1, keepdims=True)
        p = jnp.exp(s - m).astype(BF16)
        pv = jnp.dot(p, jnp.concatenate([vw, jnp.ones_like(vw)], axis=1), preferred_element_type=F32)
        o = pv[:, :LANES] * (1.0 / pv[:, LANES:])
        out = jnp.zeros((GRID_W, LANES), F32)
        for h in range(hpg):
            out = out + jnp.where(lane_head == h, o[h * GRID_W:(h + 1) * GRID_W, :], 0.0)
        o_ref[0, pl.ds(qoff, GRID_W), :] = out.astype(BF16)
        return carry

    lax.fori_loop(0, rows, row_step, 0, unroll=min(rows, 32))


def _natten(proj3, bias_tab):
    b, seq, _ = proj3.shape
    rows = seq // GRID_W
    ng = HB * DB_H // LANES
    nro = bias_tab.shape[1]
    kern = functools.partial(_natten_kernel, rows=rows)
    cb = HA
    return pl.pallas_call(
        kern,
        grid=(b, ng),
        in_specs=[
            pl.BlockSpec((1, nro, bias_tab.shape[2], LANES), lambda i, g: (g, 0, 0, 0)),
            pl.BlockSpec((1, seq, LANES), lambda i, g: (i, 0, cb + g)),
            pl.BlockSpec((1, seq, LANES), lambda i, g: (i, 0, cb + ng + g)),
            pl.BlockSpec((1, seq, LANES), lambda i, g: (i, 0, cb + 2 * ng + g)),
        ],
        out_specs=pl.BlockSpec((1, seq, LANES), lambda i, g: (i, 0, g)),
        out_shape=jax.ShapeDtypeStruct((b, seq, HB * DB_H), BF16),
        compiler_params=_cparams(("parallel", "parallel")),
        name="natten",
    )(bias_tab, proj3, proj3, proj3)


def _natten_bias_table(rpb):
    q = np.arange(GRID_W)[:, None]
    kc = np.arange(GRID_W)[None, :]
    cs = np.clip(q - WIN_C // 2, 0, GRID_W - WIN_C)
    valid = (kc >= cs) & (kc < cs + WIN_C)
    col_off = np.clip(kc - q + (WIN_C - 1), 0, 2 * WIN_C - 2)
    ncol = 2 * WIN_C - 1
    nro = 2 * WIN_R - 2
    pick = np.zeros((2, ncol, GRID_W, 2 * GRID_W), np.float32)
    for j in range(2):
        pick[j, :, :, j * GRID_W:(j + 1) * GRID_W] = (np.arange(ncol)[:, None, None] == col_off[None]) & valid[None]
    negmask = np.tile(np.where(valid, 0.0, NEG).astype(np.float32), (1, 2))
    hpg = LANES // DB_H
    r4 = rpb.astype(F32).reshape(HB // hpg, hpg, 2 * WIN_R - 1, ncol)
    rp = jnp.stack([r4[:, :, :nro], r4[:, :, 1:]], axis=0)
    tab = jnp.einsum('jghrc,jcqk->grhqk', rp, jnp.asarray(pick), precision=lax.Precision.HIGHEST) + negmask
    return tab.reshape(HB // hpg, nro, hpg * GRID_W, LANES)


def _outproj_kernel(x_ref, oa_ref, ob_ref, wo_ref, gffn_ref, wr_ref, tri_ref,
                    h_ref, xn_ref, meta_ref, meta_t_ref, cnt_ref, run_scr):
    tm = x_ref.shape[0]
    da = oa_ref.shape[1]

    @pl.when(pl.program_id(0) == 0)
    def _():
        run_scr[...] = jnp.zeros_like(run_scr)

    attn = jnp.dot(oa_ref[...], wo_ref[:da, :], preferred_element_type=F32)
    attn = attn + jnp.dot(ob_ref[...], wo_ref[da:, :], preferred_element_type=F32)
    h = x_ref[...] + attn
    h_ref[...] = h
    xn = _rms_rows(h) * gffn_ref[...]
    _store_token_tiles(xn_ref, _pack_bf16_pairs(xn))

    xh = xn.astype(BF16)
    xl = (xn - xh.astype(F32)).astype(BF16)
    hh_hl = jnp.dot(xh, wr_ref[...], preferred_element_type=F32)
    lg = hh_hl[:, :LANES] + hh_hl[:, LANES:] + jnp.dot(xl, wr_ref[:, :LANES], preferred_element_type=F32)
    lane = lax.broadcasted_iota(jnp.int32, (tm, LANES), 1).astype(F32)
    ninf = -jnp.inf
    is_g = lane < N_GROUPS
    gl = jnp.where(is_g, lg, ninf)
    gmax = jnp.max(gl, axis=1, keepdims=True)
    gidx = jnp.min(jnp.where(gl == gmax, lane, float(LANES)), axis=1, keepdims=True)
    gsum = jnp.sum(jnp.where(is_g, jnp.exp(gl - gmax), 0.0), axis=1, keepdims=True)
    g_gate = 1.0 / gsum
    lo = E_LANE0 + gidx * EXPERTS_PER_GROUP
    is_e = (lane >= lo) & (lane < lo + EXPERTS_PER_GROUP)
    el = jnp.where(is_e, lg, ninf)
    l1 = jnp.max(el, axis=1, keepdims=True)
    i1 = jnp.min(jnp.where(el == l1, lane, float(LANES)), axis=1, keepdims=True)
    el2 = jnp.where(lane == i1, ninf, el)
    l2 = jnp.max(el2, axis=1, keepdims=True)
    i2 = jnp.min(jnp.where(el2 == l2, lane, float(LANES)), axis=1, keepdims=True)
    r = jnp.exp(l2 - l1)
    w1 = g_gate / (1.0 + r)
    w2 = w1 * r

    sel1 = lane == i1
    sel2 = lane == i2
    oh = jnp.where(sel1 | sel2, 1.0, 0.0)
    before = jnp.dot(tri_ref[...], oh.astype(BF16), preferred_element_type=F32) + run_scr[...]
    rank1 = jnp.sum(jnp.where(sel1, before, 0.0), axis=1, keepdims=True)
    rank2 = jnp.sum(jnp.where(sel2, before, 0.0), axis=1, keepdims=True)
    run = run_scr[...] + jnp.sum(oh, axis=0, keepdims=True)
    run_scr[...] = run
    cnt_ref[...] = run

    meta = jnp.where(lane == 0, i1 - E_LANE0, 0.0)
    meta = jnp.where(lane == 1, i2 - E_LANE0, meta)
    meta = jnp.where(lane == 2, rank1, meta)
    meta = jnp.where(lane == 3, rank2, meta)
    meta = jnp.where(lane == 4, w1, meta)
    meta = jnp.where(lane == 5, w2, meta)
    meta_ref[...] = meta
    meta_t_ref[...] = meta.T[:SUBLANES, :]


def _outproj(x2, oa2, ob2, wo_bf, g_ffn, w_router, tri, tm):
    t, d = x2.shape
    da = oa2.shape[1]
    const = lambda i: (0, 0)
    row = lambda i: (i, 0)
    return pl.pallas_call(
        _outproj_kernel,
        grid=(t // tm,),
        in_specs=[
            pl.BlockSpec((tm, d), row),
            pl.BlockSpec((tm, da), row),
            pl.BlockSpec((tm, d - da), row),
            pl.BlockSpec((d, d), const),
            pl.BlockSpec((1, d), const),
            pl.BlockSpec((d, 2 * LANES), const),
            pl.BlockSpec((tm, tm), const),
        ],
        out_specs=[
            pl.BlockSpec((tm, d), row),
            pl.BlockSpec((tm * TOKEN_ROWS, LANES), row),
            pl.BlockSpec((tm, LANES), row),
            pl.BlockSpec((SUBLANES, tm), lambda i: (0, i)),
            pl.BlockSpec((1, LANES), const),
        ],
        out_shape=[
            jax.ShapeDtypeStruct((t, d), F32),
            jax.ShapeDtypeStruct((t * TOKEN_ROWS, LANES), U32),
            jax.ShapeDtypeStruct((t, LANES), F32),
            jax.ShapeDtypeStruct((SUBLANES, t), F32),
            jax.ShapeDtypeStruct((1, LANES), F32),
        ],
        scratch_shapes=[pltpu.VMEM((1, LANES), F32)],
        compiler_params=_cparams(("arbitrary",)),
        name="outproj_router",
    )(x2, oa2, ob2, wo_bf, g_ffn, w_router, tri)


DMA_ISSUE_UNROLL = 8
DMA_PRIORITIES = 2


def _dispatch_kernel(dest_ref, pend_ref, nu_ref, xn_ref, xs_ref, src_ref, zbuf, ibuf, sem, zsem):
    tm = xn_ref.shape[0] // TOKEN_ROWS
    t_total = pl.num_programs(0) * tm
    base = pl.program_id(0) * tm
    n_blocks = xs_ref.shape[0] // (MOE_BLOCK * TOKEN_ROWS)

    @pl.when(pl.program_id(0) == 0)
    def _():
        ibuf[...] = jnp.full(ibuf.shape, -1, jnp.int32)
        fill = pltpu.make_async_copy(ibuf, src_ref, zsem)
        fill.start()
        fill.wait()
        zbuf[...] = jnp.zeros_like(zbuf)

        def zero_block(blk):
            off = pl.multiple_of(blk * (MOE_BLOCK * TOKEN_ROWS), MOE_BLOCK * TOKEN_ROWS)
            return pltpu.make_async_copy(zbuf, xs_ref.at[pl.ds(off, MOE_BLOCK * TOKEN_ROWS), :], zsem)

        def for_each_zeroed_block(fn):
            def per_expert(e, c):
                end = pend_ref[e]
                start = jnp.where(e == 0, 0, pend_ref[jnp.maximum(e - 1, 0)])

                @pl.when(end > start)
                def _():
                    fn(zero_block(end // MOE_BLOCK - 1))
                return c
            lax.fori_loop(0, N_EXPERTS, per_expert, 0)

            def per_tail_block(blk, c):
                fn(zero_block(blk))
                return c
            lax.fori_loop(nu_ref[0], n_blocks, per_tail_block, 0)

        for_each_zeroed_block(lambda cp: cp.start())
        for_each_zeroed_block(lambda cp: cp.wait())

    def issue(group, c):
        toks = [group * DMA_ISSUE_UNROLL + g for g in range(DMA_ISSUE_UNROLL)]
        dests = [[dest_ref[slot * t_total + base + r] for slot in range(TOP_K)] for r in toks]
        for r, ds in zip(toks, dests):
            src = xn_ref.at[pl.ds(pl.multiple_of(r * TOKEN_ROWS, TOKEN_ROWS), TOKEN_ROWS), :]
            for slot, d in enumerate(ds):
                dst = xs_ref.at[pl.ds(pl.multiple_of(d * TOKEN_ROWS, TOKEN_ROWS), TOKEN_ROWS), :]
                pltpu.make_async_copy(src, dst, sem).start(priority=slot % DMA_PRIORITIES)
        for r, ds in zip(toks, dests):
            for slot, d in enumerate(ds):
                src_ref[d] = slot * t_total + base + r
        return c
    lax.fori_loop(0, tm // DMA_ISSUE_UNROLL, issue, 0)

    for slot in range(TOP_K):
        pltpu.make_async_copy(xn_ref, xs_ref.at[pl.ds(0, tm * TOKEN_ROWS), :], sem).wait()


def _dispatch(xn_tiles, dest_flat, pad_end, n_used, p_len, tm):
    t = xn_tiles.shape[0] // TOKEN_ROWS
    grid_spec = pltpu.PrefetchScalarGridSpec(
        num_scalar_prefetch=3,
        grid=(t // tm,),
        in_specs=[pl.BlockSpec((tm * TOKEN_ROWS, LANES), lambda i, dest, pend, nu: (i, 0))],
        out_specs=[pl.BlockSpec(memory_space=pl.ANY), pl.BlockSpec(memory_space=pltpu.SMEM)],
        scratch_shapes=[pltpu.VMEM((MOE_BLOCK * TOKEN_ROWS, LANES), U32),
                        pltpu.VMEM((p_len,), jnp.int32),
                        pltpu.SemaphoreType.DMA(()), pltpu.SemaphoreType.DMA(())],
    )
    return pl.pallas_call(
        _dispatch_kernel,
        grid_spec=grid_spec,
        out_shape=[jax.ShapeDtypeStruct((p_len * TOKEN_ROWS, LANES), U32),
                   jax.ShapeDtypeStruct((p_len,), jnp.int32)],
        compiler_params=_cparams(("arbitrary",)),
        name="dispatch",
    )(dest_flat, pad_end, n_used, xn_tiles)


def _experts_kernel(be_ref, nu_ref, nxt_ref, ord_ref, src_ref, xs_ref, w1_hbm, w3_hbm, w2_hbm, y2_ref,
                    w1_scr, w3_scr, w2_scr, w1_buf, w3_buf, w2_buf, ybuf, wsems, ysems):
    step = pl.program_id(0)
    first = 2 * step
    second = first + 1
    n_used = nu_ref[0]
    cur = step % 2
    pair_rows = 2 * MOE_BLOCK
    n_slot_rows = y2_ref.shape[0] // TOKEN_ROWS - 2 * pair_rows

    def send_previous(unrolled):
        prev_base = jnp.maximum(step - 1, 0) * pair_rows

        def send(r, priority):
            s = src_ref[prev_base + r]
            row = jnp.where((step > 0) & (s >= 0), s, n_slot_rows + cur * pair_rows + r)
            src = ybuf.at[1 - cur, pl.ds(pl.multiple_of(r * TOKEN_ROWS, TOKEN_ROWS), TOKEN_ROWS), :]
            dst = y2_ref.at[pl.ds(pl.multiple_of(row * TOKEN_ROWS, TOKEN_ROWS), TOKEN_ROWS), :]
            pltpu.make_async_copy(src, dst, ysems.at[cur]).start(priority=priority)

        if unrolled:
            for r in range(pair_rows):
                send(r, r % DMA_PRIORITIES)
        else:
            def send_group(j, c):
                for k in range(DMA_PRIORITIES):
                    send(j * DMA_PRIORITIES + k, k)
                return c
            lax.fori_loop(0, pair_rows // DMA_PRIORITIES, send_group, 0, unroll=DMA_ISSUE_UNROLL // DMA_PRIORITIES)

    def all_rows_sent(buf, sem):
        return pltpu.make_async_copy(ybuf.at[buf], y2_ref.at[pl.ds(0, pair_rows * TOKEN_ROWS), :], sem)

    def wait_older_sends():
        all_rows_sent(cur, ysems.at[1 - cur]).wait()

    @pl.when(step == 0)
    def _():
        ybuf[1] = jnp.zeros(ybuf.shape[1:], U32)
        odd_dump = (n_slot_rows + pair_rows) * TOKEN_ROWS
        pltpu.make_async_copy(ybuf.at[1], y2_ref.at[pl.ds(odd_dump, pair_rows * TOKEN_ROWS), :], ysems.at[1]).start()

    def weight_copies(e, slot):
        return (pltpu.make_async_copy(w1_hbm.at[e], w1_buf.at[slot], wsems.at[slot, 0]),
                pltpu.make_async_copy(w3_hbm.at[e], w3_buf.at[slot], wsems.at[slot, 1]),
                pltpu.make_async_copy(w2_hbm.at[e], w2_buf.at[slot], wsems.at[slot, 2]))

    def switch_to(e):
        slot = ord_ref[e] % 2
        for cp in weight_copies(e, slot):
            cp.wait()
        nxt = nxt_ref[e]

        @pl.when(nxt >= 0)
        def _():
            for cp in weight_copies(nxt, 1 - slot):
                cp.start()
        w1_scr[...] = w1_buf[slot].astype(BF16)
        w3_scr[...] = w3_buf[slot].astype(BF16)
        w2_scr[...] = w2_buf[slot].astype(BF16)

    def gated_mlp(row0, n_rows, first_store):
        x = _unpack_bf16_pairs(_load_token_tiles(xs_ref, n_rows, row0)).astype(BF16)
        a = jnp.dot(x, w1_scr[...], preferred_element_type=F32)
        b = jnp.dot(x, w3_scr[...], preferred_element_type=F32)
        hdn = a * (1.0 / (1.0 + jnp.exp(-a))) * b
        y = jnp.dot(hdn.astype(BF16), w2_scr[...], preferred_element_type=F32)
        if first_store:
            wait_older_sends()
        _store_token_tiles(ybuf.at[cur], _pack_bf16_pairs(y), row0)

    def zero_rows(row0, n_rows):
        ybuf[cur, row0 * TOKEN_ROWS:(row0 + n_rows) * TOKEN_ROWS, :] = jnp.zeros((n_rows * TOKEN_ROWS, LANES), U32)

    @pl.when(first < n_used)
    def _():
        e_first = be_ref[first]
        e_before = be_ref[jnp.maximum(first - 1, 0)]

        @pl.when(first == 0)
        def _():
            for cp in weight_copies(e_first, ord_ref[e_first] % 2):
                cp.start()

        @pl.when((first == 0) | (e_first != e_before))
        def _():
            switch_to(e_first)

        second_used = second < n_used
        e_second = be_ref[second]
        same_expert = second_used & (e_second == e_first)

        @pl.when(same_expert)
        def _():
            send_previous(unrolled=True)
            gated_mlp(0, 2 * MOE_BLOCK, first_store=True)

        @pl.when(jnp.logical_not(same_expert))
        def _():
            send_previous(unrolled=True)
            gated_mlp(0, MOE_BLOCK, first_store=True)

            @pl.when(second_used)
            def _():
                switch_to(e_second)
                gated_mlp(MOE_BLOCK, MOE_BLOCK, first_store=False)

            @pl.when(jnp.logical_not(second_used))
            def _():
                zero_rows(MOE_BLOCK, MOE_BLOCK)

    @pl.when(first >= n_used)
    def _():
        send_previous(unrolled=False)
        wait_older_sends()
        zero_rows(0, 2 * MOE_BLOCK)

    @pl.when(step == pl.num_programs(0) - 1)
    def _():
        all_rows_sent(1 - cur, ysems.at[cur]).wait()


def _experts(xs, row_src, block_e, n_used, next_expert, expert_ord, w1, w3, w2, n_slot_rows):
    p_len = xs.shape[0] // TOKEN_ROWS
    n_blocks = p_len // MOE_BLOCK
    assert n_blocks % 2 == 0
    n_pairs = n_blocks // 2
    d, de = w1.shape[1], w1.shape[2]
    blk = lambda i, be, nu, nxt, od, src: (jnp.maximum(jnp.minimum(i, (nu[0] - 1) // 2), 0), 0)
    hbm = pl.BlockSpec(memory_space=pl.ANY)
    grid_spec = pltpu.PrefetchScalarGridSpec(
        num_scalar_prefetch=5,
        grid=(n_pairs + 1,),
        in_specs=[pl.BlockSpec((2 * MOE_BLOCK * TOKEN_ROWS, LANES), blk), hbm, hbm, hbm],
        out_specs=hbm,
        scratch_shapes=[pltpu.VMEM((d, de), BF16), pltpu.VMEM((d, de), BF16), pltpu.VMEM((de, d), BF16),
                        pltpu.VMEM((2, d, de), F32), pltpu.VMEM((2, d, de), F32), pltpu.VMEM((2, de, d), F32),
                        pltpu.VMEM((2, 2 * MOE_BLOCK * TOKEN_ROWS, LANES), U32),
                        pltpu.SemaphoreType.DMA((2, 3)), pltpu.SemaphoreType.DMA((2,))],
    )
    return pl.pallas_call(
        _experts_kernel,
        grid_spec=grid_spec,
        out_shape=jax.ShapeDtypeStruct(((n_slot_rows + 4 * MOE_BLOCK) * TOKEN_ROWS, LANES), U32),
        compiler_params=_cparams(("arbitrary",)),
        name="experts",
    )(block_e, n_used, next_expert, expert_ord, row_src, xs, w1, w3, w2)


def _final_kernel(h_ref, meta_ref, p_ref, wple_ref, gple_ref, gplg_ref, wplg_ref, y_first_ref, y_second_ref, out_ref):
    tm = h_ref.shape[0]
    emb = jnp.dot(p_ref[...].astype(BF16), wple_ref[...], preferred_element_type=F32)
    emb = _rms_rows(emb) * gple_ref[...]
    meta = meta_ref[...]
    y1 = _unpack_bf16_pairs(_load_token_tiles(y_first_ref, tm))
    y2 = _unpack_bf16_pairs(_load_token_tiles(y_second_ref, tm))
    h = h_ref[...] + meta[:, 4:5] * y1 + meta[:, 5:6] * y2
    hn = (_rms_rows(h) * gplg_ref[...]).astype(BF16)
    z = jnp.dot(hn, wplg_ref[...], preferred_element_type=F32)
    out_ref[...] = h + emb * (1.0 / (1.0 + jnp.exp(-z)))


def _final(h2, meta, p2, wple_bf, g_ple, g_plg, wplg_bf, y_slots, tm):
    t, d = h2.shape
    dp = p2.shape[1]
    nt = t // tm
    const = lambda i: (0, 0)
    row = lambda i: (i, 0)
    return pl.pallas_call(
        _final_kernel,
        grid=(nt,),
        in_specs=[
            pl.BlockSpec((tm, d), row),
            pl.BlockSpec((tm, LANES), row),
            pl.BlockSpec((tm, dp), row),
            pl.BlockSpec((dp, d), const),
            pl.BlockSpec((1, d), const),
            pl.BlockSpec((1, d), const),
            pl.BlockSpec((d, d), const),
            pl.BlockSpec((tm * TOKEN_ROWS, LANES), row),
            pl.BlockSpec((tm * TOKEN_ROWS, LANES), lambda i: (nt + i, 0)),
        ],
        out_specs=pl.BlockSpec((tm, d), row),
        out_shape=jax.ShapeDtypeStruct((t, d), F32),
        compiler_params=_cparams(("parallel",)),
        name="combine_ple",
    )(h2, meta, p2, wple_bf, g_ple, g_plg, wplg_bf, y_slots, y_slots)


def _block_diag_mean(seg):
    m = np.kron(np.eye(256 // seg), np.full((seg, seg), 1.0 / seg))
    return jnp.asarray(m, BF16)


def _rope_tables(seq):
    f32 = np.float32
    inv = np.power(f32(ROPE_THETA), -np.arange(0, DA_QK, 2, dtype=f32) / f32(DA_QK)).astype(f32)
    ang = (np.arange(seq, dtype=f32)[:, None] * inv[None, :]).astype(f32)
    cos, sin = np.cos(ang).astype(f32), np.sin(ang).astype(f32)
    cos_t = np.tile(cos, (1, LANES // cos.shape[1]))
    sin_t = np.tile(np.concatenate([-sin, sin], axis=1), (1, LANES // (2 * sin.shape[1])))
    return jnp.asarray(cos_t), jnp.asarray(sin_t)


def _tile_for(n, pref):
    while n % pref:
        pref //= 2
    return pref


def kernel(x, p, g_mix, w_in, g_qa, g_ka, lam_q1, lam_k1, lam_q2, lam_k2, g_sub, g_qb, g_kb, rpb, w_out,
           g_ffn, w_rg, w_re, w1, w3, w2, g_plg, w_plg, w_ple, g_ple):
    b, s, d = x.shape
    assert d == 2 * TOKEN_ROWS * LANES, "token-tile layout packs one token into TOKEN_ROWS rows of 128 uint32"
    t = b * s
    depth = w_in.shape[0]
    tm = _tile_for(s, 512)
    tmd = _tile_for(t, 512)
    tmf = _tile_for(t, 512)
    n_slots = t * TOP_K
    n_blocks = n_slots // MOE_BLOCK + N_EXPERTS
    p_len = n_blocks * MOE_BLOCK

    bd64 = _block_diag_mean(DA_QK)
    bd32 = _block_diag_mean(DB_H)
    cos_t, sin_t = _rope_tables(s)
    tri = jnp.asarray(np.tril(np.ones((tm, tm)), -1), BF16)

    h = x.reshape(t, d)
    for i in range(depth):
        lam_init = 0.8 - 0.6 * math.exp(-0.3 * i)
        gsec = jnp.stack([
            jnp.tile(g_qa[i], 512 // DA_QK) * (DA_QK ** -0.5 * math.log2(math.e)),
            jnp.tile(g_ka[i], 512 // DA_QK),
            jnp.tile(g_qb[i], 512 // DB_H) * DB_H ** -0.5,
            jnp.tile(g_kb[i], 512 // DB_H),
        ]).astype(F32)
        proj, qvt = _inproj(h, g_mix[i][None, :], w_in[i].astype(BF16), bd64, bd32, gsec, cos_t, sin_t, s, tm)
        proj3 = proj.reshape(b, s, proj.shape[1])

        lam4 = jnp.stack([lam_q1[i], lam_k1[i], lam_q2[i], lam_k2[i]]).astype(F32)
        oa = _diffattn(proj3, qvt, lam4, g_sub[i][None, :].astype(F32), lam_init)
        ob = _natten(proj3, _natten_bias_table(rpb[i]))

        w_router = jnp.concatenate(
            [w_rg[i], w_re[i], jnp.zeros((d, LANES - N_GROUPS - N_EXPERTS), F32)], axis=1).astype(F32)
        wr_hi = w_router.astype(BF16)
        wr_lo = (w_router - wr_hi.astype(F32)).astype(BF16)
        h1, xn, meta, meta_t, cnt = _outproj(h, oa.reshape(t, -1), ob.reshape(t, -1), w_out[i].astype(BF16),
                                             g_ffn[i][None, :], jnp.concatenate([wr_hi, wr_lo], axis=1), tri, tm)

        counts = cnt[0, E_LANE0:E_LANE0 + N_EXPERTS].astype(jnp.int32)
        padded = (counts + MOE_BLOCK - 1) // MOE_BLOCK * MOE_BLOCK
        pad_end = jnp.cumsum(padded)
        pad_start = pad_end - padded
        mt = meta_t[0:2 * TOP_K].astype(jnp.int32)
        e_ids = jnp.arange(N_EXPERTS, dtype=jnp.int32)[None, :, None]
        start_of = jnp.sum(jnp.where(mt[:TOP_K, None, :] == e_ids, pad_start[None, :, None], 0), axis=1)
        dest = start_of + mt[TOP_K:]
        blk_start = jnp.arange(n_blocks, dtype=jnp.int32) * MOE_BLOCK
        block_e = jnp.minimum(jnp.sum(pad_end[None, :] <= blk_start[:, None], axis=1), N_EXPERTS - 1)
        block_e = block_e.astype(jnp.int32)
        n_used = (pad_end[-1:] // MOE_BLOCK).astype(jnp.int32)

        dest_flat = dest.reshape(TOP_K * t)

        xs, row_src = _dispatch(xn, dest_flat, pad_end.astype(jnp.int32), n_used, p_len, tmd)
        in_use = counts > 0
        expert_ord = (jnp.cumsum(in_use) - 1).astype(jnp.int32)
        ids = jnp.arange(N_EXPERTS, dtype=jnp.int32)
        later = jnp.where((ids[None, :] > ids[:, None]) & in_use[None, :], ids[None, :], N_EXPERTS)
        next_expert = jnp.min(later, axis=1)
        next_expert = jnp.where(next_expert == N_EXPERTS, -1, next_expert).astype(jnp.int32)
        y_slots = _experts(xs, row_src.reshape(p_len), block_e, n_used, next_expert, expert_ord,
                           w1[i], w3[i], w2[i], TOP_K * t)
        h = _final(h1, meta, p[i].reshape(t, -1), w_ple[i].astype(BF16), g_ple[i][None, :], g_plg[i][None, :],
                   w_plg[i].astype(BF16), y_slots, tmf)
    return h.reshape(b, s, d)
```

```python
import functools
import math

import numpy as np
import jax
import jax.numpy as jnp
from jax import lax
from jax.experimental import pallas as pl
from jax.experimental.pallas import tpu as pltpu

F32 = jnp.float32
BF16 = jnp.bfloat16
U32 = jnp.uint32

LANES = 128
HA = 4
DA_QK = 64
DA_V = 128
HB = 16
DB_H = 32
GRID_W = 64
WIN_R = 8
WIN_C = 16
ROPE_THETA = 10000.0
EPS = 1e-6
NEG = -1e30
N_GROUPS = 4
EXPERTS_PER_GROUP = 8
N_EXPERTS = N_GROUPS * EXPERTS_PER_GROUP
TOP_K = 2
MOE_BLOCK = 256
E_LANE0 = N_GROUPS

VMEM_LIMIT = 56 * 1024 * 1024


def _cparams(sem):
    return pltpu.CompilerParams(dimension_semantics=sem, vmem_limit_bytes=VMEM_LIMIT)


def _rms_rows(x):
    return x * lax.rsqrt(jnp.mean(x * x, axis=-1, keepdims=True) + EPS)


SUBLANES = 8

TOKEN_ROWS = 4
HIGH_HALF = 0xFFFF0000


def _pack_bf16_pairs(x):
    m = x.shape[1] // 2
    bits = pltpu.bitcast(x.astype(BF16).astype(F32), U32)
    return (bits[:, :m] >> 16) | (bits[:, m:] & jnp.uint32(HIGH_HALF))


def _unpack_bf16_pairs(u):
    lo = pltpu.bitcast(u << 16, F32)
    hi = pltpu.bitcast(u & jnp.uint32(HIGH_HALF), F32)
    return jnp.concatenate([lo, hi], axis=1)


def _store_token_tiles(ref, val, first=0):
    n = val.shape[0]
    for c in range(TOKEN_ROWS):
        ref[pl.ds(first * TOKEN_ROWS + c, n, stride=TOKEN_ROWS), :] = val[:, c * LANES:(c + 1) * LANES]


def _load_token_tiles(ref, n, first=0):
    return jnp.concatenate(
        [ref[pl.ds(first * TOKEN_ROWS + c, n, stride=TOKEN_ROWS), :] for c in range(TOKEN_ROWS)], axis=1)


def _inproj_kernel(x_ref, gmix_ref, w_ref, bd64_ref, bd32_ref, gsec_ref, cos_ref, sin_ref, out_ref, qvt_ref, w_scr):
    tm = x_ref.shape[0]

    @pl.when(pl.program_id(0) == 0)
    def _():
        w_scr[...] = w_ref[...].astype(BF16)

    a = (_rms_rows(x_ref[...]) * gmix_ref[...]).astype(BF16)
    lane = lax.broadcasted_iota(jnp.int32, (tm, LANES), 1)
    first_half = (lane % 64) < 32
    cos = cos_ref[...]
    sin = sin_ref[...]
    out_col = {1: 0, 3: 1, 4: 2, 5: 3}
    qvt_row = {0: 0, 2: 1}
    for sec in range(6):
        y = jnp.dot(a, w_scr[:, sec * 512:(sec + 1) * 512], preferred_element_type=F32)
        if sec in (0, 1, 3, 4):
            bd = bd64_ref if sec < 2 else bd32_ref
            gi = {0: 0, 1: 1, 3: 2, 4: 3}[sec]
            y2 = (y * y).astype(BF16)
            ms = jnp.concatenate([jnp.dot(y2[:, half * 256:(half + 1) * 256], bd[...], preferred_element_type=F32)
                                  for half in range(2)], axis=1)
            y = y * lax.rsqrt(ms + EPS) * gsec_ref[gi:gi + 1, :]
        for c in range(4):
            yc = y[:, c * LANES:(c + 1) * LANES]
            if sec < 2:
                rot = jnp.where(first_half, pltpu.roll(yc, 96, 1), pltpu.roll(yc, 32, 1))
                yc = yc * cos + rot * sin
            if sec in qvt_row:
                r0 = qvt_row[sec] * 512 + c * LANES
                qvt_ref[0, r0:r0 + LANES, :] = yc.T.astype(BF16)
            else:
                c0 = out_col[sec] * 512 + c * LANES
                out_ref[:, c0:c0 + LANES] = yc.astype(BF16)


def _inproj(x2, g_mix, w_f32, bd64, bd32, gsec, cos_t, sin_t, seq, tm):
    t, d = x2.shape
    n = w_f32.shape[1]
    nsb = seq // tm
    n_tok_major = 4 * 512
    n_feat_major = 2 * 512
    const = lambda i: (0, 0)
    return pl.pallas_call(
        _inproj_kernel,
        grid=(t // tm,),
        in_specs=[
            pl.BlockSpec((tm, d), lambda i: (i, 0)),
            pl.BlockSpec((1, d), const),
            pl.BlockSpec((d, n), const, pipeline_mode=pl.Buffered(1)),
            pl.BlockSpec((256, 256), const),
            pl.BlockSpec((256, 256), const),
            pl.BlockSpec((4, 512), const),
            pl.BlockSpec((tm, LANES), lambda i: (i % nsb, 0)),
            pl.BlockSpec((tm, LANES), lambda i: (i % nsb, 0)),
        ],
        out_specs=[pl.BlockSpec((tm, n_tok_major), lambda i: (i, 0)),
                   pl.BlockSpec((1, n_feat_major, tm), lambda i: (i // nsb, 0, i % nsb))],
        out_shape=[jax.ShapeDtypeStruct((t, n_tok_major), BF16),
                   jax.ShapeDtypeStruct((t // seq, n_feat_major, seq), BF16)],
        scratch_shapes=[pltpu.VMEM((d, n), BF16)],
        compiler_params=_cparams(("arbitrary",)),
        name="inproj",
    )(x2, g_mix, w_f32, bd64, bd32, gsec, cos_t, sin_t)


QBLOCK_PAIRS_PER_BODY = 4

def _diffattn_kernel(lam_ref, gsub_ref, qt_ref, k_ref, vt_ref, o_ref, sa_scr, sb_scr, *, seq, tq, tk, lam_init):
    nq = seq // tq
    nk = seq // tk
    s1 = jnp.sum(lam_ref[0:1, :] * lam_ref[1:2, :], axis=1, keepdims=True)
    s2 = jnp.sum(lam_ref[2:3, :] * lam_ref[3:4, :], axis=1, keepdims=True)
    lam = jnp.exp(s1) - jnp.exp(s2) + lam_init

    row = lax.broadcasted_iota(jnp.int32, (LANES, tq), 0)

    def scores(qi, s_scr):
        qoff = pl.multiple_of(qi * tq, tq)
        qt = qt_ref[0, :, pl.ds(qoff, tq)]
        zero = jnp.zeros_like(qt)
        qm = jnp.concatenate([jnp.where(row < DA_QK, qt, zero), jnp.where(row >= DA_QK, qt, zero)], axis=1)
        mx = None
        for j in range(nk):
            st = jnp.dot(k_ref[0, j * tk:(j + 1) * tk, :], qm, preferred_element_type=F32)
            s_scr[j * tk:(j + 1) * tk, :] = st
            cm = jnp.max(st, axis=0, keepdims=True)
            mx = cm if mx is None else jnp.maximum(mx, cm)
        return mx

    def finish(qi, s_scr, m):
        l = jnp.zeros((1, 2 * tq), F32)
        acc = jnp.zeros((DA_V, 2 * tq), F32)
        for j in range(nk):
            p = jnp.exp2(s_scr[j * tk:(j + 1) * tk, :] - m)
            l = l + jnp.sum(p, axis=0, keepdims=True)
            acc = acc + jnp.dot(vt_ref[0, :, j * tk:(j + 1) * tk], p.astype(BF16), preferred_element_type=F32)
        o = acc * (1.0 / l)
        ot = o[:, :tq] - lam * o[:, tq:]
        ot = ot * lax.rsqrt(jnp.mean(ot * ot, axis=0, keepdims=True) + EPS) * (1.0 - lam_init)
        qoff = pl.multiple_of(qi * tq, tq)
        o_ref[0, pl.ds(qoff, tq), :] = (ot.T * gsub_ref[...]).astype(BF16)

    def body(i, m_a):
        for pair in range(QBLOCK_PAIRS_PER_BODY):
            q0 = 2 * (QBLOCK_PAIRS_PER_BODY * i + pair)
            m_b = scores(q0 + 1, sb_scr)
            finish(q0, sa_scr, m_a)
            m_a = scores(jnp.minimum(q0 + 2, nq - 1), sa_scr)
            finish(q0 + 1, sb_scr, m_b)
        return m_a

    lax.fori_loop(0, nq // (2 * QBLOCK_PAIRS_PER_BODY), body, scores(0, sa_scr))


def _diffattn(proj3, qvt, lam4, g_sub, lam_init, tq=128, tk=512):
    b, seq, _ = proj3.shape
    kern = functools.partial(_diffattn_kernel, seq=seq, tq=tq, tk=tk, lam_init=lam_init)
    return pl.pallas_call(
        kern,
        grid=(b, HA),
        in_specs=[
            pl.BlockSpec((4, DA_QK), lambda i, h: (0, 0)),
            pl.BlockSpec((1, DA_V), lambda i, h: (0, 0)),
            pl.BlockSpec((1, LANES, seq), lambda i, h: (i, h, 0)),
            pl.BlockSpec((1, seq, LANES), lambda i, h: (i, 0, h)),
            pl.BlockSpec((1, LANES, seq), lambda i, h: (i, HA + h, 0)),
        ],
        out_specs=pl.BlockSpec((1, seq, LANES), lambda i, h: (i, 0, h)),
        out_shape=jax.ShapeDtypeStruct((b, seq, HA * DA_V), BF16),
        scratch_shapes=[pltpu.VMEM((seq, 2 * tq), F32), pltpu.VMEM((seq, 2 * tq), F32)],
        compiler_params=_cparams(("parallel", "parallel")),
        name="diffattn",
    )(lam4, g_sub, qvt, proj3, qvt)


def _natten_kernel(bias_ref, q_ref, k_ref, v_ref, o_ref, *, rows):
    kr = min(WIN_R, rows)
    nkeys = kr * GRID_W
    hpg = LANES // DB_H
    lane_head = lax.broadcasted_iota(jnp.int32, (GRID_W, LANES), 1) // DB_H

    def row_step(r, carry):
        rs = jnp.clip(r - kr // 2, 0, rows - kr)
        base = rs - r + (WIN_R - 1)
        qoff = pl.multiple_of(r * GRID_W, GRID_W)
        koff = pl.multiple_of(rs * GRID_W, GRID_W)
        qr = q_ref[0, pl.ds(qoff, GRID_W), :]
        zero = jnp.zeros_like(qr)
        qs = jnp.concatenate([jnp.where(lane_head == h, qr, zero) for h in range(hpg)], axis=0)
        kw = k_ref[0, pl.ds(koff, nkeys), :]
        vw = v_ref[0, pl.ds(koff, nkeys), :]
        s = lax.dot_general(qs, kw, (((1,), (1,)), ((), ())), preferred_element_type=F32)
        bias = jnp.concatenate([bias_ref[0, base + 2 * i] for i in range(kr // 2)], axis=1)
        s = s + bias
        m = jnp.max(s, axis=1, keepdims=True)
        p = jnp.exp(s - m).astype(BF16)
        pv = jnp.dot(p, jnp.concatenate([vw, jnp.ones_like(vw)], axis=1), preferred_element_type=F32)
        o = pv[:, :LANES] * (1.0 / pv[:, LANES:])
        out = jnp.zeros((GRID_W, LANES), F32)
        for h in range(hpg):
            out = out + jnp.where(lane_head == h, o[h * GRID_W:(h + 1) * GRID_W, :], 0.0)
        o_ref[0, pl.ds(qoff, GRID_W), :] = out.astype(BF16)
        return carry

    lax.fori_loop(0, rows, row_step, 0, unroll=min(rows, 64))


def _natten(proj3, bias_tab):
    b, seq, _ = proj3.shape
    rows = seq // GRID_W
    ng = HB * DB_H // LANES
    nro = bias_tab.shape[1]
    kern = functools.partial(_natten_kernel, rows=rows)
    cb = HA
    return pl.pallas_call(
        kern,
        grid=(b, ng),
        in_specs=[
            pl.BlockSpec((1, nro, bias_tab.shape[2], LANES), lambda i, g: (g, 0, 0, 0)),
            pl.BlockSpec((1, seq, LANES), lambda i, g: (i, 0, cb + g)),
            pl.BlockSpec((1, seq, LANES), lambda i, g: (i, 0, cb + ng + g)),
            pl.BlockSpec((1, seq, LANES), lambda i, g: (i, 0, cb + 2 * ng + g)),
        ],
        out_specs=pl.BlockSpec((1, seq, LANES), lambda i, g: (i, 0, g)),
        out_shape=jax.ShapeDtypeStruct((b, seq, HB * DB_H), BF16),
        compiler_params=_cparams(("parallel", "parallel")),
        name="natten",
    )(bias_tab, proj3, proj3, proj3)


def _natten_bias_table(rpb):
    q = np.arange(GRID_W)[:, None]
    kc = np.arange(GRID_W)[None, :]
    cs = np.clip(q - WIN_C // 2, 0, GRID_W - WIN_C)
    valid = (kc >= cs) & (kc < cs + WIN_C)
    col_off = np.clip(kc - q + (WIN_C - 1), 0, 2 * WIN_C - 2)
    ncol = 2 * WIN_C - 1
    nro = 2 * WIN_R - 2
    pick = np.zeros((2, ncol, GRID_W, 2 * GRID_W), np.float32)
    for j in range(2):
        pick[j, :, :, j * GRID_W:(j + 1) * GRID_W] = (np.arange(ncol)[:, None, None] == col_off[None]) & valid[None]
    negmask = np.tile(np.where(valid, 0.0, NEG).astype(np.float32), (1, 2))
    hpg = LANES // DB_H
    r4 = rpb.astype(F32).reshape(HB // hpg, hpg, 2 * WIN_R - 1, ncol)
    rp = jnp.stack([r4[:, :, :nro], r4[:, :, 1:]], axis=0)
    tab = jnp.einsum('jghrc,jcqk->grhqk', rp, jnp.asarray(pick), precision=lax.Precision.HIGHEST) + negmask
    return tab.reshape(HB // hpg, nro, hpg * GRID_W, LANES)


def _outproj_kernel(x_ref, oa_ref, ob_ref, wo_ref, gffn_ref, wr_ref, tri_ref,
                    h_ref, xn_ref, meta_ref, meta_t_ref, cnt_ref, run_scr):
    tm = x_ref.shape[0]
    da = oa_ref.shape[1]

    @pl.when(pl.program_id(0) == 0)
    def _():
        run_scr[...] = jnp.zeros_like(run_scr)

    attn = jnp.dot(oa_ref[...], wo_ref[:da, :], preferred_element_type=F32)
    attn = attn + jnp.dot(ob_ref[...], wo_ref[da:, :], preferred_element_type=F32)
    h = x_ref[...] + attn
    h_ref[...] = h
    xn = _rms_rows(h) * gffn_ref[...]
    _store_token_tiles(xn_ref, _pack_bf16_pairs(xn))

    xh = xn.astype(BF16)
    xl = (xn - xh.astype(F32)).astype(BF16)
    hh_hl = jnp.dot(xh, wr_ref[...], preferred_element_type=F32)
    lg = hh_hl[:, :LANES] + hh_hl[:, LANES:] + jnp.dot(xl, wr_ref[:, :LANES], preferred_element_type=F32)
    lane = lax.broadcasted_iota(jnp.int32, (tm, LANES), 1).astype(F32)
    ninf = -jnp.inf
    is_g = lane < N_GROUPS
    gl = jnp.where(is_g, lg, ninf)
    gmax = jnp.max(gl, axis=1, keepdims=True)
    gidx = jnp.min(jnp.where(gl == gmax, lane, float(LANES)), axis=1, keepdims=True)
    gsum = jnp.sum(jnp.where(is_g, jnp.exp(gl - gmax), 0.0), axis=1, keepdims=True)
    g_gate = 1.0 / gsum
    lo = E_LANE0 + gidx * EXPERTS_PER_GROUP
    is_e = (lane >= lo) & (lane < lo + EXPERTS_PER_GROUP)
    el = jnp.where(is_e, lg, ninf)
    l1 = jnp.max(el, axis=1, keepdims=True)
    i1 = jnp.min(jnp.where(el == l1, lane, float(LANES)), axis=1, keepdims=True)
    el2 = jnp.where(lane == i1, ninf, el)
    l2 = jnp.max(el2, axis=1, keepdims=True)
    i2 = jnp.min(jnp.where(el2 == l2, lane, float(LANES)), axis=1, keepdims=True)
    r = jnp.exp(l2 - l1)
    w1 = g_gate / (1.0 + r)
    w2 = w1 * r

    sel1 = lane == i1
    sel2 = lane == i2
    oh = jnp.where(sel1 | sel2, 1.0, 0.0)
    before = jnp.dot(tri_ref[...], oh.astype(BF16), preferred_element_type=F32) + run_scr[...]
    rank1 = jnp.sum(jnp.where(sel1, before, 0.0), axis=1, keepdims=True)
    rank2 = jnp.sum(jnp.where(sel2, before, 0.0), axis=1, keepdims=True)
    run = run_scr[...] + jnp.sum(oh, axis=0, keepdims=True)
    run_scr[...] = run
    cnt_ref[...] = run

    meta = jnp.where(lane == 0, i1 - E_LANE0, 0.0)
    meta = jnp.where(lane == 1, i2 - E_LANE0, meta)
    meta = jnp.where(lane == 2, rank1, meta)
    meta = jnp.where(lane == 3, rank2, meta)
    meta = jnp.where(lane == 4, w1, meta)
    meta = jnp.where(lane == 5, w2, meta)
    meta_ref[...] = meta
    meta_t_ref[...] = meta.T[:SUBLANES, :]


def _outproj(x2, oa2, ob2, wo_bf, g_ffn, w_router, tri, tm):
    t, d = x2.shape
    da = oa2.shape[1]
    const = lambda i: (0, 0)
    row = lambda i: (i, 0)
    return pl.pallas_call(
        _outproj_kernel,
        grid=(t // tm,),
        in_specs=[
            pl.BlockSpec((tm, d), row),
            pl.BlockSpec((tm, da), row),
            pl.BlockSpec((tm, d - da), row),
            pl.BlockSpec((d, d), const),
            pl.BlockSpec((1, d), const),
            pl.BlockSpec((d, 2 * LANES), const),
            pl.BlockSpec((tm, tm), const),
        ],
        out_specs=[
            pl.BlockSpec((tm, d), row),
            pl.BlockSpec((tm * TOKEN_ROWS, LANES), row),
            pl.BlockSpec((tm, LANES), row),
            pl.BlockSpec((SUBLANES, tm), lambda i: (0, i)),
            pl.BlockSpec((1, LANES), const),
        ],
        out_shape=[
            jax.ShapeDtypeStruct((t, d), F32),
            jax.ShapeDtypeStruct((t * TOKEN_ROWS, LANES), U32),
            jax.ShapeDtypeStruct((t, LANES), F32),
            jax.ShapeDtypeStruct((SUBLANES, t), F32),
            jax.ShapeDtypeStruct((1, LANES), F32),
        ],
        scratch_shapes=[pltpu.VMEM((1, LANES), F32)],
        compiler_params=_cparams(("arbitrary",)),
        name="outproj_router",
    )(x2, oa2, ob2, wo_bf, g_ffn, w_router, tri)


DMA_ISSUE_UNROLL = 8
DMA_PRIORITIES = 2


def _dispatch_kernel(dest_ref, pend_ref, nu_ref, xn_ref, xs_ref, src_ref, zbuf, ibuf, sem, zsem):
    tm = xn_ref.shape[0] // TOKEN_ROWS
    t_total = pl.num_programs(0) * tm
    base = pl.program_id(0) * tm
    n_blocks = xs_ref.shape[0] // (MOE_BLOCK * TOKEN_ROWS)

    @pl.when(pl.program_id(0) == 0)
    def _():
        ibuf[...] = jnp.full(ibuf.shape, -1, jnp.int32)
        fill = pltpu.make_async_copy(ibuf, src_ref, zsem)
        fill.start()
        fill.wait()
        zbuf[...] = jnp.zeros_like(zbuf)

        def zero_block(blk):
            off = pl.multiple_of(blk * (MOE_BLOCK * TOKEN_ROWS), MOE_BLOCK * TOKEN_ROWS)
            return pltpu.make_async_copy(zbuf, xs_ref.at[pl.ds(off, MOE_BLOCK * TOKEN_ROWS), :], zsem)

        def for_each_zeroed_block(fn):
            def per_expert(e, c):
                end = pend_ref[e]
                start = jnp.where(e == 0, 0, pend_ref[jnp.maximum(e - 1, 0)])

                @pl.when(end > start)
                def _():
                    fn(zero_block(end // MOE_BLOCK - 1))
                return c
            lax.fori_loop(0, N_EXPERTS, per_expert, 0)

            def per_tail_block(blk, c):
                fn(zero_block(blk))
                return c
            lax.fori_loop(nu_ref[0], n_blocks, per_tail_block, 0)

        for_each_zeroed_block(lambda cp: cp.start())
        for_each_zeroed_block(lambda cp: cp.wait())

    def issue(group, c):
        toks = [group * DMA_ISSUE_UNROLL + g for g in range(DMA_ISSUE_UNROLL)]
        dests = [[dest_ref[slot * t_total + base + r] for slot in range(TOP_K)] for r in toks]
        for r, ds in zip(toks, dests):
            src = xn_ref.at[pl.ds(pl.multiple_of(r * TOKEN_ROWS, TOKEN_ROWS), TOKEN_ROWS), :]
            for slot, d in enumerate(ds):
                dst = xs_ref.at[pl.ds(pl.multiple_of(d * TOKEN_ROWS, TOKEN_ROWS), TOKEN_ROWS), :]
                pltpu.make_async_copy(src, dst, sem).start(priority=slot % DMA_PRIORITIES)
        for r, ds in zip(toks, dests):
            for slot, d in enumerate(ds):
                src_ref[d] = slot * t_total + base + r
        return c
    lax.fori_loop(0, tm // DMA_ISSUE_UNROLL, issue, 0)

    for slot in range(TOP_K):
        pltpu.make_async_copy(xn_ref, xs_ref.at[pl.ds(0, tm * TOKEN_ROWS), :], sem).wait()


def _dispatch(xn_tiles, dest_flat, pad_end, n_used, p_len, tm):
    t = xn_tiles.shape[0] // TOKEN_ROWS
    grid_spec = pltpu.PrefetchScalarGridSpec(
        num_scalar_prefetch=3,
        grid=(t // tm,),
        in_specs=[pl.BlockSpec((tm * TOKEN_ROWS, LANES), lambda i, dest, pend, nu: (i, 0))],
        out_specs=[pl.BlockSpec(memory_space=pl.ANY), pl.BlockSpec(memory_space=pltpu.SMEM)],
        scratch_shapes=[pltpu.VMEM((MOE_BLOCK * TOKEN_ROWS, LANES), U32),
                        pltpu.VMEM((p_len,), jnp.int32),
                        pltpu.SemaphoreType.DMA(()), pltpu.SemaphoreType.DMA(())],
    )
    return pl.pallas_call(
        _dispatch_kernel,
        grid_spec=grid_spec,
        out_shape=[jax.ShapeDtypeStruct((p_len * TOKEN_ROWS, LANES), U32),
                   jax.ShapeDtypeStruct((p_len,), jnp.int32)],
        compiler_params=_cparams(("arbitrary",)),
        name="dispatch",
    )(dest_flat, pad_end, n_used, xn_tiles)


def _experts_kernel(be_ref, nu_ref, nxt_ref, ord_ref, src_ref, xs_ref, w1_hbm, w3_hbm, w2_hbm, y2_ref,
                    w1_scr, w3_scr, w2_scr, w1_buf, w3_buf, w2_buf, ybuf, wsems, ysems):
    step = pl.program_id(0)
    first = 2 * step
    second = first + 1
    n_used = nu_ref[0]
    cur = step % 2
    pair_rows = 2 * MOE_BLOCK
    n_slot_rows = y2_ref.shape[0] // TOKEN_ROWS - 2 * pair_rows

    def send_previous(unrolled):
        prev_base = jnp.maximum(step - 1, 0) * pair_rows

        def send(r, priority):
            s = src_ref[prev_base + r]
            row = jnp.where((step > 0) & (s >= 0), s, n_slot_rows + cur * pair_rows + r)
            src = ybuf.at[1 - cur, pl.ds(pl.multiple_of(r * TOKEN_ROWS, TOKEN_ROWS), TOKEN_ROWS), :]
            dst = y2_ref.at[pl.ds(pl.multiple_of(row * TOKEN_ROWS, TOKEN_ROWS), TOKEN_ROWS), :]
            pltpu.make_async_copy(src, dst, ysems.at[cur]).start(priority=priority)

        if unrolled:
            for r in range(pair_rows):
                send(r, r % DMA_PRIORITIES)
        else:
            def send_group(j, c):
                for k in range(DMA_PRIORITIES):
                    send(j * DMA_PRIORITIES + k, k)
                return c
            lax.fori_loop(0, pair_rows // DMA_PRIORITIES, send_group, 0, unroll=DMA_ISSUE_UNROLL // DMA_PRIORITIES)

    def all_rows_sent(buf, sem):
        return pltpu.make_async_copy(ybuf.at[buf], y2_ref.at[pl.ds(0, pair_rows * TOKEN_ROWS), :], sem)

    def wait_older_sends():
        all_rows_sent(cur, ysems.at[1 - cur]).wait()

    @pl.when(step == 0)
    def _():
        ybuf[1] = jnp.zeros(ybuf.shape[1:], U32)
        odd_dump = (n_slot_rows + pair_rows) * TOKEN_ROWS
        pltpu.make_async_copy(ybuf.at[1], y2_ref.at[pl.ds(odd_dump, pair_rows * TOKEN_ROWS), :], ysems.at[1]).start()

    def weight_copies(e, slot):
        return (pltpu.make_async_copy(w1_hbm.at[e], w1_buf.at[slot], wsems.at[slot, 0]),
                pltpu.make_async_copy(w3_hbm.at[e], w3_buf.at[slot], wsems.at[slot, 1]),
                pltpu.make_async_copy(w2_hbm.at[e], w2_buf.at[slot], wsems.at[slot, 2]))

    def switch_to(e):
        slot = ord_ref[e] % 2
        for cp in weight_copies(e, slot):
            cp.wait()
        nxt = nxt_ref[e]

        @pl.when(nxt >= 0)
        def _():
            for cp in weight_copies(nxt, 1 - slot):
                cp.start()
        w1_scr[...] = w1_buf[slot].astype(BF16)
        w3_scr[...] = w3_buf[slot].astype(BF16)
        w2_scr[...] = w2_buf[slot].astype(BF16)

    def gated_mlp(row0, n_rows, first_store):
        x = _unpack_bf16_pairs(_load_token_tiles(xs_ref, n_rows, row0)).astype(BF16)
        a = jnp.dot(x, w1_scr[...], preferred_element_type=F32)
        b = jnp.dot(x, w3_scr[...], preferred_element_type=F32)
        hdn = a * (1.0 / (1.0 + jnp.exp(-a))) * b
        y = jnp.dot(hdn.astype(BF16), w2_scr[...], preferred_element_type=F32)
        if first_store:
            wait_older_sends()
        _store_token_tiles(ybuf.at[cur], _pack_bf16_pairs(y), row0)

    def zero_rows(row0, n_rows):
        ybuf[cur, row0 * TOKEN_ROWS:(row0 + n_rows) * TOKEN_ROWS, :] = jnp.zeros((n_rows * TOKEN_ROWS, LANES), U32)

    @pl.when(first < n_used)
    def _():
        e_first = be_ref[first]
        e_before = be_ref[jnp.maximum(first - 1, 0)]

        @pl.when(first == 0)
        def _():
            for cp in weight_copies(e_first, ord_ref[e_first] % 2):
                cp.start()

        @pl.when((first == 0) | (e_first != e_before))
        def _():
            switch_to(e_first)

        second_used = second < n_used
        e_second = be_ref[second]
        same_expert = second_used & (e_second == e_first)

        @pl.when(same_expert)
        def _():
            send_previous(unrolled=True)
            gated_mlp(0, 2 * MOE_BLOCK, first_store=True)

        @pl.when(jnp.logical_not(same_expert))
        def _():
            send_previous(unrolled=True)
            gated_mlp(0, MOE_BLOCK, first_store=True)

            @pl.when(second_used)
            def _():
                switch_to(e_second)
                gated_mlp(MOE_BLOCK, MOE_BLOCK, first_store=False)

            @pl.when(jnp.logical_not(second_used))
            def _():
                zero_rows(MOE_BLOCK, MOE_BLOCK)

    @pl.when(first >= n_used)
    def _():
        send_previous(unrolled=False)
        wait_older_sends()
        zero_rows(0, 2 * MOE_BLOCK)

    @pl.when(step == pl.num_programs(0) - 1)
    def _():
        all_rows_sent(1 - cur, ysems.at[cur]).wait()


def _experts(xs, row_src, block_e, n_used, next_expert, expert_ord, w1, w3, w2, n_slot_rows):
    p_len = xs.shape[0] // TOKEN_ROWS
    n_blocks = p_len // MOE_BLOCK
    assert n_blocks % 2 == 0
    n_pairs = n_blocks // 2
    d, de = w1.shape[1], w1.shape[2]
    blk = lambda i, be, nu, nxt, od, src: (jnp.maximum(jnp.minimum(i, (nu[0] - 1) // 2), 0), 0)
    hbm = pl.BlockSpec(memory_space=pl.ANY)
    grid_spec = pltpu.PrefetchScalarGridSpec(
        num_scalar_prefetch=5,
        grid=(n_pairs + 1,),
        in_specs=[pl.BlockSpec((2 * MOE_BLOCK * TOKEN_ROWS, LANES), blk), hbm, hbm, hbm],
        out_specs=hbm,
        scratch_shapes=[pltpu.VMEM((d, de), BF16), pltpu.VMEM((d, de), BF16), pltpu.VMEM((de, d), BF16),
                        pltpu.VMEM((2, d, de), F32), pltpu.VMEM((2, d, de), F32), pltpu.VMEM((2, de, d), F32),
                        pltpu.VMEM((2, 2 * MOE_BLOCK * TOKEN_ROWS, LANES), U32),
                        pltpu.SemaphoreType.DMA((2, 3)), pltpu.SemaphoreType.DMA((2,))],
    )
    return pl.pallas_call(
        _experts_kernel,
        grid_spec=grid_spec,
        out_shape=jax.ShapeDtypeStruct(((n_slot_rows + 4 * MOE_BLOCK) * TOKEN_ROWS, LANES), U32),
        compiler_params=_cparams(("arbitrary",)),
        name="experts",
    )(block_e, n_used, next_expert, expert_ord, row_src, xs, w1, w3, w2)


def _final_kernel(h_ref, meta_ref, p_ref, wple_ref, gple_ref, gplg_ref, wplg_ref, y_first_ref, y_second_ref, out_ref):
    tm = h_ref.shape[0]
    emb = jnp.dot(p_ref[...].astype(BF16), wple_ref[...], preferred_element_type=F32)
    emb = _rms_rows(emb) * gple_ref[...]
    meta = meta_ref[...]
    y1 = _unpack_bf16_pairs(_load_token_tiles(y_first_ref, tm))
    y2 = _unpack_bf16_pairs(_load_token_tiles(y_second_ref, tm))
    h = h_ref[...] + meta[:, 4:5] * y1 + meta[:, 5:6] * y2
    hn = (_rms_rows(h) * gplg_ref[...]).astype(BF16)
    z = jnp.dot(hn, wplg_ref[...], preferred_element_type=F32)
    out_ref[...] = h + emb * (1.0 / (1.0 + jnp.exp(-z)))


def _final(h2, meta, p2, wple_bf, g_ple, g_plg, wplg_bf, y_slots, tm):
    t, d = h2.shape
    dp = p2.shape[1]
    nt = t // tm
    const = lambda i: (0, 0)
    row = lambda i: (i, 0)
    return pl.pallas_call(
        _final_kernel,
        grid=(nt,),
        in_specs=[
            pl.BlockSpec((tm, d), row),
            pl.BlockSpec((tm, LANES), row),
            pl.BlockSpec((tm, dp), row),
            pl.BlockSpec((dp, d), const),
            pl.BlockSpec((1, d), const),
            pl.BlockSpec((1, d), const),
            pl.BlockSpec((d, d), const),
            pl.BlockSpec((tm * TOKEN_ROWS, LANES), row),
            pl.BlockSpec((tm * TOKEN_ROWS, LANES), lambda i: (nt + i, 0)),
        ],
        out_specs=pl.BlockSpec((tm, d), row),
        out_shape=jax.ShapeDtypeStruct((t, d), F32),
        compiler_params=_cparams(("parallel",)),
        name="combine_ple",
    )(h2, meta, p2, wple_bf, g_ple, g_plg, wplg_bf, y_slots, y_slots)


def _block_diag_mean(seg):
    m = np.kron(np.eye(256 // seg), np.full((seg, seg), 1.0 / seg))
    return jnp.asarray(m, BF16)


def _rope_tables(seq):
    f32 = np.float32
    inv = np.power(f32(ROPE_THETA), -np.arange(0, DA_QK, 2, dtype=f32) / f32(DA_QK)).astype(f32)
    ang = (np.arange(seq, dtype=f32)[:, None] * inv[None, :]).astype(f32)
    cos, sin = np.cos(ang).astype(f32), np.sin(ang).astype(f32)
    cos_t = np.tile(cos, (1, LANES // cos.shape[1]))
    sin_t = np.tile(np.concatenate([-sin, sin], axis=1), (1, LANES // (2 * sin.shape[1])))
    return jnp.asarray(cos_t), jnp.asarray(sin_t)


def _tile_for(n, pref):
    while n % pref:
        pref //= 2
    return pref


def kernel(x, p, g_mix, w_in, g_qa, g_ka, lam_q1, lam_k1, lam_q2, lam_k2, g_sub, g_qb, g_kb, rpb, w_out,
           g_ffn, w_rg, w_re, w1, w3, w2, g_plg, w_plg, w_ple, g_ple):
    b, s, d = x.shape
    assert d == 2 * TOKEN_ROWS * LANES, "token-tile layout packs one token into TOKEN_ROWS rows of 128 uint32"
    t = b * s
    depth = w_in.shape[0]
    tm = _tile_for(s, 512)
    tmd = _tile_for(t, 512)
    tmf = _tile_for(t, 512)
    n_slots = t * TOP_K
    n_blocks = n_slots // MOE_BLOCK + N_EXPERTS
    p_len = n_blocks * MOE_BLOCK

    bd64 = _block_diag_mean(DA_QK)
    bd32 = _block_diag_mean(DB_H)
    cos_t, sin_t = _rope_tables(s)
    tri = jnp.asarray(np.tril(np.ones((tm, tm)), -1), BF16)

    h = x.reshape(t, d)
    for i in range(depth):
        lam_init = 0.8 - 0.6 * math.exp(-0.3 * i)
        gsec = jnp.stack([
            jnp.tile(g_qa[i], 512 // DA_QK) * (DA_QK ** -0.5 * math.log2(math.e)),
            jnp.tile(g_ka[i], 512 // DA_QK),
            jnp.tile(g_qb[i], 512 // DB_H) * DB_H ** -0.5,
            jnp.tile(g_kb[i], 512 // DB_H),
        ]).astype(F32)
        proj, qvt = _inproj(h, g_mix[i][None, :], w_in[i], bd64, bd32, gsec, cos_t, sin_t, s, tm)
        proj3 = proj.reshape(b, s, proj.shape[1])

        lam4 = jnp.stack([lam_q1[i], lam_k1[i], lam_q2[i], lam_k2[i]]).astype(F32)
        oa = _diffattn(proj3, qvt, lam4, g_sub[i][None, :].astype(F32), lam_init)
        ob = _natten(proj3, _natten_bias_table(rpb[i]))

        w_router = jnp.concatenate(
            [w_rg[i], w_re[i], jnp.zeros((d, LANES - N_GROUPS - N_EXPERTS), F32)], axis=1).astype(F32)
        wr_hi = w_router.astype(BF16)
        wr_lo = (w_router - wr_hi.astype(F32)).astype(BF16)
        h1, xn, meta, meta_t, cnt = _outproj(h, oa.reshape(t, -1), ob.reshape(t, -1), w_out[i].astype(BF16),
                                             g_ffn[i][None, :], jnp.concatenate([wr_hi, wr_lo], axis=1), tri, tm)

        counts = cnt[0, E_LANE0:E_LANE0 + N_EXPERTS].astype(jnp.int32)
        padded = (counts + MOE_BLOCK - 1) // MOE_BLOCK * MOE_BLOCK
        pad_end = jnp.cumsum(padded)
        pad_start = pad_end - padded
        mt = meta_t[0:2 * TOP_K].astype(jnp.int32)
        e_ids = jnp.arange(N_EXPERTS, dtype=jnp.int32)[None, :, None]
        start_of = jnp.sum(jnp.where(mt[:TOP_K, None, :] == e_ids, pad_start[None, :, None], 0), axis=1)
        dest = start_of + mt[TOP_K:]
        blk_start = jnp.arange(n_blocks, dtype=jnp.int32) * MOE_BLOCK
        block_e = jnp.minimum(jnp.sum(pad_end[None, :] <= blk_start[:, None], axis=1), N_EXPERTS - 1)
        block_e = block_e.astype(jnp.int32)
        n_used = (pad_end[-1:] // MOE_BLOCK).astype(jnp.int32)

        dest_flat = dest.reshape(TOP_K * t)

        xs, row_src = _dispatch(xn, dest_flat, pad_end.astype(jnp.int32), n_used, p_len, tmd)
        in_use = counts > 0
        expert_ord = (jnp.cumsum(in_use) - 1).astype(jnp.int32)
        ids = jnp.arange(N_EXPERTS, dtype=jnp.int32)
        later = jnp.where((ids[None, :] > ids[:, None]) & in_use[None, :], ids[None, :], N_EXPERTS)
        next_expert = jnp.min(later, axis=1)
        next_expert = jnp.where(next_expert == N_EXPERTS, -1, next_expert).astype(jnp.int32)
        y_slots = _experts(xs, row_src.reshape(p_len), block_e, n_used, next_expert, expert_ord,
                           w1[i], w3[i], w2[i], TOP_K * t)
        h = _final(h1, meta, p[i].reshape(t, -1), w_ple[i].astype(BF16), g_ple[i][None, :], g_plg[i][None, :],
                   w_plg[i].astype(BF16), y_slots, tmf)
    return h.reshape(b, s, d)
```

```python
import functools
import math

import numpy as np
import jax
import jax.numpy as jnp
from jax import lax
from jax.experimental import pallas as pl
from jax.experimental.pallas import tpu as pltpu

F32 = jnp.float32
BF16 = jnp.bfloat16
U32 = jnp.uint32

LANES = 128
HA = 4
DA_QK = 64
DA_V = 128
HB = 16
DB_H = 32
GRID_W = 64
WIN_R = 8
WIN_C = 16
ROPE_THETA = 10000.0
EPS = 1e-6
NEG = -1e30
N_GROUPS = 4
EXPERTS_PER_GROUP = 8
N_EXPERTS = N_GROUPS * EXPERTS_PER_GROUP
TOP_K = 2
MOE_BLOCK = 256
E_LANE0 = N_GROUPS
ROUTER_ROWS = 40

VMEM_LIMIT = 56 * 1024 * 1024


def _cparams(sem):
    return pltpu.CompilerParams(dimension_semantics=sem, vmem_limit_bytes=VMEM_LIMIT)


def _rms_rows(x):
    return x * lax.rsqrt(jnp.mean(x * x, axis=-1, keepdims=True) + EPS)


SUBLANES = 8

TOKEN_ROWS = 4
HIGH_HALF = 0xFFFF0000


def _pack_bf16_pairs(x):
    m = x.shape[1] // 2
    bits = pltpu.bitcast(x.astype(BF16).astype(F32), U32)
    return (bits[:, :m] >> 16) | (bits[:, m:] & jnp.uint32(HIGH_HALF))


def _unpack_bf16_pairs(u):
    lo = pltpu.bitcast(u << 16, F32)
    hi = pltpu.bitcast(u & jnp.uint32(HIGH_HALF), F32)
    return jnp.concatenate([lo, hi], axis=1)


def _store_token_tiles(ref, val, first=0):
    n = val.shape[0]
    for c in range(TOKEN_ROWS):
        ref[pl.ds(first * TOKEN_ROWS + c, n, stride=TOKEN_ROWS), :] = val[:, c * LANES:(c + 1) * LANES]


def _load_token_tiles(ref, n, first=0):
    return jnp.concatenate(
        [ref[pl.ds(first * TOKEN_ROWS + c, n, stride=TOKEN_ROWS), :] for c in range(TOKEN_ROWS)], axis=1)


def _inproj_kernel(x_ref, gmix_ref, w_ref, bd64_ref, bd32_ref, gsec_ref, cos_ref, sin_ref, out_ref, qvt_ref, w_scr):
    tm = x_ref.shape[0]

    @pl.when(pl.program_id(0) == 0)
    def _():
        w_scr[...] = w_ref[...].astype(BF16)

    a = (_rms_rows(x_ref[...]) * gmix_ref[...]).astype(BF16)
    lane = lax.broadcasted_iota(jnp.int32, (tm, LANES), 1)
    first_half = (lane % 64) < 32
    cos = cos_ref[...]
    sin = sin_ref[...]
    out_col = {1: 0, 3: 1, 4: 2, 5: 3}
    qvt_row = {0: 0, 2: 1}
    for sec in range(6):
        y = jnp.dot(a, w_scr[:, sec * 512:(sec + 1) * 512], preferred_element_type=F32)
        if sec in (0, 1, 3, 4):
            bd = bd64_ref if sec < 2 else bd32_ref
            gi = {0: 0, 1: 1, 3: 2, 4: 3}[sec]
            y2 = (y * y).astype(BF16)
            ms = jnp.concatenate([jnp.dot(y2[:, half * 256:(half + 1) * 256], bd[...], preferred_element_type=F32)
                                  for half in range(2)], axis=1)
            y = y * lax.rsqrt(ms + EPS) * gsec_ref[gi:gi + 1, :]
        for c in range(4):
            yc = y[:, c * LANES:(c + 1) * LANES]
            if sec < 2:
                rot = jnp.where(first_half, pltpu.roll(yc, 96, 1), pltpu.roll(yc, 32, 1))
                yc = yc * cos + rot * sin
            if sec in qvt_row:
                r0 = qvt_row[sec] * 512 + c * LANES
                qvt_ref[0, r0:r0 + LANES, :] = yc.T.astype(BF16)
            else:
                c0 = out_col[sec] * 512 + c * LANES
                out_ref[:, c0:c0 + LANES] = yc.astype(BF16)


def _inproj(x2, g_mix, w_f32, bd64, bd32, gsec, cos_t, sin_t, seq, tm):
    t, d = x2.shape
    n = w_f32.shape[1]
    nsb = seq // tm
    n_tok_major = 4 * 512
    n_feat_major = 2 * 512
    const = lambda i: (0, 0)
    return pl.pallas_call(
        _inproj_kernel,
        grid=(t // tm,),
        in_specs=[
            pl.BlockSpec((tm, d), lambda i: (i, 0)),
            pl.BlockSpec((1, d), const),
            pl.BlockSpec((d, n), const, pipeline_mode=pl.Buffered(1)),
            pl.BlockSpec((256, 256), const),
            pl.BlockSpec((256, 256), const),
            pl.BlockSpec((4, 512), const),
            pl.BlockSpec((tm, LANES), lambda i: (i % nsb, 0)),
            pl.BlockSpec((tm, LANES), lambda i: (i % nsb, 0)),
        ],
        out_specs=[pl.BlockSpec((tm, n_tok_major), lambda i: (i, 0)),
                   pl.BlockSpec((1, n_feat_major, tm), lambda i: (i // nsb, 0, i % nsb))],
        out_shape=[jax.ShapeDtypeStruct((t, n_tok_major), BF16),
                   jax.ShapeDtypeStruct((t // seq, n_feat_major, seq), BF16)],
        scratch_shapes=[pltpu.VMEM((d, n), BF16)],
        compiler_params=_cparams(("arbitrary",)),
        name="inproj",
    )(x2, g_mix, w_f32, bd64, bd32, gsec, cos_t, sin_t)


QBLOCK_PAIRS_PER_BODY = 4

def _diffattn_kernel(lam_ref, gsub_ref, qt_ref, k_ref, vt_ref, o_ref, sa_scr, sb_scr, *, seq, tq, tk, lam_init):
    nq = seq // tq
    nk = seq // tk
    s1 = jnp.sum(lam_ref[0:1, :] * lam_ref[1:2, :], axis=1, keepdims=True)
    s2 = jnp.sum(lam_ref[2:3, :] * lam_ref[3:4, :], axis=1, keepdims=True)
    lam = jnp.exp(s1) - jnp.exp(s2) + lam_init

    row = lax.broadcasted_iota(jnp.int32, (LANES, tq), 0)

    def scores(qi, s_scr):
        qoff = pl.multiple_of(qi * tq, tq)
        qt = qt_ref[0, :, pl.ds(qoff, tq)]
        zero = jnp.zeros_like(qt)
        qm = jnp.concatenate([jnp.where(row < DA_QK, qt, zero), jnp.where(row >= DA_QK, qt, zero)], axis=1)
        mx = None
        for j in range(nk):
            st = jnp.dot(k_ref[0, j * tk:(j + 1) * tk, :], qm, preferred_element_type=F32)
            s_scr[j * tk:(j + 1) * tk, :] = st
            cm = jnp.max(st, axis=0, keepdims=True)
            mx = cm if mx is None else jnp.maximum(mx, cm)
        return mx

    def finish(qi, s_scr, m):
        l = jnp.zeros((1, 2 * tq), F32)
        acc = jnp.zeros((DA_V, 2 * tq), F32)
        for j in range(nk):
            p = jnp.exp2(s_scr[j * tk:(j + 1) * tk, :] - m)
            l = l + jnp.sum(p, axis=0, keepdims=True)
            acc = acc + jnp.dot(vt_ref[0, :, j * tk:(j + 1) * tk], p.astype(BF16), preferred_element_type=F32)
        o = acc * (1.0 / l)
        ot = o[:, :tq] - lam * o[:, tq:]
        ot = ot * lax.rsqrt(jnp.mean(ot * ot, axis=0, keepdims=True) + EPS) * (1.0 - lam_init)
        qoff = pl.multiple_of(qi * tq, tq)
        o_ref[0, pl.ds(qoff, tq), :] = (ot.T * gsub_ref[...]).astype(BF16)

    def body(i, m_a):
        for pair in range(QBLOCK_PAIRS_PER_BODY):
            q0 = 2 * (QBLOCK_PAIRS_PER_BODY * i + pair)
            m_b = scores(q0 + 1, sb_scr)
            finish(q0, sa_scr, m_a)
            m_a = scores(jnp.minimum(q0 + 2, nq - 1), sa_scr)
            finish(q0 + 1, sb_scr, m_b)
        return m_a

    lax.fori_loop(0, nq // (2 * QBLOCK_PAIRS_PER_BODY), body, scores(0, sa_scr))


def _diffattn(proj3, qvt, lam4, g_sub, lam_init, tq=128, tk=512):
    b, seq, _ = proj3.shape
    kern = functools.partial(_diffattn_kernel, seq=seq, tq=tq, tk=tk, lam_init=lam_init)
    return pl.pallas_call(
        kern,
        grid=(b, HA),
        in_specs=[
            pl.BlockSpec((4, DA_QK), lambda i, h: (0, 0)),
            pl.BlockSpec((1, DA_V), lambda i, h: (0, 0)),
            pl.BlockSpec((1, LANES, seq), lambda i, h: (i, h, 0)),
            pl.BlockSpec((1, seq, LANES), lambda i, h: (i, 0, h)),
            pl.BlockSpec((1, LANES, seq), lambda i, h: (i, HA + h, 0)),
        ],
        out_specs=pl.BlockSpec((1, seq, LANES), lambda i, h: (i, 0, h)),
        out_shape=jax.ShapeDtypeStruct((b, seq, HA * DA_V), BF16),
        scratch_shapes=[pltpu.VMEM((seq, 2 * tq), F32), pltpu.VMEM((seq, 2 * tq), F32)],
        compiler_params=_cparams(("parallel", "parallel")),
        name="diffattn",
    )(lam4, g_sub, qvt, proj3, qvt)


def _natten_kernel(bias_ref, q_ref, k_ref, v_ref, o_ref, *, rows):
    kr = min(WIN_R, rows)
    nkeys = kr * GRID_W
    hpg = LANES // DB_H
    lane_head = lax.broadcasted_iota(jnp.int32, (GRID_W, LANES), 1) // DB_H

    def row_step(r, carry):
        rs = jnp.clip(r - kr // 2, 0, rows - kr)
        base = rs - r + (WIN_R - 1)
        qoff = pl.multiple_of(r * GRID_W, GRID_W)
        koff = pl.multiple_of(rs * GRID_W, GRID_W)
        qr = q_ref[0, pl.ds(qoff, GRID_W), :]
        zero = jnp.zeros_like(qr)
        qs = jnp.concatenate([jnp.where(lane_head == h, qr, zero) for h in range(hpg)], axis=0)
        kw = k_ref[0, pl.ds(koff, nkeys), :]
        vw = v_ref[0, pl.ds(koff, nkeys), :]
        s = lax.dot_general(qs, kw, (((1,), (1,)), ((), ())), preferred_element_type=F32)
        bias = jnp.concatenate([bias_ref[0, base + 2 * i] for i in range(kr // 2)], axis=1)
        s = s + bias
        m = jnp.max(s, axis=1, keepdims=True)
        p = jnp.exp(s - m).astype(BF16)
        pv = jnp.dot(p, jnp.concatenate([vw, jnp.ones_like(vw)], axis=1), preferred_element_type=F32)
        o = pv[:, :LANES] * (1.0 / pv[:, LANES:])
        out = jnp.zeros((GRID_W, LANES), F32)
        for h in range(hpg):
            out = out + jnp.where(lane_head == h, o[h * GRID_W:(h + 1) * GRID_W, :], 0.0)
        o_ref[0, pl.ds(qoff, GRID_W), :] = out.astype(BF16)
        return carry

    lax.fori_loop(0, rows, row_step, 0, unroll=min(rows, 64))


def _natten(proj3, bias_tab):
    b, seq, _ = proj3.shape
    rows = seq // GRID_W
    ng = HB * DB_H // LANES
    nro = bias_tab.shape[1]
    kern = functools.partial(_natten_kernel, rows=rows)
    cb = HA
    return pl.pallas_call(
        kern,
        grid=(b, ng),
        in_specs=[
            pl.BlockSpec((1, nro, bias_tab.shape[2], LANES), lambda i, g: (g, 0, 0, 0)),
            pl.BlockSpec((1, seq, LANES), lambda i, g: (i, 0, cb + g)),
            pl.BlockSpec((1, seq, LANES), lambda i, g: (i, 0, cb + ng + g)),
            pl.BlockSpec((1, seq, LANES), lambda i, g: (i, 0, cb + 2 * ng + g)),
        ],
        out_specs=pl.BlockSpec((1, seq, LANES), lambda i, g: (i, 0, g)),
        out_shape=jax.ShapeDtypeStruct((b, seq, HB * DB_H), BF16),
        compiler_params=_cparams(("parallel", "parallel")),
        name="natten",
    )(bias_tab, proj3, proj3, proj3)


def _natten_bias_table(rpb):
    q = np.arange(GRID_W)[:, None]
    kc = np.arange(GRID_W)[None, :]
    cs = np.clip(q - WIN_C // 2, 0, GRID_W - WIN_C)
    valid = (kc >= cs) & (kc < cs + WIN_C)
    col_off = np.clip(kc - q + (WIN_C - 1), 0, 2 * WIN_C - 2)
    ncol = 2 * WIN_C - 1
    nro = 2 * WIN_R - 2
    pick = np.zeros((2, ncol, GRID_W, 2 * GRID_W), np.float32)
    for j in range(2):
        pick[j, :, :, j * GRID_W:(j + 1) * GRID_W] = (np.arange(ncol)[:, None, None] == col_off[None]) & valid[None]
    negmask = np.tile(np.where(valid, 0.0, NEG).astype(np.float32), (1, 2))
    hpg = LANES // DB_H
    r4 = rpb.astype(F32).reshape(HB // hpg, hpg, 2 * WIN_R - 1, ncol)
    rp = jnp.stack([r4[:, :, :nro], r4[:, :, 1:]], axis=0)
    tab = jnp.einsum('jghrc,jcqk->grhqk', rp, jnp.asarray(pick), precision=lax.Precision.HIGHEST) + negmask
    return tab.reshape(HB // hpg, nro, hpg * GRID_W, LANES)


def _outproj_kernel(x_ref, oa_ref, ob_ref, wo_ref, gffn_ref, wr_ref, tri_ref,
                    h_ref, xn_ref, meta_ref, meta_t_ref, cnt_ref, run_scr):
    tm = x_ref.shape[0]
    da = oa_ref.shape[1]

    @pl.when(pl.program_id(0) == 0)
    def _():
        run_scr[...] = jnp.zeros_like(run_scr)

    attn = jnp.dot(oa_ref[...], wo_ref[:da, :], preferred_element_type=F32)
    attn = attn + jnp.dot(ob_ref[...], wo_ref[da:, :], preferred_element_type=F32)
    h = x_ref[...] + attn
    h_ref[...] = h
    xn = _rms_rows(h) * gffn_ref[...]
    _store_token_tiles(xn_ref, _pack_bf16_pairs(xn))

    xh = xn.astype(BF16)
    xl = (xn - xh.astype(F32)).astype(BF16)
    hh_hl = jnp.dot(xh, wr_ref[...], preferred_element_type=F32)
    lg = hh_hl[:, :LANES] + hh_hl[:, LANES:] + jnp.dot(xl, wr_ref[:, :LANES], preferred_element_type=F32)
    lt = lg.T
    row = [lt[j:j + 1, :] for j in range(E_LANE0 + N_EXPERTS)]

    def first_argmax(vals):
        best = vals[0]
        for v in vals[1:]:
            best = jnp.maximum(best, v)
        idx = jnp.full_like(best, float(len(vals) - 1))
        for j in range(len(vals) - 2, -1, -1):
            idx = jnp.where(vals[j] == best, float(j), idx)
        return best, idx

    gmax, gidx = first_argmax(row[:N_GROUPS])
    gsum = jnp.exp(row[0] - gmax)
    for j in range(1, N_GROUPS):
        gsum = gsum + jnp.exp(row[j] - gmax)
    g_gate = 1.0 / gsum
    el = []
    for k in range(EXPERTS_PER_GROUP):
        v = row[E_LANE0 + k]
        for g in range(1, N_GROUPS):
            v = jnp.where(gidx == float(g), row[E_LANE0 + g * EXPERTS_PER_GROUP + k], v)
        el.append(v)
    l1, k1 = first_argmax(el)
    l2, k2 = first_argmax([jnp.where(k1 == float(k), -jnp.inf, el[k]) for k in range(EXPERTS_PER_GROUP)])
    r = jnp.exp(l2 - l1)
    w1 = g_gate / (1.0 + r)
    w2 = w1 * r
    e1 = gidx * EXPERTS_PER_GROUP + k1
    e2 = gidx * EXPERTS_PER_GROUP + k2

    rowid = lax.broadcasted_iota(jnp.int32, (ROUTER_ROWS, tm), 0).astype(F32)
    sel1 = rowid == e1 + E_LANE0
    sel2 = rowid == e2 + E_LANE0
    oh = jnp.where(sel1 | sel2, 1.0, 0.0).astype(BF16)
    run = run_scr[...]
    before = jnp.dot(oh, tri_ref[...], preferred_element_type=F32) + jnp.concatenate([run] * (tm // LANES), axis=1)
    rank1 = jnp.sum(jnp.where(sel1, before, 0.0), axis=0, keepdims=True)
    rank2 = jnp.sum(jnp.where(sel2, before, 0.0), axis=0, keepdims=True)
    run = run + jnp.dot(oh, jnp.ones((tm, LANES), BF16), preferred_element_type=F32)
    run_scr[...] = run
    cnt_ref[...] = run

    meta_t = jnp.concatenate([e1, e2, rank1, rank2, w1, w2, jnp.zeros((SUBLANES - 6, tm), F32)], axis=0)
    meta_t_ref[...] = meta_t
    meta_ref[...] = jnp.concatenate([meta_t, jnp.zeros((LANES - SUBLANES, tm), F32)], axis=0).T


def _outproj(x2, oa2, ob2, wo_bf, g_ffn, w_router, tri, tm):
    t, d = x2.shape
    da = oa2.shape[1]
    const = lambda i: (0, 0)
    row = lambda i: (i, 0)
    return pl.pallas_call(
        _outproj_kernel,
        grid=(t // tm,),
        in_specs=[
            pl.BlockSpec((tm, d), row),
            pl.BlockSpec((tm, da), row),
            pl.BlockSpec((tm, d - da), row),
            pl.BlockSpec((d, d), const),
            pl.BlockSpec((1, d), const),
            pl.BlockSpec((d, 2 * LANES), const),
            pl.BlockSpec((tm, tm), const),
        ],
        out_specs=[
            pl.BlockSpec((tm, d), row),
            pl.BlockSpec((tm * TOKEN_ROWS, LANES), row),
            pl.BlockSpec((tm, LANES), row),
            pl.BlockSpec((SUBLANES, tm), lambda i: (0, i)),
            pl.BlockSpec((ROUTER_ROWS, LANES), const),
        ],
        out_shape=[
            jax.ShapeDtypeStruct((t, d), F32),
            jax.ShapeDtypeStruct((t * TOKEN_ROWS, LANES), U32),
            jax.ShapeDtypeStruct((t, LANES), F32),
            jax.ShapeDtypeStruct((SUBLANES, t), F32),
            jax.ShapeDtypeStruct((ROUTER_ROWS, LANES), F32),
        ],
        scratch_shapes=[pltpu.VMEM((ROUTER_ROWS, LANES), F32)],
        compiler_params=_cparams(("arbitrary",)),
        name="outproj_router",
    )(x2, oa2, ob2, wo_bf, g_ffn, w_router, tri)


DMA_ISSUE_UNROLL = 8
DMA_PRIORITIES = 2


def _dispatch_kernel(dest_ref, pend_ref, nu_ref, xn_ref, xs_ref, src_ref, zbuf, ibuf, sem, zsem):
    tm = xn_ref.shape[0] // TOKEN_ROWS
    t_total = pl.num_programs(0) * tm
    base = pl.program_id(0) * tm
    n_blocks = xs_ref.shape[0] // (MOE_BLOCK * TOKEN_ROWS)

    @pl.when(pl.program_id(0) == 0)
    def _():
        ibuf[...] = jnp.full(ibuf.shape, -1, jnp.int32)
        fill = pltpu.make_async_copy(ibuf, src_ref, zsem)
        fill.start()
        fill.wait()
        zbuf[...] = jnp.zeros_like(zbuf)

        def zero_block(blk):
            off = pl.multiple_of(blk * (MOE_BLOCK * TOKEN_ROWS), MOE_BLOCK * TOKEN_ROWS)
            return pltpu.make_async_copy(zbuf, xs_ref.at[pl.ds(off, MOE_BLOCK * TOKEN_ROWS), :], zsem)

        def for_each_zeroed_block(fn):
            def per_expert(e, c):
                end = pend_ref[e]
                start = jnp.where(e == 0, 0, pend_ref[jnp.maximum(e - 1, 0)])

                @pl.when(end > start)
                def _():
                    fn(zero_block(end // MOE_BLOCK - 1))
                return c
            lax.fori_loop(0, N_EXPERTS, per_expert, 0)

            def per_tail_block(blk, c):
                fn(zero_block(blk))
                return c
            lax.fori_loop(nu_ref[0], n_blocks, per_tail_block, 0)

        for_each_zeroed_block(lambda cp: cp.start())
        for_each_zeroed_block(lambda cp: cp.wait())

    def issue(group, c):
        toks = [group * DMA_ISSUE_UNROLL + g for g in range(DMA_ISSUE_UNROLL)]
        dests = [[dest_ref[slot * t_total + base + r] for slot in range(TOP_K)] for r in toks]
        for r, ds in zip(toks, dests):
            src = xn_ref.at[pl.ds(pl.multiple_of(r * TOKEN_ROWS, TOKEN_ROWS), TOKEN_ROWS), :]
            for slot, d in enumerate(ds):
                dst = xs_ref.at[pl.ds(pl.multiple_of(d * TOKEN_ROWS, TOKEN_ROWS), TOKEN_ROWS), :]
                pltpu.make_async_copy(src, dst, sem).start(priority=slot % DMA_PRIORITIES)
        for r, ds in zip(toks, dests):
            for slot, d in enumerate(ds):
                src_ref[d] = slot * t_total + base + r
        return c
    lax.fori_loop(0, tm // DMA_ISSUE_UNROLL, issue, 0)

    for slot in range(TOP_K):
        pltpu.make_async_copy(xn_ref, xs_ref.at[pl.ds(0, tm * TOKEN_ROWS), :], sem).wait()


def _dispatch(xn_tiles, dest_flat, pad_end, n_used, p_len, tm):
    t = xn_tiles.shape[0] // TOKEN_ROWS
    grid_spec = pltpu.PrefetchScalarGridSpec(
        num_scalar_prefetch=3,
        grid=(t // tm,),
        in_specs=[pl.BlockSpec((tm * TOKEN_ROWS, LANES), lambda i, dest, pend, nu: (i, 0))],
        out_specs=[pl.BlockSpec(memory_space=pl.ANY), pl.BlockSpec(memory_space=pltpu.SMEM)],
        scratch_shapes=[pltpu.VMEM((MOE_BLOCK * TOKEN_ROWS, LANES), U32),
                        pltpu.VMEM((p_len,), jnp.int32),
                        pltpu.SemaphoreType.DMA(()), pltpu.SemaphoreType.DMA(())],
    )
    return pl.pallas_call(
        _dispatch_kernel,
        grid_spec=grid_spec,
        out_shape=[jax.ShapeDtypeStruct((p_len * TOKEN_ROWS, LANES), U32),
                   jax.ShapeDtypeStruct((p_len,), jnp.int32)],
        compiler_params=_cparams(("arbitrary",)),
        name="dispatch",
    )(dest_flat, pad_end, n_used, xn_tiles)


def _experts_kernel(be_ref, nu_ref, nxt_ref, ord_ref, src_ref, xs_ref, w1_hbm, w3_hbm, w2_hbm, y2_ref,
                    w1_scr, w3_scr, w2_scr, w1_buf, w3_buf, w2_buf, ybuf, wsems, ysems):
    step = pl.program_id(0)
    first = 2 * step
    second = first + 1
    n_used = nu_ref[0]
    cur = step % 2
    pair_rows = 2 * MOE_BLOCK
    n_slot_rows = y2_ref.shape[0] // TOKEN_ROWS - 2 * pair_rows

    def send_previous(unrolled):
        prev_base = jnp.maximum(step - 1, 0) * pair_rows

        def send(r, priority):
            s = src_ref[prev_base + r]
            row = jnp.where((step > 0) & (s >= 0), s, n_slot_rows + cur * pair_rows + r)
            src = ybuf.at[1 - cur, pl.ds(pl.multiple_of(r * TOKEN_ROWS, TOKEN_ROWS), TOKEN_ROWS), :]
            dst = y2_ref.at[pl.ds(pl.multiple_of(row * TOKEN_ROWS, TOKEN_ROWS), TOKEN_ROWS), :]
            pltpu.make_async_copy(src, dst, ysems.at[cur]).start(priority=priority)

        if unrolled:
            for r in range(pair_rows):
                send(r, r % DMA_PRIORITIES)
        else:
            def send_group(j, c):
                for k in range(DMA_PRIORITIES):
                    send(j * DMA_PRIORITIES + k, k)
                return c
            lax.fori_loop(0, pair_rows // DMA_PRIORITIES, send_group, 0, unroll=DMA_ISSUE_UNROLL // DMA_PRIORITIES)

    def all_rows_sent(buf, sem):
        return pltpu.make_async_copy(ybuf.at[buf], y2_ref.at[pl.ds(0, pair_rows * TOKEN_ROWS), :], sem)

    def wait_older_sends():
        all_rows_sent(cur, ysems.at[1 - cur]).wait()

    @pl.when(step == 0)
    def _():
        ybuf[1] = jnp.zeros(ybuf.shape[1:], U32)
        odd_dump = (n_slot_rows + pair_rows) * TOKEN_ROWS
        pltpu.make_async_copy(ybuf.at[1], y2_ref.at[pl.ds(odd_dump, pair_rows * TOKEN_ROWS), :], ysems.at[1]).start()

    def weight_copies(e, slot):
        return (pltpu.make_async_copy(w1_hbm.at[e], w1_buf.at[slot], wsems.at[slot, 0]),
                pltpu.make_async_copy(w3_hbm.at[e], w3_buf.at[slot], wsems.at[slot, 1]),
                pltpu.make_async_copy(w2_hbm.at[e], w2_buf.at[slot], wsems.at[slot, 2]))

    def switch_to(e):
        slot = ord_ref[e] % 2
        for cp in weight_copies(e, slot):
            cp.wait()
        nxt = nxt_ref[e]

        @pl.when(nxt >= 0)
        def _():
            for cp in weight_copies(nxt, 1 - slot):
                cp.start()
        w1_scr[...] = w1_buf[slot].astype(BF16)
        w3_scr[...] = w3_buf[slot].astype(BF16)
        w2_scr[...] = w2_buf[slot].astype(BF16)

    def gated_mlp(row0, n_rows, first_store):
        x = _unpack_bf16_pairs(_load_token_tiles(xs_ref, n_rows, row0)).astype(BF16)
        a = jnp.dot(x, w1_scr[...], preferred_element_type=F32)
        b = jnp.dot(x, w3_scr[...], preferred_element_type=F32)
        hdn = a * (1.0 / (1.0 + jnp.exp(-a))) * b
        y = jnp.dot(hdn.astype(BF16), w2_scr[...], preferred_element_type=F32)
        if first_store:
            wait_older_sends()
        _store_token_tiles(ybuf.at[cur], _pack_bf16_pairs(y), row0)

    def zero_rows(row0, n_rows):
        ybuf[cur, row0 * TOKEN_ROWS:(row0 + n_rows) * TOKEN_ROWS, :] = jnp.zeros((n_rows * TOKEN_ROWS, LANES), U32)

    @pl.when(first < n_used)
    def _():
        e_first = be_ref[first]
        e_before = be_ref[jnp.maximum(first - 1, 0)]

        @pl.when(first == 0)
        def _():
            for cp in weight_copies(e_first, ord_ref[e_first] % 2):
                cp.start()

        @pl.when((first == 0) | (e_first != e_before))
        def _():
            switch_to(e_first)

        second_used = second < n_used
        e_second = be_ref[second]
        same_expert = second_used & (e_second == e_first)

        @pl.when(same_expert)
        def _():
            send_previous(unrolled=True)
            gated_mlp(0, 2 * MOE_BLOCK, first_store=True)

        @pl.when(jnp.logical_not(same_expert))
        def _():
            send_previous(unrolled=True)
            gated_mlp(0, MOE_BLOCK, first_store=True)

            @pl.when(second_used)
            def _():
                switch_to(e_second)
                gated_mlp(MOE_BLOCK, MOE_BLOCK, first_store=False)

            @pl.when(jnp.logical_not(second_used))
            def _():
                zero_rows(MOE_BLOCK, MOE_BLOCK)

    @pl.when(first >= n_used)
    def _():
        send_previous(unrolled=False)
        wait_older_sends()
        zero_rows(0, 2 * MOE_BLOCK)

    @pl.when(step == pl.num_programs(0) - 1)
    def _():
        all_rows_sent(1 - cur, ysems.at[cur]).wait()


def _experts(xs, row_src, block_e, n_used, next_expert, expert_ord, w1, w3, w2, n_slot_rows):
    p_len = xs.shape[0] // TOKEN_ROWS
    n_blocks = p_len // MOE_BLOCK
    assert n_blocks % 2 == 0
    n_pairs = n_blocks // 2
    d, de = w1.shape[1], w1.shape[2]
    blk = lambda i, be, nu, nxt, od, src: (jnp.maximum(jnp.minimum(i, (nu[0] - 1) // 2), 0), 0)
    hbm = pl.BlockSpec(memory_space=pl.ANY)
    grid_spec = pltpu.PrefetchScalarGridSpec(
        num_scalar_prefetch=5,
        grid=(n_pairs + 1,),
        in_specs=[pl.BlockSpec((2 * MOE_BLOCK * TOKEN_ROWS, LANES), blk), hbm, hbm, hbm],
        out_specs=hbm,
        scratch_shapes=[pltpu.VMEM((d, de), BF16), pltpu.VMEM((d, de), BF16), pltpu.VMEM((de, d), BF16),
                        pltpu.VMEM((2, d, de), F32), pltpu.VMEM((2, d, de), F32), pltpu.VMEM((2, de, d), F32),
                        pltpu.VMEM((2, 2 * MOE_BLOCK * TOKEN_ROWS, LANES), U32),
                        pltpu.SemaphoreType.DMA((2, 3)), pltpu.SemaphoreType.DMA((2,))],
    )
    return pl.pallas_call(
        _experts_kernel,
        grid_spec=grid_spec,
        out_shape=jax.ShapeDtypeStruct(((n_slot_rows + 4 * MOE_BLOCK) * TOKEN_ROWS, LANES), U32),
        compiler_params=_cparams(("arbitrary",)),
        name="experts",
    )(block_e, n_used, next_expert, expert_ord, row_src, xs, w1, w3, w2)


def _final_kernel(h_ref, meta_ref, p_ref, wple_ref, gple_ref, gplg_ref, wplg_ref, y_first_ref, y_second_ref, out_ref):
    tm = h_ref.shape[0]
    emb = jnp.dot(p_ref[...].astype(BF16), wple_ref[...], preferred_element_type=F32)
    emb = _rms_rows(emb) * gple_ref[...]
    meta = meta_ref[...]
    y1 = _unpack_bf16_pairs(_load_token_tiles(y_first_ref, tm))
    y2 = _unpack_bf16_pairs(_load_token_tiles(y_second_ref, tm))
    h = h_ref[...] + meta[:, 4:5] * y1 + meta[:, 5:6] * y2
    hn = (_rms_rows(h) * gplg_ref[...]).astype(BF16)
    z = jnp.dot(hn, wplg_ref[...], preferred_element_type=F32)
    out_ref[...] = h + emb * (1.0 / (1.0 + jnp.exp(-z)))


def _final(h2, meta, p2, wple_bf, g_ple, g_plg, wplg_bf, y_slots, tm):
    t, d = h2.shape
    dp = p2.shape[1]
    nt = t // tm
    const = lambda i: (0, 0)
    row = lambda i: (i, 0)
    return pl.pallas_call(
        _final_kernel,
        grid=(nt,),
        in_specs=[
            pl.BlockSpec((tm, d), row),
            pl.BlockSpec((tm, LANES), row),
            pl.BlockSpec((tm, dp), row),
            pl.BlockSpec((dp, d), const),
            pl.BlockSpec((1, d), const),
            pl.BlockSpec((1, d), const),
            pl.BlockSpec((d, d), const),
            pl.BlockSpec((tm * TOKEN_ROWS, LANES), row),
            pl.BlockSpec((tm * TOKEN_ROWS, LANES), lambda i: (nt + i, 0)),
        ],
        out_specs=pl.BlockSpec((tm, d), row),
        out_shape=jax.ShapeDtypeStruct((t, d), F32),
        compiler_params=_cparams(("parallel",)),
        name="combine_ple",
    )(h2, meta, p2, wple_bf, g_ple, g_plg, wplg_bf, y_slots, y_slots)


def _block_diag_mean(seg):
    m = np.kron(np.eye(256 // seg), np.full((seg, seg), 1.0 / seg))
    return jnp.asarray(m, BF16)


def _rope_tables(seq):
    f32 = np.float32
    inv = np.power(f32(ROPE_THETA), -np.arange(0, DA_QK, 2, dtype=f32) / f32(DA_QK)).astype(f32)
    ang = (np.arange(seq, dtype=f32)[:, None] * inv[None, :]).astype(f32)
    cos, sin = np.cos(ang).astype(f32), np.sin(ang).astype(f32)
    cos_t = np.tile(cos, (1, LANES // cos.shape[1]))
    sin_t = np.tile(np.concatenate([-sin, sin], axis=1), (1, LANES // (2 * sin.shape[1])))
    return jnp.asarray(cos_t), jnp.asarray(sin_t)


def _tile_for(n, pref):
    while n % pref:
        pref //= 2
    return pref


def kernel(x, p, g_mix, w_in, g_qa, g_ka, lam_q1, lam_k1, lam_q2, lam_k2, g_sub, g_qb, g_kb, rpb, w_out,
           g_ffn, w_rg, w_re, w1, w3, w2, g_plg, w_plg, w_ple, g_ple):
    b, s, d = x.shape
    assert d == 2 * TOKEN_ROWS * LANES, "token-tile layout packs one token into TOKEN_ROWS rows of 128 uint32"
    t = b * s
    depth = w_in.shape[0]
    tm = _tile_for(s, 512)
    tmd = _tile_for(t, 512)
    tmf = _tile_for(t, 512)
    n_slots = t * TOP_K
    n_blocks = n_slots // MOE_BLOCK + N_EXPERTS
    p_len = n_blocks * MOE_BLOCK

    bd64 = _block_diag_mean(DA_QK)
    bd32 = _block_diag_mean(DB_H)
    cos_t, sin_t = _rope_tables(s)
    tri = jnp.asarray(np.triu(np.ones((tm, tm)), 1), BF16)

    h = x.reshape(t, d)
    for i in range(depth):
        lam_init = 0.8 - 0.6 * math.exp(-0.3 * i)
        gsec = jnp.stack([
            jnp.tile(g_qa[i], 512 // DA_QK) * (DA_QK ** -0.5 * math.log2(math.e)),
            jnp.tile(g_ka[i], 512 // DA_QK),
            jnp.tile(g_qb[i], 512 // DB_H) * DB_H ** -0.5,
            jnp.tile(g_kb[i], 512 // DB_H),
        ]).astype(F32)
        proj, qvt = _inproj(h, g_mix[i][None, :], w_in[i], bd64, bd32, gsec, cos_t, sin_t, s, tm)
        proj3 = proj.reshape(b, s, proj.shape[1])

        lam4 = jnp.stack([lam_q1[i], lam_k1[i], lam_q2[i], lam_k2[i]]).astype(F32)
        oa = _diffattn(proj3, qvt, lam4, g_sub[i][None, :].astype(F32), lam_init)
        ob = _natten(proj3, _natten_bias_table(rpb[i]))

        w_router = jnp.concatenate(
            [w_rg[i], w_re[i], jnp.zeros((d, LANES - N_GROUPS - N_EXPERTS), F32)], axis=1).astype(F32)
        wr_hi = w_router.astype(BF16)
        wr_lo = (w_router - wr_hi.astype(F32)).astype(BF16)
        h1, xn, meta, meta_t, cnt = _outproj(h, oa.reshape(t, -1), ob.reshape(t, -1), w_out[i].astype(BF16),
                                             g_ffn[i][None, :], jnp.concatenate([wr_hi, wr_lo], axis=1), tri, tm)

        counts = cnt[E_LANE0:E_LANE0 + N_EXPERTS, 0].astype(jnp.int32)
        padded = (counts + MOE_BLOCK - 1) // MOE_BLOCK * MOE_BLOCK
        pad_end = jnp.cumsum(padded)
        pad_start = pad_end - padded
        mt = meta_t[0:2 * TOP_K].astype(jnp.int32)
        e_ids = jnp.arange(N_EXPERTS, dtype=jnp.int32)[None, :, None]
        start_of = jnp.sum(jnp.where(mt[:TOP_K, None, :] == e_ids, pad_start[None, :, None], 0), axis=1)
        dest = start_of + mt[TOP_K:]
        blk_start = jnp.arange(n_blocks, dtype=jnp.int32) * MOE_BLOCK
        block_e = jnp.minimum(jnp.sum(pad_end[None, :] <= blk_start[:, None], axis=1), N_EXPERTS - 1)
        block_e = block_e.astype(jnp.int32)
        n_used = (pad_end[-1:] // MOE_BLOCK).astype(jnp.int32)

        dest_flat = dest.reshape(TOP_K * t)

        xs, row_src = _dispatch(xn, dest_flat, pad_end.astype(jnp.int32), n_used, p_len, tmd)
        in_use = counts > 0
        expert_ord = (jnp.cumsum(in_use) - 1).astype(jnp.int32)
        ids = jnp.arange(N_EXPERTS, dtype=jnp.int32)
        later = jnp.where((ids[None, :] > ids[:, None]) & in_use[None, :], ids[None, :], N_EXPERTS)
        next_expert = jnp.min(later, axis=1)
        next_expert = jnp.where(next_expert == N_EXPERTS, -1, next_expert).astype(jnp.int32)
        y_slots = _experts(xs, row_src.reshape(p_len), block_e, n_used, next_expert, expert_ord,
                           w1[i], w3[i], w2[i], TOP_K * t)
        h = _final(h1, meta, p[i].reshape(t, -1), w_ple[i].astype(BF16), g_ple[i][None, :], g_plg[i][None, :],
                   w_plg[i].astype(BF16), y_slots, tmf)
    return h.reshape(b, s, d)
```

```python
import functools
import math

import numpy as np
import jax
import jax.numpy as jnp
from jax import lax
from jax.experimental import pallas as pl
from jax.experimental.pallas import tpu as pltpu

F32 = jnp.float32
BF16 = jnp.bfloat16
U32 = jnp.uint32

LANES = 128
HA = 4
DA_QK = 64
DA_V = 128
HB = 16
DB_H = 32
GRID_W = 64
WIN_R = 8
WIN_C = 16
ROPE_THETA = 10000.0
EPS = 1e-6
NEG = -1e30
N_GROUPS = 4
EXPERTS_PER_GROUP = 8
N_EXPERTS = N_GROUPS * EXPERTS_PER_GROUP
TOP_K = 2
MOE_BLOCK = 256
E_LANE0 = N_GROUPS
ROUTER_ROWS = 40

VMEM_LIMIT = 56 * 1024 * 1024


def _cparams(sem):
    return pltpu.CompilerParams(dimension_semantics=sem, vmem_limit_bytes=VMEM_LIMIT)


def _rms_rows(x):
    return x * lax.rsqrt(jnp.mean(x * x, axis=-1, keepdims=True) + EPS)


SUBLANES = 8

TOKEN_ROWS = 4
HIGH_HALF = 0xFFFF0000


def _pack_bf16_pairs(x):
    m = x.shape[1] // 2
    bits = pltpu.bitcast(x.astype(BF16).astype(F32), U32)
    return (bits[:, :m] >> 16) | (bits[:, m:] & jnp.uint32(HIGH_HALF))


def _unpack_bf16_pairs(u):
    lo = pltpu.bitcast(u << 16, F32)
    hi = pltpu.bitcast(u & jnp.uint32(HIGH_HALF), F32)
    return jnp.concatenate([lo, hi], axis=1)


def _store_token_tiles(ref, val, first=0):
    n = val.shape[0]
    for c in range(TOKEN_ROWS):
        ref[pl.ds(first * TOKEN_ROWS + c, n, stride=TOKEN_ROWS), :] = val[:, c * LANES:(c + 1) * LANES]


def _load_token_tiles(ref, n, first=0):
    return jnp.concatenate(
        [ref[pl.ds(first * TOKEN_ROWS + c, n, stride=TOKEN_ROWS), :] for c in range(TOKEN_ROWS)], axis=1)


def _inproj_kernel(x_ref, gmix_ref, w_ref, bd64_ref, bd32_ref, gsec_ref, cos_ref, sin_ref, out_ref, qvt_ref, w_scr):
    tm = x_ref.shape[0]

    @pl.when(pl.program_id(0) == 0)
    def _():
        w_scr[...] = w_ref[...].astype(BF16)

    a = (_rms_rows(x_ref[...]) * gmix_ref[...]).astype(BF16)
    lane = lax.broadcasted_iota(jnp.int32, (tm, LANES), 1)
    first_half = (lane % 64) < 32
    cos = cos_ref[...]
    sin = sin_ref[...]
    out_col = {1: 0, 3: 1, 4: 2, 5: 3}
    qvt_row = {0: 0, 2: 1}
    for sec in range(6):
        y = jnp.dot(a, w_scr[:, sec * 512:(sec + 1) * 512], preferred_element_type=F32)
        if sec in (0, 1, 3, 4):
            bd = bd64_ref if sec < 2 else bd32_ref
            gi = {0: 0, 1: 1, 3: 2, 4: 3}[sec]
            y2 = (y * y).astype(BF16)
            ms = jnp.concatenate([jnp.dot(y2[:, half * 256:(half + 1) * 256], bd[...], preferred_element_type=F32)
                                  for half in range(2)], axis=1)
            y = y * lax.rsqrt(ms + EPS) * gsec_ref[gi:gi + 1, :]
        for c in range(4):
            yc = y[:, c * LANES:(c + 1) * LANES]
            if sec < 2:
                rot = jnp.where(first_half, pltpu.roll(yc, 96, 1), pltpu.roll(yc, 32, 1))
                yc = yc * cos + rot * sin
            if sec in qvt_row:
                r0 = qvt_row[sec] * 512 + c * LANES
                qvt_ref[0, r0:r0 + LANES, :] = yc.T.astype(BF16)
            else:
                c0 = out_col[sec] * 512 + c * LANES
                out_ref[:, c0:c0 + LANES] = yc.astype(BF16)


def _inproj(x2, g_mix, w_f32, bd64, bd32, gsec, cos_t, sin_t, seq, tm):
    t, d = x2.shape
    n = w_f32.shape[1]
    nsb = seq // tm
    n_tok_major = 4 * 512
    n_feat_major = 2 * 512
    const = lambda i: (0, 0)
    return pl.pallas_call(
        _inproj_kernel,
        grid=(t // tm,),
        in_specs=[
            pl.BlockSpec((tm, d), lambda i: (i, 0)),
            pl.BlockSpec((1, d), const),
            pl.BlockSpec((d, n), const, pipeline_mode=pl.Buffered(1)),
            pl.BlockSpec((256, 256), const),
            pl.BlockSpec((256, 256), const),
            pl.BlockSpec((4, 512), const),
            pl.BlockSpec((tm, LANES), lambda i: (i % nsb, 0)),
            pl.BlockSpec((tm, LANES), lambda i: (i % nsb, 0)),
        ],
        out_specs=[pl.BlockSpec((tm, n_tok_major), lambda i: (i, 0)),
                   pl.BlockSpec((1, n_feat_major, tm), lambda i: (i // nsb, 0, i % nsb))],
        out_shape=[jax.ShapeDtypeStruct((t, n_tok_major), BF16),
                   jax.ShapeDtypeStruct((t // seq, n_feat_major, seq), BF16)],
        scratch_shapes=[pltpu.VMEM((d, n), BF16)],
        compiler_params=_cparams(("arbitrary",)),
        name="inproj",
    )(x2, g_mix, w_f32, bd64, bd32, gsec, cos_t, sin_t)


QBLOCK_PAIRS_PER_BODY = 4

def _diffattn_kernel(lam_ref, gsub_ref, qt_ref, k_ref, vt_ref, o_ref, sa_scr, sb_scr, *, seq, tq, tk, lam_init):
    nq = seq // tq
    nk = seq // tk
    s1 = jnp.sum(lam_ref[0:1, :] * lam_ref[1:2, :], axis=1, keepdims=True)
    s2 = jnp.sum(lam_ref[2:3, :] * lam_ref[3:4, :], axis=1, keepdims=True)
    lam = jnp.exp(s1) - jnp.exp(s2) + lam_init

    row = lax.broadcasted_iota(jnp.int32, (LANES, tq), 0)

    def scores(qi, s_scr):
        qoff = pl.multiple_of(qi * tq, tq)
        qt = qt_ref[0, :, pl.ds(qoff, tq)]
        zero = jnp.zeros_like(qt)
        qm = jnp.concatenate([jnp.where(row < DA_QK, qt, zero), jnp.where(row >= DA_QK, qt, zero)], axis=1)
        mx = None
        for j in range(nk):
            st = jnp.dot(k_ref[0, j * tk:(j + 1) * tk, :], qm, preferred_element_type=F32)
            s_scr[j * tk:(j + 1) * tk, :] = st
            cm = jnp.max(st, axis=0, keepdims=True)
            mx = cm if mx is None else jnp.maximum(mx, cm)
        return mx

    def finish(qi, s_scr, m):
        l = jnp.zeros((1, 2 * tq), F32)
        acc = jnp.zeros((DA_V, 2 * tq), F32)
        for j in range(nk):
            p = jnp.exp2(s_scr[j * tk:(j + 1) * tk, :] - m)
            l = l + jnp.sum(p, axis=0, keepdims=True)
            acc = acc + jnp.dot(vt_ref[0, :, j * tk:(j + 1) * tk], p.astype(BF16), preferred_element_type=F32)
        o = acc * (1.0 / l)
        ot = o[:, :tq] - lam * o[:, tq:]
        ot = ot * lax.rsqrt(jnp.mean(ot * ot, axis=0, keepdims=True) + EPS) * (1.0 - lam_init)
        qoff = pl.multiple_of(qi * tq, tq)
        o_ref[0, pl.ds(qoff, tq), :] = (ot.T * gsub_ref[...]).astype(BF16)

    def body(i, m_a):
        for pair in range(QBLOCK_PAIRS_PER_BODY):
            q0 = 2 * (QBLOCK_PAIRS_PER_BODY * i + pair)
            m_b = scores(q0 + 1, sb_scr)
            finish(q0, sa_scr, m_a)
            m_a = scores(jnp.minimum(q0 + 2, nq - 1), sa_scr)
            finish(q0 + 1, sb_scr, m_b)
        return m_a

    lax.fori_loop(0, nq // (2 * QBLOCK_PAIRS_PER_BODY), body, scores(0, sa_scr))


def _diffattn(proj3, qvt, lam4, g_sub, lam_init, tq=128, tk=512):
    b, seq, _ = proj3.shape
    kern = functools.partial(_diffattn_kernel, seq=seq, tq=tq, tk=tk, lam_init=lam_init)
    return pl.pallas_call(
        kern,
        grid=(b, HA),
        in_specs=[
            pl.BlockSpec((4, DA_QK), lambda i, h: (0, 0)),
            pl.BlockSpec((1, DA_V), lambda i, h: (0, 0)),
            pl.BlockSpec((1, LANES, seq), lambda i, h: (i, h, 0)),
            pl.BlockSpec((1, seq, LANES), lambda i, h: (i, 0, h)),
            pl.BlockSpec((1, LANES, seq), lambda i, h: (i, HA + h, 0)),
        ],
        out_specs=pl.BlockSpec((1, seq, LANES), lambda i, h: (i, 0, h)),
        out_shape=jax.ShapeDtypeStruct((b, seq, HA * DA_V), BF16),
        scratch_shapes=[pltpu.VMEM((seq, 2 * tq), F32), pltpu.VMEM((seq, 2 * tq), F32)],
        compiler_params=_cparams(("parallel", "parallel")),
        name="diffattn",
    )(lam4, g_sub, qvt, proj3, qvt)


def _natten_kernel(bias_ref, q_ref, k_ref, v_ref, o_ref, *, rows):
    kr = min(WIN_R, rows)
    nkeys = kr * GRID_W
    hpg = LANES // DB_H
    lane_head = lax.broadcasted_iota(jnp.int32, (GRID_W, LANES), 1) // DB_H

    def row_step(r, carry):
        rs = jnp.clip(r - kr // 2, 0, rows - kr)
        base = rs - r + (WIN_R - 1)
        qoff = pl.multiple_of(r * GRID_W, GRID_W)
        koff = pl.multiple_of(rs * GRID_W, GRID_W)
        qr = q_ref[0, pl.ds(qoff, GRID_W), :]
        zero = jnp.zeros_like(qr)
        qs = jnp.concatenate([jnp.where(lane_head == h, qr, zero) for h in range(hpg)], axis=0)
        kw = k_ref[0, pl.ds(koff, nkeys), :]
        vw = v_ref[0, pl.ds(koff, nkeys), :]
        s = lax.dot_general(qs, kw, (((1,), (1,)), ((), ())), preferred_element_type=F32)
        bias = jnp.concatenate([bias_ref[0, base + 2 * i] for i in range(kr // 2)], axis=1)
        s = s + bias
        m = jnp.max(s, axis=1, keepdims=True)
        p = jnp.exp(s - m).astype(BF16)
        pv = jnp.dot(p, jnp.concatenate([vw, jnp.ones_like(vw)], axis=1), preferred_element_type=F32)
        o = pv[:, :LANES] * (1.0 / pv[:, LANES:])
        out = jnp.zeros((GRID_W, LANES), F32)
        for h in range(hpg):
            out = out + jnp.where(lane_head == h, o[h * GRID_W:(h + 1) * GRID_W, :], 0.0)
        o_ref[0, pl.ds(qoff, GRID_W), :] = out.astype(BF16)
        return carry

    lax.fori_loop(0, rows, row_step, 0, unroll=min(rows, 64))


def _natten(proj3, bias_tab):
    b, seq, _ = proj3.shape
    rows = seq // GRID_W
    ng = HB * DB_H // LANES
    nro = bias_tab.shape[1]
    kern = functools.partial(_natten_kernel, rows=rows)
    cb = HA
    return pl.pallas_call(
        kern,
        grid=(b, ng),
        in_specs=[
            pl.BlockSpec((1, nro, bias_tab.shape[2], LANES), lambda i, g: (g, 0, 0, 0)),
            pl.BlockSpec((1, seq, LANES), lambda i, g: (i, 0, cb + g)),
            pl.BlockSpec((1, seq, LANES), lambda i, g: (i, 0, cb + ng + g)),
            pl.BlockSpec((1, seq, LANES), lambda i, g: (i, 0, cb + 2 * ng + g)),
        ],
        out_specs=pl.BlockSpec((1, seq, LANES), lambda i, g: (i, 0, g)),
        out_shape=jax.ShapeDtypeStruct((b, seq, HB * DB_H), BF16),
        compiler_params=_cparams(("parallel", "parallel")),
        name="natten",
    )(bias_tab, proj3, proj3, proj3)


def _natten_bias_table(rpb):
    q = np.arange(GRID_W)[:, None]
    kc = np.arange(GRID_W)[None, :]
    cs = np.clip(q - WIN_C // 2, 0, GRID_W - WIN_C)
    valid = (kc >= cs) & (kc < cs + WIN_C)
    col_off = np.clip(kc - q + (WIN_C - 1), 0, 2 * WIN_C - 2)
    ncol = 2 * WIN_C - 1
    nro = 2 * WIN_R - 2
    pick = np.zeros((2, ncol, GRID_W, 2 * GRID_W), np.float32)
    for j in range(2):
        pick[j, :, :, j * GRID_W:(j + 1) * GRID_W] = (np.arange(ncol)[:, None, None] == col_off[None]) & valid[None]
    negmask = np.tile(np.where(valid, 0.0, NEG).astype(np.float32), (1, 2))
    hpg = LANES // DB_H
    r4 = rpb.astype(F32).reshape(HB // hpg, hpg, 2 * WIN_R - 1, ncol)
    rp = jnp.stack([r4[:, :, :nro], r4[:, :, 1:]], axis=0)
    tab = jnp.einsum('jghrc,jcqk->grhqk', rp, jnp.asarray(pick), precision=lax.Precision.HIGHEST) + negmask
    return tab.reshape(HB // hpg, nro, hpg * GRID_W, LANES)


def _outproj_kernel(x_ref, oa_ref, ob_ref, wo_ref, gffn_ref, wr_ref, tri_ref,
                    h_ref, xn_ref, meta_ref, meta_t_ref, cnt_ref, run_scr, wo_scr):
    tm = x_ref.shape[0]
    da = oa_ref.shape[1]

    @pl.when(pl.program_id(0) == 0)
    def _():
        run_scr[...] = jnp.zeros_like(run_scr)
        wo_scr[...] = wo_ref[...].astype(BF16)

    attn = jnp.dot(oa_ref[...], wo_scr[:da, :], preferred_element_type=F32)
    attn = attn + jnp.dot(ob_ref[...], wo_scr[da:, :], preferred_element_type=F32)
    h = x_ref[...] + attn
    h_ref[...] = h
    xn = _rms_rows(h) * gffn_ref[...]
    _store_token_tiles(xn_ref, _pack_bf16_pairs(xn))

    xh = xn.astype(BF16)
    xl = (xn - xh.astype(F32)).astype(BF16)
    hh_hl = jnp.dot(xh, wr_ref[...], preferred_element_type=F32)
    lg = hh_hl[:, :LANES] + hh_hl[:, LANES:] + jnp.dot(xl, wr_ref[:, :LANES], preferred_element_type=F32)
    lt = lg.T
    row = [lt[j:j + 1, :] for j in range(E_LANE0 + N_EXPERTS)]

    def first_argmax(vals):
        best = vals[0]
        for v in vals[1:]:
            best = jnp.maximum(best, v)
        idx = jnp.full_like(best, float(len(vals) - 1))
        for j in range(len(vals) - 2, -1, -1):
            idx = jnp.where(vals[j] == best, float(j), idx)
        return best, idx

    gmax, gidx = first_argmax(row[:N_GROUPS])
    gsum = jnp.exp(row[0] - gmax)
    for j in range(1, N_GROUPS):
        gsum = gsum + jnp.exp(row[j] - gmax)
    g_gate = 1.0 / gsum
    el = []
    for k in range(EXPERTS_PER_GROUP):
        v = row[E_LANE0 + k]
        for g in range(1, N_GROUPS):
            v = jnp.where(gidx == float(g), row[E_LANE0 + g * EXPERTS_PER_GROUP + k], v)
        el.append(v)
    l1, k1 = first_argmax(el)
    l2, k2 = first_argmax([jnp.where(k1 == float(k), -jnp.inf, el[k]) for k in range(EXPERTS_PER_GROUP)])
    r = jnp.exp(l2 - l1)
    w1 = g_gate / (1.0 + r)
    w2 = w1 * r
    e1 = gidx * EXPERTS_PER_GROUP + k1
    e2 = gidx * EXPERTS_PER_GROUP + k2

    rowid = lax.broadcasted_iota(jnp.int32, (ROUTER_ROWS, tm), 0).astype(F32)
    sel1 = rowid == e1 + E_LANE0
    sel2 = rowid == e2 + E_LANE0
    oh = jnp.where(sel1 | sel2, 1.0, 0.0).astype(BF16)
    run = run_scr[...]
    before = jnp.dot(oh, tri_ref[...], preferred_element_type=F32) + jnp.concatenate([run] * (tm // LANES), axis=1)
    rank1 = jnp.sum(jnp.where(sel1, before, 0.0), axis=0, keepdims=True)
    rank2 = jnp.sum(jnp.where(sel2, before, 0.0), axis=0, keepdims=True)
    run = run + jnp.dot(oh, jnp.ones((tm, LANES), BF16), preferred_element_type=F32)
    run_scr[...] = run
    cnt_ref[...] = run

    meta_t = jnp.concatenate([e1, e2, rank1, rank2, w1, w2, jnp.zeros((SUBLANES - 6, tm), F32)], axis=0)
    meta_t_ref[...] = meta_t
    meta_ref[...] = jnp.concatenate([meta_t, jnp.zeros((LANES - SUBLANES, tm), F32)], axis=0).T


def _outproj(x2, oa2, ob2, w_out, g_ffn, w_router, tri, tm):
    t, d = x2.shape
    da = oa2.shape[1]
    const = lambda i: (0, 0)
    row = lambda i: (i, 0)
    return pl.pallas_call(
        _outproj_kernel,
        grid=(t // tm,),
        in_specs=[
            pl.BlockSpec((tm, d), row),
            pl.BlockSpec((tm, da), row),
            pl.BlockSpec((tm, d - da), row),
            pl.BlockSpec((d, d), const, pipeline_mode=pl.Buffered(1)),
            pl.BlockSpec((1, d), const),
            pl.BlockSpec((d, 2 * LANES), const),
            pl.BlockSpec((tm, tm), const),
        ],
        out_specs=[
            pl.BlockSpec((tm, d), row),
            pl.BlockSpec((tm * TOKEN_ROWS, LANES), row),
            pl.BlockSpec((tm, LANES), row),
            pl.BlockSpec((SUBLANES, tm), lambda i: (0, i)),
            pl.BlockSpec((ROUTER_ROWS, LANES), const),
        ],
        out_shape=[
            jax.ShapeDtypeStruct((t, d), F32),
            jax.ShapeDtypeStruct((t * TOKEN_ROWS, LANES), U32),
            jax.ShapeDtypeStruct((t, LANES), F32),
            jax.ShapeDtypeStruct((SUBLANES, t), F32),
            jax.ShapeDtypeStruct((ROUTER_ROWS, LANES), F32),
        ],
        scratch_shapes=[pltpu.VMEM((ROUTER_ROWS, LANES), F32), pltpu.VMEM((d, d), BF16)],
        compiler_params=_cparams(("arbitrary",)),
        name="outproj_router",
    )(x2, oa2, ob2, w_out, g_ffn, w_router, tri)


DMA_ISSUE_UNROLL = 8
DMA_PRIORITIES = 2


def _dispatch_kernel(dest_ref, pend_ref, nu_ref, xn_ref, xs_ref, src_ref, zbuf, ibuf, sem, zsem):
    tm = xn_ref.shape[0] // TOKEN_ROWS
    t_total = pl.num_programs(0) * tm
    base = pl.program_id(0) * tm
    n_blocks = xs_ref.shape[0] // (MOE_BLOCK * TOKEN_ROWS)

    @pl.when(pl.program_id(0) == 0)
    def _():
        ibuf[...] = jnp.full(ibuf.shape, -1, jnp.int32)
        fill = pltpu.make_async_copy(ibuf, src_ref, zsem)
        fill.start()
        fill.wait()
        zbuf[...] = jnp.zeros_like(zbuf)

        def zero_block(blk):
            off = pl.multiple_of(blk * (MOE_BLOCK * TOKEN_ROWS), MOE_BLOCK * TOKEN_ROWS)
            return pltpu.make_async_copy(zbuf, xs_ref.at[pl.ds(off, MOE_BLOCK * TOKEN_ROWS), :], zsem)

        def for_each_zeroed_block(fn):
            def per_expert(e, c):
                end = pend_ref[e]
                start = jnp.where(e == 0, 0, pend_ref[jnp.maximum(e - 1, 0)])

                @pl.when(end > start)
                def _():
                    fn(zero_block(end // MOE_BLOCK - 1))
                return c
            lax.fori_loop(0, N_EXPERTS, per_expert, 0)

            def per_tail_block(blk, c):
                fn(zero_block(blk))
                return c
            lax.fori_loop(nu_ref[0], n_blocks, per_tail_block, 0)

        for_each_zeroed_block(lambda cp: cp.start())
        for_each_zeroed_block(lambda cp: cp.wait())

    def issue(group, c):
        toks = [group * DMA_ISSUE_UNROLL + g for g in range(DMA_ISSUE_UNROLL)]
        dests = [[dest_ref[slot * t_total + base + r] for slot in range(TOP_K)] for r in toks]
        for r, ds in zip(toks, dests):
            src = xn_ref.at[pl.ds(pl.multiple_of(r * TOKEN_ROWS, TOKEN_ROWS), TOKEN_ROWS), :]
            for slot, d in enumerate(ds):
                dst = xs_ref.at[pl.ds(pl.multiple_of(d * TOKEN_ROWS, TOKEN_ROWS), TOKEN_ROWS), :]
                pltpu.make_async_copy(src, dst, sem).start(priority=slot % DMA_PRIORITIES)
        for r, ds in zip(toks, dests):
            for slot, d in enumerate(ds):
                src_ref[d] = slot * t_total + base + r
        return c
    lax.fori_loop(0, tm // DMA_ISSUE_UNROLL, issue, 0)

    for slot in range(TOP_K):
        pltpu.make_async_copy(xn_ref, xs_ref.at[pl.ds(0, tm * TOKEN_ROWS), :], sem).wait()


def _dispatch(xn_tiles, dest_flat, pad_end, n_used, p_len, tm):
    t = xn_tiles.shape[0] // TOKEN_ROWS
    grid_spec = pltpu.PrefetchScalarGridSpec(
        num_scalar_prefetch=3,
        grid=(t // tm,),
        in_specs=[pl.BlockSpec((tm * TOKEN_ROWS, LANES), lambda i, dest, pend, nu: (i, 0))],
        out_specs=[pl.BlockSpec(memory_space=pl.ANY), pl.BlockSpec(memory_space=pltpu.SMEM)],
        scratch_shapes=[pltpu.VMEM((MOE_BLOCK * TOKEN_ROWS, LANES), U32),
                        pltpu.VMEM((p_len,), jnp.int32),
                        pltpu.SemaphoreType.DMA(()), pltpu.SemaphoreType.DMA(())],
    )
    return pl.pallas_call(
        _dispatch_kernel,
        grid_spec=grid_spec,
        out_shape=[jax.ShapeDtypeStruct((p_len * TOKEN_ROWS, LANES), U32),
                   jax.ShapeDtypeStruct((p_len,), jnp.int32)],
        compiler_params=_cparams(("arbitrary",)),
        name="dispatch",
    )(dest_flat, pad_end, n_used, xn_tiles)


def _experts_kernel(be_ref, nu_ref, nxt_ref, ord_ref, src_ref, xs_ref, w1_hbm, w3_hbm, w2_hbm, y2_ref,
                    w1_scr, w3_scr, w2_scr, w1_buf, w3_buf, w2_buf, ybuf, wsems, ysems):
    step = pl.program_id(0)
    first = 2 * step
    second = first + 1
    n_used = nu_ref[0]
    cur = step % 2
    pair_rows = 2 * MOE_BLOCK
    n_slot_rows = y2_ref.shape[0] // TOKEN_ROWS - 2 * pair_rows

    def send_previous(unrolled):
        prev_base = jnp.maximum(step - 1, 0) * pair_rows

        def send(r, priority):
            s = src_ref[prev_base + r]
            row = jnp.where((step > 0) & (s >= 0), s, n_slot_rows + cur * pair_rows + r)
            src = ybuf.at[1 - cur, pl.ds(pl.multiple_of(r * TOKEN_ROWS, TOKEN_ROWS), TOKEN_ROWS), :]
            dst = y2_ref.at[pl.ds(pl.multiple_of(row * TOKEN_ROWS, TOKEN_ROWS), TOKEN_ROWS), :]
            pltpu.make_async_copy(src, dst, ysems.at[cur]).start(priority=priority)

        if unrolled:
            for r in range(pair_rows):
                send(r, r % DMA_PRIORITIES)
        else:
            def send_group(j, c):
                for k in range(DMA_PRIORITIES):
                    send(j * DMA_PRIORITIES + k, k)
                return c
            lax.fori_loop(0, pair_rows // DMA_PRIORITIES, send_group, 0, unroll=DMA_ISSUE_UNROLL // DMA_PRIORITIES)

    def all_rows_sent(buf, sem):
        return pltpu.make_async_copy(ybuf.at[buf], y2_ref.at[pl.ds(0, pair_rows * TOKEN_ROWS), :], sem)

    def wait_older_sends():
        all_rows_sent(cur, ysems.at[1 - cur]).wait()

    @pl.when(step == 0)
    def _():
        ybuf[1] = jnp.zeros(ybuf.shape[1:], U32)
        odd_dump = (n_slot_rows + pair_rows) * TOKEN_ROWS
        pltpu.make_async_copy(ybuf.at[1], y2_ref.at[pl.ds(odd_dump, pair_rows * TOKEN_ROWS), :], ysems.at[1]).start()

    def weight_copies(e, slot):
        return (pltpu.make_async_copy(w1_hbm.at[e], w1_buf.at[slot], wsems.at[slot, 0]),
                pltpu.make_async_copy(w3_hbm.at[e], w3_buf.at[slot], wsems.at[slot, 1]),
                pltpu.make_async_copy(w2_hbm.at[e], w2_buf.at[slot], wsems.at[slot, 2]))

    def switch_to(e):
        slot = ord_ref[e] % 2
        for cp in weight_copies(e, slot):
            cp.wait()
        nxt = nxt_ref[e]

        @pl.when(nxt >= 0)
        def _():
            for cp in weight_copies(nxt, 1 - slot):
                cp.start()
        w1_scr[...] = w1_buf[slot].astype(BF16)
        w3_scr[...] = w3_buf[slot].astype(BF16)
        w2_scr[...] = w2_buf[slot].astype(BF16)

    def gated_mlp(row0, n_rows, first_store):
        x = _unpack_bf16_pairs(_load_token_tiles(xs_ref, n_rows, row0)).astype(BF16)
        a = jnp.dot(x, w1_scr[...], preferred_element_type=F32)
        b = jnp.dot(x, w3_scr[...], preferred_element_type=F32)
        hdn = a * (1.0 / (1.0 + jnp.exp(-a))) * b
        y = jnp.dot(hdn.astype(BF16), w2_scr[...], preferred_element_type=F32)
        if first_store:
            wait_older_sends()
        _store_token_tiles(ybuf.at[cur], _pack_bf16_pairs(y), row0)

    def zero_rows(row0, n_rows):
        ybuf[cur, row0 * TOKEN_ROWS:(row0 + n_rows) * TOKEN_ROWS, :] = jnp.zeros((n_rows * TOKEN_ROWS, LANES), U32)

    @pl.when(first < n_used)
    def _():
        e_first = be_ref[first]
        e_before = be_ref[jnp.maximum(first - 1, 0)]

        @pl.when(first == 0)
        def _():
            for cp in weight_copies(e_first, ord_ref[e_first] % 2):
                cp.start()

        @pl.when((first == 0) | (e_first != e_before))
        def _():
            switch_to(e_first)

        second_used = second < n_used
        e_second = be_ref[second]
        same_expert = second_used & (e_second == e_first)

        @pl.when(same_expert)
        def _():
            send_previous(unrolled=True)
            gated_mlp(0, 2 * MOE_BLOCK, first_store=True)

        @pl.when(jnp.logical_not(same_expert))
        def _():
            send_previous(unrolled=True)
            gated_mlp(0, MOE_BLOCK, first_store=True)

            @pl.when(second_used)
            def _():
                switch_to(e_second)
                gated_mlp(MOE_BLOCK, MOE_BLOCK, first_store=False)

            @pl.when(jnp.logical_not(second_used))
            def _():
                zero_rows(MOE_BLOCK, MOE_BLOCK)

    @pl.when(first >= n_used)
    def _():
        send_previous(unrolled=False)
        wait_older_sends()
        zero_rows(0, 2 * MOE_BLOCK)

    @pl.when(step == pl.num_programs(0) - 1)
    def _():
        all_rows_sent(1 - cur, ysems.at[cur]).wait()


def _experts(xs, row_src, block_e, n_used, next_expert, expert_ord, w1, w3, w2, n_slot_rows):
    p_len = xs.shape[0] // TOKEN_ROWS
    n_blocks = p_len // MOE_BLOCK
    assert n_blocks % 2 == 0
    n_pairs = n_blocks // 2
    d, de = w1.shape[1], w1.shape[2]
    blk = lambda i, be, nu, nxt, od, src: (jnp.maximum(jnp.minimum(i, (nu[0] - 1) // 2), 0), 0)
    hbm = pl.BlockSpec(memory_space=pl.ANY)
    grid_spec = pltpu.PrefetchScalarGridSpec(
        num_scalar_prefetch=5,
        grid=(n_pairs + 1,),
        in_specs=[pl.BlockSpec((2 * MOE_BLOCK * TOKEN_ROWS, LANES), blk), hbm, hbm, hbm],
        out_specs=hbm,
        scratch_shapes=[pltpu.VMEM((d, de), BF16), pltpu.VMEM((d, de), BF16), pltpu.VMEM((de, d), BF16),
                        pltpu.VMEM((2, d, de), F32), pltpu.VMEM((2, d, de), F32), pltpu.VMEM((2, de, d), F32),
                        pltpu.VMEM((2, 2 * MOE_BLOCK * TOKEN_ROWS, LANES), U32),
                        pltpu.SemaphoreType.DMA((2, 3)), pltpu.SemaphoreType.DMA((2,))],
    )
    return pl.pallas_call(
        _experts_kernel,
        grid_spec=grid_spec,
        out_shape=jax.ShapeDtypeStruct(((n_slot_rows + 4 * MOE_BLOCK) * TOKEN_ROWS, LANES), U32),
        compiler_params=_cparams(("arbitrary",)),
        name="experts",
    )(block_e, n_used, next_expert, expert_ord, row_src, xs, w1, w3, w2)


def _final_kernel(h_ref, meta_ref, p_ref, wple_ref, gple_ref, gplg_ref, wplg_ref, y_first_ref, y_second_ref, out_ref,
                  wple_scr, wplg_scr):
    tm = h_ref.shape[0]

    @pl.when(pl.program_id(0) == 0)
    def _():
        wple_scr[...] = wple_ref[...].astype(BF16)
        wplg_scr[...] = wplg_ref[...].astype(BF16)

    emb = jnp.dot(p_ref[...].astype(BF16), wple_scr[...], preferred_element_type=F32)
    emb = _rms_rows(emb) * gple_ref[...]
    meta = meta_ref[...]
    y1 = _unpack_bf16_pairs(_load_token_tiles(y_first_ref, tm))
    y2 = _unpack_bf16_pairs(_load_token_tiles(y_second_ref, tm))
    h = h_ref[...] + meta[:, 4:5] * y1 + meta[:, 5:6] * y2
    hn = (_rms_rows(h) * gplg_ref[...]).astype(BF16)
    z = jnp.dot(hn, wplg_scr[...], preferred_element_type=F32)
    out_ref[...] = h + emb * (1.0 / (1.0 + jnp.exp(-z)))


def _final(h2, meta, p2, w_ple, g_ple, g_plg, w_plg, y_slots, tm):
    t, d = h2.shape
    dp = p2.shape[1]
    nt = t // tm
    const = lambda i: (0, 0)
    row = lambda i: (i, 0)
    return pl.pallas_call(
        _final_kernel,
        grid=(nt,),
        in_specs=[
            pl.BlockSpec((tm, d), row),
            pl.BlockSpec((tm, LANES), row),
            pl.BlockSpec((tm, dp), row),
            pl.BlockSpec((dp, d), const, pipeline_mode=pl.Buffered(1)),
            pl.BlockSpec((1, d), const),
            pl.BlockSpec((1, d), const),
            pl.BlockSpec((d, d), const, pipeline_mode=pl.Buffered(1)),
            pl.BlockSpec((tm * TOKEN_ROWS, LANES), row),
            pl.BlockSpec((tm * TOKEN_ROWS, LANES), lambda i: (nt + i, 0)),
        ],
        out_specs=pl.BlockSpec((tm, d), row),
        out_shape=jax.ShapeDtypeStruct((t, d), F32),
        scratch_shapes=[pltpu.VMEM((dp, d), BF16), pltpu.VMEM((d, d), BF16)],
        compiler_params=_cparams(("arbitrary",)),
        name="combine_ple",
    )(h2, meta, p2, w_ple, g_ple, g_plg, w_plg, y_slots, y_slots)


def _block_diag_mean(seg):
    m = np.kron(np.eye(256 // seg), np.full((seg, seg), 1.0 / seg))
    return jnp.asarray(m, BF16)


def _rope_tables(seq):
    f32 = np.float32
    inv = np.power(f32(ROPE_THETA), -np.arange(0, DA_QK, 2, dtype=f32) / f32(DA_QK)).astype(f32)
    ang = (np.arange(seq, dtype=f32)[:, None] * inv[None, :]).astype(f32)
    cos, sin = np.cos(ang).astype(f32), np.sin(ang).astype(f32)
    cos_t = np.tile(cos, (1, LANES // cos.shape[1]))
    sin_t = np.tile(np.concatenate([-sin, sin], axis=1), (1, LANES // (2 * sin.shape[1])))
    return jnp.asarray(cos_t), jnp.asarray(sin_t)


def _tile_for(n, pref):
    while n % pref:
        pref //= 2
    return pref


def kernel(x, p, g_mix, w_in, g_qa, g_ka, lam_q1, lam_k1, lam_q2, lam_k2, g_sub, g_qb, g_kb, rpb, w_out,
           g_ffn, w_rg, w_re, w1, w3, w2, g_plg, w_plg, w_ple, g_ple):
    b, s, d = x.shape
    assert d == 2 * TOKEN_ROWS * LANES, "token-tile layout packs one token into TOKEN_ROWS rows of 128 uint32"
    t = b * s
    depth = w_in.shape[0]
    tm = _tile_for(s, 512)
    tmd = _tile_for(t, 512)
    tmf = _tile_for(t, 512)
    n_slots = t * TOP_K
    n_blocks = n_slots // MOE_BLOCK + N_EXPERTS
    p_len = n_blocks * MOE_BLOCK

    bd64 = _block_diag_mean(DA_QK)
    bd32 = _block_diag_mean(DB_H)
    cos_t, sin_t = _rope_tables(s)
    tri = jnp.asarray(np.triu(np.ones((tm, tm)), 1), BF16)

    h = x.reshape(t, d)
    for i in range(depth):
        lam_init = 0.8 - 0.6 * math.exp(-0.3 * i)
        gsec = jnp.stack([
            jnp.tile(g_qa[i], 512 // DA_QK) * (DA_QK ** -0.5 * math.log2(math.e)),
            jnp.tile(g_ka[i], 512 // DA_QK),
            jnp.tile(g_qb[i], 512 // DB_H) * DB_H ** -0.5,
            jnp.tile(g_kb[i], 512 // DB_H),
        ]).astype(F32)
        proj, qvt = _inproj(h, g_mix[i][None, :], w_in[i], bd64, bd32, gsec, cos_t, sin_t, s, tm)
        proj3 = proj.reshape(b, s, proj.shape[1])

        lam4 = jnp.stack([lam_q1[i], lam_k1[i], lam_q2[i], lam_k2[i]]).astype(F32)
        oa = _diffattn(proj3, qvt, lam4, g_sub[i][None, :].astype(F32), lam_init)
        ob = _natten(proj3, _natten_bias_table(rpb[i]))

        w_router = jnp.concatenate(
            [w_rg[i], w_re[i], jnp.zeros((d, LANES - N_GROUPS - N_EXPERTS), F32)], axis=1).astype(F32)
        wr_hi = w_router.astype(BF16)
        wr_lo = (w_router - wr_hi.astype(F32)).astype(BF16)
        h1, xn, meta, meta_t, cnt = _outproj(h, oa.reshape(t, -1), ob.reshape(t, -1), w_out[i],
                                             g_ffn[i][None, :], jnp.concatenate([wr_hi, wr_lo], axis=1), tri, tm)

        counts = cnt[E_LANE0:E_LANE0 + N_EXPERTS, 0].astype(jnp.int32)
        padded = (counts + MOE_BLOCK - 1) // MOE_BLOCK * MOE_BLOCK
        pad_end = jnp.cumsum(padded)
        pad_start = pad_end - padded
        mt = meta_t[0:2 * TOP_K].astype(jnp.int32)
        e_ids = jnp.arange(N_EXPERTS, dtype=jnp.int32)[None, :, None]
        start_of = jnp.sum(jnp.where(mt[:TOP_K, None, :] == e_ids, pad_start[None, :, None], 0), axis=1)
        dest = start_of + mt[TOP_K:]
        blk_start = jnp.arange(n_blocks, dtype=jnp.int32) * MOE_BLOCK
        block_e = jnp.minimum(jnp.sum(pad_end[None, :] <= blk_start[:, None], axis=1), N_EXPERTS - 1)
        block_e = block_e.astype(jnp.int32)
        n_used = (pad_end[-1:] // MOE_BLOCK).astype(jnp.int32)

        dest_flat = dest.reshape(TOP_K * t)

        xs, row_src = _dispatch(xn, dest_flat, pad_end.astype(jnp.int32), n_used, p_len, tmd)
        in_use = counts > 0
        expert_ord = (jnp.cumsum(in_use) - 1).astype(jnp.int32)
        ids = jnp.arange(N_EXPERTS, dtype=jnp.int32)
        later = jnp.where((ids[None, :] > ids[:, None]) & in_use[None, :], ids[None, :], N_EXPERTS)
        next_expert = jnp.min(later, axis=1)
        next_expert = jnp.where(next_expert == N_EXPERTS, -1, next_expert).astype(jnp.int32)
        y_slots = _experts(xs, row_src.reshape(p_len), block_e, n_used, next_expert, expert_ord,
                           w1[i], w3[i], w2[i], TOP_K * t)
        h = _final(h1, meta, p[i].reshape(t, -1), w_ple[i], g_ple[i][None, :], g_plg[i][None, :], w_plg[i],
                   y_slots, tmf)
    return h.reshape(b, s, d)
```

```python
import functools
import math

import numpy as np
import jax
import jax.numpy as jnp
from jax import lax
from jax.experimental import pallas as pl
from jax.experimental.pallas import tpu as pltpu

F32 = jnp.float32
BF16 = jnp.bfloat16
U32 = jnp.uint32

LANES = 128
HA = 4
DA_QK = 64
DA_V = 128
HB = 16
DB_H = 32
GRID_W = 64
WIN_R = 8
WIN_C = 16
ROPE_THETA = 10000.0
EPS = 1e-6
NEG = -1e30
N_GROUPS = 4
EXPERTS_PER_GROUP = 8
N_EXPERTS = N_GROUPS * EXPERTS_PER_GROUP
TOP_K = 2
MOE_BLOCK = 256
E_LANE0 = N_GROUPS
ROUTER_ROWS = 40

VMEM_LIMIT = 56 * 1024 * 1024


def _cparams(sem):
    return pltpu.CompilerParams(dimension_semantics=sem, vmem_limit_bytes=VMEM_LIMIT)


def _rms_rows(x):
    return x * lax.rsqrt(jnp.mean(x * x, axis=-1, keepdims=True) + EPS)


SUBLANES = 8

TOKEN_ROWS = 4
HIGH_HALF = 0xFFFF0000


def _pack_bf16_pairs(x):
    m = x.shape[1] // 2
    bits = pltpu.bitcast(x.astype(BF16).astype(F32), U32)
    return (bits[:, :m] >> 16) | (bits[:, m:] & jnp.uint32(HIGH_HALF))


def _unpack_bf16_pairs(u):
    lo = pltpu.bitcast(u << 16, F32)
    hi = pltpu.bitcast(u & jnp.uint32(HIGH_HALF), F32)
    return jnp.concatenate([lo, hi], axis=1)


def _store_token_tiles(ref, val, first=0):
    n = val.shape[0]
    for c in range(TOKEN_ROWS):
        ref[pl.ds(first * TOKEN_ROWS + c, n, stride=TOKEN_ROWS), :] = val[:, c * LANES:(c + 1) * LANES]


def _load_token_tiles(ref, n, first=0):
    return jnp.concatenate(
        [ref[pl.ds(first * TOKEN_ROWS + c, n, stride=TOKEN_ROWS), :] for c in range(TOKEN_ROWS)], axis=1)


def _inproj_kernel(x_ref, gmix_ref, w_ref, bd64_ref, bd32_ref, gsec_ref, cos_ref, sin_ref, out_ref, qvt_ref, w_scr):
    tm = x_ref.shape[0]

    @pl.when(pl.program_id(0) == 0)
    def _():
        w_scr[...] = w_ref[...].astype(BF16)

    a = (_rms_rows(x_ref[...]) * gmix_ref[...]).astype(BF16)
    lane = lax.broadcasted_iota(jnp.int32, (tm, LANES), 1)
    first_half = (lane % 64) < 32
    cos = cos_ref[...]
    sin = sin_ref[...]
    out_col = {1: 0, 3: 1, 4: 2, 5: 3}
    qvt_row = {0: 0, 2: 1}
    for sec in range(6):
        y = jnp.dot(a, w_scr[:, sec * 512:(sec + 1) * 512], preferred_element_type=F32)
        if sec in (0, 1, 3, 4):
            bd = bd64_ref if sec < 2 else bd32_ref
            gi = {0: 0, 1: 1, 3: 2, 4: 3}[sec]
            y2 = (y * y).astype(BF16)
            ms = jnp.concatenate([jnp.dot(y2[:, half * 256:(half + 1) * 256], bd[...], preferred_element_type=F32)
                                  for half in range(2)], axis=1)
            y = y * lax.rsqrt(ms + EPS) * gsec_ref[gi:gi + 1, :]
        for c in range(4):
            yc = y[:, c * LANES:(c + 1) * LANES]
            if sec < 2:
                rot = jnp.where(first_half, pltpu.roll(yc, 96, 1), pltpu.roll(yc, 32, 1))
                yc = yc * cos + rot * sin
            if sec in qvt_row:
                r0 = qvt_row[sec] * 512 + c * LANES
                qvt_ref[0, r0:r0 + LANES, :] = yc.T.astype(BF16)
            else:
                c0 = out_col[sec] * 512 + c * LANES
                out_ref[:, c0:c0 + LANES] = yc.astype(BF16)


def _inproj(x2, g_mix, w_f32, bd64, bd32, gsec, cos_t, sin_t, seq, tm):
    t, d = x2.shape
    n = w_f32.shape[1]
    nsb = seq // tm
    n_tok_major = 4 * 512
    n_feat_major = 2 * 512
    const = lambda i: (0, 0)
    return pl.pallas_call(
        _inproj_kernel,
        grid=(t // tm,),
        in_specs=[
            pl.BlockSpec((tm, d), lambda i: (i, 0)),
            pl.BlockSpec((1, d), const),
            pl.BlockSpec((d, n), const, pipeline_mode=pl.Buffered(1)),
            pl.BlockSpec((256, 256), const),
            pl.BlockSpec((256, 256), const),
            pl.BlockSpec((4, 512), const),
            pl.BlockSpec((tm, LANES), lambda i: (i % nsb, 0)),
            pl.BlockSpec((tm, LANES), lambda i: (i % nsb, 0)),
        ],
        out_specs=[pl.BlockSpec((tm, n_tok_major), lambda i: (i, 0)),
                   pl.BlockSpec((1, n_feat_major, tm), lambda i: (i // nsb, 0, i % nsb))],
        out_shape=[jax.ShapeDtypeStruct((t, n_tok_major), BF16),
                   jax.ShapeDtypeStruct((t // seq, n_feat_major, seq), BF16)],
        scratch_shapes=[pltpu.VMEM((d, n), BF16)],
        compiler_params=_cparams(("arbitrary",)),
        name="inproj",
    )(x2, g_mix, w_f32, bd64, bd32, gsec, cos_t, sin_t)


QBLOCK_PAIRS_PER_BODY = 4

def _diffattn_kernel(lam_ref, gsub_ref, qt_ref, k_ref, vt_ref, o_ref, sa_scr, sb_scr, *, seq, tq, tk, lam_init):
    nq = seq // tq
    nk = seq // tk
    s1 = jnp.sum(lam_ref[0:1, :] * lam_ref[1:2, :], axis=1, keepdims=True)
    s2 = jnp.sum(lam_ref[2:3, :] * lam_ref[3:4, :], axis=1, keepdims=True)
    lam = jnp.exp(s1) - jnp.exp(s2) + lam_init

    row = lax.broadcasted_iota(jnp.int32, (LANES, tq), 0)

    def scores(qi, s_scr):
        qoff = pl.multiple_of(qi * tq, tq)
        qt = qt_ref[0, :, pl.ds(qoff, tq)]
        zero = jnp.zeros_like(qt)
        qm = jnp.concatenate([jnp.where(row < DA_QK, qt, zero), jnp.where(row >= DA_QK, qt, zero)], axis=1)
        mx = None
        for j in range(nk):
            st = jnp.dot(k_ref[0, j * tk:(j + 1) * tk, :], qm, preferred_element_type=F32)
            s_scr[j * tk:(j + 1) * tk, :] = st
            cm = jnp.max(st, axis=0, keepdims=True)
            mx = cm if mx is None else jnp.maximum(mx, cm)
        return mx

    def finish(qi, s_scr, m):
        l = jnp.zeros((1, 2 * tq), F32)
        acc = jnp.zeros((DA_V, 2 * tq), F32)
        for j in range(nk):
            p = jnp.exp2(s_scr[j * tk:(j + 1) * tk, :] - m)
            l = l + jnp.sum(p, axis=0, keepdims=True)
            acc = acc + jnp.dot(vt_ref[0, :, j * tk:(j + 1) * tk], p.astype(BF16), preferred_element_type=F32)
        o = acc * (1.0 / l)
        ot = o[:, :tq] - lam * o[:, tq:]
        ot = ot * lax.rsqrt(jnp.mean(ot * ot, axis=0, keepdims=True) + EPS) * (1.0 - lam_init)
        qoff = pl.multiple_of(qi * tq, tq)
        o_ref[0, pl.ds(qoff, tq), :] = (ot.T * gsub_ref[...]).astype(BF16)

    def body(i, m_a):
        for pair in range(QBLOCK_PAIRS_PER_BODY):
            q0 = 2 * (QBLOCK_PAIRS_PER_BODY * i + pair)
            m_b = scores(q0 + 1, sb_scr)
            finish(q0, sa_scr, m_a)
            m_a = scores(jnp.minimum(q0 + 2, nq - 1), sa_scr)
            finish(q0 + 1, sb_scr, m_b)
        return m_a

    lax.fori_loop(0, nq // (2 * QBLOCK_PAIRS_PER_BODY), body, scores(0, sa_scr))


def _diffattn(proj3, qvt, lam4, g_sub, lam_init, tq=128, tk=512):
    b, seq, _ = proj3.shape
    kern = functools.partial(_diffattn_kernel, seq=seq, tq=tq, tk=tk, lam_init=lam_init)
    return pl.pallas_call(
        kern,
        grid=(b, HA),
        in_specs=[
            pl.BlockSpec((4, DA_QK), lambda i, h: (0, 0)),
            pl.BlockSpec((1, DA_V), lambda i, h: (0, 0)),
            pl.BlockSpec((1, LANES, seq), lambda i, h: (i, h, 0)),
            pl.BlockSpec((1, seq, LANES), lambda i, h: (i, 0, h)),
            pl.BlockSpec((1, LANES, seq), lambda i, h: (i, HA + h, 0)),
        ],
        out_specs=pl.BlockSpec((1, seq, LANES), lambda i, h: (i, 0, h)),
        out_shape=jax.ShapeDtypeStruct((b, seq, HA * DA_V), BF16),
        scratch_shapes=[pltpu.VMEM((seq, 2 * tq), F32), pltpu.VMEM((seq, 2 * tq), F32)],
        compiler_params=_cparams(("parallel", "parallel")),
        name="diffattn",
    )(lam4, g_sub, qvt, proj3, qvt)


def _natten_kernel(bias_ref, q_ref, k_ref, v_ref, o_ref, *, rows):
    kr = min(WIN_R, rows)
    nkeys = kr * GRID_W
    hpg = LANES // DB_H
    lane_head = lax.broadcasted_iota(jnp.int32, (GRID_W, LANES), 1) // DB_H

    def row_step(r, carry):
        rs = jnp.clip(r - kr // 2, 0, rows - kr)
        base = rs - r + (WIN_R - 1)
        qoff = pl.multiple_of(r * GRID_W, GRID_W)
        koff = pl.multiple_of(rs * GRID_W, GRID_W)
        qr = q_ref[0, pl.ds(qoff, GRID_W), :]
        zero = jnp.zeros_like(qr)
        qs = jnp.concatenate([jnp.where(lane_head == h, qr, zero) for h in range(hpg)], axis=0)
        kw = k_ref[0, pl.ds(koff, nkeys), :]
        vw = v_ref[0, pl.ds(koff, nkeys), :]
        s = lax.dot_general(qs, kw, (((1,), (1,)), ((), ())), preferred_element_type=F32)
        bias = jnp.concatenate([bias_ref[0, base + 2 * i] for i in range(kr // 2)], axis=1)
        s = s + bias
        m = jnp.max(s, axis=1, keepdims=True)
        p = jnp.exp(s - m).astype(BF16)
        pv = jnp.dot(p, jnp.concatenate([vw, jnp.ones_like(vw)], axis=1), preferred_element_type=F32)
        o = pv[:, :LANES] * (1.0 / pv[:, LANES:])
        out = jnp.zeros((GRID_W, LANES), F32)
        for h in range(hpg):
            out = out + jnp.where(lane_head == h, o[h * GRID_W:(h + 1) * GRID_W, :], 0.0)
        o_ref[0, pl.ds(qoff, GRID_W), :] = out.astype(BF16)
        return carry

    lax.fori_loop(0, rows, row_step, 0, unroll=min(rows, 64))


def _natten(proj3, bias_tab):
    b, seq, _ = proj3.shape
    rows = seq // GRID_W
    ng = HB * DB_H // LANES
    nro = bias_tab.shape[1]
    kern = functools.partial(_natten_kernel, rows=rows)
    cb = HA
    return pl.pallas_call(
        kern,
        grid=(b, ng),
        in_specs=[
            pl.BlockSpec((1, nro, bias_tab.shape[2], LANES), lambda i, g: (g, 0, 0, 0)),
            pl.BlockSpec((1, seq, LANES), lambda i, g: (i, 0, cb + g)),
            pl.BlockSpec((1, seq, LANES), lambda i, g: (i, 0, cb + ng + g)),
            pl.BlockSpec((1, seq, LANES), lambda i, g: (i, 0, cb + 2 * ng + g)),
        ],
        out_specs=pl.BlockSpec((1, seq, LANES), lambda i, g: (i, 0, g)),
        out_shape=jax.ShapeDtypeStruct((b, seq, HB * DB_H), BF16),
        compiler_params=_cparams(("parallel", "parallel")),
        name="natten",
    )(bias_tab, proj3, proj3, proj3)


def _natten_bias_table(rpb):
    q = np.arange(GRID_W)[:, None]
    kc = np.arange(GRID_W)[None, :]
    cs = np.clip(q - WIN_C // 2, 0, GRID_W - WIN_C)
    valid = (kc >= cs) & (kc < cs + WIN_C)
    col_off = np.clip(kc - q + (WIN_C - 1), 0, 2 * WIN_C - 2)
    ncol = 2 * WIN_C - 1
    nro = 2 * WIN_R - 2
    pick = np.zeros((2, ncol, GRID_W, 2 * GRID_W), np.float32)
    for j in range(2):
        pick[j, :, :, j * GRID_W:(j + 1) * GRID_W] = (np.arange(ncol)[:, None, None] == col_off[None]) & valid[None]
    negmask = np.tile(np.where(valid, 0.0, NEG).astype(np.float32), (1, 2))
    hpg = LANES // DB_H
    r4 = rpb.astype(F32).reshape(HB // hpg, hpg, 2 * WIN_R - 1, ncol)
    rp = jnp.stack([r4[:, :, :nro], r4[:, :, 1:]], axis=0)
    tab = jnp.einsum('jghrc,jcqk->grhqk', rp, jnp.asarray(pick), precision=lax.Precision.HIGHEST) + negmask
    return tab.reshape(HB // hpg, nro, hpg * GRID_W, LANES)


def _outproj_kernel(x_ref, oa_ref, ob_ref, wo_ref, gffn_ref, wr_ref, tri_ref,
                    h_ref, xn_ref, meta_ref, meta_t_ref, cnt_ref, run_scr, wo_scr):
    tm = x_ref.shape[0]
    da = oa_ref.shape[1]

    @pl.when(pl.program_id(0) == 0)
    def _():
        run_scr[...] = jnp.zeros_like(run_scr)
        wo_scr[...] = wo_ref[...].astype(BF16)

    attn = jnp.dot(oa_ref[...], wo_scr[:da, :], preferred_element_type=F32)
    attn = attn + jnp.dot(ob_ref[...], wo_scr[da:, :], preferred_element_type=F32)
    h = x_ref[...] + attn
    h_ref[...] = h
    xn = _rms_rows(h) * gffn_ref[...]
    _store_token_tiles(xn_ref, _pack_bf16_pairs(xn))

    xh = xn.astype(BF16)
    xl = (xn - xh.astype(F32)).astype(BF16)
    hh_hl = jnp.dot(xh, wr_ref[...], preferred_element_type=F32)
    lg = hh_hl[:, :LANES] + hh_hl[:, LANES:] + jnp.dot(xl, wr_ref[:, :LANES], preferred_element_type=F32)
    lt = lg.T
    row = [lt[j:j + 1, :] for j in range(E_LANE0 + N_EXPERTS)]

    def first_argmax(vals):
        best = vals[0]
        for v in vals[1:]:
            best = jnp.maximum(best, v)
        idx = jnp.full_like(best, float(len(vals) - 1))
        for j in range(len(vals) - 2, -1, -1):
            idx = jnp.where(vals[j] == best, float(j), idx)
        return best, idx

    gmax, gidx = first_argmax(row[:N_GROUPS])
    gsum = jnp.exp(row[0] - gmax)
    for j in range(1, N_GROUPS):
        gsum = gsum + jnp.exp(row[j] - gmax)
    g_gate = 1.0 / gsum
    el = []
    for k in range(EXPERTS_PER_GROUP):
        v = row[E_LANE0 + k]
        for g in range(1, N_GROUPS):
            v = jnp.where(gidx == float(g), row[E_LANE0 + g * EXPERTS_PER_GROUP + k], v)
        el.append(v)
    l1, k1 = first_argmax(el)
    l2, k2 = first_argmax([jnp.where(k1 == float(k), -jnp.inf, el[k]) for k in range(EXPERTS_PER_GROUP)])
    r = jnp.exp(l2 - l1)
    w1 = g_gate / (1.0 + r)
    w2 = w1 * r
    e1 = gidx * EXPERTS_PER_GROUP + k1
    e2 = gidx * EXPERTS_PER_GROUP + k2

    rowid = lax.broadcasted_iota(jnp.int32, (ROUTER_ROWS, tm), 0).astype(F32)
    sel1 = rowid == e1 + E_LANE0
    sel2 = rowid == e2 + E_LANE0
    oh = jnp.where(sel1 | sel2, 1.0, 0.0).astype(BF16)
    run = run_scr[...]
    before = jnp.dot(oh, tri_ref[...], preferred_element_type=F32) + jnp.concatenate([run] * (tm // LANES), axis=1)
    rank1 = jnp.sum(jnp.where(sel1, before, 0.0), axis=0, keepdims=True)
    rank2 = jnp.sum(jnp.where(sel2, before, 0.0), axis=0, keepdims=True)
    run = run + jnp.dot(oh, jnp.ones((tm, LANES), BF16), preferred_element_type=F32)
    run_scr[...] = run
    cnt_ref[...] = run

    meta_t = jnp.concatenate([e1, e2, rank1, rank2, w1, w2, jnp.zeros((SUBLANES - 6, tm), F32)], axis=0)
    meta_t_ref[...] = meta_t
    meta_ref[...] = jnp.concatenate([meta_t, jnp.zeros((LANES - SUBLANES, tm), F32)], axis=0).T


def _outproj(x2, oa2, ob2, w_out, g_ffn, w_router, tri, tm):
    t, d = x2.shape
    da = oa2.shape[1]
    const = lambda i: (0, 0)
    row = lambda i: (i, 0)
    return pl.pallas_call(
        _outproj_kernel,
        grid=(t // tm,),
        in_specs=[
            pl.BlockSpec((tm, d), row),
            pl.BlockSpec((tm, da), row),
            pl.BlockSpec((tm, d - da), row),
            pl.BlockSpec((d, d), const, pipeline_mode=pl.Buffered(1)),
            pl.BlockSpec((1, d), const),
            pl.BlockSpec((d, 2 * LANES), const),
            pl.BlockSpec((tm, tm), const),
        ],
        out_specs=[
            pl.BlockSpec((tm, d), row),
            pl.BlockSpec((tm * TOKEN_ROWS, LANES), row),
            pl.BlockSpec((tm, LANES), row),
            pl.BlockSpec((SUBLANES, tm), lambda i: (0, i)),
            pl.BlockSpec((ROUTER_ROWS, LANES), const),
        ],
        out_shape=[
            jax.ShapeDtypeStruct((t, d), F32),
            jax.ShapeDtypeStruct((t * TOKEN_ROWS, LANES), U32),
            jax.ShapeDtypeStruct((t, LANES), F32),
            jax.ShapeDtypeStruct((SUBLANES, t), F32),
            jax.ShapeDtypeStruct((ROUTER_ROWS, LANES), F32),
        ],
        scratch_shapes=[pltpu.VMEM((ROUTER_ROWS, LANES), F32), pltpu.VMEM((d, d), BF16)],
        compiler_params=_cparams(("arbitrary",)),
        name="outproj_router",
    )(x2, oa2, ob2, w_out, g_ffn, w_router, tri)


DMA_ISSUE_UNROLL = 8
DMA_PRIORITIES = 2


def _dispatch_kernel(dest_ref, pend_ref, nu_ref, xn_ref, xs_ref, src_ref, zbuf, ibuf, sem, zsem):
    tm = xn_ref.shape[0] // TOKEN_ROWS
    t_total = pl.num_programs(0) * tm
    base = pl.program_id(0) * tm
    n_blocks = xs_ref.shape[0] // (MOE_BLOCK * TOKEN_ROWS)

    @pl.when(pl.program_id(0) == 0)
    def _():
        ibuf[...] = jnp.full(ibuf.shape, -1, jnp.int32)
        fill = pltpu.make_async_copy(ibuf, src_ref, zsem)
        fill.start()
        fill.wait()
        zbuf[...] = jnp.zeros_like(zbuf)

        def zero_block(blk):
            off = pl.multiple_of(blk * (MOE_BLOCK * TOKEN_ROWS), MOE_BLOCK * TOKEN_ROWS)
            return pltpu.make_async_copy(zbuf, xs_ref.at[pl.ds(off, MOE_BLOCK * TOKEN_ROWS), :], zsem)

        def for_each_zeroed_block(fn):
            def per_expert(e, c):
                end = pend_ref[e]
                start = jnp.where(e == 0, 0, pend_ref[jnp.maximum(e - 1, 0)])

                @pl.when(end > start)
                def _():
                    fn(zero_block(end // MOE_BLOCK - 1))
                return c
            lax.fori_loop(0, N_EXPERTS, per_expert, 0)

            def per_tail_block(blk, c):
                fn(zero_block(blk))
                return c
            lax.fori_loop(nu_ref[0], n_blocks, per_tail_block, 0)

        for_each_zeroed_block(lambda cp: cp.start())
        for_each_zeroed_block(lambda cp: cp.wait())

    def issue(group, c):
        toks = [group * DMA_ISSUE_UNROLL + g for g in range(DMA_ISSUE_UNROLL)]
        dests = [[dest_ref[slot * t_total + base + r] for slot in range(TOP_K)] for r in toks]
        for r, ds in zip(toks, dests):
            src = xn_ref.at[pl.ds(pl.multiple_of(r * TOKEN_ROWS, TOKEN_ROWS), TOKEN_ROWS), :]
            for slot, d in enumerate(ds):
                dst = xs_ref.at[pl.ds(pl.multiple_of(d * TOKEN_ROWS, TOKEN_ROWS), TOKEN_ROWS), :]
                pltpu.make_async_copy(src, dst, sem).start(priority=slot % DMA_PRIORITIES)
        for r, ds in zip(toks, dests):
            for slot, d in enumerate(ds):
                src_ref[d] = slot * t_total + base + r
        return c
    lax.fori_loop(0, tm // DMA_ISSUE_UNROLL, issue, 0)

    for slot in range(TOP_K):
        pltpu.make_async_copy(xn_ref, xs_ref.at[pl.ds(0, tm * TOKEN_ROWS), :], sem).wait()


def _dispatch(xn_tiles, dest_flat, pad_end, n_used, p_len, tm):
    t = xn_tiles.shape[0] // TOKEN_ROWS
    grid_spec = pltpu.PrefetchScalarGridSpec(
        num_scalar_prefetch=3,
        grid=(t // tm,),
        in_specs=[pl.BlockSpec((tm * TOKEN_ROWS, LANES), lambda i, dest, pend, nu: (i, 0))],
        out_specs=[pl.BlockSpec(memory_space=pl.ANY), pl.BlockSpec(memory_space=pltpu.SMEM)],
        scratch_shapes=[pltpu.VMEM((MOE_BLOCK * TOKEN_ROWS, LANES), U32),
                        pltpu.VMEM((p_len,), jnp.int32),
                        pltpu.SemaphoreType.DMA(()), pltpu.SemaphoreType.DMA(())],
    )
    return pl.pallas_call(
        _dispatch_kernel,
        grid_spec=grid_spec,
        out_shape=[jax.ShapeDtypeStruct((p_len * TOKEN_ROWS, LANES), U32),
                   jax.ShapeDtypeStruct((p_len,), jnp.int32)],
        compiler_params=_cparams(("arbitrary",)),
        name="dispatch",
    )(dest_flat, pad_end, n_used, xn_tiles)


def _experts_kernel(be_ref, nu_ref, nxt_ref, ord_ref, src_ref, xs_ref, w1_hbm, w3_hbm, w2_hbm, y2_ref,
                    w1_scr, w3_scr, w2_scr, w1_buf, w3_buf, w2_buf, ybuf, wsems, ysems):
    step = pl.program_id(0)
    first = 2 * step
    second = first + 1
    n_used = nu_ref[0]
    cur = step % 2
    pair_rows = 2 * MOE_BLOCK
    n_slot_rows = y2_ref.shape[0] // TOKEN_ROWS - 2 * pair_rows

    def send_previous(unrolled):
        prev_base = jnp.maximum(step - 1, 0) * pair_rows

        def send(r, priority):
            s = src_ref[prev_base + r]
            row = jnp.where((step > 0) & (s >= 0), s, n_slot_rows + cur * pair_rows + r)
            src = ybuf.at[1 - cur, pl.ds(pl.multiple_of(r * TOKEN_ROWS, TOKEN_ROWS), TOKEN_ROWS), :]
            dst = y2_ref.at[pl.ds(pl.multiple_of(row * TOKEN_ROWS, TOKEN_ROWS), TOKEN_ROWS), :]
            pltpu.make_async_copy(src, dst, ysems.at[cur]).start(priority=priority)

        if unrolled:
            for r in range(pair_rows):
                send(r, r % DMA_PRIORITIES)
        else:
            def send_group(j, c):
                for k in range(DMA_PRIORITIES):
                    send(j * DMA_PRIORITIES + k, k)
                return c
            lax.fori_loop(0, pair_rows // DMA_PRIORITIES, send_group, 0, unroll=DMA_ISSUE_UNROLL // DMA_PRIORITIES)

    def all_rows_sent(buf, sem):
        return pltpu.make_async_copy(ybuf.at[buf], y2_ref.at[pl.ds(0, pair_rows * TOKEN_ROWS), :], sem)

    def wait_older_sends():
        all_rows_sent(cur, ysems.at[1 - cur]).wait()

    @pl.when(step == 0)
    def _():
        ybuf[1] = jnp.zeros(ybuf.shape[1:], U32)
        odd_dump = (n_slot_rows + pair_rows) * TOKEN_ROWS
        pltpu.make_async_copy(ybuf.at[1], y2_ref.at[pl.ds(odd_dump, pair_rows * TOKEN_ROWS), :], ysems.at[1]).start()

    def weight_copies(e, slot):
        return (pltpu.make_async_copy(w1_hbm.at[e], w1_buf.at[slot], wsems.at[slot, 0]),
                pltpu.make_async_copy(w3_hbm.at[e], w3_buf.at[slot], wsems.at[slot, 1]),
                pltpu.make_async_copy(w2_hbm.at[e], w2_buf.at[slot], wsems.at[slot, 2]))

    def switch_to(e):
        slot = ord_ref[e] % 2
        for cp in weight_copies(e, slot):
            cp.wait()
        nxt = nxt_ref[e]

        @pl.when(nxt >= 0)
        def _():
            for cp in weight_copies(nxt, 1 - slot):
                cp.start()
        w1_scr[...] = w1_buf[slot].astype(BF16)
        w3_scr[...] = w3_buf[slot].astype(BF16)
        w2_scr[...] = w2_buf[slot].astype(BF16)

    def gated_mlp(row0, n_rows, first_store):
        x = _unpack_bf16_pairs(_load_token_tiles(xs_ref, n_rows, row0)).astype(BF16)
        a = jnp.dot(x, w1_scr[...], preferred_element_type=F32)
        b = jnp.dot(x, w3_scr[...], preferred_element_type=F32)
        hdn = a * (1.0 / (1.0 + jnp.exp(-a))) * b
        y = jnp.dot(hdn.astype(BF16), w2_scr[...], preferred_element_type=F32)
        if first_store:
            wait_older_sends()
        _store_token_tiles(ybuf.at[cur], _pack_bf16_pairs(y), row0)

    def zero_rows(row0, n_rows):
        ybuf[cur, row0 * TOKEN_ROWS:(row0 + n_rows) * TOKEN_ROWS, :] = jnp.zeros((n_rows * TOKEN_ROWS, LANES), U32)

    @pl.when(first < n_used)
    def _():
        e_first = be_ref[first]
        e_before = be_ref[jnp.maximum(first - 1, 0)]

        @pl.when(first == 0)
        def _():
            for cp in weight_copies(e_first, ord_ref[e_first] % 2):
                cp.start()

        @pl.when((first == 0) | (e_first != e_before))
        def _():
            switch_to(e_first)

        second_used = second < n_used
        e_second = be_ref[second]
        same_expert = second_used & (e_second == e_first)

        @pl.when(same_expert)
        def _():
            send_previous(unrolled=True)
            gated_mlp(0, 2 * MOE_BLOCK, first_store=True)

        @pl.when(jnp.logical_not(same_expert))
        def _():
            send_previous(unrolled=True)
            gated_mlp(0, MOE_BLOCK, first_store=True)

            @pl.when(second_used)
            def _():
                switch_to(e_second)
                gated_mlp(MOE_BLOCK, MOE_BLOCK, first_store=False)

            @pl.when(jnp.logical_not(second_used))
            def _():
                zero_rows(MOE_BLOCK, MOE_BLOCK)

    @pl.when(first >= n_used)
    def _():
        send_previous(unrolled=False)
        wait_older_sends()
        zero_rows(0, 2 * MOE_BLOCK)

    @pl.when(step == pl.num_programs(0) - 1)
    def _():
        all_rows_sent(1 - cur, ysems.at[cur]).wait()


def _experts(xs, row_src, block_e, n_used, next_expert, expert_ord, w1, w3, w2, n_slot_rows):
    p_len = xs.shape[0] // TOKEN_ROWS
    n_blocks = p_len // MOE_BLOCK
    assert n_blocks % 2 == 0
    n_pairs = n_blocks // 2
    d, de = w1.shape[1], w1.shape[2]
    blk = lambda i, be, nu, nxt, od, src: (jnp.maximum(jnp.minimum(i, (nu[0] - 1) // 2), 0), 0)
    hbm = pl.BlockSpec(memory_space=pl.ANY)
    grid_spec = pltpu.PrefetchScalarGridSpec(
        num_scalar_prefetch=5,
        grid=(n_pairs + 1,),
        in_specs=[pl.BlockSpec((2 * MOE_BLOCK * TOKEN_ROWS, LANES), blk), hbm, hbm, hbm],
        out_specs=hbm,
        scratch_shapes=[pltpu.VMEM((d, de), BF16), pltpu.VMEM((d, de), BF16), pltpu.VMEM((de, d), BF16),
                        pltpu.VMEM((2, d, de), F32), pltpu.VMEM((2, d, de), F32), pltpu.VMEM((2, de, d), F32),
                        pltpu.VMEM((2, 2 * MOE_BLOCK * TOKEN_ROWS, LANES), U32),
                        pltpu.SemaphoreType.DMA((2, 3)), pltpu.SemaphoreType.DMA((2,))],
    )
    return pl.pallas_call(
        _experts_kernel,
        grid_spec=grid_spec,
        out_shape=jax.ShapeDtypeStruct(((n_slot_rows + 4 * MOE_BLOCK) * TOKEN_ROWS, LANES), U32),
        compiler_params=_cparams(("arbitrary",)),
        name="experts",
    )(block_e, n_used, next_expert, expert_ord, row_src, xs, w1, w3, w2)


def _final_kernel(h_ref, meta_ref, p_ref, wple_ref, gple_ref, gplg_ref, wplg_ref, y_first_ref, y_second_ref, out_ref,
                  wple_scr, wplg_scr):
    tm = h_ref.shape[0]

    @pl.when(pl.program_id(0) == 0)
    def _():
        wple_scr[...] = wple_ref[...].astype(BF16)
        wplg_scr[...] = wplg_ref[...].astype(BF16)

    emb = jnp.dot(p_ref[...].astype(BF16), wple_scr[...], preferred_element_type=F32)
    emb = _rms_rows(emb) * gple_ref[...]
    meta = meta_ref[...]
    y1 = _unpack_bf16_pairs(_load_token_tiles(y_first_ref, tm))
    y2 = _unpack_bf16_pairs(_load_token_tiles(y_second_ref, tm))
    h = h_ref[...] + meta[:, 4:5] * y1 + meta[:, 5:6] * y2
    hn = (_rms_rows(h) * gplg_ref[...]).astype(BF16)
    z = jnp.dot(hn, wplg_scr[...], preferred_element_type=F32)
    out_ref[...] = h + emb * (1.0 / (1.0 + jnp.exp(-z)))


def _final(h2, meta, p2, w_ple, g_ple, g_plg, w_plg, y_slots, tm):
    t, d = h2.shape
    dp = p2.shape[1]
    nt = t // tm
    const = lambda i: (0, 0)
    row = lambda i: (i, 0)
    return pl.pallas_call(
        _final_kernel,
        grid=(nt,),
        in_specs=[
            pl.BlockSpec((tm, d), row),
            pl.BlockSpec((tm, LANES), row),
            pl.BlockSpec((tm, dp), row),
            pl.BlockSpec((dp, d), const, pipeline_mode=pl.Buffered(1)),
            pl.BlockSpec((1, d), const),
            pl.BlockSpec((1, d), const),
            pl.BlockSpec((d, d), const, pipeline_mode=pl.Buffered(1)),
            pl.BlockSpec((tm * TOKEN_ROWS, LANES), row),
            pl.BlockSpec((tm * TOKEN_ROWS, LANES), lambda i: (nt + i, 0)),
        ],
        out_specs=pl.BlockSpec((tm, d), row),
        out_shape=jax.ShapeDtypeStruct((t, d), F32),
        scratch_shapes=[pltpu.VMEM((dp, d), BF16), pltpu.VMEM((d, d), BF16)],
        compiler_params=_cparams(("arbitrary",)),
        name="combine_ple",
    )(h2, meta, p2, w_ple, g_ple, g_plg, w_plg, y_slots, y_slots)


def _block_diag_mean(seg):
    m = np.kron(np.eye(256 // seg), np.full((seg, seg), 1.0 / seg))
    return jnp.asarray(m, BF16)


def _rope_tables(seq):
    f32 = np.float32
    inv = np.power(f32(ROPE_THETA), -np.arange(0, DA_QK, 2, dtype=f32) / f32(DA_QK)).astype(f32)
    ang = (np.arange(seq, dtype=f32)[:, None] * inv[None, :]).astype(f32)
    cos, sin = np.cos(ang).astype(f32), np.sin(ang).astype(f32)
    cos_t = np.tile(cos, (1, LANES // cos.shape[1]))
    sin_t = np.tile(np.concatenate([-sin, sin], axis=1), (1, LANES // (2 * sin.shape[1])))
    return jnp.asarray(cos_t), jnp.asarray(sin_t)


def _tile_for(n, pref):
    while n % pref:
        pref //= 2
    return pref


def kernel(x, p, g_mix, w_in, g_qa, g_ka, lam_q1, lam_k1, lam_q2, lam_k2, g_sub, g_qb, g_kb, rpb, w_out,
           g_ffn, w_rg, w_re, w1, w3, w2, g_plg, w_plg, w_ple, g_ple):
    b, s, d = x.shape
    assert d == 2 * TOKEN_ROWS * LANES, "token-tile layout packs one token into TOKEN_ROWS rows of 128 uint32"
    t = b * s
    depth = w_in.shape[0]
    tm = _tile_for(s, 1024)
    tmd = _tile_for(t, 1024)
    tmf = _tile_for(t, 512)
    n_slots = t * TOP_K
    n_blocks = n_slots // MOE_BLOCK + N_EXPERTS
    p_len = n_blocks * MOE_BLOCK

    bd64 = _block_diag_mean(DA_QK)
    bd32 = _block_diag_mean(DB_H)
    cos_t, sin_t = _rope_tables(s)
    tri = jnp.asarray(np.triu(np.ones((tm, tm)), 1), BF16)

    h = x.reshape(t, d)
    for i in range(depth):
        lam_init = 0.8 - 0.6 * math.exp(-0.3 * i)
        gsec = jnp.stack([
            jnp.tile(g_qa[i], 512 // DA_QK) * (DA_QK ** -0.5 * math.log2(math.e)),
            jnp.tile(g_ka[i], 512 // DA_QK),
            jnp.tile(g_qb[i], 512 // DB_H) * DB_H ** -0.5,
            jnp.tile(g_kb[i], 512 // DB_H),
        ]).astype(F32)
        proj, qvt = _inproj(h, g_mix[i][None, :], w_in[i], bd64, bd32, gsec, cos_t, sin_t, s, tm)
        proj3 = proj.reshape(b, s, proj.shape[1])

        lam4 = jnp.stack([lam_q1[i], lam_k1[i], lam_q2[i], lam_k2[i]]).astype(F32)
        oa = _diffattn(proj3, qvt, lam4, g_sub[i][None, :].astype(F32), lam_init)
        ob = _natten(proj3, _natten_bias_table(rpb[i]))

        w_router = jnp.concatenate(
            [w_rg[i], w_re[i], jnp.zeros((d, LANES - N_GROUPS - N_EXPERTS), F32)], axis=1).astype(F32)
        wr_hi = w_router.astype(BF16)
        wr_lo = (w_router - wr_hi.astype(F32)).astype(BF16)
        h1, xn, meta, meta_t, cnt = _outproj(h, oa.reshape(t, -1), ob.reshape(t, -1), w_out[i],
                                             g_ffn[i][None, :], jnp.concatenate([wr_hi, wr_lo], axis=1), tri, tm)

        counts = cnt[E_LANE0:E_LANE0 + N_EXPERTS, 0].astype(jnp.int32)
        padded = (counts + MOE_BLOCK - 1) // MOE_BLOCK * MOE_BLOCK
        pad_end = jnp.cumsum(padded)
        pad_start = pad_end - padded
        mt = meta_t[0:2 * TOP_K].astype(jnp.int32)
        e_ids = jnp.arange(N_EXPERTS, dtype=jnp.int32)[None, :, None]
        start_of = jnp.sum(jnp.where(mt[:TOP_K, None, :] == e_ids, pad_start[None, :, None], 0), axis=1)
        dest = start_of + mt[TOP_K:]
        blk_start = jnp.arange(n_blocks, dtype=jnp.int32) * MOE_BLOCK
        block_e = jnp.minimum(jnp.sum(pad_end[None, :] <= blk_start[:, None], axis=1), N_EXPERTS - 1)
        block_e = block_e.astype(jnp.int32)
        n_used = (pad_end[-1:] // MOE_BLOCK).astype(jnp.int32)

        dest_flat = dest.reshape(TOP_K * t)

        xs, row_src = _dispatch(xn, dest_flat, pad_end.astype(jnp.int32), n_used, p_len, tmd)
        in_use = counts > 0
        expert_ord = (jnp.cumsum(in_use) - 1).astype(jnp.int32)
        ids = jnp.arange(N_EXPERTS, dtype=jnp.int32)
        later = jnp.where((ids[None, :] > ids[:, None]) & in_use[None, :], ids[None, :], N_EXPERTS)
        next_expert = jnp.min(later, axis=1)
        next_expert = jnp.where(next_expert == N_EXPERTS, -1, next_expert).astype(jnp.int32)
        y_slots = _experts(xs, row_src.reshape(p_len), block_e, n_used, next_expert, expert_ord,
                           w1[i], w3[i], w2[i], TOP_K * t)
        h = _final(h1, meta, p[i].reshape(t, -1), w_ple[i], g_ple[i][None, :], g_plg[i][None, :], w_plg[i],
                   y_slots, tmf)
    return h.reshape(b, s, d)
```

```python
import functools
import math

import numpy as np
import jax
import jax.numpy as jnp
from jax import lax
from jax.experimental import pallas as pl
from jax.experimental.pallas import tpu as pltpu

F32 = jnp.float32
BF16 = jnp.bfloat16
U32 = jnp.uint32

LANES = 128
HA = 4
DA_QK = 64
DA_V = 128
HB = 16
DB_H = 32
GRID_W = 64
WIN_R = 8
WIN_C = 16
ROPE_THETA = 10000.0
EPS = 1e-6
NEG = -1e30
N_GROUPS = 4
EXPERTS_PER_GROUP = 8
N_EXPERTS = N_GROUPS * EXPERTS_PER_GROUP
TOP_K = 2
MOE_BLOCK = 256
E_LANE0 = N_GROUPS
ROUTER_ROWS = 40

VMEM_LIMIT = 56 * 1024 * 1024


def _cparams(sem):
    return pltpu.CompilerParams(dimension_semantics=sem, vmem_limit_bytes=VMEM_LIMIT)


def _rms_rows(x):
    return x * lax.rsqrt(jnp.mean(x * x, axis=-1, keepdims=True) + EPS)


SUBLANES = 8

TOKEN_ROWS = 4
HIGH_HALF = 0xFFFF0000


def _pack_bf16_pairs(x):
    m = x.shape[1] // 2
    bits = pltpu.bitcast(x.astype(BF16).astype(F32), U32)
    return (bits[:, :m] >> 16) | (bits[:, m:] & jnp.uint32(HIGH_HALF))


def _unpack_bf16_pairs(u):
    lo = pltpu.bitcast(u << 16, F32)
    hi = pltpu.bitcast(u & jnp.uint32(HIGH_HALF), F32)
    return jnp.concatenate([lo, hi], axis=1)


def _store_token_tiles(ref, val, first=0):
    n = val.shape[0]
    for c in range(TOKEN_ROWS):
        ref[pl.ds(first * TOKEN_ROWS + c, n, stride=TOKEN_ROWS), :] = val[:, c * LANES:(c + 1) * LANES]


def _load_token_tiles(ref, n, first=0):
    return jnp.concatenate(
        [ref[pl.ds(first * TOKEN_ROWS + c, n, stride=TOKEN_ROWS), :] for c in range(TOKEN_ROWS)], axis=1)


def _inproj_kernel(x_ref, gmix_ref, w_ref, bd64_ref, bd32_ref, gsec_ref, cos_ref, sin_ref, out_ref, qvt_ref, w_scr):
    tm = x_ref.shape[0]

    @pl.when(pl.program_id(0) == 0)
    def _():
        w_scr[...] = w_ref[...].astype(BF16)

    a = (_rms_rows(x_ref[...]) * gmix_ref[...]).astype(BF16)
    lane = lax.broadcasted_iota(jnp.int32, (tm, LANES), 1)
    first_half = (lane % 64) < 32
    cos = cos_ref[...]
    sin = sin_ref[...]
    out_col = {1: 0, 3: 1, 4: 2, 5: 3}
    qvt_row = {0: 0, 2: 1}
    for sec in range(6):
        y = jnp.dot(a, w_scr[:, sec * 512:(sec + 1) * 512], preferred_element_type=F32)
        if sec in (0, 1, 3, 4):
            bd = bd64_ref if sec < 2 else bd32_ref
            gi = {0: 0, 1: 1, 3: 2, 4: 3}[sec]
            y2 = (y * y).astype(BF16)
            ms = jnp.concatenate([jnp.dot(y2[:, half * 256:(half + 1) * 256], bd[...], preferred_element_type=F32)
                                  for half in range(2)], axis=1)
            y = y * lax.rsqrt(ms + EPS) * gsec_ref[gi:gi + 1, :]
        for c in range(4):
            yc = y[:, c * LANES:(c + 1) * LANES]
            if sec < 2:
                rot = jnp.where(first_half, pltpu.roll(yc, 96, 1), pltpu.roll(yc, 32, 1))
                yc = yc * cos + rot * sin
            if sec in qvt_row:
                r0 = qvt_row[sec] * 512 + c * LANES
                qvt_ref[0, r0:r0 + LANES, :] = yc.T.astype(BF16)
            else:
                c0 = out_col[sec] * 512 + c * LANES
                out_ref[:, c0:c0 + LANES] = yc.astype(BF16)


def _inproj(x2, g_mix, w_f32, bd64, bd32, gsec, cos_t, sin_t, seq, tm):
    t, d = x2.shape
    n = w_f32.shape[1]
    nsb = seq // tm
    n_tok_major = 4 * 512
    n_feat_major = 2 * 512
    const = lambda i: (0, 0)
    return pl.pallas_call(
        _inproj_kernel,
        grid=(t // tm,),
        in_specs=[
            pl.BlockSpec((tm, d), lambda i: (i, 0)),
            pl.BlockSpec((1, d), const),
            pl.BlockSpec((d, n), const, pipeline_mode=pl.Buffered(1)),
            pl.BlockSpec((256, 256), const),
            pl.BlockSpec((256, 256), const),
            pl.BlockSpec((4, 512), const),
            pl.BlockSpec((tm, LANES), lambda i: (i % nsb, 0)),
            pl.BlockSpec((tm, LANES), lambda i: (i % nsb, 0)),
        ],
        out_specs=[pl.BlockSpec((tm, n_tok_major), lambda i: (i, 0)),
                   pl.BlockSpec((1, n_feat_major, tm), lambda i: (i // nsb, 0, i % nsb))],
        out_shape=[jax.ShapeDtypeStruct((t, n_tok_major), BF16),
                   jax.ShapeDtypeStruct((t // seq, n_feat_major, seq), BF16)],
        scratch_shapes=[pltpu.VMEM((d, n), BF16)],
        compiler_params=_cparams(("arbitrary",)),
        name="inproj",
    )(x2, g_mix, w_f32, bd64, bd32, gsec, cos_t, sin_t)


QBLOCK_PAIRS_PER_BODY = 4

def _diffattn_kernel(lam_ref, gsub_ref, qt_ref, k_ref, vt_ref, o_ref, sa_scr, sb_scr, *, seq, tq, tk, lam_init):
    nq = seq // tq
    nk = seq // tk
    s1 = jnp.sum(lam_ref[0:1, :] * lam_ref[1:2, :], axis=1, keepdims=True)
    s2 = jnp.sum(lam_ref[2:3, :] * lam_ref[3:4, :], axis=1, keepdims=True)
    lam = jnp.exp(s1) - jnp.exp(s2) + lam_init

    row = lax.broadcasted_iota(jnp.int32, (LANES, tq), 0)

    def scores(qi, s_scr):
        qoff = pl.multiple_of(qi * tq, tq)
        qt = qt_ref[0, :, pl.ds(qoff, tq)]
        zero = jnp.zeros_like(qt)
        qm = jnp.concatenate([jnp.where(row < DA_QK, qt, zero), jnp.where(row >= DA_QK, qt, zero)], axis=1)
        mx = None
        for j in range(nk):
            st = jnp.dot(k_ref[0, j * tk:(j + 1) * tk, :], qm, preferred_element_type=F32)
            s_scr[j * tk:(j + 1) * tk, :] = st
            cm = jnp.max(st, axis=0, keepdims=True)
            mx = cm if mx is None else jnp.maximum(mx, cm)
        return mx

    def finish(qi, s_scr, m):
        l = jnp.zeros((1, 2 * tq), F32)
        acc = jnp.zeros((DA_V, 2 * tq), F32)
        for j in range(nk):
            p = jnp.exp2(s_scr[j * tk:(j + 1) * tk, :] - m)
            l = l + jnp.sum(p, axis=0, keepdims=True)
            acc = acc + jnp.dot(vt_ref[0, :, j * tk:(j + 1) * tk], p.astype(BF16), preferred_element_type=F32)
        o = acc * (1.0 / l)
        ot = o[:, :tq] - lam * o[:, tq:]
        ot = ot * lax.rsqrt(jnp.mean(ot * ot, axis=0, keepdims=True) + EPS) * (1.0 - lam_init)
        qoff = pl.multiple_of(qi * tq, tq)
        o_ref[0, pl.ds(qoff, tq), :] = (ot.T * gsub_ref[...]).astype(BF16)

    def body(i, m_a):
        for pair in range(QBLOCK_PAIRS_PER_BODY):
            q0 = 2 * (QBLOCK_PAIRS_PER_BODY * i + pair)
            m_b = scores(q0 + 1, sb_scr)
            finish(q0, sa_scr, m_a)
            m_a = scores(jnp.minimum(q0 + 2, nq - 1), sa_scr)
            finish(q0 + 1, sb_scr, m_b)
        return m_a

    lax.fori_loop(0, nq // (2 * QBLOCK_PAIRS_PER_BODY), body, scores(0, sa_scr))


def _diffattn(proj3, qvt, lam4, g_sub, lam_init, tq=128, tk=512):
    b, seq, _ = proj3.shape
    kern = functools.partial(_diffattn_kernel, seq=seq, tq=tq, tk=tk, lam_init=lam_init)
    return pl.pallas_call(
        kern,
        grid=(b, HA),
        in_specs=[
            pl.BlockSpec((4, DA_QK), lambda i, h: (0, 0)),
            pl.BlockSpec((1, DA_V), lambda i, h: (0, 0)),
            pl.BlockSpec((1, LANES, seq), lambda i, h: (i, h, 0)),
            pl.BlockSpec((1, seq, LANES), lambda i, h: (i, 0, h)),
            pl.BlockSpec((1, LANES, seq), lambda i, h: (i, HA + h, 0)),
        ],
        out_specs=pl.BlockSpec((1, seq, LANES), lambda i, h: (i, 0, h)),
        out_shape=jax.ShapeDtypeStruct((b, seq, HA * DA_V), BF16),
        scratch_shapes=[pltpu.VMEM((seq, 2 * tq), F32), pltpu.VMEM((seq, 2 * tq), F32)],
        compiler_params=_cparams(("parallel", "parallel")),
        name="diffattn",
    )(lam4, g_sub, qvt, proj3, qvt)


def _natten_kernel(bias_ref, q_ref, k_ref, v_ref, o_ref, *, rows):
    kr = min(WIN_R, rows)
    nkeys = kr * GRID_W
    hpg = LANES // DB_H
    lane_head = lax.broadcasted_iota(jnp.int32, (GRID_W, LANES), 1) // DB_H

    def row_step(r, carry):
        rs = jnp.clip(r - kr // 2, 0, rows - kr)
        base = rs - r + (WIN_R - 1)
        qoff = pl.multiple_of(r * GRID_W, GRID_W)
        koff = pl.multiple_of(rs * GRID_W, GRID_W)
        qr = q_ref[0, pl.ds(qoff, GRID_W), :]
        zero = jnp.zeros_like(qr)
        qs = jnp.concatenate([jnp.where(lane_head == h, qr, zero) for h in range(hpg)], axis=0)
        kw = k_ref[0, pl.ds(koff, nkeys), :]
        vw = v_ref[0, pl.ds(koff, nkeys), :]
        s = lax.dot_general(qs, kw, (((1,), (1,)), ((), ())), preferred_element_type=F32)
        bias = jnp.concatenate([bias_ref[0, base + 2 * i] for i in range(kr // 2)], axis=1)
        s = s + bias
        m = jnp.max(s, axis=1, keepdims=True)
        p = jnp.exp(s - m).astype(BF16)
        pv = jnp.dot(p, jnp.concatenate([vw, jnp.ones_like(vw)], axis=1), preferred_element_type=F32)
        o = pv[:, :LANES] * (1.0 / pv[:, LANES:])
        out = jnp.zeros((GRID_W, LANES), F32)
        for h in range(hpg):
            out = out + jnp.where(lane_head == h, o[h * GRID_W:(h + 1) * GRID_W, :], 0.0)
        o_ref[0, pl.ds(qoff, GRID_W), :] = out.astype(BF16)
        return carry

    lax.fori_loop(0, rows, row_step, 0, unroll=min(rows, 64))


def _natten(proj3, bias_tab):
    b, seq, _ = proj3.shape
    rows = seq // GRID_W
    ng = HB * DB_H // LANES
    nro = bias_tab.shape[1]
    kern = functools.partial(_natten_kernel, rows=rows)
    cb = HA
    return pl.pallas_call(
        kern,
        grid=(b, ng),
        in_specs=[
            pl.BlockSpec((1, nro, bias_tab.shape[2], LANES), lambda i, g: (g, 0, 0, 0)),
            pl.BlockSpec((1, seq, LANES), lambda i, g: (i, 0, cb + g)),
            pl.BlockSpec((1, seq, LANES), lambda i, g: (i, 0, cb + ng + g)),
            pl.BlockSpec((1, seq, LANES), lambda i, g: (i, 0, cb + 2 * ng + g)),
        ],
        out_specs=pl.BlockSpec((1, seq, LANES), lambda i, g: (i, 0, g)),
        out_shape=jax.ShapeDtypeStruct((b, seq, HB * DB_H), BF16),
        compiler_params=_cparams(("parallel", "parallel")),
        name="natten",
    )(bias_tab, proj3, proj3, proj3)


def _natten_bias_table(rpb):
    q = np.arange(GRID_W)[:, None]
    kc = np.arange(GRID_W)[None, :]
    cs = np.clip(q - WIN_C // 2, 0, GRID_W - WIN_C)
    valid = (kc >= cs) & (kc < cs + WIN_C)
    col_off = np.clip(kc - q + (WIN_C - 1), 0, 2 * WIN_C - 2)
    ncol = 2 * WIN_C - 1
    nro = 2 * WIN_R - 2
    pick = np.zeros((2, ncol, GRID_W, 2 * GRID_W), np.float32)
    for j in range(2):
        pick[j, :, :, j * GRID_W:(j + 1) * GRID_W] = (np.arange(ncol)[:, None, None] == col_off[None]) & valid[None]
    negmask = np.tile(np.where(valid, 0.0, NEG).astype(np.float32), (1, 2))
    hpg = LANES // DB_H
    r4 = rpb.astype(F32).reshape(HB // hpg, hpg, 2 * WIN_R - 1, ncol)
    rp = jnp.stack([r4[:, :, :nro], r4[:, :, 1:]], axis=0)
    tab = jnp.einsum('jghrc,jcqk->grhqk', rp, jnp.asarray(pick), precision=lax.Precision.HIGHEST) + negmask
    return tab.reshape(HB // hpg, nro, hpg * GRID_W, LANES)


def _outproj_kernel(x_ref, oa_ref, ob_ref, wo_ref, gffn_ref, wr_ref, tri_ref,
                    h_ref, xn_ref, meta_ref, meta_t_ref, cnt_ref, run_scr, wo_scr):
    tm = x_ref.shape[0]
    da = oa_ref.shape[1]

    @pl.when(pl.program_id(0) == 0)
    def _():
        run_scr[...] = jnp.zeros_like(run_scr)
        wo_scr[...] = wo_ref[...].astype(BF16)

    attn = jnp.dot(oa_ref[...], wo_scr[:da, :], preferred_element_type=F32)
    attn = attn + jnp.dot(ob_ref[...], wo_scr[da:, :], preferred_element_type=F32)
    h = x_ref[...] + attn
    h_ref[...] = h
    xn = _rms_rows(h) * gffn_ref[...]
    _store_token_tiles(xn_ref, _pack_bf16_pairs(xn))

    xh = xn.astype(BF16)
    xl = (xn - xh.astype(F32)).astype(BF16)
    hh_hl = jnp.dot(xh, wr_ref[...], preferred_element_type=F32)
    lg = hh_hl[:, :LANES] + hh_hl[:, LANES:] + jnp.dot(xl, wr_ref[:, :LANES], preferred_element_type=F32)
    lt = lg.T
    row = [lt[j:j + 1, :] for j in range(E_LANE0 + N_EXPERTS)]

    def first_argmax(vals):
        best = vals[0]
        for v in vals[1:]:
            best = jnp.maximum(best, v)
        idx = jnp.full_like(best, float(len(vals) - 1))
        for j in range(len(vals) - 2, -1, -1):
            idx = jnp.where(vals[j] == best, float(j), idx)
        return best, idx

    gmax, gidx = first_argmax(row[:N_GROUPS])
    gsum = jnp.exp(row[0] - gmax)
    for j in range(1, N_GROUPS):
        gsum = gsum + jnp.exp(row[j] - gmax)
    g_gate = 1.0 / gsum
    el = []
    for k in range(EXPERTS_PER_GROUP):
        v = row[E_LANE0 + k]
        for g in range(1, N_GROUPS):
            v = jnp.where(gidx == float(g), row[E_LANE0 + g * EXPERTS_PER_GROUP + k], v)
        el.append(v)
    l1, k1 = first_argmax(el)
    l2, k2 = first_argmax([jnp.where(k1 == float(k), -jnp.inf, el[k]) for k in range(EXPERTS_PER_GROUP)])
    r = jnp.exp(l2 - l1)
    w1 = g_gate / (1.0 + r)
    w2 = w1 * r
    e1 = gidx * EXPERTS_PER_GROUP + k1
    e2 = gidx * EXPERTS_PER_GROUP + k2

    rowid = lax.broadcasted_iota(jnp.int32, (ROUTER_ROWS, tm), 0).astype(F32)
    sel1 = rowid == e1 + E_LANE0
    sel2 = rowid == e2 + E_LANE0
    oh = jnp.where(sel1 | sel2, 1.0, 0.0).astype(BF16)
    run = run_scr[...]
    before = jnp.dot(oh, tri_ref[...], preferred_element_type=F32) + jnp.concatenate([run] * (tm // LANES), axis=1)
    rank1 = jnp.sum(jnp.where(sel1, before, 0.0), axis=0, keepdims=True)
    rank2 = jnp.sum(jnp.where(sel2, before, 0.0), axis=0, keepdims=True)
    run = run + jnp.dot(oh, jnp.ones((tm, LANES), BF16), preferred_element_type=F32)
    run_scr[...] = run
    cnt_ref[...] = run

    meta_t = jnp.concatenate([e1, e2, rank1, rank2, w1, w2, jnp.zeros((SUBLANES - 6, tm), F32)], axis=0)
    meta_t_ref[...] = meta_t
    meta_ref[...] = jnp.concatenate([meta_t, jnp.zeros((LANES - SUBLANES, tm), F32)], axis=0).T


def _outproj(x2, oa2, ob2, w_out, g_ffn, w_router, tri, tm):
    t, d = x2.shape
    da = oa2.shape[1]
    const = lambda i: (0, 0)
    row = lambda i: (i, 0)
    return pl.pallas_call(
        _outproj_kernel,
        grid=(t // tm,),
        in_specs=[
            pl.BlockSpec((tm, d), row),
            pl.BlockSpec((tm, da), row),
            pl.BlockSpec((tm, d - da), row),
            pl.BlockSpec((d, d), const, pipeline_mode=pl.Buffered(1)),
            pl.BlockSpec((1, d), const),
            pl.BlockSpec((d, 2 * LANES), const),
            pl.BlockSpec((tm, tm), const),
        ],
        out_specs=[
            pl.BlockSpec((tm, d), row),
            pl.BlockSpec((tm * TOKEN_ROWS, LANES), row),
            pl.BlockSpec((tm, LANES), row),
            pl.BlockSpec((SUBLANES, tm), lambda i: (0, i)),
            pl.BlockSpec((ROUTER_ROWS, LANES), const),
        ],
        out_shape=[
            jax.ShapeDtypeStruct((t, d), F32),
            jax.ShapeDtypeStruct((t * TOKEN_ROWS, LANES), U32),
            jax.ShapeDtypeStruct((t, LANES), F32),
            jax.ShapeDtypeStruct((SUBLANES, t), F32),
            jax.ShapeDtypeStruct((ROUTER_ROWS, LANES), F32),
        ],
        scratch_shapes=[pltpu.VMEM((ROUTER_ROWS, LANES), F32), pltpu.VMEM((d, d), BF16)],
        compiler_params=_cparams(("arbitrary",)),
        name="outproj_router",
    )(x2, oa2, ob2, w_out, g_ffn, w_router, tri)


DMA_ISSUE_UNROLL = 8
DMA_PRIORITIES = 2


def _dispatch_kernel(dest_ref, pend_ref, nu_ref, xn_ref, xs_ref, src_ref, zbuf, ibuf, sem, zsem):
    tm = xn_ref.shape[0] // TOKEN_ROWS
    t_total = pl.num_programs(0) * tm
    base = pl.program_id(0) * tm
    n_blocks = xs_ref.shape[0] // (MOE_BLOCK * TOKEN_ROWS)

    @pl.when(pl.program_id(0) == 0)
    def _():
        ibuf[...] = jnp.full(ibuf.shape, -1, jnp.int32)
        fill = pltpu.make_async_copy(ibuf, src_ref, zsem)
        fill.start()
        fill.wait()
        zbuf[...] = jnp.zeros_like(zbuf)

        def zero_block(blk):
            off = pl.multiple_of(blk * (MOE_BLOCK * TOKEN_ROWS), MOE_BLOCK * TOKEN_ROWS)
            return pltpu.make_async_copy(zbuf, xs_ref.at[pl.ds(off, MOE_BLOCK * TOKEN_ROWS), :], zsem)

        def for_each_zeroed_block(fn):
            def per_expert(e, c):
                end = pend_ref[e]
                start = jnp.where(e == 0, 0, pend_ref[jnp.maximum(e - 1, 0)])

                @pl.when(end > start)
                def _():
                    fn(zero_block(end // MOE_BLOCK - 1))
                return c
            lax.fori_loop(0, N_EXPERTS, per_expert, 0)

            def per_tail_block(blk, c):
                fn(zero_block(blk))
                return c
            lax.fori_loop(nu_ref[0], n_blocks, per_tail_block, 0)

        for_each_zeroed_block(lambda cp: cp.start())
        for_each_zeroed_block(lambda cp: cp.wait())

    def issue(group, c):
        toks = [group * DMA_ISSUE_UNROLL + g for g in range(DMA_ISSUE_UNROLL)]
        dests = [[dest_ref[slot * t_total + base + r] for slot in range(TOP_K)] for r in toks]
        for r, ds in zip(toks, dests):
            src = xn_ref.at[pl.ds(pl.multiple_of(r * TOKEN_ROWS, TOKEN_ROWS), TOKEN_ROWS), :]
            for slot, d in enumerate(ds):
                dst = xs_ref.at[pl.ds(pl.multiple_of(d * TOKEN_ROWS, TOKEN_ROWS), TOKEN_ROWS), :]
                pltpu.make_async_copy(src, dst, sem).start(priority=slot % DMA_PRIORITIES)
        for r, ds in zip(toks, dests):
            for slot, d in enumerate(ds):
                src_ref[d] = slot * t_total + base + r
        return c
    lax.fori_loop(0, tm // DMA_ISSUE_UNROLL, issue, 0)

    for slot in range(TOP_K):
        pltpu.make_async_copy(xn_ref, xs_ref.at[pl.ds(0, tm * TOKEN_ROWS), :], sem).wait()


def _dispatch(xn_tiles, dest_flat, pad_end, n_used, p_len, tm):
    t = xn_tiles.shape[0] // TOKEN_ROWS
    grid_spec = pltpu.PrefetchScalarGridSpec(
        num_scalar_prefetch=3,
        grid=(t // tm,),
        in_specs=[pl.BlockSpec((tm * TOKEN_ROWS, LANES), lambda i, dest, pend, nu: (i, 0))],
        out_specs=[pl.BlockSpec(memory_space=pl.ANY), pl.BlockSpec(memory_space=pltpu.SMEM)],
        scratch_shapes=[pltpu.VMEM((MOE_BLOCK * TOKEN_ROWS, LANES), U32),
                        pltpu.VMEM((p_len,), jnp.int32),
                        pltpu.SemaphoreType.DMA(()), pltpu.SemaphoreType.DMA(())],
    )
    return pl.pallas_call(
        _dispatch_kernel,
        grid_spec=grid_spec,
        out_shape=[jax.ShapeDtypeStruct((p_len * TOKEN_ROWS, LANES), U32),
                   jax.ShapeDtypeStruct((p_len,), jnp.int32)],
        compiler_params=_cparams(("arbitrary",)),
        name="dispatch",
    )(dest_flat, pad_end, n_used, xn_tiles)


def _experts_kernel(be_ref, nu_ref, nxt_ref, ord_ref, src_ref, xs_ref, w1_hbm, w3_hbm, w2_hbm, y2_ref,
                    w1_scr, w3_scr, w2_scr, w1_buf, w3_buf, w2_buf, ybuf, wsems, ysems):
    step = pl.program_id(0)
    first = 2 * step
    second = first + 1
    n_used = nu_ref[0]
    cur = step % 2
    pair_rows = 2 * MOE_BLOCK
    n_slot_rows = y2_ref.shape[0] // TOKEN_ROWS - 2 * pair_rows

    def send_previous(unrolled):
        prev_base = jnp.maximum(step - 1, 0) * pair_rows

        def send(r, priority):
            s = src_ref[prev_base + r]
            row = jnp.where((step > 0) & (s >= 0), s, n_slot_rows + cur * pair_rows + r)
            src = ybuf.at[1 - cur, pl.ds(pl.multiple_of(r * TOKEN_ROWS, TOKEN_ROWS), TOKEN_ROWS), :]
            dst = y2_ref.at[pl.ds(pl.multiple_of(row * TOKEN_ROWS, TOKEN_ROWS), TOKEN_ROWS), :]
            pltpu.make_async_copy(src, dst, ysems.at[cur]).start(priority=priority)

        if unrolled:
            for r in range(pair_rows):
                send(r, r % DMA_PRIORITIES)
        else:
            def send_group(j, c):
                for k in range(DMA_PRIORITIES):
                    send(j * DMA_PRIORITIES + k, k)
                return c
            lax.fori_loop(0, pair_rows // DMA_PRIORITIES, send_group, 0, unroll=DMA_ISSUE_UNROLL // DMA_PRIORITIES)

    def all_rows_sent(buf, sem):
        return pltpu.make_async_copy(ybuf.at[buf], y2_ref.at[pl.ds(0, pair_rows * TOKEN_ROWS), :], sem)

    def wait_older_sends():
        all_rows_sent(cur, ysems.at[1 - cur]).wait()

    @pl.when(step == 0)
    def _():
        ybuf[1] = jnp.zeros(ybuf.shape[1:], U32)
        odd_dump = (n_slot_rows + pair_rows) * TOKEN_ROWS
        pltpu.make_async_copy(ybuf.at[1], y2_ref.at[pl.ds(odd_dump, pair_rows * TOKEN_ROWS), :], ysems.at[1]).start()

    def weight_copies(e, slot):
        return (pltpu.make_async_copy(w1_hbm.at[e], w1_buf.at[slot], wsems.at[slot, 0]),
                pltpu.make_async_copy(w3_hbm.at[e], w3_buf.at[slot], wsems.at[slot, 1]),
                pltpu.make_async_copy(w2_hbm.at[e], w2_buf.at[slot], wsems.at[slot, 2]))

    def switch_to(e):
        slot = ord_ref[e] % 2
        for cp in weight_copies(e, slot):
            cp.wait()
        nxt = nxt_ref[e]

        @pl.when(nxt >= 0)
        def _():
            for cp in weight_copies(nxt, 1 - slot):
                cp.start()
        w1_scr[...] = w1_buf[slot].astype(BF16)
        w3_scr[...] = w3_buf[slot].astype(BF16)
        w2_scr[...] = w2_buf[slot].astype(BF16)

    def gated_mlp(row0, n_rows, first_store):
        x = _unpack_bf16_pairs(_load_token_tiles(xs_ref, n_rows, row0)).astype(BF16)
        a = jnp.dot(x, w1_scr[...], preferred_element_type=F32)
        b = jnp.dot(x, w3_scr[...], preferred_element_type=F32)
        hdn = a * (1.0 / (1.0 + jnp.exp(-a))) * b
        y = jnp.dot(hdn.astype(BF16), w2_scr[...], preferred_element_type=F32)
        if first_store:
            wait_older_sends()
        _store_token_tiles(ybuf.at[cur], _pack_bf16_pairs(y), row0)

    def zero_rows(row0, n_rows):
        ybuf[cur, row0 * TOKEN_ROWS:(row0 + n_rows) * TOKEN_ROWS, :] = jnp.zeros((n_rows * TOKEN_ROWS, LANES), U32)

    @pl.when(first < n_used)
    def _():
        e_first = be_ref[first]
        e_before = be_ref[jnp.maximum(first - 1, 0)]

        @pl.when(first == 0)
        def _():
            for cp in weight_copies(e_first, ord_ref[e_first] % 2):
                cp.start()

        @pl.when((first == 0) | (e_first != e_before))
        def _():
            switch_to(e_first)

        second_used = second < n_used
        e_second = be_ref[second]
        same_expert = second_used & (e_second == e_first)

        @pl.when(same_expert)
        def _():
            send_previous(unrolled=True)
            gated_mlp(0, 2 * MOE_BLOCK, first_store=True)

        @pl.when(jnp.logical_not(same_expert))
        def _():
            send_previous(unrolled=True)
            gated_mlp(0, MOE_BLOCK, first_store=True)

            @pl.when(second_used)
            def _():
                switch_to(e_second)
                gated_mlp(MOE_BLOCK, MOE_BLOCK, first_store=False)

            @pl.when(jnp.logical_not(second_used))
            def _():
                zero_rows(MOE_BLOCK, MOE_BLOCK)

    @pl.when(first >= n_used)
    def _():
        send_previous(unrolled=False)
        wait_older_sends()
        zero_rows(0, 2 * MOE_BLOCK)

    @pl.when(step == pl.num_programs(0) - 1)
    def _():
        all_rows_sent(1 - cur, ysems.at[cur]).wait()


def _experts(xs, row_src, block_e, n_used, next_expert, expert_ord, w1, w3, w2, n_slot_rows):
    p_len = xs.shape[0] // TOKEN_ROWS
    n_blocks = p_len // MOE_BLOCK
    assert n_blocks % 2 == 0
    n_pairs = n_blocks // 2
    d, de = w1.shape[1], w1.shape[2]
    blk = lambda i, be, nu, nxt, od, src: (jnp.maximum(jnp.minimum(i, (nu[0] - 1) // 2), 0), 0)
    hbm = pl.BlockSpec(memory_space=pl.ANY)
    grid_spec = pltpu.PrefetchScalarGridSpec(
        num_scalar_prefetch=5,
        grid=(n_pairs + 1,),
        in_specs=[pl.BlockSpec((2 * MOE_BLOCK * TOKEN_ROWS, LANES), blk), hbm, hbm, hbm],
        out_specs=hbm,
        scratch_shapes=[pltpu.VMEM((d, de), BF16), pltpu.VMEM((d, de), BF16), pltpu.VMEM((de, d), BF16),
                        pltpu.VMEM((2, d, de), F32), pltpu.VMEM((2, d, de), F32), pltpu.VMEM((2, de, d), F32),
                        pltpu.VMEM((2, 2 * MOE_BLOCK * TOKEN_ROWS, LANES), U32),
                        pltpu.SemaphoreType.DMA((2, 3)), pltpu.SemaphoreType.DMA((2,))],
    )
    return pl.pallas_call(
        _experts_kernel,
        grid_spec=grid_spec,
        out_shape=jax.ShapeDtypeStruct(((n_slot_rows + 4 * MOE_BLOCK) * TOKEN_ROWS, LANES), U32),
        compiler_params=_cparams(("arbitrary",)),
        name="experts",
    )(block_e, n_used, next_expert, expert_ord, row_src, xs, w1, w3, w2)


def _final_kernel(h_ref, meta_ref, p_ref, wple_ref, gple_ref, gplg_ref, wplg_ref, y_first_ref, y_second_ref, out_ref,
                  wple_scr, wplg_scr):
    tm = h_ref.shape[0]

    @pl.when(pl.program_id(0) == 0)
    def _():
        wple_scr[...] = wple_ref[...].astype(BF16)
        wplg_scr[...] = wplg_ref[...].astype(BF16)

    emb = jnp.dot(p_ref[...].astype(BF16), wple_scr[...], preferred_element_type=F32)
    emb = _rms_rows(emb) * gple_ref[...]
    meta = meta_ref[...]
    y1 = _unpack_bf16_pairs(_load_token_tiles(y_first_ref, tm))
    y2 = _unpack_bf16_pairs(_load_token_tiles(y_second_ref, tm))
    h = h_ref[...] + meta[:, 4:5] * y1 + meta[:, 5:6] * y2
    hn = (_rms_rows(h) * gplg_ref[...]).astype(BF16)
    z = jnp.dot(hn, wplg_scr[...], preferred_element_type=F32)
    out_ref[...] = h + emb * (1.0 / (1.0 + jnp.exp(-z)))


def _final(h2, meta, p2, w_ple, g_ple, g_plg, w_plg, y_slots, tm):
    t, d = h2.shape
    dp = p2.shape[1]
    nt = t // tm
    const = lambda i: (0, 0)
    row = lambda i: (i, 0)
    return pl.pallas_call(
        _final_kernel,
        grid=(nt,),
        in_specs=[
            pl.BlockSpec((tm, d), row),
            pl.BlockSpec((tm, LANES), row),
            pl.BlockSpec((tm, dp), row),
            pl.BlockSpec((dp, d), const, pipeline_mode=pl.Buffered(1)),
            pl.BlockSpec((1, d), const),
            pl.BlockSpec((1, d), const),
            pl.BlockSpec((d, d), const, pipeline_mode=pl.Buffered(1)),
            pl.BlockSpec((tm * TOKEN_ROWS, LANES), row),
            pl.BlockSpec((tm * TOKEN_ROWS, LANES), lambda i: (nt + i, 0)),
        ],
        out_specs=pl.BlockSpec((tm, d), row),
        out_shape=jax.ShapeDtypeStruct((t, d), F32),
        scratch_shapes=[pltpu.VMEM((dp, d), BF16), pltpu.VMEM((d, d), BF16)],
        compiler_params=_cparams(("arbitrary",)),
        name="combine_ple",
    )(h2, meta, p2, w_ple, g_ple, g_plg, w_plg, y_slots, y_slots)


def _block_diag_mean(seg):
    m = np.kron(np.eye(256 // seg), np.full((seg, seg), 1.0 / seg))
    return jnp.asarray(m, BF16)


def _rope_tables(seq):
    f32 = np.float32
    inv = np.power(f32(ROPE_THETA), -np.arange(0, DA_QK, 2, dtype=f32) / f32(DA_QK)).astype(f32)
    ang = (np.arange(seq, dtype=f32)[:, None] * inv[None, :]).astype(f32)
    cos, sin = np.cos(ang).astype(f32), np.sin(ang).astype(f32)
    cos_t = np.tile(cos, (1, LANES // cos.shape[1]))
    sin_t = np.tile(np.concatenate([-sin, sin], axis=1), (1, LANES // (2 * sin.shape[1])))
    return jnp.asarray(cos_t), jnp.asarray(sin_t)


def _tile_for(n, pref):
    while n % pref:
        pref //= 2
    return pref


def kernel(x, p, g_mix, w_in, g_qa, g_ka, lam_q1, lam_k1, lam_q2, lam_k2, g_sub, g_qb, g_kb, rpb, w_out,
           g_ffn, w_rg, w_re, w1, w3, w2, g_plg, w_plg, w_ple, g_ple):
    b, s, d = x.shape
    assert d == 2 * TOKEN_ROWS * LANES, "token-tile layout packs one token into TOKEN_ROWS rows of 128 uint32"
    t = b * s
    depth = w_in.shape[0]
    tm = _tile_for(s, 1024)
    tmd = _tile_for(t, 2048)
    tmf = _tile_for(t, 1024)
    n_slots = t * TOP_K
    n_blocks = n_slots // MOE_BLOCK + N_EXPERTS
    p_len = n_blocks * MOE_BLOCK

    bd64 = _block_diag_mean(DA_QK)
    bd32 = _block_diag_mean(DB_H)
    cos_t, sin_t = _rope_tables(s)
    tri = jnp.asarray(np.triu(np.ones((tm, tm)), 1), BF16)

    h = x.reshape(t, d)
    for i in range(depth):
        lam_init = 0.8 - 0.6 * math.exp(-0.3 * i)
        gsec = jnp.stack([
            jnp.tile(g_qa[i], 512 // DA_QK) * (DA_QK ** -0.5 * math.log2(math.e)),
            jnp.tile(g_ka[i], 512 // DA_QK),
            jnp.tile(g_qb[i], 512 // DB_H) * DB_H ** -0.5,
            jnp.tile(g_kb[i], 512 // DB_H),
        ]).astype(F32)
        proj, qvt = _inproj(h, g_mix[i][None, :], w_in[i], bd64, bd32, gsec, cos_t, sin_t, s, tm)
        proj3 = proj.reshape(b, s, proj.shape[1])

        lam4 = jnp.stack([lam_q1[i], lam_k1[i], lam_q2[i], lam_k2[i]]).astype(F32)
        oa = _diffattn(proj3, qvt, lam4, g_sub[i][None, :].astype(F32), lam_init)
        ob = _natten(proj3, _natten_bias_table(rpb[i]))

        w_router = jnp.concatenate(
            [w_rg[i], w_re[i], jnp.zeros((d, LANES - N_GROUPS - N_EXPERTS), F32)], axis=1).astype(F32)
        wr_hi = w_router.astype(BF16)
        wr_lo = (w_router - wr_hi.astype(F32)).astype(BF16)
        h1, xn, meta, meta_t, cnt = _outproj(h, oa.reshape(t, -1), ob.reshape(t, -1), w_out[i],
                                             g_ffn[i][None, :], jnp.concatenate([wr_hi, wr_lo], axis=1), tri, tm)

        counts = cnt[E_LANE0:E_LANE0 + N_EXPERTS, 0].astype(jnp.int32)
        padded = (counts + MOE_BLOCK - 1) // MOE_BLOCK * MOE_BLOCK
        pad_end = jnp.cumsum(padded)
        pad_start = pad_end - padded
        mt = meta_t[0:2 * TOP_K].astype(jnp.int32)
        e_ids = jnp.arange(N_EXPERTS, dtype=jnp.int32)[None, :, None]
        start_of = jnp.sum(jnp.where(mt[:TOP_K, None, :] == e_ids, pad_start[None, :, None], 0), axis=1)
        dest = start_of + mt[TOP_K:]
        blk_start = jnp.arange(n_blocks, dtype=jnp.int32) * MOE_BLOCK
        block_e = jnp.minimum(jnp.sum(pad_end[None, :] <= blk_start[:, None], axis=1), N_EXPERTS - 1)
        block_e = block_e.astype(jnp.int32)
        n_used = (pad_end[-1:] // MOE_BLOCK).astype(jnp.int32)

        dest_flat = dest.reshape(TOP_K * t)

        xs, row_src = _dispatch(xn, dest_flat, pad_end.astype(jnp.int32), n_used, p_len, tmd)
        in_use = counts > 0
        expert_ord = (jnp.cumsum(in_use) - 1).astype(jnp.int32)
        ids = jnp.arange(N_EXPERTS, dtype=jnp.int32)
        later = jnp.where((ids[None, :] > ids[:, None]) & in_use[None, :], ids[None, :], N_EXPERTS)
        next_expert = jnp.min(later, axis=1)
        next_expert = jnp.where(next_expert == N_EXPERTS, -1, next_expert).astype(jnp.int32)
        y_slots = _experts(xs, row_src.reshape(p_len), block_e, n_used, next_expert, expert_ord,
                           w1[i], w3[i], w2[i], TOP_K * t)
        h = _final(h1, meta, p[i].reshape(t, -1), w_ple[i], g_ple[i][None, :], g_plg[i][None, :], w_plg[i],
                   y_slots, tmf)
    return h.reshape(b, s, d)
```

```python
import functools
import math

import numpy as np
import jax
import jax.numpy as jnp
from jax import lax
from jax.experimental import pallas as pl
from jax.experimental.pallas import tpu as pltpu

F32 = jnp.float32
BF16 = jnp.bfloat16
U32 = jnp.uint32

LANES = 128
HA = 4
DA_QK = 64
DA_V = 128
HB = 16
DB_H = 32
GRID_W = 64
WIN_R = 8
WIN_C = 16
ROPE_THETA = 10000.0
EPS = 1e-6
NEG = -1e30
N_GROUPS = 4
EXPERTS_PER_GROUP = 8
N_EXPERTS = N_GROUPS * EXPERTS_PER_GROUP
TOP_K = 2
MOE_BLOCK = 256
E_LANE0 = N_GROUPS
ROUTER_ROWS = 40

VMEM_LIMIT = 56 * 1024 * 1024


def _cparams(sem):
    return pltpu.CompilerParams(dimension_semantics=sem, vmem_limit_bytes=VMEM_LIMIT)


def _rms_rows(x):
    return x * lax.rsqrt(jnp.mean(x * x, axis=-1, keepdims=True) + EPS)


SUBLANES = 8

TOKEN_ROWS = 4
HIGH_HALF = 0xFFFF0000


def _pack_bf16_pairs(x):
    m = x.shape[1] // 2
    bits = pltpu.bitcast(x.astype(BF16).astype(F32), U32)
    return (bits[:, :m] >> 16) | (bits[:, m:] & jnp.uint32(HIGH_HALF))


def _unpack_bf16_pairs(u):
    lo = pltpu.bitcast(u << 16, F32)
    hi = pltpu.bitcast(u & jnp.uint32(HIGH_HALF), F32)
    return jnp.concatenate([lo, hi], axis=1)


def _store_token_tiles(ref, val, first=0):
    n = val.shape[0]
    for c in range(TOKEN_ROWS):
        ref[pl.ds(first * TOKEN_ROWS + c, n, stride=TOKEN_ROWS), :] = val[:, c * LANES:(c + 1) * LANES]


def _load_token_tiles(ref, n, first=0):
    return jnp.concatenate(
        [ref[pl.ds(first * TOKEN_ROWS + c, n, stride=TOKEN_ROWS), :] for c in range(TOKEN_ROWS)], axis=1)


def _inproj_kernel(x_ref, gmix_ref, w_ref, bd64_ref, bd32_ref, gsec_ref, cos_ref, sin_ref, out_ref, qvt_ref, w_scr):
    tm = x_ref.shape[0]

    @pl.when(pl.program_id(0) == 0)
    def _():
        w_scr[...] = w_ref[...].astype(BF16)

    a = (_rms_rows(x_ref[...]) * gmix_ref[...]).astype(BF16)
    lane = lax.broadcasted_iota(jnp.int32, (tm, LANES), 1)
    first_half = (lane % 64) < 32
    cos = cos_ref[...]
    sin = sin_ref[...]
    out_col = {1: 0, 3: 1, 4: 2, 5: 3}
    qvt_row = {0: 0, 2: 1}
    for sec in range(6):
        y = jnp.dot(a, w_scr[:, sec * 512:(sec + 1) * 512], preferred_element_type=F32)
        if sec in (0, 1, 3, 4):
            bd = bd64_ref if sec < 2 else bd32_ref
            gi = {0: 0, 1: 1, 3: 2, 4: 3}[sec]
            y2 = (y * y).astype(BF16)
            ms = jnp.concatenate([jnp.dot(y2[:, half * 256:(half + 1) * 256], bd[...], preferred_element_type=F32)
                                  for half in range(2)], axis=1)
            y = y * lax.rsqrt(ms + EPS) * gsec_ref[gi:gi + 1, :]
        for c in range(4):
            yc = y[:, c * LANES:(c + 1) * LANES]
            if sec < 2:
                rot = jnp.where(first_half, pltpu.roll(yc, 96, 1), pltpu.roll(yc, 32, 1))
                yc = yc * cos + rot * sin
            if sec in qvt_row:
                r0 = qvt_row[sec] * 512 + c * LANES
                qvt_ref[0, r0:r0 + LANES, :] = yc.T.astype(BF16)
            else:
                c0 = out_col[sec] * 512 + c * LANES
                out_ref[:, c0:c0 + LANES] = yc.astype(BF16)


def _inproj(x2, g_mix, w_f32, bd64, bd32, gsec, cos_t, sin_t, seq, tm):
    t, d = x2.shape
    n = w_f32.shape[1]
    nsb = seq // tm
    n_tok_major = 4 * 512
    n_feat_major = 2 * 512
    const = lambda i: (0, 0)
    return pl.pallas_call(
        _inproj_kernel,
        grid=(t // tm,),
        in_specs=[
            pl.BlockSpec((tm, d), lambda i: (i, 0)),
            pl.BlockSpec((1, d), const),
            pl.BlockSpec((d, n), const, pipeline_mode=pl.Buffered(1)),
            pl.BlockSpec((256, 256), const),
            pl.BlockSpec((256, 256), const),
            pl.BlockSpec((4, 512), const),
            pl.BlockSpec((tm, LANES), lambda i: (i % nsb, 0)),
            pl.BlockSpec((tm, LANES), lambda i: (i % nsb, 0)),
        ],
        out_specs=[pl.BlockSpec((tm, n_tok_major), lambda i: (i, 0)),
                   pl.BlockSpec((1, n_feat_major, tm), lambda i: (i // nsb, 0, i % nsb))],
        out_shape=[jax.ShapeDtypeStruct((t, n_tok_major), BF16),
                   jax.ShapeDtypeStruct((t // seq, n_feat_major, seq), BF16)],
        scratch_shapes=[pltpu.VMEM((d, n), BF16)],
        compiler_params=_cparams(("arbitrary",)),
        name="inproj",
    )(x2, g_mix, w_f32, bd64, bd32, gsec, cos_t, sin_t)


QBLOCK_PAIRS_PER_BODY = 4

def _diffattn_kernel(lam_ref, gsub_ref, qt_ref, k_ref, vt_ref, o_ref, sa_scr, sb_scr, *, seq, tq, tk, lam_init):
    nq = seq // tq
    nk = seq // tk
    s1 = jnp.sum(lam_ref[0:1, :] * lam_ref[1:2, :], axis=1, keepdims=True)
    s2 = jnp.sum(lam_ref[2:3, :] * lam_ref[3:4, :], axis=1, keepdims=True)
    lam = jnp.exp(s1) - jnp.exp(s2) + lam_init

    row = lax.broadcasted_iota(jnp.int32, (LANES, tq), 0)

    def scores(qi, s_scr):
        qoff = pl.multiple_of(qi * tq, tq)
        qt = qt_ref[0, :, pl.ds(qoff, tq)]
        zero = jnp.zeros_like(qt)
        qm = jnp.concatenate([jnp.where(row < DA_QK, qt, zero), jnp.where(row >= DA_QK, qt, zero)], axis=1)
        mx = None
        for j in range(nk):
            st = jnp.dot(k_ref[0, j * tk:(j + 1) * tk, :], qm, preferred_element_type=F32)
            s_scr[j * tk:(j + 1) * tk, :] = st
            cm = jnp.max(st, axis=0, keepdims=True)
            mx = cm if mx is None else jnp.maximum(mx, cm)
        return mx

    def finish(qi, s_scr, m):
        l = jnp.zeros((1, 2 * tq), F32)
        acc = jnp.zeros((DA_V, 2 * tq), F32)
        for j in range(nk):
            p = jnp.exp2(s_scr[j * tk:(j + 1) * tk, :] - m)
            l = l + jnp.sum(p, axis=0, keepdims=True)
            acc = acc + jnp.dot(vt_ref[0, :, j * tk:(j + 1) * tk], p.astype(BF16), preferred_element_type=F32)
        o = acc * (1.0 / l)
        ot = o[:, :tq] - lam * o[:, tq:]
        ot = ot * lax.rsqrt(jnp.mean(ot * ot, axis=0, keepdims=True) + EPS) * (1.0 - lam_init)
        qoff = pl.multiple_of(qi * tq, tq)
        o_ref[0, pl.ds(qoff, tq), :] = (ot.T * gsub_ref[...]).astype(BF16)

    def body(i, m_a):
        for pair in range(QBLOCK_PAIRS_PER_BODY):
            q0 = 2 * (QBLOCK_PAIRS_PER_BODY * i + pair)
            m_b = scores(q0 + 1, sb_scr)
            finish(q0, sa_scr, m_a)
            m_a = scores(jnp.minimum(q0 + 2, nq - 1), sa_scr)
            finish(q0 + 1, sb_scr, m_b)
        return m_a

    lax.fori_loop(0, nq // (2 * QBLOCK_PAIRS_PER_BODY), body, scores(0, sa_scr))


def _diffattn(proj3, qvt, lam4, g_sub, lam_init, tq=128, tk=512):
    b, seq, _ = proj3.shape
    kern = functools.partial(_diffattn_kernel, seq=seq, tq=tq, tk=tk, lam_init=lam_init)
    return pl.pallas_call(
        kern,
        grid=(b, HA),
        in_specs=[
            pl.BlockSpec((4, DA_QK), lambda i, h: (0, 0)),
            pl.BlockSpec((1, DA_V), lambda i, h: (0, 0)),
            pl.BlockSpec((1, LANES, seq), lambda i, h: (i, h, 0)),
            pl.BlockSpec((1, seq, LANES), lambda i, h: (i, 0, h)),
            pl.BlockSpec((1, LANES, seq), lambda i, h: (i, HA + h, 0)),
        ],
        out_specs=pl.BlockSpec((1, seq, LANES), lambda i, h: (i, 0, h)),
        out_shape=jax.ShapeDtypeStruct((b, seq, HA * DA_V), BF16),
        scratch_shapes=[pltpu.VMEM((seq, 2 * tq), F32), pltpu.VMEM((seq, 2 * tq), F32)],
        compiler_params=_cparams(("parallel", "parallel")),
        name="diffattn",
    )(lam4, g_sub, qvt, proj3, qvt)


def _natten_kernel(bias_ref, q_ref, k_ref, v_ref, o_ref, *, rows):
    kr = min(WIN_R, rows)
    nkeys = kr * GRID_W
    hpg = LANES // DB_H
    lane_head = lax.broadcasted_iota(jnp.int32, (GRID_W, LANES), 1) // DB_H

    def row_step(r, carry):
        rs = jnp.clip(r - kr // 2, 0, rows - kr)
        base = rs - r + (WIN_R - 1)
        qoff = pl.multiple_of(r * GRID_W, GRID_W)
        koff = pl.multiple_of(rs * GRID_W, GRID_W)
        qr = q_ref[0, pl.ds(qoff, GRID_W), :]
        zero = jnp.zeros_like(qr)
        qs = jnp.concatenate([jnp.where(lane_head == h, qr, zero) for h in range(hpg)], axis=0)
        kw = k_ref[0, pl.ds(koff, nkeys), :]
        vw = v_ref[0, pl.ds(koff, nkeys), :]
        s = lax.dot_general(qs, kw, (((1,), (1,)), ((), ())), preferred_element_type=F32)
        bias = jnp.concatenate([bias_ref[0, base + 2 * i] for i in range(kr // 2)], axis=1)
        s = s + bias
        m = jnp.max(s, axis=1, keepdims=True)
        p = jnp.exp(s - m).astype(BF16)
        pv = jnp.dot(p, jnp.concatenate([vw, jnp.ones_like(vw)], axis=1), preferred_element_type=F32)
        o = pv[:, :LANES] * (1.0 / pv[:, LANES:])
        out = jnp.zeros((GRID_W, LANES), F32)
        for h in range(hpg):
            out = out + jnp.where(lane_head == h, o[h * GRID_W:(h + 1) * GRID_W, :], 0.0)
        o_ref[0, pl.ds(qoff, GRID_W), :] = out.astype(BF16)
        return carry

    lax.fori_loop(0, rows, row_step, 0, unroll=min(rows, 64))


def _natten(proj3, bias_tab):
    b, seq, _ = proj3.shape
    rows = seq // GRID_W
    ng = HB * DB_H // LANES
    nro = bias_tab.shape[1]
    kern = functools.partial(_natten_kernel, rows=rows)
    cb = HA
    return pl.pallas_call(
        kern,
        grid=(b, ng),
        in_specs=[
            pl.BlockSpec((1, nro, bias_tab.shape[2], LANES), lambda i, g: (g, 0, 0, 0)),
            pl.BlockSpec((1, seq, LANES), lambda i, g: (i, 0, cb + g)),
            pl.BlockSpec((1, seq, LANES), lambda i, g: (i, 0, cb + ng + g)),
            pl.BlockSpec((1, seq, LANES), lambda i, g: (i, 0, cb + 2 * ng + g)),
        ],
        out_specs=pl.BlockSpec((1, seq, LANES), lambda i, g: (i, 0, g)),
        out_shape=jax.ShapeDtypeStruct((b, seq, HB * DB_H), BF16),
        compiler_params=_cparams(("parallel", "parallel")),
        name="natten",
    )(bias_tab, proj3, proj3, proj3)


def _natten_bias_table(rpb):
    q = np.arange(GRID_W)[:, None]
    kc = np.arange(GRID_W)[None, :]
    cs = np.clip(q - WIN_C // 2, 0, GRID_W - WIN_C)
    valid = (kc >= cs) & (kc < cs + WIN_C)
    col_off = np.clip(kc - q + (WIN_C - 1), 0, 2 * WIN_C - 2)
    ncol = 2 * WIN_C - 1
    nro = 2 * WIN_R - 2
    pick = np.zeros((2, ncol, GRID_W, 2 * GRID_W), np.float32)
    for j in range(2):
        pick[j, :, :, j * GRID_W:(j + 1) * GRID_W] = (np.arange(ncol)[:, None, None] == col_off[None]) & valid[None]
    negmask = np.tile(np.where(valid, 0.0, NEG).astype(np.float32), (1, 2))
    hpg = LANES // DB_H
    r4 = rpb.astype(F32).reshape(HB // hpg, hpg, 2 * WIN_R - 1, ncol)
    rp = jnp.stack([r4[:, :, :nro], r4[:, :, 1:]], axis=0)
    tab = jnp.einsum('jghrc,jcqk->grhqk', rp, jnp.asarray(pick), precision=lax.Precision.HIGHEST) + negmask
    return tab.reshape(HB // hpg, nro, hpg * GRID_W, LANES)


def _outproj_kernel(x_ref, oa_ref, ob_ref, wo_ref, gffn_ref, wr_ref, tri_ref,
                    h_ref, xn_ref, meta_ref, meta_t_ref, cnt_ref, run_scr, wo_scr):
    tm = x_ref.shape[0]
    da = oa_ref.shape[1]

    @pl.when(pl.program_id(0) == 0)
    def _():
        run_scr[...] = jnp.zeros_like(run_scr)
        wo_scr[...] = wo_ref[...].astype(BF16)

    attn = jnp.dot(oa_ref[...], wo_scr[:da, :], preferred_element_type=F32)
    attn = attn + jnp.dot(ob_ref[...], wo_scr[da:, :], preferred_element_type=F32)
    h = x_ref[...] + attn
    h_ref[...] = h
    xn = _rms_rows(h) * gffn_ref[...]
    _store_token_tiles(xn_ref, _pack_bf16_pairs(xn))

    xh = xn.astype(BF16)
    xl = (xn - xh.astype(F32)).astype(BF16)
    hh_hl = jnp.dot(xh, wr_ref[...], preferred_element_type=F32)
    lg = hh_hl[:, :LANES] + hh_hl[:, LANES:] + jnp.dot(xl, wr_ref[:, :LANES], preferred_element_type=F32)
    lt = lg.T
    row = [lt[j:j + 1, :] for j in range(E_LANE0 + N_EXPERTS)]

    def first_argmax(vals):
        best = vals[0]
        for v in vals[1:]:
            best = jnp.maximum(best, v)
        idx = jnp.full_like(best, float(len(vals) - 1))
        for j in range(len(vals) - 2, -1, -1):
            idx = jnp.where(vals[j] == best, float(j), idx)
        return best, idx

    gmax, gidx = first_argmax(row[:N_GROUPS])
    gsum = jnp.exp(row[0] - gmax)
    for j in range(1, N_GROUPS):
        gsum = gsum + jnp.exp(row[j] - gmax)
    g_gate = 1.0 / gsum
    el = []
    for k in range(EXPERTS_PER_GROUP):
        v = row[E_LANE0 + k]
        for g in range(1, N_GROUPS):
            v = jnp.where(gidx == float(g), row[E_LANE0 + g * EXPERTS_PER_GROUP + k], v)
        el.append(v)
    l1, k1 = first_argmax(el)
    l2, k2 = first_argmax([jnp.where(k1 == float(k), -jnp.inf, el[k]) for k in range(EXPERTS_PER_GROUP)])
    r = jnp.exp(l2 - l1)
    w1 = g_gate / (1.0 + r)
    w2 = w1 * r
    e1 = gidx * EXPERTS_PER_GROUP + k1
    e2 = gidx * EXPERTS_PER_GROUP + k2

    rowid = lax.broadcasted_iota(jnp.int32, (ROUTER_ROWS, tm), 0).astype(F32)
    sel1 = rowid == e1 + E_LANE0
    sel2 = rowid == e2 + E_LANE0
    oh = jnp.where(sel1 | sel2, 1.0, 0.0).astype(BF16)
    run = run_scr[...]
    before = jnp.dot(oh, tri_ref[...], preferred_element_type=F32) + jnp.concatenate([run] * (tm // LANES), axis=1)
    rank1 = jnp.sum(jnp.where(sel1, before, 0.0), axis=0, keepdims=True)
    rank2 = jnp.sum(jnp.where(sel2, before, 0.0), axis=0, keepdims=True)
    run = run + jnp.dot(oh, jnp.ones((tm, LANES), BF16), preferred_element_type=F32)
    run_scr[...] = run
    cnt_ref[...] = run

    meta_t = jnp.concatenate([e1, e2, rank1, rank2, w1, w2, jnp.zeros((SUBLANES - 6, tm), F32)], axis=0)
    meta_t_ref[...] = meta_t
    meta_ref[...] = jnp.concatenate([meta_t, jnp.zeros((LANES - SUBLANES, tm), F32)], axis=0).T


def _outproj(x2, oa2, ob2, w_out, g_ffn, w_router, tri, tm):
    t, d = x2.shape
    da = oa2.shape[1]
    const = lambda i: (0, 0)
    row = lambda i: (i, 0)
    return pl.pallas_call(
        _outproj_kernel,
        grid=(t // tm,),
        in_specs=[
            pl.BlockSpec((tm, d), row),
            pl.BlockSpec((tm, da), row),
            pl.BlockSpec((tm, d - da), row),
            pl.BlockSpec((d, d), const, pipeline_mode=pl.Buffered(1)),
            pl.BlockSpec((1, d), const),
            pl.BlockSpec((d, 2 * LANES), const),
            pl.BlockSpec((tm, tm), const),
        ],
        out_specs=[
            pl.BlockSpec((tm, d), row),
            pl.BlockSpec((tm * TOKEN_ROWS, LANES), row),
            pl.BlockSpec((tm, LANES), row),
            pl.BlockSpec((SUBLANES, tm), lambda i: (0, i)),
            pl.BlockSpec((ROUTER_ROWS, LANES), const),
        ],
        out_shape=[
            jax.ShapeDtypeStruct((t, d), F32),
            jax.ShapeDtypeStruct((t * TOKEN_ROWS, LANES), U32),
            jax.ShapeDtypeStruct((t, LANES), F32),
            jax.ShapeDtypeStruct((SUBLANES, t), F32),
            jax.ShapeDtypeStruct((ROUTER_ROWS, LANES), F32),
        ],
        scratch_shapes=[pltpu.VMEM((ROUTER_ROWS, LANES), F32), pltpu.VMEM((d, d), BF16)],
        compiler_params=_cparams(("arbitrary",)),
        name="outproj_router",
    )(x2, oa2, ob2, w_out, g_ffn, w_router, tri)


DMA_ISSUE_UNROLL = 8
DMA_PRIORITIES = 2


def _dispatch_kernel(dest_ref, pend_ref, nu_ref, xn_ref, xs_ref, src_ref, zbuf, ibuf, sem, zsem):
    tm = xn_ref.shape[0] // TOKEN_ROWS
    t_total = pl.num_programs(0) * tm
    base = pl.program_id(0) * tm
    n_blocks = xs_ref.shape[0] // (MOE_BLOCK * TOKEN_ROWS)

    @pl.when(pl.program_id(0) == 0)
    def _():
        ibuf[...] = jnp.full(ibuf.shape, -1, jnp.int32)
        fill = pltpu.make_async_copy(ibuf, src_ref, zsem)
        fill.start()
        fill.wait()
        zbuf[...] = jnp.zeros_like(zbuf)

        def zero_block(blk):
            off = pl.multiple_of(blk * (MOE_BLOCK * TOKEN_ROWS), MOE_BLOCK * TOKEN_ROWS)
            return pltpu.make_async_copy(zbuf, xs_ref.at[pl.ds(off, MOE_BLOCK * TOKEN_ROWS), :], zsem)

        def for_each_zeroed_block(fn):
            def per_expert(e, c):
                end = pend_ref[e]
                start = jnp.where(e == 0, 0, pend_ref[jnp.maximum(e - 1, 0)])

                @pl.when(end > start)
                def _():
                    fn(zero_block(end // MOE_BLOCK - 1))
                return c
            lax.fori_loop(0, N_EXPERTS, per_expert, 0)

            def per_tail_block(blk, c):
                fn(zero_block(blk))
                return c
            lax.fori_loop(nu_ref[0], n_blocks, per_tail_block, 0)

        for_each_zeroed_block(lambda cp: cp.start())
        for_each_zeroed_block(lambda cp: cp.wait())

    def issue(group, c):
        toks = [group * DMA_ISSUE_UNROLL + g for g in range(DMA_ISSUE_UNROLL)]
        dests = [[dest_ref[slot * t_total + base + r] for slot in range(TOP_K)] for r in toks]
        for r, ds in zip(toks, dests):
            src = xn_ref.at[pl.ds(pl.multiple_of(r * TOKEN_ROWS, TOKEN_ROWS), TOKEN_ROWS), :]
            for slot, d in enumerate(ds):
                dst = xs_ref.at[pl.ds(pl.multiple_of(d * TOKEN_ROWS, TOKEN_ROWS), TOKEN_ROWS), :]
                pltpu.make_async_copy(src, dst, sem).start(priority=slot % DMA_PRIORITIES)
        for r, ds in zip(toks, dests):
            for slot, d in enumerate(ds):
                src_ref[d] = slot * t_total + base + r
        return c
    lax.fori_loop(0, tm // DMA_ISSUE_UNROLL, issue, 0)

    for slot in range(TOP_K):
        pltpu.make_async_copy(xn_ref, xs_ref.at[pl.ds(0, tm * TOKEN_ROWS), :], sem).wait()


def _dispatch(xn_tiles, dest_flat, pad_end, n_used, p_len, tm):
    t = xn_tiles.shape[0] // TOKEN_ROWS
    grid_spec = pltpu.PrefetchScalarGridSpec(
        num_scalar_prefetch=3,
        grid=(t // tm,),
        in_specs=[pl.BlockSpec((tm * TOKEN_ROWS, LANES), lambda i, dest, pend, nu: (i, 0))],
        out_specs=[pl.BlockSpec(memory_space=pl.ANY), pl.BlockSpec(memory_space=pltpu.SMEM)],
        scratch_shapes=[pltpu.VMEM((MOE_BLOCK * TOKEN_ROWS, LANES), U32),
                        pltpu.VMEM((p_len,), jnp.int32),
                        pltpu.SemaphoreType.DMA(()), pltpu.SemaphoreType.DMA(())],
    )
    return pl.pallas_call(
        _dispatch_kernel,
        grid_spec=grid_spec,
        out_shape=[jax.ShapeDtypeStruct((p_len * TOKEN_ROWS, LANES), U32),
                   jax.ShapeDtypeStruct((p_len,), jnp.int32)],
        compiler_params=_cparams(("arbitrary",)),
        name="dispatch",
    )(dest_flat, pad_end, n_used, xn_tiles)


def _experts_kernel(be_ref, nu_ref, nxt_ref, ord_ref, src_ref, xs_ref, w1_hbm, w3_hbm, w2_hbm, y2_ref,
                    w1_scr, w3_scr, w2_scr, w1_buf, w3_buf, w2_buf, ybuf, wsems, ysems):
    step = pl.program_id(0)
    first = 2 * step
    second = first + 1
    n_used = nu_ref[0]
    cur = step % 2
    pair_rows = 2 * MOE_BLOCK
    n_slot_rows = y2_ref.shape[0] // TOKEN_ROWS - 2 * pair_rows

    def send_previous(unrolled):
        prev_base = jnp.maximum(step - 1, 0) * pair_rows

        def send(r, priority):
            s = src_ref[prev_base + r]
            row = jnp.where((step > 0) & (s >= 0), s, n_slot_rows + cur * pair_rows + r)
            src = ybuf.at[1 - cur, pl.ds(pl.multiple_of(r * TOKEN_ROWS, TOKEN_ROWS), TOKEN_ROWS), :]
            dst = y2_ref.at[pl.ds(pl.multiple_of(row * TOKEN_ROWS, TOKEN_ROWS), TOKEN_ROWS), :]
            pltpu.make_async_copy(src, dst, ysems.at[cur]).start(priority=priority)

        if unrolled:
            for r in range(pair_rows):
                send(r, 1)
        else:
            def send_group(j, c):
                for k in range(DMA_PRIORITIES):
                    send(j * DMA_PRIORITIES + k, 1)
                return c
            lax.fori_loop(0, pair_rows // DMA_PRIORITIES, send_group, 0, unroll=DMA_ISSUE_UNROLL // DMA_PRIORITIES)

    def all_rows_sent(buf, sem):
        return pltpu.make_async_copy(ybuf.at[buf], y2_ref.at[pl.ds(0, pair_rows * TOKEN_ROWS), :], sem)

    def wait_older_sends():
        all_rows_sent(cur, ysems.at[1 - cur]).wait()

    @pl.when(step == 0)
    def _():
        ybuf[1] = jnp.zeros(ybuf.shape[1:], U32)
        odd_dump = (n_slot_rows + pair_rows) * TOKEN_ROWS
        pltpu.make_async_copy(ybuf.at[1], y2_ref.at[pl.ds(odd_dump, pair_rows * TOKEN_ROWS), :], ysems.at[1]).start()

    def weight_copies(e, slot):
        return (pltpu.make_async_copy(w1_hbm.at[e], w1_buf.at[slot], wsems.at[slot, 0]),
                pltpu.make_async_copy(w3_hbm.at[e], w3_buf.at[slot], wsems.at[slot, 1]),
                pltpu.make_async_copy(w2_hbm.at[e], w2_buf.at[slot], wsems.at[slot, 2]))

    def switch_to(e):
        slot = ord_ref[e] % 2
        for cp in weight_copies(e, slot):
            cp.wait()
        nxt = nxt_ref[e]

        @pl.when(nxt >= 0)
        def _():
            for cp in weight_copies(nxt, 1 - slot):
                cp.start()
        w1_scr[...] = w1_buf[slot].astype(BF16)
        w3_scr[...] = w3_buf[slot].astype(BF16)
        w2_scr[...] = w2_buf[slot].astype(BF16)

    def gated_mlp(row0, n_rows, first_store):
        x = _unpack_bf16_pairs(_load_token_tiles(xs_ref, n_rows, row0)).astype(BF16)
        a = jnp.dot(x, w1_scr[...], preferred_element_type=F32)
        b = jnp.dot(x, w3_scr[...], preferred_element_type=F32)
        hdn = a * (1.0 / (1.0 + jnp.exp(-a))) * b
        y = jnp.dot(hdn.astype(BF16), w2_scr[...], preferred_element_type=F32)
        if first_store:
            wait_older_sends()
        _store_token_tiles(ybuf.at[cur], _pack_bf16_pairs(y), row0)

    def zero_rows(row0, n_rows):
        ybuf[cur, row0 * TOKEN_ROWS:(row0 + n_rows) * TOKEN_ROWS, :] = jnp.zeros((n_rows * TOKEN_ROWS, LANES), U32)

    @pl.when(first < n_used)
    def _():
        e_first = be_ref[first]
        e_before = be_ref[jnp.maximum(first - 1, 0)]

        @pl.when(first == 0)
        def _():
            for cp in weight_copies(e_first, ord_ref[e_first] % 2):
                cp.start()

        @pl.when((first == 0) | (e_first != e_before))
        def _():
            switch_to(e_first)

        second_used = second < n_used
        e_second = be_ref[second]
        same_expert = second_used & (e_second == e_first)

        @pl.when(same_expert)
        def _():
            send_previous(unrolled=True)
            gated_mlp(0, 2 * MOE_BLOCK, first_store=True)

        @pl.when(jnp.logical_not(same_expert))
        def _():
            send_previous(unrolled=True)
            gated_mlp(0, MOE_BLOCK, first_store=True)

            @pl.when(second_used)
            def _():
                switch_to(e_second)
                gated_mlp(MOE_BLOCK, MOE_BLOCK, first_store=False)

            @pl.when(jnp.logical_not(second_used))
            def _():
                zero_rows(MOE_BLOCK, MOE_BLOCK)

    @pl.when(first >= n_used)
    def _():
        send_previous(unrolled=False)
        wait_older_sends()
        zero_rows(0, 2 * MOE_BLOCK)

    @pl.when(step == pl.num_programs(0) - 1)
    def _():
        all_rows_sent(1 - cur, ysems.at[cur]).wait()


def _experts(xs, row_src, block_e, n_used, next_expert, expert_ord, w1, w3, w2, n_slot_rows):
    p_len = xs.shape[0] // TOKEN_ROWS
    n_blocks = p_len // MOE_BLOCK
    assert n_blocks % 2 == 0
    n_pairs = n_blocks // 2
    d, de = w1.shape[1], w1.shape[2]
    blk = lambda i, be, nu, nxt, od, src: (jnp.maximum(jnp.minimum(i, (nu[0] - 1) // 2), 0), 0)
    hbm = pl.BlockSpec(memory_space=pl.ANY)
    grid_spec = pltpu.PrefetchScalarGridSpec(
        num_scalar_prefetch=5,
        grid=(n_pairs + 1,),
        in_specs=[pl.BlockSpec((2 * MOE_BLOCK * TOKEN_ROWS, LANES), blk), hbm, hbm, hbm],
        out_specs=hbm,
        scratch_shapes=[pltpu.VMEM((d, de), BF16), pltpu.VMEM((d, de), BF16), pltpu.VMEM((de, d), BF16),
                        pltpu.VMEM((2, d, de), F32), pltpu.VMEM((2, d, de), F32), pltpu.VMEM((2, de, d), F32),
                        pltpu.VMEM((2, 2 * MOE_BLOCK * TOKEN_ROWS, LANES), U32),
                        pltpu.SemaphoreType.DMA((2, 3)), pltpu.SemaphoreType.DMA((2,))],
    )
    return pl.pallas_call(
        _experts_kernel,
        grid_spec=grid_spec,
        out_shape=jax.ShapeDtypeStruct(((n_slot_rows + 4 * MOE_BLOCK) * TOKEN_ROWS, LANES), U32),
        compiler_params=_cparams(("arbitrary",)),
        name="experts",
    )(block_e, n_used, next_expert, expert_ord, row_src, xs, w1, w3, w2)


def _final_kernel(h_ref, meta_ref, p_ref, wple_ref, gple_ref, gplg_ref, wplg_ref, y_first_ref, y_second_ref, out_ref,
                  wple_scr, wplg_scr):
    tm = h_ref.shape[0]

    @pl.when(pl.program_id(0) == 0)
    def _():
        wple_scr[...] = wple_ref[...].astype(BF16)
        wplg_scr[...] = wplg_ref[...].astype(BF16)

    emb = jnp.dot(p_ref[...].astype(BF16), wple_scr[...], preferred_element_type=F32)
    emb = _rms_rows(emb) * gple_ref[...]
    meta = meta_ref[...]
    y1 = _unpack_bf16_pairs(_load_token_tiles(y_first_ref, tm))
    y2 = _unpack_bf16_pairs(_load_token_tiles(y_second_ref, tm))
    h = h_ref[...] + meta[:, 4:5] * y1 + meta[:, 5:6] * y2
    hn = (_rms_rows(h) * gplg_ref[...]).astype(BF16)
    z = jnp.dot(hn, wplg_scr[...], preferred_element_type=F32)
    out_ref[...] = h + emb * (1.0 / (1.0 + jnp.exp(-z)))


def _final(h2, meta, p2, w_ple, g_ple, g_plg, w_plg, y_slots, tm):
    t, d = h2.shape
    dp = p2.shape[1]
    nt = t // tm
    const = lambda i: (0, 0)
    row = lambda i: (i, 0)
    return pl.pallas_call(
        _final_kernel,
        grid=(nt,),
        in_specs=[
            pl.BlockSpec((tm, d), row),
            pl.BlockSpec((tm, LANES), row),
            pl.BlockSpec((tm, dp), row),
            pl.BlockSpec((dp, d), const, pipeline_mode=pl.Buffered(1)),
            pl.BlockSpec((1, d), const),
            pl.BlockSpec((1, d), const),
            pl.BlockSpec((d, d), const, pipeline_mode=pl.Buffered(1)),
            pl.BlockSpec((tm * TOKEN_ROWS, LANES), row),
            pl.BlockSpec((tm * TOKEN_ROWS, LANES), lambda i: (nt + i, 0)),
        ],
        out_specs=pl.BlockSpec((tm, d), row),
        out_shape=jax.ShapeDtypeStruct((t, d), F32),
        scratch_shapes=[pltpu.VMEM((dp, d), BF16), pltpu.VMEM((d, d), BF16)],
        compiler_params=_cparams(("arbitrary",)),
        name="combine_ple",
    )(h2, meta, p2, w_ple, g_ple, g_plg, w_plg, y_slots, y_slots)


def _block_diag_mean(seg):
    m = np.kron(np.eye(256 // seg), np.full((seg, seg), 1.0 / seg))
    return jnp.asarray(m, BF16)


def _rope_tables(seq):
    f32 = np.float32
    inv = np.power(f32(ROPE_THETA), -np.arange(0, DA_QK, 2, dtype=f32) / f32(DA_QK)).astype(f32)
    ang = (np.arange(seq, dtype=f32)[:, None] * inv[None, :]).astype(f32)
    cos, sin = np.cos(ang).astype(f32), np.sin(ang).astype(f32)
    cos_t = np.tile(cos, (1, LANES // cos.shape[1]))
    sin_t = np.tile(np.concatenate([-sin, sin], axis=1), (1, LANES // (2 * sin.shape[1])))
    return jnp.asarray(cos_t), jnp.asarray(sin_t)


def _tile_for(n, pref):
    while n % pref:
        pref //= 2
    return pref


def kernel(x, p, g_mix, w_in, g_qa, g_ka, lam_q1, lam_k1, lam_q2, lam_k2, g_sub, g_qb, g_kb, rpb, w_out,
           g_ffn, w_rg, w_re, w1, w3, w2, g_plg, w_plg, w_ple, g_ple):
    b, s, d = x.shape
    assert d == 2 * TOKEN_ROWS * LANES, "token-tile layout packs one token into TOKEN_ROWS rows of 128 uint32"
    t = b * s
    depth = w_in.shape[0]
    tm = _tile_for(s, 1024)
    tmd = _tile_for(t, 2048)
    tmf = _tile_for(t, 1024)
    n_slots = t * TOP_K
    n_blocks = n_slots // MOE_BLOCK + N_EXPERTS
    p_len = n_blocks * MOE_BLOCK

    bd64 = _block_diag_mean(DA_QK)
    bd32 = _block_diag_mean(DB_H)
    cos_t, sin_t = _rope_tables(s)
    tri = jnp.asarray(np.triu(np.ones((tm, tm)), 1), BF16)

    h = x.reshape(t, d)
    for i in range(depth):
        lam_init = 0.8 - 0.6 * math.exp(-0.3 * i)
        gsec = jnp.stack([
            jnp.tile(g_qa[i], 512 // DA_QK) * (DA_QK ** -0.5 * math.log2(math.e)),
            jnp.tile(g_ka[i], 512 // DA_QK),
            jnp.tile(g_qb[i], 512 // DB_H) * DB_H ** -0.5,
            jnp.tile(g_kb[i], 512 // DB_H),
        ]).astype(F32)
        proj, qvt = _inproj(h, g_mix[i][None, :], w_in[i], bd64, bd32, gsec, cos_t, sin_t, s, tm)
        proj3 = proj.reshape(b, s, proj.shape[1])

        lam4 = jnp.stack([lam_q1[i], lam_k1[i], lam_q2[i], lam_k2[i]]).astype(F32)
        oa = _diffattn(proj3, qvt, lam4, g_sub[i][None, :].astype(F32), lam_init)
        ob = _natten(proj3, _natten_bias_table(rpb[i]))

        w_router = jnp.concatenate(
            [w_rg[i], w_re[i], jnp.zeros((d, LANES - N_GROUPS - N_EXPERTS), F32)], axis=1).astype(F32)
        wr_hi = w_router.astype(BF16)
        wr_lo = (w_router - wr_hi.astype(F32)).astype(BF16)
        h1, xn, meta, meta_t, cnt = _outproj(h, oa.reshape(t, -1), ob.reshape(t, -1), w_out[i],
                                             g_ffn[i][None, :], jnp.concatenate([wr_hi, wr_lo], axis=1), tri, tm)

        counts = cnt[E_LANE0:E_LANE0 + N_EXPERTS, 0].astype(jnp.int32)
        padded = (counts + MOE_BLOCK - 1) // MOE_BLOCK * MOE_BLOCK
        pad_end = jnp.cumsum(padded)
        pad_start = pad_end - padded
        mt = meta_t[0:2 * TOP_K].astype(jnp.int32)
        e_ids = jnp.arange(N_EXPERTS, dtype=jnp.int32)[None, :, None]
        start_of = jnp.sum(jnp.where(mt[:TOP_K, None, :] == e_ids, pad_start[None, :, None], 0), axis=1)
        dest = start_of + mt[TOP_K:]
        blk_start = jnp.arange(n_blocks, dtype=jnp.int32) * MOE_BLOCK
        block_e = jnp.minimum(jnp.sum(pad_end[None, :] <= blk_start[:, None], axis=1), N_EXPERTS - 1)
        block_e = block_e.astype(jnp.int32)
        n_used = (pad_end[-1:] // MOE_BLOCK).astype(jnp.int32)

        dest_flat = dest.reshape(TOP_K * t)

        xs, row_src = _dispatch(xn, dest_flat, pad_end.astype(jnp.int32), n_used, p_len, tmd)
        in_use = counts > 0
        expert_ord = (jnp.cumsum(in_use) - 1).astype(jnp.int32)
        ids = jnp.arange(N_EXPERTS, dtype=jnp.int32)
        later = jnp.where((ids[None, :] > ids[:, None]) & in_use[None, :], ids[None, :], N_EXPERTS)
        next_expert = jnp.min(later, axis=1)
        next_expert = jnp.where(next_expert == N_EXPERTS, -1, next_expert).astype(jnp.int32)
        y_slots = _experts(xs, row_src.reshape(p_len), block_e, n_used, next_expert, expert_ord,
                           w1[i], w3[i], w2[i], TOP_K * t)
        h = _final(h1, meta, p[i].reshape(t, -1), w_ple[i], g_ple[i][None, :], g_plg[i][None, :], w_plg[i],
                   y_slots, tmf)
    return h.reshape(b, s, d)
```

```python
import functools
import math

import numpy as np
import jax
import jax.numpy as jnp
from jax import lax
from jax.experimental import pallas as pl
from jax.experimental.pallas import tpu as pltpu

F32 = jnp.float32
BF16 = jnp.bfloat16
U32 = jnp.uint32

LANES = 128
HA = 4
DA_QK = 64
DA_V = 128
HB = 16
DB_H = 32
GRID_W = 64
WIN_R = 8
WIN_C = 16
ROPE_THETA = 10000.0
EPS = 1e-6
NEG = -1e30
N_GROUPS = 4
EXPERTS_PER_GROUP = 8
N_EXPERTS = N_GROUPS * EXPERTS_PER_GROUP
TOP_K = 2
MOE_BLOCK = 256
E_LANE0 = N_GROUPS
ROUTER_ROWS = 40

VMEM_LIMIT = 56 * 1024 * 1024


def _cparams(sem):
    return pltpu.CompilerParams(dimension_semantics=sem, vmem_limit_bytes=VMEM_LIMIT)


def _rms_rows(x):
    return x * lax.rsqrt(jnp.mean(x * x, axis=-1, keepdims=True) + EPS)


SUBLANES = 8

TOKEN_ROWS = 4
HIGH_HALF = 0xFFFF0000


def _pack_bf16_pairs(x):
    m = x.shape[1] // 2
    bits = pltpu.bitcast(x.astype(BF16).astype(F32), U32)
    return (bits[:, :m] >> 16) | (bits[:, m:] & jnp.uint32(HIGH_HALF))


def _unpack_bf16_pairs(u):
    lo = pltpu.bitcast(u << 16, F32)
    hi = pltpu.bitcast(u & jnp.uint32(HIGH_HALF), F32)
    return jnp.concatenate([lo, hi], axis=1)


def _store_token_tiles(ref, val, first=0):
    n = val.shape[0]
    for c in range(TOKEN_ROWS):
        ref[pl.ds(first * TOKEN_ROWS + c, n, stride=TOKEN_ROWS), :] = val[:, c * LANES:(c + 1) * LANES]


def _load_token_tiles(ref, n, first=0):
    return jnp.concatenate(
        [ref[pl.ds(first * TOKEN_ROWS + c, n, stride=TOKEN_ROWS), :] for c in range(TOKEN_ROWS)], axis=1)


def _inproj_kernel(x_ref, gmix_ref, w_ref, bd64_ref, bd32_ref, gsec_ref, cos_ref, sin_ref, out_ref, qvt_ref, w_scr):
    tm = x_ref.shape[0]

    @pl.when(pl.program_id(0) == 0)
    def _():
        w_scr[...] = w_ref[...].astype(BF16)

    a = (_rms_rows(x_ref[...]) * gmix_ref[...]).astype(BF16)
    lane = lax.broadcasted_iota(jnp.int32, (tm, LANES), 1)
    first_half = (lane % 64) < 32
    cos = cos_ref[...]
    sin = sin_ref[...]
    out_col = {1: 0, 3: 1, 4: 2, 5: 3}
    qvt_row = {0: 0, 2: 1}
    for sec in range(6):
        y = jnp.dot(a, w_scr[:, sec * 512:(sec + 1) * 512], preferred_element_type=F32)
        if sec in (0, 1, 3, 4):
            bd = bd64_ref if sec < 2 else bd32_ref
            gi = {0: 0, 1: 1, 3: 2, 4: 3}[sec]
            y2 = (y * y).astype(BF16)
            ms = jnp.concatenate([jnp.dot(y2[:, half * 256:(half + 1) * 256], bd[...], preferred_element_type=F32)
                                  for half in range(2)], axis=1)
            y = y * lax.rsqrt(ms + EPS) * gsec_ref[gi:gi + 1, :]
        for c in range(4):
            yc = y[:, c * LANES:(c + 1) * LANES]
            if sec < 2:
                rot = jnp.where(first_half, pltpu.roll(yc, 96, 1), pltpu.roll(yc, 32, 1))
                yc = yc * cos + rot * sin
            if sec in qvt_row:
                r0 = qvt_row[sec] * 512 + c * LANES
                qvt_ref[0, r0:r0 + LANES, :] = yc.T.astype(BF16)
            else:
                c0 = out_col[sec] * 512 + c * LANES
                out_ref[:, c0:c0 + LANES] = yc.astype(BF16)


def _inproj(x2, g_mix, w_f32, bd64, bd32, gsec, cos_t, sin_t, seq, tm):
    t, d = x2.shape
    n = w_f32.shape[1]
    nsb = seq // tm
    n_tok_major = 4 * 512
    n_feat_major = 2 * 512
    const = lambda i: (0, 0)
    return pl.pallas_call(
        _inproj_kernel,
        grid=(t // tm,),
        in_specs=[
            pl.BlockSpec((tm, d), lambda i: (i, 0)),
            pl.BlockSpec((1, d), const),
            pl.BlockSpec((d, n), const, pipeline_mode=pl.Buffered(1)),
            pl.BlockSpec((256, 256), const),
            pl.BlockSpec((256, 256), const),
            pl.BlockSpec((4, 512), const),
            pl.BlockSpec((tm, LANES), lambda i: (i % nsb, 0)),
            pl.BlockSpec((tm, LANES), lambda i: (i % nsb, 0)),
        ],
        out_specs=[pl.BlockSpec((tm, n_tok_major), lambda i: (i, 0)),
                   pl.BlockSpec((1, n_feat_major, tm), lambda i: (i // nsb, 0, i % nsb))],
        out_shape=[jax.ShapeDtypeStruct((t, n_tok_major), BF16),
                   jax.ShapeDtypeStruct((t // seq, n_feat_major, seq), BF16)],
        scratch_shapes=[pltpu.VMEM((d, n), BF16)],
        compiler_params=_cparams(("arbitrary",)),
        name="inproj",
    )(x2, g_mix, w_f32, bd64, bd32, gsec, cos_t, sin_t)


QBLOCK_PAIRS_PER_BODY = 4

def _diffattn_kernel(lam_ref, gsub_ref, qt_ref, k_ref, vt_ref, o_ref, sa_scr, sb_scr, *, seq, tq, tk, lam_init):
    nq = seq // tq
    nk = seq // tk
    s1 = jnp.sum(lam_ref[0:1, :] * lam_ref[1:2, :], axis=1, keepdims=True)
    s2 = jnp.sum(lam_ref[2:3, :] * lam_ref[3:4, :], axis=1, keepdims=True)
    lam = jnp.exp(s1) - jnp.exp(s2) + lam_init

    row = lax.broadcasted_iota(jnp.int32, (LANES, tq), 0)

    def scores(qi, s_scr):
        qoff = pl.multiple_of(qi * tq, tq)
        qt = qt_ref[0, :, pl.ds(qoff, tq)]
        zero = jnp.zeros_like(qt)
        qm = jnp.concatenate([jnp.where(row < DA_QK, qt, zero), jnp.where(row >= DA_QK, qt, zero)], axis=1)
        mx = None
        for j in range(nk):
            st = jnp.dot(k_ref[0, j * tk:(j + 1) * tk, :], qm, preferred_element_type=F32)
            s_scr[j * tk:(j + 1) * tk, :] = st
            cm = jnp.max(st, axis=0, keepdims=True)
            mx = cm if mx is None else jnp.maximum(mx, cm)
        return mx

    def finish(qi, s_scr, m):
        l = jnp.zeros((1, 2 * tq), F32)
        acc = jnp.zeros((DA_V, 2 * tq), F32)
        for j in range(nk):
            p = jnp.exp2(s_scr[j * tk:(j + 1) * tk, :] - m)
            l = l + jnp.sum(p, axis=0, keepdims=True)
            acc = acc + jnp.dot(vt_ref[0, :, j * tk:(j + 1) * tk], p.astype(BF16), preferred_element_type=F32)
        o = acc * (1.0 / l)
        ot = o[:, :tq] - lam * o[:, tq:]
        ot = ot * lax.rsqrt(jnp.mean(ot * ot, axis=0, keepdims=True) + EPS) * (1.0 - lam_init)
        qoff = pl.multiple_of(qi * tq, tq)
        o_ref[0, pl.ds(qoff, tq), :] = (ot.T * gsub_ref[...]).astype(BF16)

    def body(i, m_a):
        for pair in range(QBLOCK_PAIRS_PER_BODY):
            q0 = 2 * (QBLOCK_PAIRS_PER_BODY * i + pair)
            m_b = scores(q0 + 1, sb_scr)
            finish(q0, sa_scr, m_a)
            m_a = scores(jnp.minimum(q0 + 2, nq - 1), sa_scr)
            finish(q0 + 1, sb_scr, m_b)
        return m_a

    lax.fori_loop(0, nq // (2 * QBLOCK_PAIRS_PER_BODY), body, scores(0, sa_scr))


def _diffattn(proj3, qvt, lam4, g_sub, lam_init, tq=128, tk=512):
    b, seq, _ = proj3.shape
    kern = functools.partial(_diffattn_kernel, seq=seq, tq=tq, tk=tk, lam_init=lam_init)
    return pl.pallas_call(
        kern,
        grid=(b, HA),
        in_specs=[
            pl.BlockSpec((4, DA_QK), lambda i, h: (0, 0)),
            pl.BlockSpec((1, DA_V), lambda i, h: (0, 0)),
            pl.BlockSpec((1, LANES, seq), lambda i, h: (i, h, 0)),
            pl.BlockSpec((1, seq, LANES), lambda i, h: (i, 0, h)),
            pl.BlockSpec((1, LANES, seq), lambda i, h: (i, HA + h, 0)),
        ],
        out_specs=pl.BlockSpec((1, seq, LANES), lambda i, h: (i, 0, h)),
        out_shape=jax.ShapeDtypeStruct((b, seq, HA * DA_V), BF16),
        scratch_shapes=[pltpu.VMEM((seq, 2 * tq), F32), pltpu.VMEM((seq, 2 * tq), F32)],
        compiler_params=_cparams(("parallel", "parallel")),
        name="diffattn",
    )(lam4, g_sub, qvt, proj3, qvt)


def _natten_kernel(bias_ref, q_ref, k_ref, v_ref, o_ref, *, rows):
    kr = min(WIN_R, rows)
    nkeys = kr * GRID_W
    hpg = LANES // DB_H
    lane_head = lax.broadcasted_iota(jnp.int32, (GRID_W, LANES), 1) // DB_H

    def row_step(r, carry):
        rs = jnp.clip(r - kr // 2, 0, rows - kr)
        base = rs - r + (WIN_R - 1)
        qoff = pl.multiple_of(r * GRID_W, GRID_W)
        koff = pl.multiple_of(rs * GRID_W, GRID_W)
        qr = q_ref[0, pl.ds(qoff, GRID_W), :]
        zero = jnp.zeros_like(qr)
        qs = jnp.concatenate([jnp.where(lane_head == h, qr, zero) for h in range(hpg)], axis=0)
        kw = k_ref[0, pl.ds(koff, nkeys), :]
        vw = v_ref[0, pl.ds(koff, nkeys), :]
        s = lax.dot_general(qs, kw, (((1,), (1,)), ((), ())), preferred_element_type=F32)
        bias = jnp.concatenate([bias_ref[0, base + 2 * i] for i in range(kr // 2)], axis=1)
        s = s + bias
        m = jnp.max(s, axis=1, keepdims=True)
        p = jnp.exp(s - m).astype(BF16)
        pv = jnp.dot(p, jnp.concatenate([vw, jnp.ones_like(vw)], axis=1), preferred_element_type=F32)
        o = pv[:, :LANES] * (1.0 / pv[:, LANES:])
        out = jnp.zeros((GRID_W, LANES), F32)
        for h in range(hpg):
            out = out + jnp.where(lane_head == h, o[h * GRID_W:(h + 1) * GRID_W, :], 0.0)
        o_ref[0, pl.ds(qoff, GRID_W), :] = out.astype(BF16)
        return carry

    lax.fori_loop(0, rows, row_step, 0, unroll=min(rows, 64))


def _natten(proj3, bias_tab):
    b, seq, _ = proj3.shape
    rows = seq // GRID_W
    ng = HB * DB_H // LANES
    nro = bias_tab.shape[1]
    kern = functools.partial(_natten_kernel, rows=rows)
    cb = HA
    return pl.pallas_call(
        kern,
        grid=(b, ng),
        in_specs=[
            pl.BlockSpec((1, nro, bias_tab.shape[2], LANES), lambda i, g: (g, 0, 0, 0)),
            pl.BlockSpec((1, seq, LANES), lambda i, g: (i, 0, cb + g)),
            pl.BlockSpec((1, seq, LANES), lambda i, g: (i, 0, cb + ng + g)),
            pl.BlockSpec((1, seq, LANES), lambda i, g: (i, 0, cb + 2 * ng + g)),
        ],
        out_specs=pl.BlockSpec((1, seq, LANES), lambda i, g: (i, 0, g)),
        out_shape=jax.ShapeDtypeStruct((b, seq, HB * DB_H), BF16),
        compiler_params=_cparams(("parallel", "parallel")),
        name="natten",
    )(bias_tab, proj3, proj3, proj3)


def _natten_bias_table(rpb):
    q = np.arange(GRID_W)[:, None]
    kc = np.arange(GRID_W)[None, :]
    cs = np.clip(q - WIN_C // 2, 0, GRID_W - WIN_C)
    valid = (kc >= cs) & (kc < cs + WIN_C)
    col_off = np.clip(kc - q + (WIN_C - 1), 0, 2 * WIN_C - 2)
    ncol = 2 * WIN_C - 1
    nro = 2 * WIN_R - 2
    pick = np.zeros((2, ncol, GRID_W, 2 * GRID_W), np.float32)
    for j in range(2):
        pick[j, :, :, j * GRID_W:(j + 1) * GRID_W] = (np.arange(ncol)[:, None, None] == col_off[None]) & valid[None]
    negmask = np.tile(np.where(valid, 0.0, NEG).astype(np.float32), (1, 2))
    hpg = LANES // DB_H
    r4 = rpb.astype(F32).reshape(HB // hpg, hpg, 2 * WIN_R - 1, ncol)
    rp = jnp.stack([r4[:, :, :nro], r4[:, :, 1:]], axis=0)
    tab = jnp.einsum('jghrc,jcqk->grhqk', rp, jnp.asarray(pick), precision=lax.Precision.HIGHEST) + negmask
    return tab.reshape(HB // hpg, nro, hpg * GRID_W, LANES)


def _outproj_kernel(x_ref, oa_ref, ob_ref, wo_ref, gffn_ref, wr_ref, tri_ref,
                    h_ref, xn_ref, meta_ref, meta_t_ref, cnt_ref, run_scr, wo_scr):
    tm = x_ref.shape[0]
    da = oa_ref.shape[1]

    @pl.when(pl.program_id(0) == 0)
    def _():
        run_scr[...] = jnp.zeros_like(run_scr)
        wo_scr[...] = wo_ref[...].astype(BF16)

    attn = jnp.dot(oa_ref[...], wo_scr[:da, :], preferred_element_type=F32)
    attn = attn + jnp.dot(ob_ref[...], wo_scr[da:, :], preferred_element_type=F32)
    h = x_ref[...] + attn
    h_ref[...] = h
    xn = _rms_rows(h) * gffn_ref[...]
    _store_token_tiles(xn_ref, _pack_bf16_pairs(xn))

    xh = xn.astype(BF16)
    xl = (xn - xh.astype(F32)).astype(BF16)
    hh_hl = jnp.dot(xh, wr_ref[...], preferred_element_type=F32)
    lg = hh_hl[:, :LANES] + hh_hl[:, LANES:] + jnp.dot(xl, wr_ref[:, :LANES], preferred_element_type=F32)
    lt = lg.T
    row = [lt[j:j + 1, :] for j in range(E_LANE0 + N_EXPERTS)]

    def first_argmax(vals):
        best = vals[0]
        for v in vals[1:]:
            best = jnp.maximum(best, v)
        idx = jnp.full_like(best, float(len(vals) - 1))
        for j in range(len(vals) - 2, -1, -1):
            idx = jnp.where(vals[j] == best, float(j), idx)
        return best, idx

    gmax, gidx = first_argmax(row[:N_GROUPS])
    gsum = jnp.exp(row[0] - gmax)
    for j in range(1, N_GROUPS):
        gsum = gsum + jnp.exp(row[j] - gmax)
    g_gate = 1.0 / gsum
    el = []
    for k in range(EXPERTS_PER_GROUP):
        v = row[E_LANE0 + k]
        for g in range(1, N_GROUPS):
            v = jnp.where(gidx == float(g), row[E_LANE0 + g * EXPERTS_PER_GROUP + k], v)
        el.append(v)
    l1, k1 = first_argmax(el)
    l2, k2 = first_argmax([jnp.where(k1 == float(k), -jnp.inf, el[k]) for k in range(EXPERTS_PER_GROUP)])
    r = jnp.exp(l2 - l1)
    w1 = g_gate / (1.0 + r)
    w2 = w1 * r
    e1 = gidx * EXPERTS_PER_GROUP + k1
    e2 = gidx * EXPERTS_PER_GROUP + k2

    rowid = lax.broadcasted_iota(jnp.int32, (ROUTER_ROWS, tm), 0).astype(F32)
    sel1 = rowid == e1 + E_LANE0
    sel2 = rowid == e2 + E_LANE0
    oh = jnp.where(sel1 | sel2, 1.0, 0.0).astype(BF16)
    run = run_scr[...]
    before = jnp.dot(oh, tri_ref[...], preferred_element_type=F32) + jnp.concatenate([run] * (tm // LANES), axis=1)
    rank1 = jnp.sum(jnp.where(sel1, before, 0.0), axis=0, keepdims=True)
    rank2 = jnp.sum(jnp.where(sel2, before, 0.0), axis=0, keepdims=True)
    run = run + jnp.dot(oh, jnp.ones((tm, LANES), BF16), preferred_element_type=F32)
    run_scr[...] = run
    cnt_ref[...] = run

    meta_t = jnp.concatenate([e1, e2, rank1, rank2, w1, w2, jnp.zeros((SUBLANES - 6, tm), F32)], axis=0)
    meta_t_ref[...] = meta_t
    meta_ref[...] = jnp.concatenate([meta_t, jnp.zeros((LANES - SUBLANES, tm), F32)], axis=0).T


def _outproj(x2, oa2, ob2, w_out, g_ffn, w_router, tri, tm):
    t, d = x2.shape
    da = oa2.shape[1]
    const = lambda i: (0, 0)
    row = lambda i: (i, 0)
    return pl.pallas_call(
        _outproj_kernel,
        grid=(t // tm,),
        in_specs=[
            pl.BlockSpec((tm, d), row),
            pl.BlockSpec((tm, da), row),
            pl.BlockSpec((tm, d - da), row),
            pl.BlockSpec((d, d), const, pipeline_mode=pl.Buffered(1)),
            pl.BlockSpec((1, d), const),
            pl.BlockSpec((d, 2 * LANES), const),
            pl.BlockSpec((tm, tm), const),
        ],
        out_specs=[
            pl.BlockSpec((tm, d), row),
            pl.BlockSpec((tm * TOKEN_ROWS, LANES), row),
            pl.BlockSpec((tm, LANES), row),
            pl.BlockSpec((SUBLANES, tm), lambda i: (0, i)),
            pl.BlockSpec((ROUTER_ROWS, LANES), const),
        ],
        out_shape=[
            jax.ShapeDtypeStruct((t, d), F32),
            jax.ShapeDtypeStruct((t * TOKEN_ROWS, LANES), U32),
            jax.ShapeDtypeStruct((t, LANES), F32),
            jax.ShapeDtypeStruct((SUBLANES, t), F32),
            jax.ShapeDtypeStruct((ROUTER_ROWS, LANES), F32),
        ],
        scratch_shapes=[pltpu.VMEM((ROUTER_ROWS, LANES), F32), pltpu.VMEM((d, d), BF16)],
        compiler_params=_cparams(("arbitrary",)),
        name="outproj_router",
    )(x2, oa2, ob2, w_out, g_ffn, w_router, tri)


DMA_ISSUE_UNROLL = 8
DMA_PRIORITIES = 2


def _dispatch_kernel(dest_ref, pend_ref, nu_ref, xn_ref, xs_ref, src_ref, zbuf, ibuf, sem, zsem):
    tm = xn_ref.shape[0] // TOKEN_ROWS
    t_total = pl.num_programs(0) * tm
    base = pl.program_id(0) * tm
    n_blocks = xs_ref.shape[0] // (MOE_BLOCK * TOKEN_ROWS)

    @pl.when(pl.program_id(0) == 0)
    def _():
        ibuf[...] = jnp.full(ibuf.shape, -1, jnp.int32)
        fill = pltpu.make_async_copy(ibuf, src_ref, zsem)
        fill.start()
        fill.wait()
        zbuf[...] = jnp.zeros_like(zbuf)

        def zero_block(blk):
            off = pl.multiple_of(blk * (MOE_BLOCK * TOKEN_ROWS), MOE_BLOCK * TOKEN_ROWS)
            return pltpu.make_async_copy(zbuf, xs_ref.at[pl.ds(off, MOE_BLOCK * TOKEN_ROWS), :], zsem)

        def for_each_zeroed_block(fn):
            def per_expert(e, c):
                end = pend_ref[e]
                start = jnp.where(e == 0, 0, pend_ref[jnp.maximum(e - 1, 0)])

                @pl.when(end > start)
                def _():
                    fn(zero_block(end // MOE_BLOCK - 1))
                return c
            lax.fori_loop(0, N_EXPERTS, per_expert, 0)

            def per_tail_block(blk, c):
                fn(zero_block(blk))
                return c
            lax.fori_loop(nu_ref[0], n_blocks, per_tail_block, 0)

        for_each_zeroed_block(lambda cp: cp.start())
        for_each_zeroed_block(lambda cp: cp.wait())

    def issue(group, c):
        toks = [group * DMA_ISSUE_UNROLL + g for g in range(DMA_ISSUE_UNROLL)]
        dests = [[dest_ref[slot * t_total + base + r] for slot in range(TOP_K)] for r in toks]
        for r, ds in zip(toks, dests):
            src = xn_ref.at[pl.ds(pl.multiple_of(r * TOKEN_ROWS, TOKEN_ROWS), TOKEN_ROWS), :]
            for slot, d in enumerate(ds):
                dst = xs_ref.at[pl.ds(pl.multiple_of(d * TOKEN_ROWS, TOKEN_ROWS), TOKEN_ROWS), :]
                pltpu.make_async_copy(src, dst, sem).start(priority=slot % DMA_PRIORITIES)
        for r, ds in zip(toks, dests):
            for slot, d in enumerate(ds):
                src_ref[d] = slot * t_total + base + r
        return c
    lax.fori_loop(0, tm // DMA_ISSUE_UNROLL, issue, 0)

    for slot in range(TOP_K):
        pltpu.make_async_copy(xn_ref, xs_ref.at[pl.ds(0, tm * TOKEN_ROWS), :], sem).wait()


def _dispatch(xn_tiles, dest_flat, pad_end, n_used, p_len, tm):
    t = xn_tiles.shape[0] // TOKEN_ROWS
    grid_spec = pltpu.PrefetchScalarGridSpec(
        num_scalar_prefetch=3,
        grid=(t // tm,),
        in_specs=[pl.BlockSpec((tm * TOKEN_ROWS, LANES), lambda i, dest, pend, nu: (i, 0))],
        out_specs=[pl.BlockSpec(memory_space=pl.ANY), pl.BlockSpec(memory_space=pltpu.SMEM)],
        scratch_shapes=[pltpu.VMEM((MOE_BLOCK * TOKEN_ROWS, LANES), U32),
                        pltpu.VMEM((p_len,), jnp.int32),
                        pltpu.SemaphoreType.DMA(()), pltpu.SemaphoreType.DMA(())],
    )
    return pl.pallas_call(
        _dispatch_kernel,
        grid_spec=grid_spec,
        out_shape=[jax.ShapeDtypeStruct((p_len * TOKEN_ROWS, LANES), U32),
                   jax.ShapeDtypeStruct((p_len,), jnp.int32)],
        compiler_params=_cparams(("arbitrary",)),
        name="dispatch",
    )(dest_flat, pad_end, n_used, xn_tiles)


def _experts_kernel(be_ref, nu_ref, nxt_ref, ord_ref, src_ref, xs_ref, w1_hbm, w3_hbm, w2_hbm, y2_ref,
                    w1_scr, w3_scr, w2_scr, w1_buf, w3_buf, w2_buf, ybuf, wsems, ysems):
    step = pl.program_id(0)
    first = 2 * step
    second = first + 1
    n_used = nu_ref[0]
    cur = step % 2
    pair_rows = 2 * MOE_BLOCK
    n_slot_rows = y2_ref.shape[0] // TOKEN_ROWS - 2 * pair_rows

    def send_previous(unrolled):
        prev_base = jnp.maximum(step - 1, 0) * pair_rows

        def send(r, priority):
            s = src_ref[prev_base + r]
            row = jnp.where((step > 0) & (s >= 0), s, n_slot_rows + cur * pair_rows + r)
            src = ybuf.at[1 - cur, pl.ds(pl.multiple_of(r * TOKEN_ROWS, TOKEN_ROWS), TOKEN_ROWS), :]
            dst = y2_ref.at[pl.ds(pl.multiple_of(row * TOKEN_ROWS, TOKEN_ROWS), TOKEN_ROWS), :]
            pltpu.make_async_copy(src, dst, ysems.at[cur]).start(priority=priority)

        if unrolled:
            for r in range(pair_rows):
                send(r, 1)
        else:
            def send_group(j, c):
                for k in range(DMA_PRIORITIES):
                    send(j * DMA_PRIORITIES + k, 1)
                return c
            lax.fori_loop(0, pair_rows // DMA_PRIORITIES, send_group, 0, unroll=DMA_ISSUE_UNROLL // DMA_PRIORITIES)

    def all_rows_sent(buf, sem):
        return pltpu.make_async_copy(ybuf.at[buf], y2_ref.at[pl.ds(0, pair_rows * TOKEN_ROWS), :], sem)

    def wait_older_sends():
        all_rows_sent(cur, ysems.at[1 - cur]).wait()

    @pl.when(step == 0)
    def _():
        ybuf[1] = jnp.zeros(ybuf.shape[1:], U32)
        odd_dump = (n_slot_rows + pair_rows) * TOKEN_ROWS
        pltpu.make_async_copy(ybuf.at[1], y2_ref.at[pl.ds(odd_dump, pair_rows * TOKEN_ROWS), :], ysems.at[1]).start()

    def weight_copies(e, slot):
        return (pltpu.make_async_copy(w1_hbm.at[e], w1_buf.at[slot], wsems.at[slot, 0]),
                pltpu.make_async_copy(w3_hbm.at[e], w3_buf.at[slot], wsems.at[slot, 1]),
                pltpu.make_async_copy(w2_hbm.at[e], w2_buf.at[slot], wsems.at[slot, 2]))

    def switch_to(e):
        slot = ord_ref[e] % 2
        for cp in weight_copies(e, slot):
            cp.wait()
        nxt = nxt_ref[e]

        @pl.when(nxt >= 0)
        def _():
            for cp in weight_copies(nxt, 1 - slot):
                cp.start()
        w1_scr[...] = w1_buf[slot].astype(BF16)
        w3_scr[...] = w3_buf[slot].astype(BF16)
        w2_scr[...] = w2_buf[slot].astype(BF16)

    def gated_mlp(row0, n_rows, first_store):
        x = _unpack_bf16_pairs(_load_token_tiles(xs_ref, n_rows, row0)).astype(BF16)
        a = jnp.dot(x, w1_scr[...], preferred_element_type=F32)
        b = jnp.dot(x, w3_scr[...], preferred_element_type=F32)
        if first_store:
            send_previous(unrolled=True)
        hdn = a * (1.0 / (1.0 + jnp.exp(-a))) * b
        y = jnp.dot(hdn.astype(BF16), w2_scr[...], preferred_element_type=F32)
        if first_store:
            wait_older_sends()
        _store_token_tiles(ybuf.at[cur], _pack_bf16_pairs(y), row0)

    def zero_rows(row0, n_rows):
        ybuf[cur, row0 * TOKEN_ROWS:(row0 + n_rows) * TOKEN_ROWS, :] = jnp.zeros((n_rows * TOKEN_ROWS, LANES), U32)

    @pl.when(first < n_used)
    def _():
        e_first = be_ref[first]
        e_before = be_ref[jnp.maximum(first - 1, 0)]

        @pl.when(first == 0)
        def _():
            for cp in weight_copies(e_first, ord_ref[e_first] % 2):
                cp.start()

        @pl.when((first == 0) | (e_first != e_before))
        def _():
            switch_to(e_first)

        second_used = second < n_used
        e_second = be_ref[second]
        same_expert = second_used & (e_second == e_first)

        @pl.when(same_expert)
        def _():
            gated_mlp(0, 2 * MOE_BLOCK, first_store=True)

        @pl.when(jnp.logical_not(same_expert))
        def _():
            gated_mlp(0, MOE_BLOCK, first_store=True)

            @pl.when(second_used)
            def _():
                switch_to(e_second)
                gated_mlp(MOE_BLOCK, MOE_BLOCK, first_store=False)

            @pl.when(jnp.logical_not(second_used))
            def _():
                zero_rows(MOE_BLOCK, MOE_BLOCK)

    @pl.when(first >= n_used)
    def _():
        send_previous(unrolled=False)
        wait_older_sends()
        zero_rows(0, 2 * MOE_BLOCK)

    @pl.when(step == pl.num_programs(0) - 1)
    def _():
        all_rows_sent(1 - cur, ysems.at[cur]).wait()


def _experts(xs, row_src, block_e, n_used, next_expert, expert_ord, w1, w3, w2, n_slot_rows):
    p_len = xs.shape[0] // TOKEN_ROWS
    n_blocks = p_len // MOE_BLOCK
    assert n_blocks % 2 == 0
    n_pairs = n_blocks // 2
    d, de = w1.shape[1], w1.shape[2]
    blk = lambda i, be, nu, nxt, od, src: (jnp.maximum(jnp.minimum(i, (nu[0] - 1) // 2), 0), 0)
    hbm = pl.BlockSpec(memory_space=pl.ANY)
    grid_spec = pltpu.PrefetchScalarGridSpec(
        num_scalar_prefetch=5,
        grid=(n_pairs + 1,),
        in_specs=[pl.BlockSpec((2 * MOE_BLOCK * TOKEN_ROWS, LANES), blk), hbm, hbm, hbm],
        out_specs=hbm,
        scratch_shapes=[pltpu.VMEM((d, de), BF16), pltpu.VMEM((d, de), BF16), pltpu.VMEM((de, d), BF16),
                        pltpu.VMEM((2, d, de), F32), pltpu.VMEM((2, d, de), F32), pltpu.VMEM((2, de, d), F32),
                        pltpu.VMEM((2, 2 * MOE_BLOCK * TOKEN_ROWS, LANES), U32),
                        pltpu.SemaphoreType.DMA((2, 3)), pltpu.SemaphoreType.DMA((2,))],
    )
    return pl.pallas_call(
        _experts_kernel,
        grid_spec=grid_spec,
        out_shape=jax.ShapeDtypeStruct(((n_slot_rows + 4 * MOE_BLOCK) * TOKEN_ROWS, LANES), U32),
        compiler_params=_cparams(("arbitrary",)),
        name="experts",
    )(block_e, n_used, next_expert, expert_ord, row_src, xs, w1, w3, w2)


def _final_kernel(h_ref, meta_ref, p_ref, wple_ref, gple_ref, gplg_ref, wplg_ref, y_first_ref, y_second_ref, out_ref,
                  wple_scr, wplg_scr):
    tm = h_ref.shape[0]

    @pl.when(pl.program_id(0) == 0)
    def _():
        wple_scr[...] = wple_ref[...].astype(BF16)
        wplg_scr[...] = wplg_ref[...].astype(BF16)

    emb = jnp.dot(p_ref[...].astype(BF16), wple_scr[...], preferred_element_type=F32)
    emb = _rms_rows(emb) * gple_ref[...]
    meta = meta_ref[...]
    y1 = _unpack_bf16_pairs(_load_token_tiles(y_first_ref, tm))
    y2 = _unpack_bf16_pairs(_load_token_tiles(y_second_ref, tm))
    h = h_ref[...] + meta[:, 4:5] * y1 + meta[:, 5:6] * y2
    hn = (_rms_rows(h) * gplg_ref[...]).astype(BF16)
    z = jnp.dot(hn, wplg_scr[...], preferred_element_type=F32)
    out_ref[...] = h + emb * (1.0 / (1.0 + jnp.exp(-z)))


def _final(h2, meta, p2, w_ple, g_ple, g_plg, w_plg, y_slots, tm):
    t, d = h2.shape
    dp = p2.shape[1]
    nt = t // tm
    const = lambda i: (0, 0)
    row = lambda i: (i, 0)
    return pl.pallas_call(
        _final_kernel,
        grid=(nt,),
        in_specs=[
            pl.BlockSpec((tm, d), row),
            pl.BlockSpec((tm, LANES), row),
            pl.BlockSpec((tm, dp), row),
            pl.BlockSpec((dp, d), const, pipeline_mode=pl.Buffered(1)),
            pl.BlockSpec((1, d), const),
            pl.BlockSpec((1, d), const),
            pl.BlockSpec((d, d), const, pipeline_mode=pl.Buffered(1)),
            pl.BlockSpec((tm * TOKEN_ROWS, LANES), row),
            pl.BlockSpec((tm * TOKEN_ROWS, LANES), lambda i: (nt + i, 0)),
        ],
        out_specs=pl.BlockSpec((tm, d), row),
        out_shape=jax.ShapeDtypeStruct((t, d), F32),
        scratch_shapes=[pltpu.VMEM((dp, d), BF16), pltpu.VMEM((d, d), BF16)],
        compiler_params=_cparams(("arbitrary",)),
        name="combine_ple",
    )(h2, meta, p2, w_ple, g_ple, g_plg, w_plg, y_slots, y_slots)


def _block_diag_mean(seg):
    m = np.kron(np.eye(256 // seg), np.full((seg, seg), 1.0 / seg))
    return jnp.asarray(m, BF16)


def _rope_tables(seq):
    f32 = np.float32
    inv = np.power(f32(ROPE_THETA), -np.arange(0, DA_QK, 2, dtype=f32) / f32(DA_QK)).astype(f32)
    ang = (np.arange(seq, dtype=f32)[:, None] * inv[None, :]).astype(f32)
    cos, sin = np.cos(ang).astype(f32), np.sin(ang).astype(f32)
    cos_t = np.tile(cos, (1, LANES // cos.shape[1]))
    sin_t = np.tile(np.concatenate([-sin, sin], axis=1), (1, LANES // (2 * sin.shape[1])))
    return jnp.asarray(cos_t), jnp.asarray(sin_t)


def _tile_for(n, pref):
    while n % pref:
        pref //= 2
    return pref


def kernel(x, p, g_mix, w_in, g_qa, g_ka, lam_q1, lam_k1, lam_q2, lam_k2, g_sub, g_qb, g_kb, rpb, w_out,
           g_ffn, w_rg, w_re, w1, w3, w2, g_plg, w_plg, w_ple, g_ple):
    b, s, d = x.shape
    assert d == 2 * TOKEN_ROWS * LANES, "token-tile layout packs one token into TOKEN_ROWS rows of 128 uint32"
    t = b * s
    depth = w_in.shape[0]
    tm = _tile_for(s, 1024)
    tmd = _tile_for(t, 2048)
    tmf = _tile_for(t, 1024)
    n_slots = t * TOP_K
    n_blocks = n_slots // MOE_BLOCK + N_EXPERTS
    p_len = n_blocks * MOE_BLOCK

    bd64 = _block_diag_mean(DA_QK)
    bd32 = _block_diag_mean(DB_H)
    cos_t, sin_t = _rope_tables(s)
    tri = jnp.asarray(np.triu(np.ones((tm, tm)), 1), BF16)

    h = x.reshape(t, d)
    for i in range(depth):
        lam_init = 0.8 - 0.6 * math.exp(-0.3 * i)
        gsec = jnp.stack([
            jnp.tile(g_qa[i], 512 // DA_QK) * (DA_QK ** -0.5 * math.log2(math.e)),
            jnp.tile(g_ka[i], 512 // DA_QK),
            jnp.tile(g_qb[i], 512 // DB_H) * DB_H ** -0.5,
            jnp.tile(g_kb[i], 512 // DB_H),
        ]).astype(F32)
        proj, qvt = _inproj(h, g_mix[i][None, :], w_in[i], bd64, bd32, gsec, cos_t, sin_t, s, tm)
        proj3 = proj.reshape(b, s, proj.shape[1])

        lam4 = jnp.stack([lam_q1[i], lam_k1[i], lam_q2[i], lam_k2[i]]).astype(F32)
        oa = _diffattn(proj3, qvt, lam4, g_sub[i][None, :].astype(F32), lam_init)
        ob = _natten(proj3, _natten_bias_table(rpb[i]))

        w_router = jnp.concatenate(
            [w_rg[i], w_re[i], jnp.zeros((d, LANES - N_GROUPS - N_EXPERTS), F32)], axis=1).astype(F32)
        wr_hi = w_router.astype(BF16)
        wr_lo = (w_router - wr_hi.astype(F32)).astype(BF16)
        h1, xn, meta, meta_t, cnt = _outproj(h, oa.reshape(t, -1), ob.reshape(t, -1), w_out[i],
                                             g_ffn[i][None, :], jnp.concatenate([wr_hi, wr_lo], axis=1), tri, tm)

        counts = cnt[E_LANE0:E_LANE0 + N_EXPERTS, 0].astype(jnp.int32)
        padded = (counts + MOE_BLOCK - 1) // MOE_BLOCK * MOE_BLOCK
        pad_end = jnp.cumsum(padded)
        pad_start = pad_end - padded
        mt = meta_t[0:2 * TOP_K].astype(jnp.int32)
        e_ids = jnp.arange(N_EXPERTS, dtype=jnp.int32)[None, :, None]
        start_of = jnp.sum(jnp.where(mt[:TOP_K, None, :] == e_ids, pad_start[None, :, None], 0), axis=1)
        dest = start_of + mt[TOP_K:]
        blk_start = jnp.arange(n_blocks, dtype=jnp.int32) * MOE_BLOCK
        block_e = jnp.minimum(jnp.sum(pad_end[None, :] <= blk_start[:, None], axis=1), N_EXPERTS - 1)
        block_e = block_e.astype(jnp.int32)
        n_used = (pad_end[-1:] // MOE_BLOCK).astype(jnp.int32)

        dest_flat = dest.reshape(TOP_K * t)

        xs, row_src = _dispatch(xn, dest_flat, pad_end.astype(jnp.int32), n_used, p_len, tmd)
        in_use = counts > 0
        expert_ord = (jnp.cumsum(in_use) - 1).astype(jnp.int32)
        ids = jnp.arange(N_EXPERTS, dtype=jnp.int32)
        later = jnp.where((ids[None, :] > ids[:, None]) & in_use[None, :], ids[None, :], N_EXPERTS)
        next_expert = jnp.min(later, axis=1)
        next_expert = jnp.where(next_expert == N_EXPERTS, -1, next_expert).astype(jnp.int32)
        y_slots = _experts(xs, row_src.reshape(p_len), block_e, n_used, next_expert, expert_ord,
                           w1[i], w3[i], w2[i], TOP_K * t)
        h = _final(h1, meta, p[i].reshape(t, -1), w_ple[i], g_ple[i][None, :], g_plg[i][None, :], w_plg[i],
                   y_slots, tmf)
    return h.reshape(b, s, d)
```
